```python
import math
import jax, jax.numpy as jnp
from jax import lax
import numpy as np

D_MODEL = 1024
BATCH = 8
SEQ = 4096
DEPTH = 4
DEC_BATCH = 4
DEC_SEQ = 8192
PAST_LEN = 128

N_MEM = 256
A_GROUPS = ((128, 1), (512, 4), (2048, 16))
A_HEADS_PER_GROUP = 4
A_HEADS = A_HEADS_PER_GROUP * len(A_GROUPS)
A_HEAD_DIM = 64
A_WIDTH = A_HEADS * A_HEAD_DIM
REL_BUCKETS = 32
REL_MAX_DIST = 1024
S5_GROUP = 16
S5_WIDTH = 768
S5_GROUPS = S5_WIDTH // S5_GROUP
S5_STATE = 64
RET_HEADS = 6
RET_QK_DIM = 64
RET_V_DIM = 128
RET_WIDTH = RET_HEADS * RET_V_DIM
RET_CHUNK = 128
ROPE_BASE = 10000.0
MEM_HEADS = 4
MEM_HEAD_DIM = 192
MEM_WIDTH = MEM_HEADS * MEM_HEAD_DIM
N_BRANCH = 4
BRANCH_WIDTH = 768
IN_SPLITS = (A_WIDTH, A_WIDTH, A_WIDTH, S5_WIDTH,
             RET_HEADS * RET_QK_DIM, RET_HEADS * RET_QK_DIM, RET_WIDTH, RET_WIDTH,
             MEM_WIDTH, N_BRANCH * D_MODEL)
D_IN = sum(IN_SPLITS)
N_EXPERTS = 32
TOP_K = 4
D_FF = 1024
SWIGLU_LIMIT = 7.0
SWIGLU_ALPHA = 1.702
MOE_BLOCK = 128
LN_EPS = 1e-5
DEEPNORM_ALPHA = (2 * DEPTH) ** 0.25
DEEPNORM_BETA = (8 * DEPTH) ** -0.25
NEG_INF = -1e30

kernel_name = "hybrid_dilated_s5_retention_moe_encoder"


def layer_norm(x, g, b):
    xf = x.astype(jnp.float32)
    mu = jnp.mean(xf, axis=-1, keepdims=True)
    var = jnp.mean(jnp.square(xf - mu), axis=-1, keepdims=True)
    return ((xf - mu) * lax.rsqrt(var + LN_EPS) * g.astype(jnp.float32) + b.astype(jnp.float32)).astype(x.dtype)


def t5_bucket(rel):
    nb = REL_BUCKETS // 2
    max_exact = nb // 2
    ret = jnp.where(rel > 0, nb, 0)
    n = jnp.abs(rel)
    nf = jnp.maximum(n, 1).astype(jnp.float32)
    large = max_exact + (jnp.log(nf / max_exact) / math.log(REL_MAX_DIST / max_exact)
                         * (nb - max_exact)).astype(jnp.int32)
    large = jnp.minimum(large, nb - 1)
    return ret + jnp.where(n < max_exact, n, large)


def dilated_window_attention(q, k, v, bias_table, dilation, radius):
    b, s, h, dh = q.shape
    L = s // dilation
    nb = -(-L // radius)
    Lp = nb * radius
    N = b * dilation

    def to_sub(t):
        return t.reshape(b, L, dilation, h, dh).transpose(0, 2, 1, 3, 4).reshape(N, L, h, dh)

    qs, ks, vs = to_sub(q), to_sub(k), to_sub(v)
    qb = jnp.pad(qs, ((0, 0), (0, Lp - L), (0, 0), (0, 0))).reshape(N, nb, radius, h, dh)

    def neighbours(t):
        tp = jnp.pad(t, ((0, 0), (radius, Lp - L + radius), (0, 0), (0, 0)))
        tp = tp.reshape(N, nb + 2, radius, h, dh)
        return jnp.concatenate([tp[:, :-2], tp[:, 1:-1], tp[:, 2:]], axis=2)

    kb, vb = neighbours(ks), neighbours(vs)
    qi = jnp.arange(radius)[:, None]
    kj = jnp.arange(3 * radius)[None, :] - radius
    rel = kj - qi
    key_idx = jnp.arange(nb)[:, None, None] * radius + kj[None]
    valid = (jnp.abs(rel) <= radius)[None] & (key_idx >= 0) & (key_idx < L)
    bias = jnp.transpose(bias_table[t5_bucket(rel * dilation)], (2, 0, 1)).astype(jnp.float32)
    logits = jnp.einsum('nbqhd,nbkhd->nbhqk', qb, kb, preferred_element_type=jnp.float32) * (dh ** -0.5) + bias
    logits = jnp.where(valid[None, :, None], logits, NEG_INF)
    m = jnp.max(logits, axis=-1, keepdims=True)
    p = jnp.exp(logits - m)
    den = jnp.sum(p, axis=-1)
    out = jnp.einsum('nbhqk,nbkhd->nbqhd', p.astype(v.dtype), vb, preferred_element_type=jnp.float32)
    out = out / jnp.transpose(den, (0, 1, 3, 2))[..., None]
    lse = m[..., 0] + jnp.log(den)
    out = out.reshape(b, dilation, Lp, h, dh)[:, :, :L].transpose(0, 2, 1, 3, 4).reshape(b, s, h, dh)
    lse = jnp.transpose(lse, (0, 1, 3, 2)).reshape(b, dilation, Lp, h)[:, :, :L]
    lse = lse.transpose(0, 2, 1, 3).reshape(b, s, h)
    return out, lse


def dilated_attention_branch(q, k, v, rel_bias):
    b, s, _ = q.shape
    q = q.reshape(b, s, A_HEADS, A_HEAD_DIM)
    k = k.reshape(b, s, A_HEADS, A_HEAD_DIM)
    v = v.reshape(b, s, A_HEADS, A_HEAD_DIM)
    outs, lses = [], []
    for g, (window, dilation) in enumerate(A_GROUPS):
        hs = slice(g * A_HEADS_PER_GROUP, (g + 1) * A_HEADS_PER_GROUP)
        o, l = dilated_window_attention(q[:, :, hs], k[:, :, hs], v[:, :, hs], rel_bias[:, hs],
                                        dilation, window // (2 * dilation))
        outs.append(o)
        lses.append(l)
    o = jnp.stack(outs, axis=2)
    wts = jax.nn.softmax(jnp.stack(lses, axis=2), axis=2)
    return (o * wts[..., None]).reshape(b, s, A_WIDTH).astype(q.dtype)


def _complex_affine_combine(e1, e2):
    a1r, a1i, b1r, b1i = e1
    a2r, a2i, b2r, b2i = e2
    return (a1r * a2r - a1i * a2i, a1r * a2i + a1i * a2r,
            a2r * b1r - a2i * b1i + b2r, a2r * b1i + a2i * b1r + b2i)


def s5_branch(u, lam_re, lam_im, log_step, b_re, b_im, c_re, c_im, d_skip, w_glu, b_glu):
    f32 = jnp.float32
    bsz, s, _ = u.shape
    uf = u.astype(f32)
    ug = uf.reshape(bsz, s, S5_GROUPS, S5_GROUP)
    y = d_skip.astype(f32) * uf
    for direction in range(2):
        step = jnp.exp(log_step[direction].astype(f32))[:, None]
        lr = lam_re[direction].astype(f32)
        li = lam_im[direction].astype(f32)
        mag = jnp.exp(lr * step)
        ar, ai = mag * jnp.cos(li * step), mag * jnp.sin(li * step)
        nr, ni = ar - 1.0, ai
        den = lr * lr + li * li
        fr = (nr * lr + ni * li) / den
        fi = (ni * lr - nr * li) / den
        br, bi = b_re[direction].astype(f32), b_im[direction].astype(f32)
        bbr = fr[..., None] * br - fi[..., None] * bi
        bbi = fr[..., None] * bi + fi[..., None] * br
        bu_r = jnp.einsum('gpc,bsgc->bsgp', bbr, ug)
        bu_i = jnp.einsum('gpc,bsgc->bsgp', bbi, ug)
        a_r = jnp.broadcast_to(ar, (1, s) + ar.shape)
        a_i = jnp.broadcast_to(ai, (1, s) + ai.shape)
        _, _, xr, xi = lax.associative_scan(_complex_affine_combine, (a_r, a_i, bu_r, bu_i),
                                            axis=1, reverse=(direction == 1))
        yc = (jnp.einsum('gcp,bsgp->bsgc', c_re[direction].astype(f32), xr)
              - jnp.einsum('gcp,bsgp->bsgc', c_im[direction].astype(f32), xi))
        y = y + yc.reshape(bsz, s, S5_WIDTH)
    z = jax.nn.gelu(y)
    out = z * jax.nn.sigmoid(z @ w_glu.astype(f32) + b_glu.astype(f32))
    return out.astype(u.dtype)


def rotary(x, pos):
    half = x.shape[-1] // 2
    inv = ROPE_BASE ** (-jnp.arange(half, dtype=jnp.float32) / half)
    ang = pos.astype(jnp.float32)[:, None] * inv[None, :]
    cos, sin = jnp.cos(ang)[:, None, :], jnp.sin(ang)[:, None, :]
    x1 = x[..., :half].astype(jnp.float32)
    x2 = x[..., half:].astype(jnp.float32)
    return jnp.concatenate([x1 * cos - x2 * sin, x1 * sin + x2 * cos], axis=-1)


def retention_branch(q, k, v, g):
    f32 = jnp.float32
    b, s, _ = q.shape
    C = RET_CHUNK
    nc = s // C
    pos = jnp.arange(s)
    qh = rotary(q.reshape(b, s, RET_HEADS, RET_QK_DIM), pos).reshape(b, nc, C, RET_HEADS, RET_QK_DIM)
    kh = (rotary(k.reshape(b, s, RET_HEADS, RET_QK_DIM), pos) * (RET_QK_DIM ** -0.5)).reshape(
        b, nc, C, RET_HEADS, RET_QK_DIM)
    vh = v.astype(f32).reshape(b, nc, C, RET_HEADS, RET_V_DIM)
    hidx = jnp.arange(RET_HEADS, dtype=f32)
    log_gf = jnp.log1p(-jnp.exp2(-5.0 - hidx))
    log_gb = jnp.log1p(-jnp.exp2(-5.5 - hidx))
    j = jnp.arange(C, dtype=f32)
    rel = j[:, None] - j[None, :]
    intra = jnp.exp(jnp.abs(rel)[None] * jnp.where(rel[None] >= 0, log_gf[:, None, None], log_gb[:, None, None]))
    scores = jnp.einsum('bcnhk,bcmhk->bchnm', qh, kh) * intra[None, None]
    o = jnp.einsum('bchnm,bcmhv->bcnhv', scores, vh)
    tail_f = jnp.exp((C - 1 - j)[None, :] * log_gf[:, None])
    head_b = jnp.exp(j[None, :] * log_gb[:, None])
    kv_f = jnp.einsum('bcmhk,bcmhv,hm->cbhkv', kh, vh, tail_f)
    kv_b = jnp.einsum('bcmhk,bcmhv,hm->cbhkv', kh, vh, head_b)
    chunk_f = jnp.exp(C * log_gf)[None, :, None, None]
    chunk_b = jnp.exp(C * log_gb)[None, :, None, None]

    def step_f(state, kv):
        return chunk_f * state + kv, state

    def step_b(state, kv):
        return chunk_b * state + kv, state

    init = jnp.zeros((b, RET_HEADS, RET_QK_DIM, RET_V_DIM), f32)
    _, s_f = lax.scan(step_f, init, kv_f)
    _, s_b = lax.scan(step_b, init, kv_b, reverse=True)
    o = (o + jnp.einsum('bcnhk,cbhkv,hn->bcnhv', qh, s_f, jnp.exp((j + 1.0)[None, :] * log_gf[:, None]))
         + jnp.einsum('bcnhk,cbhkv,hn->bcnhv', qh, s_b, jnp.exp((C - j)[None, :] * log_gb[:, None])))
    o = o.reshape(b, s, RET_HEADS, RET_V_DIM)
    mu = jnp.mean(o, axis=-1, keepdims=True)
    var = jnp.mean(jnp.square(o - mu), axis=-1, keepdims=True)
    o = ((o - mu) * lax.rsqrt(var + LN_EPS)).reshape(b, s, RET_WIDTH)
    return (o * jax.nn.silu(g.astype(f32))).astype(g.dtype)


def memory_attention(q, mem, w_mem_kv):
    b, s, _ = q.shape
    m = mem.shape[1]
    kv = mem @ w_mem_kv
    kh = kv[..., :MEM_WIDTH].reshape(b, m, MEM_HEADS, MEM_HEAD_DIM)
    vh = kv[..., MEM_WIDTH:].reshape(b, m, MEM_HEADS, MEM_HEAD_DIM)
    qh = q.reshape(b, s, MEM_HEADS, MEM_HEAD_DIM)
    logits = jnp.einsum('bshd,bmhd->bhsm', qh, kh, preferred_element_type=jnp.float32) * (MEM_HEAD_DIM ** -0.5)
    p = jax.nn.softmax(logits, axis=-1)
    o = jnp.einsum('bhsm,bmhd->bshd', p.astype(vh.dtype), vh)
    return o.reshape(b, s, MEM_WIDTH).astype(q.dtype)


def moe(x, router_w, router_b, w1, b1, w2, b2):
    bsz, s, d = x.shape
    xt = x.reshape(-1, d)
    T = xt.shape[0]
    logits = (xt @ router_w + router_b).astype(jnp.float32)
    top_val, top_idx = lax.top_k(logits, TOP_K)
    gates = jax.nn.softmax(top_val, axis=-1)
    n_assign = T * TOP_K
    flat_e = top_idx.reshape(-1)
    flat_tok = jnp.repeat(jnp.arange(T, dtype=jnp.int32), TOP_K)
    flat_g = gates.reshape(-1)
    order = jnp.argsort(flat_e)
    e_sorted, tok_sorted, g_sorted = flat_e[order], flat_tok[order], flat_g[order]
    counts = jnp.zeros((N_EXPERTS,), jnp.int32).at[flat_e].add(1)
    starts = jnp.cumsum(counts) - counts
    padded = (counts + MOE_BLOCK - 1) // MOE_BLOCK * MOE_BLOCK
    pad_ends = jnp.cumsum(padded)
    pad_starts = pad_ends - padded
    dest = pad_starts[e_sorted] + (jnp.arange(n_assign, dtype=jnp.int32) - starts[e_sorted])
    n_blocks = -(-n_assign // MOE_BLOCK) + N_EXPERTS
    n_rows = n_blocks * MOE_BLOCK
    row_tok = jnp.full((n_rows,), T, jnp.int32).at[dest].set(tok_sorted)
    row_gate = jnp.zeros((n_rows,), jnp.float32).at[dest].set(g_sorted)
    block_expert = jnp.minimum(
        jnp.searchsorted(pad_ends, jnp.arange(n_blocks, dtype=jnp.int32) * MOE_BLOCK, side='right'),
        N_EXPERTS - 1).astype(jnp.int32)
    x_pad = jnp.concatenate([xt, jnp.zeros((1, d), xt.dtype)], axis=0)
    xb = x_pad[row_tok].reshape(n_blocks, MOE_BLOCK, d)

    def expert_block(args):
        xs, e = args
        h = xs @ w1[e] + b1[e]
        gate = jnp.minimum(h[..., :D_FF], SWIGLU_LIMIT)
        up = jnp.clip(h[..., D_FF:], -SWIGLU_LIMIT, SWIGLU_LIMIT)
        hid = gate * jax.nn.sigmoid(SWIGLU_ALPHA * gate) * (up + 1.0)
        return hid @ w2[e] + b2[e]

    yb = lax.map(expert_block, (xb, block_expert))
    y = jnp.zeros((T + 1, d), jnp.float32).at[row_tok].add(
        yb.reshape(n_rows, d).astype(jnp.float32) * row_gate[:, None])
    return y[:T].reshape(bsz, s, d).astype(x.dtype)


def trunk_layer(x, mem, rel_bias, w_in, s5_lam_re, s5_lam_im, s5_log_step, s5_b_re, s5_b_im,
                s5_c_re, s5_c_im, s5_d, s5_w_glu, s5_b_glu, w_mem_kv, w_branch, w_out,
                ln1_g, ln1_b, router_w, router_b, moe_w1, moe_b1, moe_w2, moe_b2, ln2_g, ln2_b):
    b, s, _ = x.shape
    proj = x @ w_in
    cols = []
    start = 0
    for width in IN_SPLITS:
        cols.append(proj[..., start:start + width])
        start += width
    aq, ak, av, su, rq, rk, rv, rg, mq, gate_logits = cols
    ya = dilated_attention_branch(aq, ak, av, rel_bias)
    yb = s5_branch(su, s5_lam_re, s5_lam_im, s5_log_step, s5_b_re, s5_b_im, s5_c_re, s5_c_im,
                   s5_d, s5_w_glu, s5_b_glu)
    yc = retention_branch(rq, rk, rv, rg)
    yd = memory_attention(mq, mem, w_mem_kv)
    branches = jnp.stack([ya, yb, yc, yd], axis=2)
    branch_out = jnp.einsum('bsnc,ncd->bsnd', branches, w_branch)
    gates = jax.nn.sigmoid(gate_logits.reshape(b, s, N_BRANCH, D_MODEL))
    merged = jnp.sum(gates * branch_out, axis=2)
    x = layer_norm(DEEPNORM_ALPHA * x + merged @ w_out, ln1_g, ln1_b)
    x = layer_norm(DEEPNORM_ALPHA * x + moe(x, router_w, router_b, moe_w1, moe_b1, moe_w2, moe_b2), ln2_g, ln2_b)
    return x


def setup_inputs(seed: int = 0) -> dict:
    key = jax.random.key(seed)
    ks = jax.random.split(key, 32)
    nrm = jax.random.normal
    f32 = jnp.float32
    return {
        "x_prompt": nrm(ks[0], (BATCH, SEQ, D_MODEL), f32),
        "x_sample": nrm(ks[1], (DEC_BATCH, DEC_SEQ, D_MODEL), f32),
        "mem_prompt": nrm(ks[2], (BATCH, N_MEM, D_MODEL), f32),
        "mem_sample": nrm(ks[3], (DEC_BATCH, N_MEM, D_MODEL), f32),
        "ln_in_g": 1.0 + 0.02 * nrm(ks[4], (D_MODEL,), f32),
        "ln_in_b": 0.02 * nrm(ks[5], (D_MODEL,), f32),
        "rel_bias": 0.2 * nrm(ks[6], (REL_BUCKETS, A_HEADS), f32),
        "w_in": nrm(ks[7], (DEPTH, D_MODEL, D_IN), f32) * D_MODEL ** -0.5,
        "s5_lam_re": -0.5 + 0.01 * nrm(ks[8], (DEPTH, 2, S5_GROUPS, S5_STATE), f32),
        "s5_lam_im": math.pi * jnp.arange(S5_STATE, dtype=f32) + 0.01 * nrm(ks[9], (DEPTH, 2, S5_GROUPS, S5_STATE), f32),
        "s5_log_step": jax.random.uniform(ks[10], (DEPTH, 2, S5_GROUPS), f32, math.log(1e-3), math.log(1e-1)),
        "s5_b_re": nrm(ks[11], (DEPTH, 2, S5_GROUPS, S5_STATE, S5_GROUP), f32) * (2 * S5_GROUP) ** -0.5,
        "s5_b_im": nrm(ks[12], (DEPTH, 2, S5_GROUPS, S5_STATE, S5_GROUP), f32) * (2 * S5_GROUP) ** -0.5,
        "s5_c_re": nrm(ks[13], (DEPTH, 2, S5_GROUPS, S5_GROUP, S5_STATE), f32) * (2 * S5_STATE) ** -0.5,
        "s5_c_im": nrm(ks[14], (DEPTH, 2, S5_GROUPS, S5_GROUP, S5_STATE), f32) * (2 * S5_STATE) ** -0.5,
        "s5_d": nrm(ks[15], (DEPTH, S5_WIDTH), f32),
        "s5_w_glu": nrm(ks[16], (DEPTH, S5_WIDTH, S5_WIDTH), f32) * S5_WIDTH ** -0.5,
        "s5_b_glu": 0.01 * nrm(ks[17], (DEPTH, S5_WIDTH), f32),
        "w_mem_kv": nrm(ks[18], (DEPTH, D_MODEL, 2 * MEM_WIDTH), f32) * D_MODEL ** -0.5,
        "w_branch": nrm(ks[19], (DEPTH, N_BRANCH, BRANCH_WIDTH, D_MODEL), f32) * BRANCH_WIDTH ** -0.5,
        "w_out": nrm(ks[20], (DEPTH, D_MODEL, D_MODEL), f32) * (D_MODEL ** -0.5 * DEEPNORM_BETA),
        "ln1_g": 1.0 + 0.02 * nrm(ks[21], (DEPTH, D_MODEL), f32),
        "ln1_b": 0.02 * nrm(ks[22], (DEPTH, D_MODEL), f32),
        "router_w": nrm(ks[23], (DEPTH, D_MODEL, N_EXPERTS), f32) * D_MODEL ** -0.5,
        "router_b": 0.01 * nrm(ks[24], (DEPTH, N_EXPERTS), f32),
        "moe_w1": nrm(ks[25], (DEPTH, N_EXPERTS, D_MODEL, 2 * D_FF), f32) * D_MODEL ** -0.5,
        "moe_b1": 0.01 * nrm(ks[26], (DEPTH, N_EXPERTS, 2 * D_FF), f32),
        "moe_w2": nrm(ks[27], (DEPTH, N_EXPERTS, D_FF, D_MODEL), f32) * (D_FF ** -0.5 * DEEPNORM_BETA),
        "moe_b2": 0.01 * nrm(ks[28], (DEPTH, N_EXPERTS, D_MODEL), f32),
        "ln2_g": 1.0 + 0.02 * nrm(ks[29], (DEPTH, D_MODEL), f32),
        "ln2_b": 0.02 * nrm(ks[30], (DEPTH, D_MODEL), f32),
    }


def reference(x_prompt, x_sample, mem_prompt, mem_sample, ln_in_g, ln_in_b, rel_bias, w_in,
              s5_lam_re, s5_lam_im, s5_log_step, s5_b_re, s5_b_im, s5_c_re, s5_c_im, s5_d,
              s5_w_glu, s5_b_glu, w_mem_kv, w_branch, w_out, ln1_g, ln1_b, router_w, router_b,
              moe_w1, moe_b1, moe_w2, moe_b2, ln2_g, ln2_b):
    def trunk(x, mem):
        x = layer_norm(x, ln_in_g, ln_in_b)
        for l in range(DEPTH):
            x = trunk_layer(x, mem, rel_bias, w_in[l], s5_lam_re[l], s5_lam_im[l], s5_log_step[l],
                            s5_b_re[l], s5_b_im[l], s5_c_re[l], s5_c_im[l], s5_d[l], s5_w_glu[l],
                            s5_b_glu[l], w_mem_kv[l], w_branch[l], w_out[l], ln1_g[l], ln1_b[l],
                            router_w[l], router_b[l], moe_w1[l], moe_b1[l], moe_w2[l], moe_b2[l],
                            ln2_g[l], ln2_b[l])
        return x

    y_prompt = trunk(x_prompt, mem_prompt)
    y_sample = trunk(x_sample, mem_sample)
    return (y_prompt, y_sample)
```

```python
import functools
import math

import jax
import jax.numpy as jnp
from jax import lax
from jax.experimental import pallas as pl
from jax.experimental.pallas import tpu as pltpu

F32 = jnp.float32
BF16 = jnp.bfloat16
HIGHEST = lax.Precision.HIGHEST

D_MODEL = 1024
DEPTH = 4
N_MEM = 256
A_GROUPS = ((128, 1), (512, 4), (2048, 16))
A_HEADS_PER_GROUP = 4
A_HEADS = 12
A_HEAD_DIM = 64
A_RADIUS = 64
REL_BUCKETS = 32
REL_MAX_DIST = 1024
S5_GROUP = 16
S5_WIDTH = 768
S5_GROUPS = 48
S5_STATE = 64
RET_HEADS = 6
RET_QK_DIM = 64
RET_V_DIM = 128
RET_CHUNK = 128
ROPE_BASE = 10000.0
MEM_HEADS = 4
MEM_HEAD_DIM = 192
MEM_HEAD_PAD = 256
N_EXPERTS = 32
TOP_K = 4
D_FF = 1024
SWIGLU_LIMIT = 7.0
SWIGLU_ALPHA = 1.702
LN_EPS = 1e-5
DEEPNORM_ALPHA = (2 * DEPTH) ** 0.25
NEG_INF = -1e30

LANE = 128
VMEM_LIMIT = 56 * 1024 * 1024
S5_CHUNK = 128
MOE_BLOCK_ROWS = 256

_SEG = {}
_off = 0
for _name, _width, _align in (
        ("gate", 32, 8), ("mq", 8, 8), ("aq", 6, 2), ("ak", 6, 2), ("av", 6, 2), ("pad", 2, 2),
        ("rv", 6, 6), ("rg", 6, 6), ("rq", 3, 3), ("rqs", 3, 3), ("rk", 3, 3), ("rks", 3, 3)):
    assert _off % _align == 0, (_name, _off)
    _SEG[_name] = _off
    _off += _width
NP_UNITS = _off
NP = NP_UNITS * LANE
assert NP % 256 == 0


def _cparams(sem, vmem=VMEM_LIMIT):
    return pltpu.CompilerParams(dimension_semantics=sem, vmem_limit_bytes=vmem)


def _ln(h, g, b):
    mu = jnp.mean(h, axis=-1, keepdims=True)
    d = h - mu
    var = jnp.mean(d * d, axis=-1, keepdims=True)
    return d * lax.rsqrt(var + LN_EPS) * g + b


def _ln_kernel(x_ref, g_ref, b_ref, o_ref):
    o_ref[...] = _ln(x_ref[...], g_ref[...], b_ref[...])


def _layer_norm_call(x, g, b):
    t = x.shape[0]
    tm = 512
    return pl.pallas_call(
        _ln_kernel, name="input_ln",
        grid=(t // tm,),
        in_specs=[pl.BlockSpec((tm, D_MODEL), lambda i: (i, 0)),
                  pl.BlockSpec((1, D_MODEL), lambda i: (0, 0)),
                  pl.BlockSpec((1, D_MODEL), lambda i: (0, 0))],
        out_specs=pl.BlockSpec((tm, D_MODEL), lambda i: (i, 0)),
        out_shape=jax.ShapeDtypeStruct((t, D_MODEL), F32),
        compiler_params=_cparams(("parallel",)),
    )(x, g.reshape(1, -1), b.reshape(1, -1))


def _proj_kernel(x_ref, w_ref, wsut_ref, p_ref, ut_ref, xb_ref):
    @pl.when(pl.program_id(1) == 0)
    def _():
        xb = x_ref[...].astype(BF16)
        xb_ref[...] = xb
        ut_ref[...] = lax.dot_general(wsut_ref[...], xb, (((1,), (1,)), ((), ())),
                                      preferred_element_type=F32).astype(BF16)

    p_ref[...] = jnp.dot(xb_ref[...], w_ref[...], preferred_element_type=F32).astype(BF16)


def _proj_call(x, w_p, w_su_t):
    t = x.shape[0]
    tm, tn = 512, NP // 7
    return pl.pallas_call(
        _proj_kernel, name="in_proj",
        grid=(t // tm, NP // tn),
        in_specs=[pl.BlockSpec((tm, D_MODEL), lambda i, j: (i, 0)),
                  pl.BlockSpec((D_MODEL, tn), lambda i, j: (0, j)),
                  pl.BlockSpec((S5_WIDTH, D_MODEL), lambda i, j: (0, 0))],
        out_specs=[pl.BlockSpec((tm, tn), lambda i, j: (i, j)),
                   pl.BlockSpec((S5_WIDTH, tm), lambda i, j: (0, i))],
        out_shape=[jax.ShapeDtypeStruct((t, NP), BF16),
                   jax.ShapeDtypeStruct((S5_WIDTH, t), BF16)],
        scratch_shapes=[pltpu.VMEM((tm, D_MODEL), BF16)],
        compiler_params=_cparams(("parallel", "arbitrary")),
    )(x, w_p, w_su_t)


def _attn_kernel(q_ref, kp_ref, kc_ref, kn_ref, vp_ref, vc_ref, vn_ref, bias_ref, o_ref, lse_ref, *, sub_len):
    tq = q_ref.shape[0]
    half = A_RADIUS
    qi = pl.program_id(2)
    q = q_ref[...]
    kwin = jnp.concatenate([kp_ref[tq - half:, :], kc_ref[...], kn_ref[:half, :]], axis=0)
    vwin = jnp.concatenate([vp_ref[tq - half:, :], vc_ref[...], vn_ref[:half, :]], axis=0)
    nk = kwin.shape[0]
    col = lax.broadcasted_iota(jnp.int32, (tq, nk), 1)
    kidx = qi * tq - half + col
    valid = (kidx >= 0) & (kidx < sub_len)
    lane_head = lax.broadcasted_iota(jnp.int32, (1, q.shape[1]), 1) // A_HEAD_DIM
    acc = jnp.zeros((tq, q.shape[1]), F32)
    lse_acc = jnp.zeros((tq, q.shape[1]), F32)
    scale = jnp.asarray(A_HEAD_DIM ** -0.5, BF16)
    for h in range(A_HEADS_PER_GROUP):
        hm = lane_head == h
        qh = jnp.where(hm, q, jnp.zeros_like(q)) * scale
        s = lax.dot_general(qh, kwin, (((1,), (1,)), ((), ())), preferred_element_type=F32)
        s = s + bias_ref[h]
        s = jnp.where(valid, s, NEG_INF)
        m = jnp.max(s, axis=-1, keepdims=True)
        p = jnp.exp(s - m)
        den = jnp.sum(p, axis=-1, keepdims=True)
        oh = jnp.dot(p.astype(BF16), vwin, preferred_element_type=F32)
        acc = acc + jnp.where(hm, oh / den, 0.0)
        lse_acc = lse_acc + jnp.where(hm, m + jnp.log(den), 0.0)
    o_ref[...] = acc.astype(BF16)
    lse_ref[...] = lse_acc


def _attn_call(proj, bias, b, s, g, dil):
    sub_len = s // dil
    tq = 128
    nqb = sub_len // tq
    gw = A_HEADS_PER_GROUP * A_HEAD_DIM
    npb = NP // gw
    pv = proj.reshape(b, sub_len, dil * NP)
    qo, ko, vo = (_SEG[n] * LANE // gw + g for n in ("aq", "ak", "av"))
    blk = (None, tq, gw)
    cur = lambda off: (lambda bi, r, qi: (bi, qi, r * npb + off))
    prv = lambda off: (lambda bi, r, qi: (bi, jnp.maximum(qi - 1, 0), r * npb + off))
    nxt = lambda off: (lambda bi, r, qi: (bi, jnp.minimum(qi + 1, nqb - 1), r * npb + off))
    out_map = lambda bi, r, qi: (bi, qi, r)
    o, lse = pl.pallas_call(
        functools.partial(_attn_kernel, sub_len=sub_len), name=f"dilated_attn_g{g}",
        grid=(b, dil, nqb),
        in_specs=[pl.BlockSpec(blk, cur(qo)),
                  pl.BlockSpec(blk, prv(ko)), pl.BlockSpec(blk, cur(ko)), pl.BlockSpec(blk, nxt(ko)),
                  pl.BlockSpec(blk, prv(vo)), pl.BlockSpec(blk, cur(vo)), pl.BlockSpec(blk, nxt(vo)),
                  pl.BlockSpec((A_HEADS_PER_GROUP, tq, 2 * tq), lambda bi, r, qi: (0, 0, 0))],
        out_specs=[pl.BlockSpec(blk, out_map), pl.BlockSpec(blk, out_map)],
        out_shape=[jax.ShapeDtypeStruct((b, sub_len, dil * gw), BF16),
                   jax.ShapeDtypeStruct((b, sub_len, dil * gw), F32)],
        compiler_params=_cparams(("parallel", "parallel", "parallel")),
    )(pv, pv, pv, pv, pv, pv, pv, bias)
    return o.reshape(b * s, gw), lse.reshape(b * s, gw)


def _t5_bucket(rel):
    nb = REL_BUCKETS // 2
    max_exact = nb // 2
    ret = jnp.where(rel > 0, nb, 0)
    n = jnp.abs(rel)
    nf = jnp.maximum(n, 1).astype(F32)
    large = max_exact + (jnp.log(nf / max_exact) / math.log(REL_MAX_DIST / max_exact)
                         * (nb - max_exact)).astype(jnp.int32)
    large = jnp.minimum(large, nb - 1)
    return ret + jnp.where(n < max_exact, n, large)


def _attn_bias_table(rel_bias, g, dil, tq=128):
    qi = jnp.arange(tq)[:, None]
    kj = jnp.arange(2 * tq)[None, :] - A_RADIUS
    rel = kj - qi
    tab = rel_bias[:, g * A_HEADS_PER_GROUP:(g + 1) * A_HEADS_PER_GROUP].astype(F32)
    bias = jnp.transpose(tab[_t5_bucket(rel * dil)], (2, 0, 1))
    return jnp.where((jnp.abs(rel) <= A_RADIUS)[None], bias, NEG_INF)


def _s5_toeplitz_kernel(bcf_ref, pwf_ref, bcb_ref, pwb_ref, d_ref, t_ref, gf_ref, gb_ref):
    gf_ref[...] = jnp.dot(bcf_ref[...], pwf_ref[...], precision=HIGHEST, preferred_element_type=F32)
    gb_ref[...] = jnp.dot(bcb_ref[...], pwb_ref[...], precision=HIGHEST, preferred_element_type=F32)
    c = S5_CHUNK
    row = lax.broadcasted_iota(jnp.int32, (c, c), 0)
    col = lax.broadcasted_iota(jnp.int32, (c, c), 1)

    def body(ci, carry):
        dval = d_ref[pl.ds(ci, 1), :]
        for co in range(S5_GROUP):
            r = ci * S5_GROUP + co
            gf = jnp.broadcast_to(gf_ref[pl.ds(r, 1), :], (c, c))
            gb = jnp.broadcast_to(gb_ref[pl.ds(r, 1), :], (c, c))
            tf = pltpu.roll(gf, 0, 1, stride=1, stride_axis=0)
            tb = pltpu.roll(gb, 1, 1, stride=1, stride_axis=0)
            tile = jnp.where(col >= row, tf, 0.0) + jnp.where(row >= col, tb, 0.0)
            tile = tile + jnp.where((row == col) & (ci == co), dval, 0.0)
            t_ref[pl.ds(pl.multiple_of(ci * c, c), c), co * c:(co + 1) * c] = tile.astype(BF16)
        return carry

    lax.fori_loop(0, S5_GROUP, body, 0)


def _s5_toeplitz_call(bcf, pwf, bcb, pwb, dskip):
    n = S5_GROUP * S5_CHUNK
    return pl.pallas_call(
        _s5_toeplitz_kernel, name="s5_toeplitz",
        grid=(S5_GROUPS,),
        in_specs=[pl.BlockSpec((None, S5_GROUP * S5_GROUP, 2 * S5_STATE), lambda g: (g, 0, 0)),
                  pl.BlockSpec((None, 2 * S5_STATE, S5_CHUNK), lambda g: (g, 0, 0)),
                  pl.BlockSpec((None, S5_GROUP * S5_GROUP, 2 * S5_STATE), lambda g: (g, 0, 0)),
                  pl.BlockSpec((None, 2 * S5_STATE, S5_CHUNK), lambda g: (g, 0, 0)),
                  pl.BlockSpec((None, S5_GROUP, LANE), lambda g: (g, 0, 0))],
        out_specs=pl.BlockSpec((None, n, n), lambda g: (g, 0, 0)),
        out_shape=jax.ShapeDtypeStruct((S5_GROUPS, n, n), BF16),
        scratch_shapes=[pltpu.VMEM((S5_GROUP * S5_GROUP, S5_CHUNK), F32),
                        pltpu.VMEM((S5_GROUP * S5_GROUP, S5_CHUNK), F32)],
        compiler_params=_cparams(("parallel",)),
    )(bcf, pwf, bcb, pwb, dskip)


def _s5_tables(lam_re, lam_im, log_step, b_re, b_im, c_re, c_im, d_skip):
    c = S5_CHUNK
    k = jnp.arange(c, dtype=F32)
    per_dir = []
    for direction in range(2):
        step = jnp.exp(log_step[direction].astype(F32))[:, None]
        lr, li = lam_re[direction].astype(F32), lam_im[direction].astype(F32)
        mag = jnp.exp(lr * step)
        ar, ai = mag * jnp.cos(li * step), mag * jnp.sin(li * step)
        nr, ni = ar - 1.0, ai
        den = lr * lr + li * li
        fr = (nr * lr + ni * li) / den
        fi = (ni * lr - nr * li) / den
        br, bi = b_re[direction].astype(F32), b_im[direction].astype(F32)
        bbr = fr[..., None] * br - fi[..., None] * bi
        bbi = fr[..., None] * bi + fi[..., None] * br
        cr, cim = c_re[direction].astype(F32), c_im[direction].astype(F32)

        def power(e, log_mag=lr * step, phase=li * step):
            m = jnp.exp(e[None, None, :] * log_mag[..., None])
            th = e[None, None, :] * phase[..., None]
            return m * jnp.cos(th), m * jnp.sin(th)

        per_dir.append(dict(bbr=bbr, bbi=bbi, cr=cr, cim=cim, power=power))

    def bc_table(d):
        bbr_t = jnp.transpose(d["bbr"], (0, 2, 1))[:, :, None, :]
        bbi_t = jnp.transpose(d["bbi"], (0, 2, 1))[:, :, None, :]
        cr, cim = d["cr"][:, None], d["cim"][:, None]
        re = bbr_t * cr - bbi_t * cim
        im = bbr_t * cim + bbi_t * cr
        return jnp.concatenate([re, -im], axis=-1).reshape(S5_GROUPS, S5_GROUP * S5_GROUP, 2 * S5_STATE)

    f, bw = per_dir
    pfr, pfi = f["power"](k)
    pbr, pbi = bw["power"](c - 1 - k)
    pwf = jnp.concatenate([pfr, pfi], axis=1)
    pwb = jnp.concatenate([pbr, pbi], axis=1)
    dsk = jnp.broadcast_to(d_skip.astype(F32).reshape(S5_GROUPS, S5_GROUP, 1), (S5_GROUPS, S5_GROUP, LANE))
    toep = _s5_toeplitz_call(bc_table(f), pwf, bc_table(bw), pwb, dsk)

    def w_in(d, e):
        pr, pi = d["power"](e)
        re = d["bbr"][:, :, :, None] * pr[:, :, None, :] - d["bbi"][:, :, :, None] * pi[:, :, None, :]
        im = d["bbr"][:, :, :, None] * pi[:, :, None, :] + d["bbi"][:, :, :, None] * pr[:, :, None, :]
        tr = lambda z: jnp.transpose(z, (0, 2, 3, 1)).reshape(S5_GROUPS, S5_GROUP * c, S5_STATE)
        return tr(re), tr(im)

    fre, fim = w_in(f, c - 1 - k)
    bre, bim = w_in(bw, k)
    win = jnp.concatenate([fre, bre, fim, bim], axis=-1).astype(BF16)

    def w_out(d, e):
        pr, pi = d["power"](e)
        re = d["cr"][:, :, :, None] * pr[:, None] - d["cim"][:, :, :, None] * pi[:, None]
        im = d["cr"][:, :, :, None] * pi[:, None] + d["cim"][:, :, :, None] * pr[:, None]
        tr = lambda z: jnp.transpose(z, (0, 2, 1, 3)).reshape(S5_GROUPS, S5_STATE, S5_GROUP * c)
        return tr(re), tr(-im)

    ofre, ofim = w_out(f, k + 1.0)
    obre, obim = w_out(bw, c - k)
    wout = jnp.concatenate([ofre, obre, ofim, obim], axis=1).astype(BF16)

    cc = jnp.asarray([float(c)], F32)
    afr, afi = f["power"](cc)
    abr, abi = bw["power"](cc)
    a_re = jnp.concatenate([afr[..., 0], abr[..., 0]], axis=-1).reshape(1, -1)
    a_im = jnp.concatenate([afi[..., 0], abi[..., 0]], axis=-1).reshape(1, -1)
    return toep, win, wout, a_re, a_im


def _s5_contrib_kernel(u_ref, win_ref, re_ref, im_ref):
    uc = jnp.concatenate([u_ref[ci] for ci in range(S5_GROUP)], axis=-1)
    res = jnp.dot(uc, win_ref[...], preferred_element_type=F32)
    nc, nb = re_ref.shape[0], re_ref.shape[1]
    half = 2 * S5_STATE
    for bi in range(nb):
        re_ref[:, bi, :] = res[bi * nc:(bi + 1) * nc, :half]
        im_ref[:, bi, :] = res[bi * nc:(bi + 1) * nc, half:]


def _s5_contrib_call(ut, win, b, nc):
    nch = b * nc
    half = 2 * S5_STATE
    uv = ut.reshape(S5_GROUPS, S5_GROUP, nch, S5_CHUNK)
    return pl.pallas_call(
        _s5_contrib_kernel, name="s5_contrib",
        grid=(S5_GROUPS,),
        in_specs=[pl.BlockSpec((None, S5_GROUP, nch, S5_CHUNK), lambda g: (g, 0, 0, 0)),
                  pl.BlockSpec((None, S5_GROUP * S5_CHUNK, 4 * S5_STATE), lambda g: (g, 0, 0))],
        out_specs=[pl.BlockSpec((nc, b, half), lambda g: (0, 0, g))] * 2,
        out_shape=[jax.ShapeDtypeStruct((nc, b, S5_GROUPS * half), F32)] * 2,
        compiler_params=_cparams(("parallel",)),
    )(uv, win)


def _s5_scan_kernel(cre_ref, cim_ref, are_ref, aim_ref, ore_ref, oim_ref):
    nc, nb, width = cre_ref.shape
    a_re, a_im = are_ref[...], aim_ref[...]
    lane = lax.broadcasted_iota(jnp.int32, (1, width), 1)
    fwd_lane = (lane % (2 * S5_STATE)) < S5_STATE
    zero = jnp.zeros((nb, width), F32)

    def advance(s, c):
        sr, si = s
        return a_re * sr - a_im * si + cre_ref[c], a_re * si + a_im * sr + cim_ref[c]

    def up(c, s):
        ore_ref[c] = s[0]
        oim_ref[c] = s[1]
        return advance(s, c)

    lax.fori_loop(0, nc, up, (zero, zero))

    def down(i, s):
        c = nc - 1 - i
        ore_ref[c] = jnp.where(fwd_lane, ore_ref[c], s[0])
        oim_ref[c] = jnp.where(fwd_lane, oim_ref[c], s[1])
        return advance(s, c)

    lax.fori_loop(0, nc, down, (zero, zero))


def _s5_scan_call(cre, cim, a_re, a_im):
    return pl.pallas_call(
        _s5_scan_kernel, name="s5_chunk_scan",
        out_shape=[jax.ShapeDtypeStruct(cre.shape, F32)] * 2,
        compiler_params=pltpu.CompilerParams(vmem_limit_bytes=VMEM_LIMIT),
    )(cre, cim, a_re, a_im)


def _gelu_tanh(y):
    return 0.5 * y * (1.0 + jnp.tanh(math.sqrt(2.0 / math.pi) * (y + 0.044715 * (y * y * y))))


def _s5_out_kernel(u_ref, t_ref, sre_ref, sim_ref, wout_ref, z_ref):
    uc = jnp.concatenate([u_ref[ci] for ci in range(S5_GROUP)], axis=-1)
    y = jnp.dot(uc, t_ref[...], preferred_element_type=F32)
    sp = jnp.concatenate([jnp.concatenate([sre_ref[:, bi, :], sim_ref[:, bi, :]], axis=-1)
                          for bi in range(sre_ref.shape[1])], axis=0)
    y = y + jnp.dot(sp.astype(BF16), wout_ref[...], preferred_element_type=F32)
    z = _gelu_tanh(y)
    for co in range(S5_GROUP):
        z_ref[co] = z[:, co * S5_CHUNK:(co + 1) * S5_CHUNK].astype(BF16)


def _s5_out_call(ut, toep, s_re, s_im, wout, b, nc):
    nch = b * nc
    uv = ut.reshape(S5_GROUPS, S5_GROUP, nch, S5_CHUNK)
    n = S5_GROUP * S5_CHUNK
    half = 2 * S5_STATE
    zt = pl.pallas_call(
        _s5_out_kernel, name="s5_out",
        grid=(S5_GROUPS,),
        in_specs=[pl.BlockSpec((None, S5_GROUP, nch, S5_CHUNK), lambda g: (g, 0, 0, 0)),
                  pl.BlockSpec((None, n, n), lambda g: (g, 0, 0)),
                  pl.BlockSpec((nc, b, half), lambda g: (0, 0, g)),
                  pl.BlockSpec((nc, b, half), lambda g: (0, 0, g)),
                  pl.BlockSpec((None, 4 * S5_STATE, n), lambda g: (g, 0, 0))],
        out_specs=pl.BlockSpec((None, S5_GROUP, nch, S5_CHUNK), lambda g: (g, 0, 0, 0)),
        out_shape=jax.ShapeDtypeStruct((S5_GROUPS, S5_GROUP, nch, S5_CHUNK), BF16),
        compiler_params=_cparams(("parallel",)),
    )(uv, toep, s_re, s_im, wout)
    return zt.reshape(S5_WIDTH, nch * S5_CHUNK)


def _ret_tables(s):
    c = RET_CHUNK
    half = RET_QK_DIM // 2
    inv = ROPE_BASE ** (-jnp.arange(half, dtype=F32) / half)
    ang = jnp.arange(s, dtype=F32)[:, None] * inv[None, :]
    cos = jnp.tile(jnp.cos(ang), (1, 2 * RET_HEADS))
    sin = jnp.tile(jnp.sin(ang), (1, 2 * RET_HEADS))
    hidx = jnp.arange(RET_HEADS, dtype=F32)
    lgf = jnp.log1p(-jnp.exp2(-5.0 - hidx))
    lgb = jnp.log1p(-jnp.exp2(-5.5 - hidx))
    j = jnp.arange(c, dtype=F32)
    rel = j[:, None] - j[None, :]
    intra = jnp.exp(jnp.abs(rel)[None] * jnp.where(rel[None] >= 0, lgf[:, None, None], lgb[:, None, None]))
    qk_f = jnp.repeat(lgf, RET_QK_DIM)[None, :]
    qk_b = jnp.repeat(lgb, RET_QK_DIM)[None, :]
    v_f = jnp.repeat(lgf, RET_V_DIM)[None, :]
    v_b = jnp.repeat(lgb, RET_V_DIM)[None, :]
    tabs = dict(
        cos=cos, sin=sin, intra=intra,
        tail_f=jnp.exp((c - 1 - j)[:, None] * qk_f), head_b=jnp.exp(j[:, None] * qk_b),
        decq_f=jnp.exp((j + 1.0)[:, None] * qk_f), decq_b=jnp.exp((c - j)[:, None] * qk_b),
        chunk_f=jnp.exp(c * v_f), chunk_b=jnp.exp(c * v_b),
        bd=(jnp.arange(RET_HEADS * RET_QK_DIM)[:, None] // RET_QK_DIM
            == jnp.arange(RET_HEADS * RET_V_DIM)[None, :] // RET_V_DIM).astype(F32))
    return tabs


def _rot(x_ref, xs_ref, cos_ref, sin_ref):
    return x_ref[...].astype(F32) * cos_ref[...] + xs_ref[...].astype(F32) * sin_ref[...]


def _ret_bwd_kernel(k_ref, ks_ref, v_ref, cos_ref, sin_ref, headb_ref, chunkb_ref, bd_ref, sb_ref, st_ref):
    @pl.when(pl.program_id(1) == 0)
    def _():
        st_ref[...] = jnp.zeros_like(st_ref)

    sb_ref[...] = st_ref[...].astype(BF16)
    kr = _rot(k_ref, ks_ref, cos_ref, sin_ref) * (RET_QK_DIM ** -0.5)
    kwt = jnp.transpose(kr * headb_ref[...]).astype(BF16)
    kv = jnp.dot(kwt, v_ref[...], preferred_element_type=F32)
    st_ref[...] = st_ref[...] * chunkb_ref[...] + kv * bd_ref[...]


def _ret_fwd_kernel(q_ref, qs_ref, k_ref, ks_ref, v_ref, g_ref, cos_ref, sin_ref, sb_ref, intra_ref,
                    tailf_ref, decqf_ref, decqb_ref, chunkf_ref, bd_ref, o_ref, st_ref):
    @pl.when(pl.program_id(1) == 0)
    def _():
        st_ref[...] = jnp.zeros_like(st_ref)

    qr = _rot(q_ref, qs_ref, cos_ref, sin_ref)
    kr = _rot(k_ref, ks_ref, cos_ref, sin_ref) * (RET_QK_DIM ** -0.5)
    qb, kb = qr.astype(BF16), kr.astype(BF16)
    vb = v_ref[...]
    lane_head = lax.broadcasted_iota(jnp.int32, (1, qb.shape[1]), 1) // RET_QK_DIM
    inter = jnp.dot((qr * decqf_ref[...]).astype(BF16), st_ref[...].astype(BF16), preferred_element_type=F32)
    inter = inter + jnp.dot((qr * decqb_ref[...]).astype(BF16), sb_ref[...], preferred_element_type=F32)
    outs = []
    for h in range(RET_HEADS):
        qh = jnp.where(lane_head == h, qb, jnp.zeros_like(qb))
        sc = lax.dot_general(qh, kb, (((1,), (1,)), ((), ())), preferred_element_type=F32) * intra_ref[h]
        vs = slice(h * RET_V_DIM, (h + 1) * RET_V_DIM)
        oh = jnp.dot(sc.astype(BF16), vb[:, vs], preferred_element_type=F32) + inter[:, vs]
        mu = jnp.mean(oh, axis=-1, keepdims=True)
        dd = oh - mu
        var = jnp.mean(dd * dd, axis=-1, keepdims=True)
        outs.append(dd * lax.rsqrt(var + LN_EPS))
    on = jnp.concatenate(outs, axis=-1)
    gg = g_ref[...].astype(F32)
    o_ref[...] = (on * (gg * jax.nn.sigmoid(gg))).astype(BF16)
    kwt = jnp.transpose(kr * tailf_ref[...]).astype(BF16)
    kv = jnp.dot(kwt, vb, preferred_element_type=F32)
    st_ref[...] = st_ref[...] * chunkf_ref[...] + kv * bd_ref[...]


def _retention_call(proj, tabs, b, s):
    c = RET_CHUNK
    nc = s // c
    qkw, vw = RET_HEADS * RET_QK_DIM, RET_HEADS * RET_V_DIM
    pv = proj.reshape(b, s, NP)
    qk_blk, v_blk = (None, c, qkw), (None, c, vw)
    qo, qso, ko, kso = (_SEG[n] * LANE // qkw for n in ("rq", "rqs", "rk", "rks"))
    vo, go = (_SEG[n] * LANE // vw for n in ("rv", "rg"))
    const2 = lambda shape: pl.BlockSpec(shape, lambda bi, i: (0, 0))
    rev = lambda off: (lambda bi, i: (bi, nc - 1 - i, off))
    fwd = lambda off: (lambda bi, i: (bi, i, off))
    sb = pl.pallas_call(
        _ret_bwd_kernel, name="retention_bwd_state",
        grid=(b, nc),
        in_specs=[pl.BlockSpec(qk_blk, rev(ko)), pl.BlockSpec(qk_blk, rev(kso)), pl.BlockSpec(v_blk, rev(vo)),
                  pl.BlockSpec((c, qkw), lambda bi, i: (nc - 1 - i, 0)),
                  pl.BlockSpec((c, qkw), lambda bi, i: (nc - 1 - i, 0)),
                  const2((c, qkw)), const2((1, vw)), const2((qkw, vw))],
        out_specs=pl.BlockSpec((None, None, qkw, vw), lambda bi, i: (bi, nc - 1 - i, 0, 0)),
        out_shape=jax.ShapeDtypeStruct((b, nc, qkw, vw), BF16),
        scratch_shapes=[pltpu.VMEM((qkw, vw), F32)],
        compiler_params=_cparams(("parallel", "arbitrary")),
    )(pv, pv, pv, tabs["cos"], tabs["sin"], tabs["head_b"], tabs["chunk_b"], tabs["bd"])
    yc = pl.pallas_call(
        _ret_fwd_kernel, name="retention_fwd",
        grid=(b, nc),
        in_specs=[pl.BlockSpec(qk_blk, fwd(qo)), pl.BlockSpec(qk_blk, fwd(qso)),
                  pl.BlockSpec(qk_blk, fwd(ko)), pl.BlockSpec(qk_blk, fwd(kso)),
                  pl.BlockSpec(v_blk, fwd(vo)), pl.BlockSpec(v_blk, fwd(go)),
                  pl.BlockSpec((c, qkw), lambda bi, i: (i, 0)), pl.BlockSpec((c, qkw), lambda bi, i: (i, 0)),
                  pl.BlockSpec((None, None, qkw, vw), lambda bi, i: (bi, i, 0, 0)),
                  pl.BlockSpec((RET_HEADS, c, c), lambda bi, i: (0, 0, 0)),
                  const2((c, qkw)), const2((c, qkw)), const2((c, qkw)), const2((1, vw)), const2((qkw, vw))],
        out_specs=pl.BlockSpec((None, c, vw), lambda bi, i: (bi, i, 0)),
        out_shape=jax.ShapeDtypeStruct((b, s, vw), BF16),
        scratch_shapes=[pltpu.VMEM((qkw, vw), F32)],
        compiler_params=_cparams(("parallel", "arbitrary")),
    )(pv, pv, pv, pv, pv, pv, tabs["cos"], tabs["sin"], sb, tabs["intra"],
      tabs["tail_f"], tabs["decq_f"], tabs["decq_b"], tabs["chunk_f"], tabs["bd"])
    return yc.reshape(b * s, vw)


def _mm_kernel(x_ref, w_ref, o_ref):
    o_ref[...] = jnp.dot(x_ref[...].astype(BF16), w_ref[...], preferred_element_type=F32).astype(o_ref.dtype)


def _mem_kv_call(mem2d, w_kv):
    m, n = mem2d.shape[0], w_kv.shape[1]
    tm = 512
    return pl.pallas_call(
        _mm_kernel, name="mem_kv_proj",
        grid=(m // tm,),
        in_specs=[pl.BlockSpec((tm, D_MODEL), lambda i: (i, 0)),
                  pl.BlockSpec((D_MODEL, n), lambda i: (0, 0))],
        out_specs=pl.BlockSpec((tm, n), lambda i: (i, 0)),
        out_shape=jax.ShapeDtypeStruct((m, n), BF16),
        compiler_params=_cparams(("parallel",)),
    )(mem2d, w_kv)


def _mem_attn_kernel(q_ref, k_ref, v_ref, o_ref):
    q = q_ref[...]
    outs = []
    for h in range(MEM_HEADS):
        hs = slice(h * MEM_HEAD_PAD, (h + 1) * MEM_HEAD_PAD)
        s = lax.dot_general(q[:, hs], k_ref[:, hs], (((1,), (1,)), ((), ())),
                            preferred_element_type=F32) * (MEM_HEAD_DIM ** -0.5)
        m = jnp.max(s, axis=-1, keepdims=True)
        e = jnp.exp(s - m)
        p = e / jnp.sum(e, axis=-1, keepdims=True)
        outs.append(jnp.dot(p.astype(BF16), v_ref[:, hs], preferred_element_type=F32))
    o_ref[...] = jnp.concatenate(outs, axis=-1).astype(BF16)


def _mem_attn_call(proj, kv, b, s):
    tq = 256
    w = MEM_HEADS * MEM_HEAD_PAD
    pv = proj.reshape(b, s, NP)
    kvv = kv.reshape(b, N_MEM, 2 * w)
    qo = _SEG["mq"] * LANE // w
    yd = pl.pallas_call(
        _mem_attn_kernel, name="mem_attn",
        grid=(b, s // tq),
        in_specs=[pl.BlockSpec((None, tq, w), lambda bi, i: (bi, i, qo)),
                  pl.BlockSpec((None, N_MEM, w), lambda bi, i: (bi, 0, 0)),
                  pl.BlockSpec((None, N_MEM, w), lambda bi, i: (bi, 0, 1))],
        out_specs=pl.BlockSpec((None, tq, w), lambda bi, i: (bi, i, 0)),
        out_shape=jax.ShapeDtypeStruct((b, s, w), BF16),
        compiler_params=_cparams(("parallel", "parallel")),
    )(pv, kvv, kvv)
    return yd.reshape(b * s, w)


def _merge_kernel(x_ref, ao0_ref, ao1_ref, ao2_ref, al0_ref, al1_ref, al2_ref, zt_ref, yc_ref, yd_ref, gl_ref,
                  wglut_ref, bglu_ref, wba_ref, wbb_ref, wbc_ref, wbd_ref, wout_ref, g1_ref, b1_ref,
                  rw_ref, rb_ref,
                  x1_ref, route_ref, counts_ref, carry_ref):
    tm = x_ref.shape[0]

    @pl.when(pl.program_id(0) == 0)
    def _():
        carry_ref[...] = jnp.zeros_like(carry_ref)

    lses = [al0_ref[...], al1_ref[...], al2_ref[...]]
    mx = jnp.maximum(jnp.maximum(lses[0], lses[1]), lses[2])
    es = [jnp.exp(l - mx) for l in lses]
    tot = es[0] + es[1] + es[2]
    ya = jnp.concatenate([r[...].astype(F32) * (e / tot) for r, e in zip((ao0_ref, ao1_ref, ao2_ref), es)],
                         axis=-1).astype(BF16)
    zt = zt_ref[...]
    glu = jnp.dot(wglut_ref[...], zt, preferred_element_type=F32) + bglu_ref[...]
    ybt = zt.astype(F32) * jax.nn.sigmoid(glu)
    yb = jnp.transpose(ybt).astype(BF16)
    merged = jnp.zeros((tm, D_MODEL), F32)
    for n, (y, w_ref) in enumerate(((ya, wba_ref), (yb, wbb_ref), (yc_ref[...], wbc_ref), (yd_ref[...], wbd_ref))):
        bo = jnp.dot(y, w_ref[...], preferred_element_type=F32)
        gate = jax.nn.sigmoid(gl_ref[:, n * D_MODEL:(n + 1) * D_MODEL].astype(F32))
        merged = merged + gate * bo
    h = DEEPNORM_ALPHA * x_ref[...] + jnp.dot(merged.astype(BF16), wout_ref[...], preferred_element_type=F32)
    x1 = _ln(h, g1_ref[...], b1_ref[...])
    x1_ref[...] = x1

    logits = jnp.dot(x1, rw_ref[...], precision=HIGHEST, preferred_element_type=F32) + rb_ref[...]
    lane = lax.broadcasted_iota(jnp.int32, (tm, LANE), 1)
    work = logits
    sel, vals, hots = [], [], []
    for _ in range(TOP_K):
        mk = jnp.max(work, axis=-1, keepdims=True)
        ik = jnp.min(jnp.where(work == mk, lane, LANE), axis=-1, keepdims=True)
        hot = lane == ik
        work = jnp.where(hot, -jnp.inf, work)
        sel.append(ik)
        vals.append(mk)
        hots.append(hot)
    ex = [jnp.exp(v - vals[0]) for v in vals]
    den = ex[0] + ex[1] + ex[2] + ex[3]
    chosen = (hots[0] | hots[1] | hots[2] | hots[3]).astype(F32)
    r_i = lax.broadcasted_iota(jnp.int32, (tm, tm), 0)
    c_i = lax.broadcasted_iota(jnp.int32, (tm, tm), 1)
    strict_lower = (c_i < r_i).astype(BF16)
    before = jnp.dot(strict_lower, chosen.astype(BF16), preferred_element_type=F32) + carry_ref[0:1, :]
    route = jnp.zeros((tm, LANE), F32)
    for k in range(TOP_K):
        rank = jnp.sum(jnp.where(hots[k], before, 0.0), axis=-1, keepdims=True)
        route = jnp.where(lane == k, sel[k].astype(F32), route)
        route = jnp.where(lane == TOP_K + k, ex[k] / den, route)
        route = jnp.where(lane == 2 * TOP_K + k, rank, route)
    route_ref[...] = route
    new_carry = carry_ref[0:1, :] + jnp.sum(chosen, axis=0, keepdims=True)
    carry_ref[...] = jnp.broadcast_to(new_carry, carry_ref.shape)
    counts_ref[...] = jnp.broadcast_to(new_carry, counts_ref.shape)


def _merge_call(x, aos, alses, zt, yc, yd, proj, lw):
    t = x.shape[0]
    tm = 256
    gw = A_HEADS_PER_GROUP * A_HEAD_DIM
    gate_blk = 4 * D_MODEL
    row = lambda w: pl.BlockSpec((tm, w), lambda i: (i, 0))
    full = lambda a: pl.BlockSpec(a.shape, lambda i: (0,) * a.ndim)
    consts = (lw["w_glu_t"], lw["b_glu"], lw["wb_a"], lw["wb_b"], lw["wb_c"], lw["wb_d"], lw["w_out"],
              lw["ln1_g"], lw["ln1_b"], lw["router_w"], lw["router_b"])
    return pl.pallas_call(
        _merge_kernel, name="branch_merge_router",
        grid=(t // tm,),
        in_specs=[row(D_MODEL)] + [row(gw)] * 6
                 + [pl.BlockSpec((S5_WIDTH, tm), lambda i: (0, i)),
                  row(RET_HEADS * RET_V_DIM), row(MEM_HEADS * MEM_HEAD_PAD),
                  pl.BlockSpec((tm, gate_blk), lambda i: (i, _SEG["gate"] * LANE // gate_blk))]
                 + [full(a) for a in consts],
        out_specs=[row(D_MODEL), row(LANE), pl.BlockSpec((8, LANE), lambda i: (0, 0))],
        out_shape=[jax.ShapeDtypeStruct((t, D_MODEL), F32),
                   jax.ShapeDtypeStruct((t, LANE), F32),
                   jax.ShapeDtypeStruct((8, LANE), F32)],
        scratch_shapes=[pltpu.VMEM((8, LANE), F32)],
        compiler_params=_cparams(("arbitrary",)),
    )(x, *aos, *alses, zt, yc, yd, proj, *consts)


def _moe_gather_kernel(tok_ref, x_hbm, o_ref, sem):
    rows = o_ref.shape[0]

    def row_copy(r):
        return pltpu.make_async_copy(x_hbm.at[pl.ds(tok_ref[0, 0, r], 1), :], o_ref.at[pl.ds(r, 1), :], sem)

    def start(r, c):
        row_copy(r).start()
        return c

    lax.fori_loop(0, rows, start, 0)

    def wait(r, c):
        row_copy(r).wait()
        return c

    lax.fori_loop(0, rows, wait, 0)


def _moe_gather_call(x1, row_tok, n_blocks):
    bm = MOE_BLOCK_ROWS
    return pl.pallas_call(
        _moe_gather_kernel, name="moe_dispatch_gather",
        grid=(n_blocks,),
        in_specs=[pl.BlockSpec((1, 1, bm), lambda i: (i, 0, 0), memory_space=pltpu.SMEM),
                  pl.BlockSpec(memory_space=pl.ANY)],
        out_specs=pl.BlockSpec((bm, D_MODEL), lambda i: (i, 0)),
        out_shape=jax.ShapeDtypeStruct((n_blocks * bm, D_MODEL), F32),
        scratch_shapes=[pltpu.SemaphoreType.DMA(())],
        compiler_params=_cparams(("arbitrary",)),
    )(row_tok.reshape(n_blocks, 1, bm), x1)


def _moe_expert_kernel(be_ref, nu_ref, x_ref, w1_ref, b1_ref, w2_ref, b2_ref, o_ref):
    @pl.when(pl.program_id(0) < nu_ref[0])
    def _():
        h = jnp.dot(x_ref[...].astype(BF16), w1_ref[...], preferred_element_type=F32) + b1_ref[...]
        gate = jnp.minimum(h[:, :D_FF], SWIGLU_LIMIT)
        up = jnp.clip(h[:, D_FF:], -SWIGLU_LIMIT, SWIGLU_LIMIT)
        hid = gate * jax.nn.sigmoid(SWIGLU_ALPHA * gate) * (up + 1.0)
        o_ref[...] = jnp.dot(hid.astype(BF16), w2_ref[...], preferred_element_type=F32) + b2_ref[...]

    @pl.when(pl.program_id(0) >= nu_ref[0])
    def _():
        o_ref[...] = jnp.zeros_like(o_ref)


def _moe_expert_call(xg, block_expert, n_used, w1, b1, w2, b2):
    bm = MOE_BLOCK_ROWS
    n_blocks = xg.shape[0] // bm
    grid_spec = pltpu.PrefetchScalarGridSpec(
        num_scalar_prefetch=2, grid=(n_blocks,),
        in_specs=[pl.BlockSpec((bm, D_MODEL), lambda i, be, nu: (i, 0)),
                  pl.BlockSpec((None, D_MODEL, 2 * D_FF), lambda i, be, nu: (be[i], 0, 0)),
                  pl.BlockSpec((None, 1, 2 * D_FF), lambda i, be, nu: (be[i], 0, 0)),
                  pl.BlockSpec((None, D_FF, D_MODEL), lambda i, be, nu: (be[i], 0, 0)),
                  pl.BlockSpec((None, 1, D_MODEL), lambda i, be, nu: (be[i], 0, 0))],
        out_specs=pl.BlockSpec((bm, D_MODEL), lambda i, be, nu: (i, 0)))
    return pl.pallas_call(
        _moe_expert_kernel, name="moe_experts",
        grid_spec=grid_spec,
        out_shape=jax.ShapeDtypeStruct((n_blocks * bm, D_MODEL), F32),
        compiler_params=_cparams(("arbitrary",)),
    )(block_expert, n_used, xg, w1, b1, w2, b2)


def _moe_combine_kernel(dest_ref, x_ref, route_ref, y_hbm, g_ref, b_ref, o_ref, buf_ref, sem):
    tm = x_ref.shape[0]

    def row_copy(j):
        k, r = j // tm, j % tm
        return pltpu.make_async_copy(y_hbm.at[pl.ds(dest_ref[0, 0, j], 1), :], buf_ref.at[k, pl.ds(r, 1), :], sem)

    def start(j, c):
        row_copy(j).start()
        return c

    lax.fori_loop(0, TOP_K * tm, start, 0)

    def wait(j, c):
        row_copy(j).wait()
        return c

    lax.fori_loop(0, TOP_K * tm, wait, 0)
    route = route_ref[...]
    moe = jnp.zeros((tm, D_MODEL), F32)
    for k in range(TOP_K):
        moe = moe + route[:, TOP_K + k:TOP_K + k + 1] * buf_ref[k]
    o_ref[...] = _ln(DEEPNORM_ALPHA * x_ref[...] + moe, g_ref[...], b_ref[...])


def _moe_combine_call(x1, route, dest_km, yb, g2, b2):
    t = x1.shape[0]
    tm = 128
    return pl.pallas_call(
        _moe_combine_kernel, name="moe_combine_ln",
        grid=(t // tm,),
        in_specs=[pl.BlockSpec((1, 1, TOP_K * tm), lambda i: (i, 0, 0), memory_space=pltpu.SMEM),
                  pl.BlockSpec((tm, D_MODEL), lambda i: (i, 0)),
                  pl.BlockSpec((tm, LANE), lambda i: (i, 0)),
                  pl.BlockSpec(memory_space=pl.ANY),
                  pl.BlockSpec((1, D_MODEL), lambda i: (0, 0)),
                  pl.BlockSpec((1, D_MODEL), lambda i: (0, 0))],
        out_specs=pl.BlockSpec((tm, D_MODEL), lambda i: (i, 0)),
        out_shape=jax.ShapeDtypeStruct((t, D_MODEL), F32),
        scratch_shapes=[pltpu.VMEM((TOP_K, tm, D_MODEL), F32), pltpu.SemaphoreType.DMA(())],
        compiler_params=_cparams(("arbitrary",)),
    )(dest_km, x1, route, yb, g2, b2)


def _moe(x1, route, counts, lw):
    t = x1.shape[0]
    bm = MOE_BLOCK_ROWS
    n_assign = t * TOP_K
    n_blocks = n_assign // bm + N_EXPERTS
    idx = route[:, :TOP_K].astype(jnp.int32)
    rank = route[:, 2 * TOP_K:3 * TOP_K].astype(jnp.int32)
    cnt = counts[0, :N_EXPERTS].astype(jnp.int32)
    padded = (cnt + bm - 1) // bm * bm
    pad_ends = jnp.cumsum(padded)
    pad_starts = pad_ends - padded
    dest = pad_starts[idx] + rank
    tok = jnp.broadcast_to(jnp.arange(t, dtype=jnp.int32)[:, None], (t, TOP_K))
    row_tok = jnp.zeros((n_blocks * bm,), jnp.int32).at[dest.reshape(-1)].set(tok.reshape(-1))
    block_expert = jnp.minimum(
        jnp.searchsorted(pad_ends, jnp.arange(n_blocks, dtype=jnp.int32) * bm, side="right"),
        N_EXPERTS - 1).astype(jnp.int32)
    n_used = (pad_ends[-1:] // bm).astype(jnp.int32)
    xg = _moe_gather_call(x1, row_tok, n_blocks)
    yb = _moe_expert_call(xg, block_expert, n_used, lw["moe_w1"], lw["moe_b1"], lw["moe_w2"], lw["moe_b2"])
    tmc = 128
    dest_km = jnp.transpose(dest.reshape(t // tmc, tmc, TOP_K), (0, 2, 1)).reshape(t // tmc, 1, TOP_K * tmc)
    return _moe_combine_call(x1, route, dest_km, yb, lw["ln2_g"], lw["ln2_b"])


def _rotate_half_cols(w):
    half = RET_QK_DIM // 2
    w4 = w.reshape(w.shape[0], RET_HEADS, 2, half)
    return jnp.concatenate([-w4[:, :, 1:], w4[:, :, :1]], axis=2).reshape(w.shape)


def _pad_heads(w, axis):
    shp = list(w.shape)
    shp[axis:axis + 1] = [MEM_HEADS, MEM_HEAD_DIM]
    w = w.reshape(shp)
    pad = [(0, 0)] * w.ndim
    pad[axis + 1] = (0, MEM_HEAD_PAD - MEM_HEAD_DIM)
    w = jnp.pad(w, pad)
    shp[axis:axis + 2] = [MEM_HEADS * MEM_HEAD_PAD]
    return w.reshape(shp)


def _layer_weights(l, p):
    w_in = p["w_in"][l]
    aw, sw, qkw, vw, mw = 768, S5_WIDTH, RET_HEADS * RET_QK_DIM, RET_HEADS * RET_V_DIM, 768
    offs = [0]
    for wdt in (aw, aw, aw, sw, qkw, qkw, vw, vw, mw, 4 * D_MODEL):
        offs.append(offs[-1] + wdt)
    aq, ak, av, su, rq, rk, rv, rg, mq, gate = (w_in[:, offs[i]:offs[i + 1]] for i in range(10))
    seg = dict(gate=gate, mq=_pad_heads(mq, 1), aq=aq, ak=ak, av=av,
               pad=jnp.zeros((D_MODEL, 2 * LANE), F32), rv=rv, rg=rg,
               rq=rq, rqs=_rotate_half_cols(rq), rk=rk, rks=_rotate_half_cols(rk))
    order = sorted(_SEG, key=_SEG.get)
    w_p = jnp.concatenate([seg[n] for n in order], axis=1).astype(BF16)
    assert w_p.shape[1] == NP
    wkv = p["w_mem_kv"][l]
    w_kv = jnp.concatenate([_pad_heads(wkv[:, :768], 1), _pad_heads(wkv[:, 768:], 1)], axis=1).astype(BF16)
    wb = p["w_branch"][l]
    rw = jnp.pad(p["router_w"][l].astype(F32), ((0, 0), (0, LANE - N_EXPERTS)))
    rb = jnp.pad(p["router_b"][l].astype(F32), (0, LANE - N_EXPERTS), constant_values=NEG_INF)
    return dict(
        w_p=w_p, w_su_t=jnp.transpose(su).astype(BF16), w_kv=w_kv,
        w_glu_t=jnp.transpose(p["s5_w_glu"][l]).astype(BF16), b_glu=p["s5_b_glu"][l].astype(F32).reshape(-1, 1),
        wb_a=wb[0].astype(BF16), wb_b=wb[1].astype(BF16), wb_c=wb[2].astype(BF16),
        wb_d=_pad_heads(wb[3], 0).astype(BF16), w_out=p["w_out"][l].astype(BF16),
        ln1_g=p["ln1_g"][l].reshape(1, -1), ln1_b=p["ln1_b"][l].reshape(1, -1),
        router_w=rw, router_b=rb.reshape(1, -1),
        moe_w1=p["moe_w1"][l].astype(BF16), moe_b1=p["moe_b1"][l].astype(F32).reshape(N_EXPERTS, 1, -1),
        moe_w2=p["moe_w2"][l].astype(BF16), moe_b2=p["moe_b2"][l].astype(F32).reshape(N_EXPERTS, 1, -1),
        ln2_g=p["ln2_g"][l].reshape(1, -1), ln2_b=p["ln2_b"][l].reshape(1, -1),
        s5=_s5_tables(p["s5_lam_re"][l], p["s5_lam_im"][l], p["s5_log_step"][l], p["s5_b_re"][l],
                      p["s5_b_im"][l], p["s5_c_re"][l], p["s5_c_im"][l], p["s5_d"][l]))


def _trunk_layer(x, mem2d, b, s, lw, attn_bias, ret_tabs):
    proj, ut = _proj_call(x, lw["w_p"], lw["w_su_t"])
    outs = [_attn_call(proj, attn_bias[g], b, s, g, dil) for g, (_, dil) in enumerate(A_GROUPS)]
    aos, alses = [o for o, _ in outs], [l for _, l in outs]
    toep, win, wout, a_re, a_im = lw["s5"]
    nc = s // S5_CHUNK
    c_re, c_im = _s5_contrib_call(ut, win, b, nc)
    s_re, s_im = _s5_scan_call(c_re, c_im, a_re, a_im)
    zt = _s5_out_call(ut, toep, s_re, s_im, wout, b, nc)
    yc = _retention_call(proj, ret_tabs, b, s)
    kv = _mem_kv_call(mem2d, lw["w_kv"])
    yd = _mem_attn_call(proj, kv, b, s)
    x1, route, counts = _merge_call(x, aos, alses, zt, yc, yd, proj, lw)
    return _moe(x1, route, counts, lw)


def kernel(x_prompt, x_sample, mem_prompt, mem_sample, ln_in_g, ln_in_b, rel_bias, w_in, s5_lam_re, s5_lam_im, s5_log_step, s5_b_re, s5_b_im, s5_c_re, s5_c_im, s5_d, s5_w_glu, s5_b_glu, w_mem_kv, w_branch, w_out, ln1_g, ln1_b, router_w, router_b, moe_w1, moe_b1, moe_w2, moe_b2, ln2_g, ln2_b):
    p = dict(w_in=w_in, s5_lam_re=s5_lam_re, s5_lam_im=s5_lam_im, s5_log_step=s5_log_step, s5_b_re=s5_b_re,
             s5_b_im=s5_b_im, s5_c_re=s5_c_re, s5_c_im=s5_c_im, s5_d=s5_d, s5_w_glu=s5_w_glu, s5_b_glu=s5_b_glu,
             w_mem_kv=w_mem_kv, w_branch=w_branch, w_out=w_out, ln1_g=ln1_g, ln1_b=ln1_b, router_w=router_w,
             router_b=router_b, moe_w1=moe_w1, moe_b1=moe_b1, moe_w2=moe_w2, moe_b2=moe_b2, ln2_g=ln2_g,
             ln2_b=ln2_b)
    attn_bias = [_attn_bias_table(rel_bias, g, dil) for g, (_, dil) in enumerate(A_GROUPS)]
    trunks = []
    for x, mem in ((x_prompt, mem_prompt), (x_sample, mem_sample)):
        b, s, _ = x.shape
        trunks.append(dict(x=_layer_norm_call(x.reshape(b * s, D_MODEL), ln_in_g, ln_in_b),
                           mem=mem.reshape(b * N_MEM, D_MODEL), b=b, s=s, tabs=_ret_tables(s)))
    for l in range(DEPTH):
        lw = _layer_weights(l, p)
        for tr in trunks:
            tr["x"] = _trunk_layer(tr["x"], tr["mem"], tr["b"], tr["s"], lw, attn_bias, tr["tabs"])
    return tuple(tr["x"].reshape(tr["b"], tr["s"], D_MODEL) for tr in trunks)
```

```python
import functools
import math

import jax
import jax.numpy as jnp
from jax import lax
from jax.experimental import pallas as pl
from jax.experimental.pallas import tpu as pltpu
from jax.experimental.pallas import tpu_sc as plsc

F32 = jnp.float32
BF16 = jnp.bfloat16
HIGHEST = lax.Precision.HIGHEST

D_MODEL = 1024
DEPTH = 4
N_MEM = 256
A_GROUPS = ((128, 1), (512, 4), (2048, 16))
A_HEADS_PER_GROUP = 4
A_HEADS = 12
A_HEAD_DIM = 64
A_RADIUS = 64
REL_BUCKETS = 32
REL_MAX_DIST = 1024
S5_GROUP = 16
S5_WIDTH = 768
S5_GROUPS = 48
S5_STATE = 64
RET_HEADS = 6
RET_QK_DIM = 64
RET_V_DIM = 128
RET_CHUNK = 128
ROPE_BASE = 10000.0
MEM_HEADS = 4
MEM_HEAD_DIM = 192
MEM_HEAD_PAD = 256
N_EXPERTS = 32
TOP_K = 4
D_FF = 1024
SWIGLU_LIMIT = 7.0
SWIGLU_ALPHA = 1.702
LN_EPS = 1e-5
DEEPNORM_ALPHA = (2 * DEPTH) ** 0.25
NEG_INF = -1e30

LANE = 128
VMEM_LIMIT = 56 * 1024 * 1024
S5_CHUNK = 128
MOE_BLOCK_ROWS = 512
SC_GATHER_WINDOW = 32

_SEG = {}
_off = 0
for _name, _width, _align in (
        ("gate", 32, 8), ("mq", 8, 8), ("aq", 2, 2), ("ak", 2, 2), ("av", 2, 2), ("pad", 2, 2),
        ("rv", 6, 6), ("rg", 6, 6), ("rq", 3, 3), ("rqs", 3, 3), ("rk", 3, 3), ("rks", 3, 3)):
    assert _off % _align == 0, (_name, _off)
    _SEG[_name] = _off
    _off += _width
NP_UNITS = _off
NP = NP_UNITS * LANE
assert NP % 256 == 0
GW = A_HEADS_PER_GROUP * A_HEAD_DIM
ND = 2 * 3 * GW


def _cparams(sem, vmem=VMEM_LIMIT):
    return pltpu.CompilerParams(dimension_semantics=sem, vmem_limit_bytes=vmem)


def _ln(h, g, b):
    mu = jnp.mean(h, axis=-1, keepdims=True)
    d = h - mu
    var = jnp.mean(d * d, axis=-1, keepdims=True)
    return d * lax.rsqrt(var + LN_EPS) * g + b


def _ln_kernel(x_ref, g_ref, b_ref, o_ref):
    o_ref[...] = _ln(x_ref[...], g_ref[...], b_ref[...])


def _layer_norm_call(x, g, b):
    t = x.shape[0]
    tm = 512
    return pl.pallas_call(
        _ln_kernel, name="input_ln",
        grid=(t // tm,),
        in_specs=[pl.BlockSpec((tm, D_MODEL), lambda i: (i, 0)),
                  pl.BlockSpec((1, D_MODEL), lambda i: (0, 0)),
                  pl.BlockSpec((1, D_MODEL), lambda i: (0, 0))],
        out_specs=pl.BlockSpec((tm, D_MODEL), lambda i: (i, 0)),
        out_shape=jax.ShapeDtypeStruct((t, D_MODEL), F32),
        compiler_params=_cparams(("parallel",)),
    )(x, g.reshape(1, -1), b.reshape(1, -1))


def _proj_kernel(x_ref, w_ref, wsut_ref, wd_ref, p_ref, ut_ref, pd_ref, xb_ref):
    @pl.when(pl.program_id(1) == 0)
    def _():
        xb = x_ref[...].astype(BF16)
        xb_ref[...] = xb
        ut_ref[...] = lax.dot_general(wsut_ref[...], xb, (((1,), (1,)), ((), ())),
                                      preferred_element_type=F32).astype(BF16)
        pd = jnp.dot(xb, wd_ref[...], preferred_element_type=F32)
        for c in range(ND // LANE):
            pd_ref[c] = pd[:, c * LANE:(c + 1) * LANE]

    p_ref[...] = jnp.dot(xb_ref[...], w_ref[...], preferred_element_type=F32).astype(BF16)


def _proj_call(x, w_p, w_su_t, w_d):
    t = x.shape[0]
    tm, tn = 512, NP // 6
    return pl.pallas_call(
        _proj_kernel, name="in_proj",
        grid=(t // tm, NP // tn),
        in_specs=[pl.BlockSpec((tm, D_MODEL), lambda i, j: (i, 0)),
                  pl.BlockSpec((D_MODEL, tn), lambda i, j: (0, j)),
                  pl.BlockSpec((S5_WIDTH, D_MODEL), lambda i, j: (0, 0)),
                  pl.BlockSpec((D_MODEL, ND), lambda i, j: (0, 0))],
        out_specs=[pl.BlockSpec((tm, tn), lambda i, j: (i, j)),
                   pl.BlockSpec((S5_WIDTH, tm), lambda i, j: (0, i)),
                   pl.BlockSpec((ND // LANE, tm, LANE), lambda i, j: (0, i, 0))],
        out_shape=[jax.ShapeDtypeStruct((t, NP), BF16),
                   jax.ShapeDtypeStruct((S5_WIDTH, t), BF16),
                   jax.ShapeDtypeStruct((ND // LANE, t, LANE), F32)],
        scratch_shapes=[pltpu.VMEM((tm, D_MODEL), BF16)],
        compiler_params=_cparams(("parallel", "arbitrary")),
    )(x, w_p, w_su_t, w_d)


ATTN_TQ = 2 * A_RADIUS


def _banded_heads(q, kwin, vwin, bias_ref, first_key, sub_len):
    tq, nk = q.shape[0], kwin.shape[0]
    kidx = first_key + lax.broadcasted_iota(jnp.int32, (tq, nk), 1)
    valid = (kidx >= 0) & (kidx < sub_len)
    lane_head = lax.broadcasted_iota(jnp.int32, (1, q.shape[1]), 1) // A_HEAD_DIM
    scale = jnp.asarray(A_HEAD_DIM ** -0.5, BF16)
    heads = range(A_HEADS_PER_GROUP)
    qs = [jnp.where(lane_head == h, q, jnp.zeros_like(q)) * scale for h in heads]
    ss = [lax.dot_general(qh, kwin, (((1,), (1,)), ((), ())), preferred_element_type=F32) for qh in qs]
    ss = [jnp.where(valid, s + bias_ref[h], NEG_INF) for h, s in zip(heads, ss)]
    ms = [jnp.max(s, axis=-1, keepdims=True) for s in ss]
    ps = [jnp.exp(s - m) for s, m in zip(ss, ms)]
    dens = [jnp.sum(p, axis=-1, keepdims=True) for p in ps]
    ohs = [jnp.dot(p.astype(BF16), vwin, preferred_element_type=F32) for p in ps]
    acc = jnp.zeros((tq, q.shape[1]), F32)
    lse_acc = jnp.zeros((tq, q.shape[1]), F32)
    for h in heads:
        hm = lane_head == h
        acc = jnp.where(hm, ohs[h] / dens[h], acc)
        lse_acc = jnp.where(hm, ms[h] + jnp.log(dens[h]), lse_acc)
    return acc, lse_acc


def _attn_kernel(q_ref, kp_ref, kc_ref, kn_ref, vp_ref, vc_ref, vn_ref, bias_ref, o_ref, lse_ref, *, sub_len):
    tq, half = ATTN_TQ, A_RADIUS
    kwin = jnp.concatenate([kp_ref[tq - half:, :], kc_ref[...], kn_ref[:half, :]], axis=0)
    vwin = jnp.concatenate([vp_ref[tq - half:, :], vc_ref[...], vn_ref[:half, :]], axis=0)
    acc, lse = _banded_heads(q_ref[...], kwin, vwin, bias_ref, pl.program_id(1) * tq - half, sub_len)
    o_ref[...] = acc
    lse_ref[...] = lse


def _attn_call(proj, bias, b, s):
    tq = ATTN_TQ
    nqb = s // tq
    pv = proj.reshape(b, s, NP)
    qo, ko, vo = (_SEG[n] * LANE // GW for n in ("aq", "ak", "av"))
    blk = (None, tq, GW)
    cur = lambda off: (lambda bi, qi: (bi, qi, off))
    prv = lambda off: (lambda bi, qi: (bi, jnp.maximum(qi - 1, 0), off))
    nxt = lambda off: (lambda bi, qi: (bi, jnp.minimum(qi + 1, nqb - 1), off))
    out_map = lambda bi, qi: (bi, qi, 0)
    o, lse = pl.pallas_call(
        functools.partial(_attn_kernel, sub_len=s), name="window_attn_g0",
        grid=(b, nqb),
        in_specs=[pl.BlockSpec(blk, cur(qo)),
                  pl.BlockSpec(blk, prv(ko)), pl.BlockSpec(blk, cur(ko)), pl.BlockSpec(blk, nxt(ko)),
                  pl.BlockSpec(blk, prv(vo)), pl.BlockSpec(blk, cur(vo)), pl.BlockSpec(blk, nxt(vo)),
                  pl.BlockSpec((A_HEADS_PER_GROUP, tq, 2 * tq), lambda bi, qi: (0, 0, 0))],
        out_specs=[pl.BlockSpec(blk, out_map), pl.BlockSpec(blk, out_map)],
        out_shape=[jax.ShapeDtypeStruct((b, s, GW), F32), jax.ShapeDtypeStruct((b, s, GW), F32)],
        compiler_params=_cparams(("parallel", "parallel")),
    )(pv, pv, pv, pv, pv, pv, pv, bias)
    return o.reshape(b * s, GW), lse.reshape(b * s, GW)


def _dil_attn_kernel(q_ref, kp_ref, kc_ref, kn_ref, vp_ref, vc_ref, vn_ref, bias_ref, o_ref, lse_ref, *,
                     dil, sub_len):
    tq, half = ATTN_TQ, A_RADIUS
    first_key = pl.program_id(1) * tq - half

    def residue(r, carry):
        def rows(ref, n):
            return jnp.concatenate([ref[c, pl.ds(r, n, stride=dil), :] for c in range(GW // LANE)], axis=-1)

        q = rows(q_ref, tq).astype(BF16)
        kwin = jnp.concatenate([rows(kp_ref, half), rows(kc_ref, tq), rows(kn_ref, half)], axis=0).astype(BF16)
        vwin = jnp.concatenate([rows(vp_ref, half), rows(vc_ref, tq), rows(vn_ref, half)], axis=0).astype(BF16)
        acc, lse = _banded_heads(q, kwin, vwin, bias_ref, first_key, sub_len)
        for c in range(GW // LANE):
            o_ref[c, pl.ds(r, tq, stride=dil), :] = acc[:, c * LANE:(c + 1) * LANE]
            lse_ref[c, pl.ds(r, tq, stride=dil), :] = lse[:, c * LANE:(c + 1) * LANE]
        return carry

    lax.fori_loop(0, dil, residue, 0)


def _dil_attn_call(pd, bias, b, s, g, dil):
    tq = ATTN_TQ
    rows = tq * dil
    nblk = s // rows
    nslab = GW // LANE
    pv = pd.reshape(ND // LANE, b, s, LANE)
    qo, ko, vo = (3 * (g - 1) + j for j in range(3))
    blk, hblk = (nslab, None, rows, LANE), (nslab, None, rows // 2, LANE)
    cur = lambda off: (lambda bi, i: (off, bi, i, 0))
    prv = lambda off: (lambda bi, i: (off, bi, jnp.maximum(2 * i - 1, 0), 0))
    nxt = lambda off: (lambda bi, i: (off, bi, jnp.minimum(2 * i + 2, 2 * nblk - 1), 0))
    out_map = lambda bi, i: (0, bi, i, 0)
    o, lse = pl.pallas_call(
        functools.partial(_dil_attn_kernel, dil=dil, sub_len=s // dil), name=f"dilated_attn_g{g}",
        grid=(b, nblk),
        in_specs=[pl.BlockSpec(blk, cur(qo)),
                  pl.BlockSpec(hblk, prv(ko)), pl.BlockSpec(blk, cur(ko)), pl.BlockSpec(hblk, nxt(ko)),
                  pl.BlockSpec(hblk, prv(vo)), pl.BlockSpec(blk, cur(vo)), pl.BlockSpec(hblk, nxt(vo)),
                  pl.BlockSpec((A_HEADS_PER_GROUP, tq, 2 * tq), lambda bi, i: (0, 0, 0))],
        out_specs=[pl.BlockSpec(blk, out_map), pl.BlockSpec(blk, out_map)],
        out_shape=[jax.ShapeDtypeStruct((nslab, b, s, LANE), F32)] * 2,
        compiler_params=_cparams(("parallel", "parallel")),
    )(pv, pv, pv, pv, pv, pv, pv, bias)
    return o.reshape(nslab, b * s, LANE), lse.reshape(nslab, b * s, LANE)


def _t5_bucket(rel):
    nb = REL_BUCKETS // 2
    max_exact = nb // 2
    ret = jnp.where(rel > 0, nb, 0)
    n = jnp.abs(rel)
    nf = jnp.maximum(n, 1).astype(F32)
    large = max_exact + (jnp.log(nf / max_exact) / math.log(REL_MAX_DIST / max_exact)
                         * (nb - max_exact)).astype(jnp.int32)
    large = jnp.minimum(large, nb - 1)
    return ret + jnp.where(n < max_exact, n, large)


def _attn_bias_table(rel_bias, g, dil, tq=128):
    qi = jnp.arange(tq)[:, None]
    kj = jnp.arange(2 * tq)[None, :] - A_RADIUS
    rel = kj - qi
    tab = rel_bias[:, g * A_HEADS_PER_GROUP:(g + 1) * A_HEADS_PER_GROUP].astype(F32)
    bias = jnp.transpose(tab[_t5_bucket(rel * dil)], (2, 0, 1))
    return jnp.where((jnp.abs(rel) <= A_RADIUS)[None], bias, NEG_INF)


def _s5_toeplitz_kernel(bcf_ref, pwf_ref, bcb_ref, pwb_ref, d_ref, t_ref, gf_ref, gb_ref):
    gf_ref[...] = jnp.dot(bcf_ref[...], pwf_ref[...], precision=HIGHEST, preferred_element_type=F32)
    gb_ref[...] = jnp.dot(bcb_ref[...], pwb_ref[...], precision=HIGHEST, preferred_element_type=F32)
    c = S5_CHUNK
    row = lax.broadcasted_iota(jnp.int32, (c, c), 0)
    col = lax.broadcasted_iota(jnp.int32, (c, c), 1)

    def body(ci, carry):
        dval = d_ref[pl.ds(ci, 1), :]
        for co in range(S5_GROUP):
            r = ci * S5_GROUP + co
            gf = jnp.broadcast_to(gf_ref[pl.ds(r, 1), :], (c, c))
            gb = jnp.broadcast_to(gb_ref[pl.ds(r, 1), :], (c, c))
            tf = pltpu.roll(gf, 0, 1, stride=1, stride_axis=0)
            tb = pltpu.roll(gb, 1, 1, stride=1, stride_axis=0)
            tile = jnp.where(col >= row, tf, 0.0) + jnp.where(row >= col, tb, 0.0)
            tile = tile + jnp.where((row == col) & (ci == co), dval, 0.0)
            t_ref[pl.ds(pl.multiple_of(ci * c, c), c), co * c:(co + 1) * c] = tile.astype(BF16)
        return carry

    lax.fori_loop(0, S5_GROUP, body, 0)


def _s5_toeplitz_call(bcf, pwf, bcb, pwb, dskip):
    n = S5_GROUP * S5_CHUNK
    return pl.pallas_call(
        _s5_toeplitz_kernel, name="s5_toeplitz",
        grid=(S5_GROUPS,),
        in_specs=[pl.BlockSpec((None, S5_GROUP * S5_GROUP, 2 * S5_STATE), lambda g: (g, 0, 0)),
                  pl.BlockSpec((None, 2 * S5_STATE, S5_CHUNK), lambda g: (g, 0, 0)),
                  pl.BlockSpec((None, S5_GROUP * S5_GROUP, 2 * S5_STATE), lambda g: (g, 0, 0)),
                  pl.BlockSpec((None, 2 * S5_STATE, S5_CHUNK), lambda g: (g, 0, 0)),
                  pl.BlockSpec((None, S5_GROUP, LANE), lambda g: (g, 0, 0))],
        out_specs=pl.BlockSpec((None, n, n), lambda g: (g, 0, 0)),
        out_shape=jax.ShapeDtypeStruct((S5_GROUPS, n, n), BF16),
        scratch_shapes=[pltpu.VMEM((S5_GROUP * S5_GROUP, S5_CHUNK), F32),
                        pltpu.VMEM((S5_GROUP * S5_GROUP, S5_CHUNK), F32)],
        compiler_params=_cparams(("parallel",)),
    )(bcf, pwf, bcb, pwb, dskip)


def _s5_tables(lam_re, lam_im, log_step, b_re, b_im, c_re, c_im, d_skip):
    c = S5_CHUNK
    k = jnp.arange(c, dtype=F32)
    per_dir = []
    for direction in range(2):
        step = jnp.exp(log_step[direction].astype(F32))[:, None]
        lr, li = lam_re[direction].astype(F32), lam_im[direction].astype(F32)
        mag = jnp.exp(lr * step)
        ar, ai = mag * jnp.cos(li * step), mag * jnp.sin(li * step)
        nr, ni = ar - 1.0, ai
        den = lr * lr + li * li
        fr = (nr * lr + ni * li) / den
        fi = (ni * lr - nr * li) / den
        br, bi = b_re[direction].astype(F32), b_im[direction].astype(F32)
        bbr = fr[..., None] * br - fi[..., None] * bi
        bbi = fr[..., None] * bi + fi[..., None] * br
        cr, cim = c_re[direction].astype(F32), c_im[direction].astype(F32)

        def power(e, log_mag=lr * step, phase=li * step):
            m = jnp.exp(e[None, None, :] * log_mag[..., None])
            th = e[None, None, :] * phase[..., None]
            return m * jnp.cos(th), m * jnp.sin(th)

        per_dir.append(dict(bbr=bbr, bbi=bbi, cr=cr, cim=cim, power=power))

    def bc_table(d):
        bbr_t = jnp.transpose(d["bbr"], (0, 2, 1))[:, :, None, :]
        bbi_t = jnp.transpose(d["bbi"], (0, 2, 1))[:, :, None, :]
        cr, cim = d["cr"][:, None], d["cim"][:, None]
        re = bbr_t * cr - bbi_t * cim
        im = bbr_t * cim + bbi_t * cr
        return jnp.concatenate([re, -im], axis=-1).reshape(S5_GROUPS, S5_GROUP * S5_GROUP, 2 * S5_STATE)

    f, bw = per_dir
    pfr, pfi = f["power"](k)
    pbr, pbi = bw["power"](c - 1 - k)
    pwf = jnp.concatenate([pfr, pfi], axis=1)
    pwb = jnp.concatenate([pbr, pbi], axis=1)
    dsk = jnp.broadcast_to(d_skip.astype(F32).reshape(S5_GROUPS, S5_GROUP, 1), (S5_GROUPS, S5_GROUP, LANE))
    toep = _s5_toeplitz_call(bc_table(f), pwf, bc_table(bw), pwb, dsk)

    def w_in(d, e):
        pr, pi = d["power"](e)
        re = d["bbr"][:, :, :, None] * pr[:, :, None, :] - d["bbi"][:, :, :, None] * pi[:, :, None, :]
        im = d["bbr"][:, :, :, None] * pi[:, :, None, :] + d["bbi"][:, :, :, None] * pr[:, :, None, :]
        tr = lambda z: jnp.transpose(z, (0, 2, 3, 1)).reshape(S5_GROUPS, S5_GROUP * c, S5_STATE)
        return tr(re), tr(im)

    fre, fim = w_in(f, c - 1 - k)
    bre, bim = w_in(bw, k)
    win = jnp.concatenate([fre, bre, fim, bim], axis=-1).astype(BF16)

    def w_out(d, e):
        pr, pi = d["power"](e)
        re = d["cr"][:, :, :, None] * pr[:, None] - d["cim"][:, :, :, None] * pi[:, None]
        im = d["cr"][:, :, :, None] * pi[:, None] + d["cim"][:, :, :, None] * pr[:, None]
        tr = lambda z: jnp.transpose(z, (0, 2, 1, 3)).reshape(S5_GROUPS, S5_STATE, S5_GROUP * c)
        return tr(re), tr(-im)

    ofre, ofim = w_out(f, k + 1.0)
    obre, obim = w_out(bw, c - k)
    wout = jnp.concatenate([ofre, obre, ofim, obim], axis=1).astype(BF16)

    cc = jnp.asarray([float(c)], F32)
    afr, afi = f["power"](cc)
    abr, abi = bw["power"](cc)
    a_re = jnp.concatenate([afr[..., 0], abr[..., 0]], axis=-1).reshape(1, -1)
    a_im = jnp.concatenate([afi[..., 0], abi[..., 0]], axis=-1).reshape(1, -1)
    return toep, win, wout, a_re, a_im


def _s5_contrib_kernel(u_ref, win_ref, re_ref, im_ref):
    uc = jnp.concatenate([u_ref[ci] for ci in range(S5_GROUP)], axis=-1)
    res = jnp.dot(uc, win_ref[...], preferred_element_type=F32)
    nc, nb = re_ref.shape[0], re_ref.shape[1]
    half = 2 * S5_STATE
    for bi in range(nb):
        re_ref[:, bi, :] = res[bi * nc:(bi + 1) * nc, :half]
        im_ref[:, bi, :] = res[bi * nc:(bi + 1) * nc, half:]


def _s5_contrib_call(ut, win, b, nc):
    nch = b * nc
    half = 2 * S5_STATE
    uv = ut.reshape(S5_GROUPS, S5_GROUP, nch, S5_CHUNK)
    return pl.pallas_call(
        _s5_contrib_kernel, name="s5_contrib",
        grid=(S5_GROUPS,),
        in_specs=[pl.BlockSpec((None, S5_GROUP, nch, S5_CHUNK), lambda g: (g, 0, 0, 0)),
                  pl.BlockSpec((None, S5_GROUP * S5_CHUNK, 4 * S5_STATE), lambda g: (g, 0, 0))],
        out_specs=[pl.BlockSpec((nc, b, half), lambda g: (0, 0, g))] * 2,
        out_shape=[jax.ShapeDtypeStruct((nc, b, S5_GROUPS * half), F32)] * 2,
        compiler_params=_cparams(("parallel",)),
    )(uv, win)


def _s5_scan_kernel(cre_ref, cim_ref, are_ref, aim_ref, ore_ref, oim_ref):
    nc, nb, width = cre_ref.shape
    a_re, a_im = are_ref[...], aim_ref[...]
    lane = lax.broadcasted_iota(jnp.int32, (1, width), 1)
    fwd_lane = (lane % (2 * S5_STATE)) < S5_STATE
    zero = jnp.zeros((nb, width), F32)

    def advance(s, c):
        sr, si = s
        return a_re * sr - a_im * si + cre_ref[c], a_re * si + a_im * sr + cim_ref[c]

    def up(c, s):
        ore_ref[c] = s[0]
        oim_ref[c] = s[1]
        return advance(s, c)

    lax.fori_loop(0, nc, up, (zero, zero))

    def down(i, s):
        c = nc - 1 - i
        ore_ref[c] = jnp.where(fwd_lane, ore_ref[c], s[0])
        oim_ref[c] = jnp.where(fwd_lane, oim_ref[c], s[1])
        return advance(s, c)

    lax.fori_loop(0, nc, down, (zero, zero))


def _s5_scan_call(cre, cim, a_re, a_im):
    return pl.pallas_call(
        _s5_scan_kernel, name="s5_chunk_scan",
        out_shape=[jax.ShapeDtypeStruct(cre.shape, F32)] * 2,
        compiler_params=pltpu.CompilerParams(vmem_limit_bytes=VMEM_LIMIT),
    )(cre, cim, a_re, a_im)


def _gelu_tanh(y):
    return 0.5 * y * (1.0 + jnp.tanh(math.sqrt(2.0 / math.pi) * (y + 0.044715 * (y * y * y))))


def _s5_out_kernel(u_ref, t_ref, sre_ref, sim_ref, wout_ref, z_ref):
    uc = jnp.concatenate([u_ref[ci] for ci in range(S5_GROUP)], axis=-1)
    y = jnp.dot(uc, t_ref[...], preferred_element_type=F32)
    sp = jnp.concatenate([jnp.concatenate([sre_ref[:, bi, :], sim_ref[:, bi, :]], axis=-1)
                          for bi in range(sre_ref.shape[1])], axis=0)
    y = y + jnp.dot(sp.astype(BF16), wout_ref[...], preferred_element_type=F32)
    z = _gelu_tanh(y)
    for co in range(S5_GROUP):
        z_ref[co] = z[:, co * S5_CHUNK:(co + 1) * S5_CHUNK].astype(BF16)


def _s5_out_call(ut, toep, s_re, s_im, wout, b, nc):
    nch = b * nc
    uv = ut.reshape(S5_GROUPS, S5_GROUP, nch, S5_CHUNK)
    n = S5_GROUP * S5_CHUNK
    half = 2 * S5_STATE
    zt = pl.pallas_call(
        _s5_out_kernel, name="s5_out",
        grid=(S5_GROUPS,),
        in_specs=[pl.BlockSpec((None, S5_GROUP, nch, S5_CHUNK), lambda g: (g, 0, 0, 0)),
                  pl.BlockSpec((None, n, n), lambda g: (g, 0, 0)),
                  pl.BlockSpec((nc, b, half), lambda g: (0, 0, g)),
                  pl.BlockSpec((nc, b, half), lambda g: (0, 0, g)),
                  pl.BlockSpec((None, 4 * S5_STATE, n), lambda g: (g, 0, 0))],
        out_specs=pl.BlockSpec((None, S5_GROUP, nch, S5_CHUNK), lambda g: (g, 0, 0, 0)),
        out_shape=jax.ShapeDtypeStruct((S5_GROUPS, S5_GROUP, nch, S5_CHUNK), BF16),
        compiler_params=_cparams(("parallel",)),
    )(uv, toep, s_re, s_im, wout)
    return zt.reshape(S5_WIDTH, nch * S5_CHUNK)


def _ret_tables(s):
    c = RET_CHUNK
    half = RET_QK_DIM // 2
    inv = ROPE_BASE ** (-jnp.arange(half, dtype=F32) / half)
    ang = jnp.arange(s, dtype=F32)[:, None] * inv[None, :]
    cos = jnp.tile(jnp.cos(ang), (1, 2 * RET_HEADS))
    sin = jnp.tile(jnp.sin(ang), (1, 2 * RET_HEADS))
    hidx = jnp.arange(RET_HEADS, dtype=F32)
    lgf = jnp.log1p(-jnp.exp2(-5.0 - hidx))
    lgb = jnp.log1p(-jnp.exp2(-5.5 - hidx))
    j = jnp.arange(c, dtype=F32)
    rel = j[:, None] - j[None, :]
    intra = jnp.exp(jnp.abs(rel)[None] * jnp.where(rel[None] >= 0, lgf[:, None, None], lgb[:, None, None]))
    qk_f = jnp.repeat(lgf, RET_QK_DIM)[None, :]
    qk_b = jnp.repeat(lgb, RET_QK_DIM)[None, :]
    v_f = jnp.repeat(lgf, RET_V_DIM)[None, :]
    v_b = jnp.repeat(lgb, RET_V_DIM)[None, :]
    tabs = dict(
        cos=cos, sin=sin, intra=intra,
        tail_f=jnp.exp((c - 1 - j)[:, None] * qk_f), head_b=jnp.exp(j[:, None] * qk_b),
        decq_f=jnp.exp((j + 1.0)[:, None] * qk_f), decq_b=jnp.exp((c - j)[:, None] * qk_b),
        chunk_f=jnp.exp(c * v_f), chunk_b=jnp.exp(c * v_b),
        bd=(jnp.arange(RET_HEADS * RET_QK_DIM)[:, None] // RET_QK_DIM
            == jnp.arange(RET_HEADS * RET_V_DIM)[None, :] // RET_V_DIM).astype(F32))
    return tabs


def _rot(x_ref, xs_ref, cos_ref, sin_ref):
    return x_ref[...].astype(F32) * cos_ref[...] + xs_ref[...].astype(F32) * sin_ref[...]


def _ret_bwd_kernel(k_ref, ks_ref, v_ref, cos_ref, sin_ref, headb_ref, chunkb_ref, bd_ref, sb_ref, st_ref):
    @pl.when(pl.program_id(1) == 0)
    def _():
        st_ref[...] = jnp.zeros_like(st_ref)

    sb_ref[...] = st_ref[...].astype(BF16)
    kr = _rot(k_ref, ks_ref, cos_ref, sin_ref) * (RET_QK_DIM ** -0.5)
    kwt = jnp.transpose(kr * headb_ref[...]).astype(BF16)
    kv = jnp.dot(kwt, v_ref[...], preferred_element_type=F32)
    st_ref[...] = st_ref[...] * chunkb_ref[...] + kv * bd_ref[...]


def _ret_fwd_kernel(q_ref, qs_ref, k_ref, ks_ref, v_ref, g_ref, cos_ref, sin_ref, sb_ref, intra_ref,
                    tailf_ref, decqf_ref, decqb_ref, chunkf_ref, bd_ref, o_ref, st_ref):
    @pl.when(pl.program_id(1) == 0)
    def _():
        st_ref[...] = jnp.zeros_like(st_ref)

    qr = _rot(q_ref, qs_ref, cos_ref, sin_ref)
    kr = _rot(k_ref, ks_ref, cos_ref, sin_ref) * (RET_QK_DIM ** -0.5)
    qb, kb = qr.astype(BF16), kr.astype(BF16)
    vb = v_ref[...]
    lane_head = lax.broadcasted_iota(jnp.int32, (1, qb.shape[1]), 1) // RET_QK_DIM
    inter = jnp.dot((qr * decqf_ref[...]).astype(BF16), st_ref[...].astype(BF16), preferred_element_type=F32)
    inter = inter + jnp.dot((qr * decqb_ref[...]).astype(BF16), sb_ref[...], preferred_element_type=F32)
    heads = range(RET_HEADS)
    vcols = [slice(h * RET_V_DIM, (h + 1) * RET_V_DIM) for h in heads]
    qhs = [jnp.where(lane_head == h, qb, jnp.zeros_like(qb)) for h in heads]
    scs = [lax.dot_general(qh, kb, (((1,), (1,)), ((), ())), preferred_element_type=F32) * intra_ref[h]
           for h, qh in zip(heads, qhs)]
    ohs = [jnp.dot(sc.astype(BF16), vb[:, vs], preferred_element_type=F32) + inter[:, vs]
           for sc, vs in zip(scs, vcols)]
    mus = [jnp.mean(oh, axis=-1, keepdims=True) for oh in ohs]
    dds = [oh - mu for oh, mu in zip(ohs, mus)]
    vrs = [jnp.mean(dd * dd, axis=-1, keepdims=True) for dd in dds]
    on = jnp.concatenate([dd * lax.rsqrt(var + LN_EPS) for dd, var in zip(dds, vrs)], axis=-1)
    gg = g_ref[...].astype(F32)
    o_ref[...] = (on * (gg * jax.nn.sigmoid(gg))).astype(BF16)
    kwt = jnp.transpose(kr * tailf_ref[...]).astype(BF16)
    kv = jnp.dot(kwt, vb, preferred_element_type=F32)
    st_ref[...] = st_ref[...] * chunkf_ref[...] + kv * bd_ref[...]


def _retention_call(proj, tabs, b, s):
    c = RET_CHUNK
    nc = s // c
    qkw, vw = RET_HEADS * RET_QK_DIM, RET_HEADS * RET_V_DIM
    pv = proj.reshape(b, s, NP)
    qk_blk, v_blk = (None, c, qkw), (None, c, vw)
    qo, qso, ko, kso = (_SEG[n] * LANE // qkw for n in ("rq", "rqs", "rk", "rks"))
    vo, go = (_SEG[n] * LANE // vw for n in ("rv", "rg"))
    const2 = lambda shape: pl.BlockSpec(shape, lambda bi, i: (0, 0))
    rev = lambda off: (lambda bi, i: (bi, nc - 1 - i, off))
    fwd = lambda off: (lambda bi, i: (bi, i, off))
    sb = pl.pallas_call(
        _ret_bwd_kernel, name="retention_bwd_state",
        grid=(b, nc),
        in_specs=[pl.BlockSpec(qk_blk, rev(ko)), pl.BlockSpec(qk_blk, rev(kso)), pl.BlockSpec(v_blk, rev(vo)),
                  pl.BlockSpec((c, qkw), lambda bi, i: (nc - 1 - i, 0)),
                  pl.BlockSpec((c, qkw), lambda bi, i: (nc - 1 - i, 0)),
                  const2((c, qkw)), const2((1, vw)), const2((qkw, vw))],
        out_specs=pl.BlockSpec((None, None, qkw, vw), lambda bi, i: (bi, nc - 1 - i, 0, 0)),
        out_shape=jax.ShapeDtypeStruct((b, nc, qkw, vw), BF16),
        scratch_shapes=[pltpu.VMEM((qkw, vw), F32)],
        compiler_params=_cparams(("parallel", "arbitrary")),
    )(pv, pv, pv, tabs["cos"], tabs["sin"], tabs["head_b"], tabs["chunk_b"], tabs["bd"])
    yc = pl.pallas_call(
        _ret_fwd_kernel, name="retention_fwd",
        grid=(b, nc),
        in_specs=[pl.BlockSpec(qk_blk, fwd(qo)), pl.BlockSpec(qk_blk, fwd(qso)),
                  pl.BlockSpec(qk_blk, fwd(ko)), pl.BlockSpec(qk_blk, fwd(kso)),
                  pl.BlockSpec(v_blk, fwd(vo)), pl.BlockSpec(v_blk, fwd(go)),
                  pl.BlockSpec((c, qkw), lambda bi, i: (i, 0)), pl.BlockSpec((c, qkw), lambda bi, i: (i, 0)),
                  pl.BlockSpec((None, None, qkw, vw), lambda bi, i: (bi, i, 0, 0)),
                  pl.BlockSpec((RET_HEADS, c, c), lambda bi, i: (0, 0, 0)),
                  const2((c, qkw)), const2((c, qkw)), const2((c, qkw)), const2((1, vw)), const2((qkw, vw))],
        out_specs=pl.BlockSpec((None, c, vw), lambda bi, i: (bi, i, 0)),
        out_shape=jax.ShapeDtypeStruct((b, s, vw), BF16),
        scratch_shapes=[pltpu.VMEM((qkw, vw), F32)],
        compiler_params=_cparams(("parallel", "arbitrary")),
    )(pv, pv, pv, pv, pv, pv, tabs["cos"], tabs["sin"], sb, tabs["intra"],
      tabs["tail_f"], tabs["decq_f"], tabs["decq_b"], tabs["chunk_f"], tabs["bd"])
    return yc.reshape(b * s, vw)


def _mm_kernel(x_ref, w_ref, o_ref):
    o_ref[...] = jnp.dot(x_ref[...].astype(BF16), w_ref[...], preferred_element_type=F32).astype(o_ref.dtype)


def _mem_kv_call(mem2d, w_kv):
    m, n = mem2d.shape[0], w_kv.shape[1]
    tm = 512
    return pl.pallas_call(
        _mm_kernel, name="mem_kv_proj",
        grid=(m // tm,),
        in_specs=[pl.BlockSpec((tm, D_MODEL), lambda i: (i, 0)),
                  pl.BlockSpec((D_MODEL, n), lambda i: (0, 0))],
        out_specs=pl.BlockSpec((tm, n), lambda i: (i, 0)),
        out_shape=jax.ShapeDtypeStruct((m, n), BF16),
        compiler_params=_cparams(("parallel",)),
    )(mem2d, w_kv)


def _mem_attn_kernel(q_ref, k_ref, v_ref, o_ref):
    q = q_ref[...]
    cols = [slice(h * MEM_HEAD_PAD, (h + 1) * MEM_HEAD_PAD) for h in range(MEM_HEADS)]
    ss = [lax.dot_general(q[:, hs], k_ref[:, hs], (((1,), (1,)), ((), ())),
                          preferred_element_type=F32) * (MEM_HEAD_DIM ** -0.5) for hs in cols]
    ms = [jnp.max(s, axis=-1, keepdims=True) for s in ss]
    es = [jnp.exp(s - m) for s, m in zip(ss, ms)]
    ps = [e / jnp.sum(e, axis=-1, keepdims=True) for e in es]
    outs = [jnp.dot(p.astype(BF16), v_ref[:, hs], preferred_element_type=F32) for p, hs in zip(ps, cols)]
    o_ref[...] = jnp.concatenate(outs, axis=-1).astype(BF16)


def _mem_attn_call(proj, kv, b, s):
    tq = 256
    w = MEM_HEADS * MEM_HEAD_PAD
    pv = proj.reshape(b, s, NP)
    kvv = kv.reshape(b, N_MEM, 2 * w)
    qo = _SEG["mq"] * LANE // w
    yd = pl.pallas_call(
        _mem_attn_kernel, name="mem_attn",
        grid=(b, s // tq),
        in_specs=[pl.BlockSpec((None, tq, w), lambda bi, i: (bi, i, qo)),
                  pl.BlockSpec((None, N_MEM, w), lambda bi, i: (bi, 0, 0)),
                  pl.BlockSpec((None, N_MEM, w), lambda bi, i: (bi, 0, 1))],
        out_specs=pl.BlockSpec((None, tq, w), lambda bi, i: (bi, i, 0)),
        out_shape=jax.ShapeDtypeStruct((b, s, w), BF16),
        compiler_params=_cparams(("parallel", "parallel")),
    )(pv, kvv, kvv)
    return yd.reshape(b * s, w)


def _merge_kernel(x_ref, ao0_ref, ao1_ref, ao2_ref, al0_ref, al1_ref, al2_ref, zt_ref, yc_ref, yd_ref, gl_ref,
                  wglut_ref, bglu_ref, wba_ref, wbb_ref, wbc_ref, wbd_ref, wout_ref, g1_ref, b1_ref,
                  rwh_ref, rwl_ref, rb_ref,
                  x1_ref, route_ref, counts_ref, carry_ref):
    tm = x_ref.shape[0]

    @pl.when(pl.program_id(0) == 0)
    def _():
        carry_ref[...] = jnp.zeros_like(carry_ref)

    slabs = lambda ref: jnp.concatenate([ref[c] for c in range(ref.shape[0])], axis=-1)
    lses = [al0_ref[...], slabs(al1_ref), slabs(al2_ref)]
    aos = [ao0_ref[...], slabs(ao1_ref), slabs(ao2_ref)]
    mx = jnp.maximum(jnp.maximum(lses[0], lses[1]), lses[2])
    es = [jnp.exp(l - mx) for l in lses]
    tot = es[0] + es[1] + es[2]
    ya = jnp.concatenate([o * (e / tot) for o, e in zip(aos, es)], axis=-1).astype(BF16)
    zt = zt_ref[...]
    glu = jnp.dot(wglut_ref[...], zt, preferred_element_type=F32) + bglu_ref[...]
    ybt = zt.astype(F32) * jax.nn.sigmoid(glu)
    yb = jnp.transpose(ybt).astype(BF16)
    merged = jnp.zeros((tm, D_MODEL), F32)
    for n, (y, w_ref) in enumerate(((ya, wba_ref), (yb, wbb_ref), (yc_ref[...], wbc_ref), (yd_ref[...], wbd_ref))):
        bo = jnp.dot(y, w_ref[...], preferred_element_type=F32)
        gate = jax.nn.sigmoid(gl_ref[:, n * D_MODEL:(n + 1) * D_MODEL].astype(F32))
        merged = merged + gate * bo
    h = DEEPNORM_ALPHA * x_ref[...] + jnp.dot(merged.astype(BF16), wout_ref[...], preferred_element_type=F32)
    x1 = _ln(h, g1_ref[...], b1_ref[...])
    x1_ref[...] = x1

    x_hi = x1.astype(BF16)
    x_lo = (x1 - x_hi.astype(F32)).astype(BF16)
    logits = (jnp.dot(x_hi, rwh_ref[...], preferred_element_type=F32)
              + jnp.dot(x_lo, rwh_ref[...], preferred_element_type=F32)
              + jnp.dot(x_hi, rwl_ref[...], preferred_element_type=F32)) + rb_ref[...]
    lane = lax.broadcasted_iota(jnp.int32, (tm, LANE), 1)
    work = logits
    sel, vals, hots = [], [], []
    for _ in range(TOP_K):
        mk = jnp.max(work, axis=-1, keepdims=True)
        ik = jnp.min(jnp.where(work == mk, lane, LANE), axis=-1, keepdims=True)
        hot = lane == ik
        work = jnp.where(hot, -jnp.inf, work)
        sel.append(ik)
        vals.append(mk)
        hots.append(hot)
    ex = [jnp.exp(v - vals[0]) for v in vals]
    den = ex[0] + ex[1] + ex[2] + ex[3]
    chosen = (hots[0] | hots[1] | hots[2] | hots[3]).astype(F32)
    r_i = lax.broadcasted_iota(jnp.int32, (tm, tm), 0)
    c_i = lax.broadcasted_iota(jnp.int32, (tm, tm), 1)
    strict_lower = (c_i < r_i).astype(BF16)
    before = jnp.dot(strict_lower, chosen.astype(BF16), preferred_element_type=F32) + carry_ref[0:1, :]
    route = jnp.zeros((tm, LANE), F32)
    for k in range(TOP_K):
        rank = jnp.sum(jnp.where(hots[k], before, 0.0), axis=-1, keepdims=True)
        route = jnp.where(lane == k, sel[k].astype(F32), route)
        route = jnp.where(lane == TOP_K + k, ex[k] / den, route)
        route = jnp.where(lane == 2 * TOP_K + k, rank, route)
    route_ref[...] = route
    new_carry = carry_ref[0:1, :] + jnp.sum(chosen, axis=0, keepdims=True)
    carry_ref[...] = jnp.broadcast_to(new_carry, carry_ref.shape)
    counts_ref[...] = jnp.broadcast_to(new_carry, counts_ref.shape)


def _merge_call(x, aos, alses, zt, yc, yd, proj, lw):
    t = x.shape[0]
    tm = 256
    gw = GW
    gate_blk = 4 * D_MODEL
    row = lambda w: pl.BlockSpec((tm, w), lambda i: (i, 0))
    slab = pl.BlockSpec((gw // LANE, tm, LANE), lambda i: (0, i, 0))
    full = lambda a: pl.BlockSpec(a.shape, lambda i: (0,) * a.ndim)
    consts = (lw["w_glu_t"], lw["b_glu"], lw["wb_a"], lw["wb_b"], lw["wb_c"], lw["wb_d"], lw["w_out"],
              lw["ln1_g"], lw["ln1_b"], lw["router_w_hi"], lw["router_w_lo"], lw["router_b"])
    return pl.pallas_call(
        _merge_kernel, name="branch_merge_router",
        grid=(t // tm,),
        in_specs=[row(D_MODEL)] + [row(gw), slab, slab] * 2
                 + [pl.BlockSpec((S5_WIDTH, tm), lambda i: (0, i)),
                  row(RET_HEADS * RET_V_DIM), row(MEM_HEADS * MEM_HEAD_PAD),
                  pl.BlockSpec((tm, gate_blk), lambda i: (i, _SEG["gate"] * LANE // gate_blk))]
                 + [full(a) for a in consts],
        out_specs=[row(D_MODEL), row(LANE), pl.BlockSpec((8, LANE), lambda i: (0, 0))],
        out_shape=[jax.ShapeDtypeStruct((t, D_MODEL), F32),
                   jax.ShapeDtypeStruct((t, LANE), F32),
                   jax.ShapeDtypeStruct((8, LANE), F32)],
        scratch_shapes=[pltpu.VMEM((8, LANE), F32)],
        compiler_params=_cparams(("arbitrary",)),
    )(x, *aos, *alses, zt, yc, yd, proj, *consts)


def _sc_gather_rows(x, idx):
    m, d = idx.shape[0], x.shape[1]
    window = SC_GATHER_WINDOW
    mesh = plsc.VectorSubcoreMesh(core_axis_name="core", subcore_axis_name="subcore")

    @pl.kernel(out_type=jax.ShapeDtypeStruct((m, d), x.dtype), mesh=mesh, scratch_types=[])
    def gather_kernel(x_hbm, i_hbm, o_hbm):
        def body(i_vmem, o_vmem):
            pltpu.sync_copy(x_hbm.at[i_vmem.at[0, pl.ds(0, window)]], o_vmem)

        pltpu.emit_pipeline(
            body,
            grid=(m // window,),
            in_specs=[pl.BlockSpec((1, LANE), index_map=lambda i: (i, 0))],
            out_specs=[pl.BlockSpec((window, d), index_map=lambda i: (i, 0))],
            core_axis_name=("core", "subcore"),
            dimension_semantics=(pltpu.PARALLEL,),
            trace_scopes=False,
        )(i_hbm, o_hbm)

    idx2d = jnp.pad(idx.reshape(m // window, window), ((0, 0), (0, LANE - window)))
    return gather_kernel(x, idx2d)


def _moe_expert_kernel(be_ref, nu_ref, x_ref, w1_ref, b1_ref, w2_ref, b2_ref, o_ref):
    @pl.when(pl.program_id(0) < nu_ref[0])
    def _():
        h = jnp.dot(x_ref[...].astype(BF16), w1_ref[...], preferred_element_type=F32) + b1_ref[...]
        gate = jnp.minimum(h[:, :D_FF], SWIGLU_LIMIT)
        up = jnp.clip(h[:, D_FF:], -SWIGLU_LIMIT, SWIGLU_LIMIT)
        hid = gate * jax.nn.sigmoid(SWIGLU_ALPHA * gate) * (up + 1.0)
        o_ref[...] = jnp.dot(hid.astype(BF16), w2_ref[...], preferred_element_type=F32) + b2_ref[...]

    @pl.when(pl.program_id(0) >= nu_ref[0])
    def _():
        o_ref[...] = jnp.zeros_like(o_ref)


def _moe_expert_call(xg, block_expert, n_used, w1, b1, w2, b2):
    bm = MOE_BLOCK_ROWS
    n_blocks = xg.shape[0] // bm
    grid_spec = pltpu.PrefetchScalarGridSpec(
        num_scalar_prefetch=2, grid=(n_blocks,),
        in_specs=[pl.BlockSpec((bm, D_MODEL), lambda i, be, nu: (i, 0)),
                  pl.BlockSpec((None, D_MODEL, 2 * D_FF), lambda i, be, nu: (be[i], 0, 0)),
                  pl.BlockSpec((None, 1, 2 * D_FF), lambda i, be, nu: (be[i], 0, 0)),
                  pl.BlockSpec((None, D_FF, D_MODEL), lambda i, be, nu: (be[i], 0, 0)),
                  pl.BlockSpec((None, 1, D_MODEL), lambda i, be, nu: (be[i], 0, 0))],
        out_specs=pl.BlockSpec((bm, D_MODEL), lambda i, be, nu: (i, 0)))
    return pl.pallas_call(
        _moe_expert_kernel, name="moe_experts",
        grid_spec=grid_spec,
        out_shape=jax.ShapeDtypeStruct((n_blocks * bm, D_MODEL), F32),
        compiler_params=_cparams(("arbitrary",)),
    )(block_expert, n_used, xg, w1, b1, w2, b2)


def _moe_combine_kernel(x_ref, route_ref, y0_ref, y1_ref, y2_ref, y3_ref, g_ref, b_ref, o_ref):
    route = route_ref[...]
    moe = jnp.zeros(x_ref.shape, F32)
    for k, y_ref in enumerate((y0_ref, y1_ref, y2_ref, y3_ref)):
        moe = moe + route[:, TOP_K + k:TOP_K + k + 1] * y_ref[...]
    o_ref[...] = _ln(DEEPNORM_ALPHA * x_ref[...] + moe, g_ref[...], b_ref[...])


def _moe_combine_call(x1, route, yg, g2, b2):
    t = x1.shape[0]
    tm = 256
    nt = t // tm
    ysel = lambda k: pl.BlockSpec((tm, D_MODEL), lambda i: (k * nt + i, 0))
    return pl.pallas_call(
        _moe_combine_kernel, name="moe_combine_ln",
        grid=(nt,),
        in_specs=[pl.BlockSpec((tm, D_MODEL), lambda i: (i, 0)),
                  pl.BlockSpec((tm, LANE), lambda i: (i, 0))]
                 + [ysel(k) for k in range(TOP_K)]
                 + [pl.BlockSpec((1, D_MODEL), lambda i: (0, 0)),
                    pl.BlockSpec((1, D_MODEL), lambda i: (0, 0))],
        out_specs=pl.BlockSpec((tm, D_MODEL), lambda i: (i, 0)),
        out_shape=jax.ShapeDtypeStruct((t, D_MODEL), F32),
        compiler_params=_cparams(("parallel",)),
    )(x1, route, yg, yg, yg, yg, g2, b2)


def _moe(x1, route, counts, lw):
    t = x1.shape[0]
    bm = MOE_BLOCK_ROWS
    n_assign = t * TOP_K
    n_blocks = n_assign // bm + N_EXPERTS
    idx = route[:, :TOP_K].astype(jnp.int32)
    rank = route[:, 2 * TOP_K:3 * TOP_K].astype(jnp.int32)
    cnt = counts[0, :N_EXPERTS].astype(jnp.int32)
    padded = (cnt + bm - 1) // bm * bm
    pad_ends = jnp.cumsum(padded)
    pad_starts = pad_ends - padded
    dest = pad_starts[idx] + rank
    tok = jnp.broadcast_to(jnp.arange(t, dtype=jnp.int32)[:, None], (t, TOP_K))
    row_tok = jnp.zeros((n_blocks * bm,), jnp.int32).at[dest.reshape(-1)].set(tok.reshape(-1))
    block_start = jnp.arange(n_blocks, dtype=jnp.int32) * bm
    block_expert = jnp.minimum(jnp.sum(pad_ends[None, :] <= block_start[:, None], axis=1),
                               N_EXPERTS - 1).astype(jnp.int32)
    n_used = (pad_ends[-1:] // bm).astype(jnp.int32)
    xg = _sc_gather_rows(x1, row_tok)
    yb = _moe_expert_call(xg, block_expert, n_used, lw["moe_w1"], lw["moe_b1"], lw["moe_w2"], lw["moe_b2"])
    yg = _sc_gather_rows(yb, jnp.transpose(dest).reshape(-1))
    return _moe_combine_call(x1, route, yg, lw["ln2_g"], lw["ln2_b"])


def _rotate_half_cols(w):
    half = RET_QK_DIM // 2
    w4 = w.reshape(w.shape[0], RET_HEADS, 2, half)
    return jnp.concatenate([-w4[:, :, 1:], w4[:, :, :1]], axis=2).reshape(w.shape)


def _pad_heads(w, axis):
    shp = list(w.shape)
    shp[axis:axis + 1] = [MEM_HEADS, MEM_HEAD_DIM]
    w = w.reshape(shp)
    pad = [(0, 0)] * w.ndim
    pad[axis + 1] = (0, MEM_HEAD_PAD - MEM_HEAD_DIM)
    w = jnp.pad(w, pad)
    shp[axis:axis + 2] = [MEM_HEADS * MEM_HEAD_PAD]
    return w.reshape(shp)


def _layer_weights(l, p):
    w_in = p["w_in"][l]
    aw, sw, qkw, vw, mw = 768, S5_WIDTH, RET_HEADS * RET_QK_DIM, RET_HEADS * RET_V_DIM, 768
    offs = [0]
    for wdt in (aw, aw, aw, sw, qkw, qkw, vw, vw, mw, 4 * D_MODEL):
        offs.append(offs[-1] + wdt)
    aq, ak, av, su, rq, rk, rv, rg, mq, gate = (w_in[:, offs[i]:offs[i + 1]] for i in range(10))
    seg = dict(gate=gate, mq=_pad_heads(mq, 1), aq=aq[:, :GW], ak=ak[:, :GW], av=av[:, :GW],
               pad=jnp.zeros((D_MODEL, 2 * LANE), F32), rv=rv, rg=rg,
               rq=rq, rqs=_rotate_half_cols(rq), rk=rk, rks=_rotate_half_cols(rk))
    order = sorted(_SEG, key=_SEG.get)
    w_p = jnp.concatenate([seg[n] for n in order], axis=1).astype(BF16)
    assert w_p.shape[1] == NP
    w_d = jnp.concatenate([w[:, g * GW:(g + 1) * GW] for g in (1, 2) for w in (aq, ak, av)], axis=1).astype(BF16)
    wkv = p["w_mem_kv"][l]
    w_kv = jnp.concatenate([_pad_heads(wkv[:, :768], 1), _pad_heads(wkv[:, 768:], 1)], axis=1).astype(BF16)
    wb = p["w_branch"][l]
    rw = jnp.pad(p["router_w"][l].astype(F32), ((0, 0), (0, LANE - N_EXPERTS)))
    rb = jnp.pad(p["router_b"][l].astype(F32), (0, LANE - N_EXPERTS), constant_values=NEG_INF)
    return dict(
        w_p=w_p, w_d=w_d, w_su_t=jnp.transpose(su).astype(BF16), w_kv=w_kv,
        w_glu_t=jnp.transpose(p["s5_w_glu"][l]).astype(BF16), b_glu=p["s5_b_glu"][l].astype(F32).reshape(-1, 1),
        wb_a=wb[0].astype(BF16), wb_b=wb[1].astype(BF16), wb_c=wb[2].astype(BF16),
        wb_d=_pad_heads(wb[3], 0).astype(BF16), w_out=p["w_out"][l].astype(BF16),
        ln1_g=p["ln1_g"][l].reshape(1, -1), ln1_b=p["ln1_b"][l].reshape(1, -1),
        router_w_hi=rw.astype(BF16), router_w_lo=(rw - rw.astype(BF16).astype(F32)).astype(BF16),
        router_b=rb.reshape(1, -1),
        moe_w1=p["moe_w1"][l].astype(BF16), moe_b1=p["moe_b1"][l].astype(F32).reshape(N_EXPERTS, 1, -1),
        moe_w2=p["moe_w2"][l].astype(BF16), moe_b2=p["moe_b2"][l].astype(F32).reshape(N_EXPERTS, 1, -1),
        ln2_g=p["ln2_g"][l].reshape(1, -1), ln2_b=p["ln2_b"][l].reshape(1, -1),
        s5=_s5_tables(p["s5_lam_re"][l], p["s5_lam_im"][l], p["s5_log_step"][l], p["s5_b_re"][l],
                      p["s5_b_im"][l], p["s5_c_re"][l], p["s5_c_im"][l], p["s5_d"][l]))


def _trunk_layer(x, mem2d, b, s, lw, attn_bias, ret_tabs):
    proj, ut, pd = _proj_call(x, lw["w_p"], lw["w_su_t"], lw["w_d"])
    outs = [_attn_call(proj, attn_bias[0], b, s)]
    outs += [_dil_attn_call(pd, attn_bias[g], b, s, g, A_GROUPS[g][1]) for g in (1, 2)]
    aos, alses = [o for o, _ in outs], [l for _, l in outs]
    toep, win, wout, a_re, a_im = lw["s5"]
    nc = s // S5_CHUNK
    c_re, c_im = _s5_contrib_call(ut, win, b, nc)
    s_re, s_im = _s5_scan_call(c_re, c_im, a_re, a_im)
    zt = _s5_out_call(ut, toep, s_re, s_im, wout, b, nc)
    yc = _retention_call(proj, ret_tabs, b, s)
    kv = _mem_kv_call(mem2d, lw["w_kv"])
    yd = _mem_attn_call(proj, kv, b, s)
    x1, route, counts = _merge_call(x, aos, alses, zt, yc, yd, proj, lw)
    return _moe(x1, route, counts, lw)


def kernel(x_prompt, x_sample, mem_prompt, mem_sample, ln_in_g, ln_in_b, rel_bias, w_in, s5_lam_re, s5_lam_im, s5_log_step, s5_b_re, s5_b_im, s5_c_re, s5_c_im, s5_d, s5_w_glu, s5_b_glu, w_mem_kv, w_branch, w_out, ln1_g, ln1_b, router_w, router_b, moe_w1, moe_b1, moe_w2, moe_b2, ln2_g, ln2_b):
    p = dict(w_in=w_in, s5_lam_re=s5_lam_re, s5_lam_im=s5_lam_im, s5_log_step=s5_log_step, s5_b_re=s5_b_re,
             s5_b_im=s5_b_im, s5_c_re=s5_c_re, s5_c_im=s5_c_im, s5_d=s5_d, s5_w_glu=s5_w_glu, s5_b_glu=s5_b_glu,
             w_mem_kv=w_mem_kv, w_branch=w_branch, w_out=w_out, ln1_g=ln1_g, ln1_b=ln1_b, router_w=router_w,
             router_b=router_b, moe_w1=moe_w1, moe_b1=moe_b1, moe_w2=moe_w2, moe_b2=moe_b2, ln2_g=ln2_g,
             ln2_b=ln2_b)
    attn_bias = [_attn_bias_table(rel_bias, g, dil) for g, (_, dil) in enumerate(A_GROUPS)]
    trunks = []
    for x, mem in ((x_prompt, mem_prompt), (x_sample, mem_sample)):
        b, s, _ = x.shape
        trunks.append(dict(x=_layer_norm_call(x.reshape(b * s, D_MODEL), ln_in_g, ln_in_b),
                           mem=mem.reshape(b * N_MEM, D_MODEL), b=b, s=s, tabs=_ret_tables(s)))
    for l in range(DEPTH):
        lw = _layer_weights(l, p)
        for tr in trunks:
            tr["x"] = _trunk_layer(tr["x"], tr["mem"], tr["b"], tr["s"], lw, attn_bias, tr["tabs"])
    return tuple(tr["x"].reshape(tr["b"], tr["s"], D_MODEL) for tr in trunks)
```

```python
import functools
import math

import jax
import jax.numpy as jnp
from jax import lax
from jax.experimental import pallas as pl
from jax.experimental.pallas import tpu as pltpu
from jax.experimental.pallas import tpu_sc as plsc

F32 = jnp.float32
BF16 = jnp.bfloat16
HIGHEST = lax.Precision.HIGHEST

D_MODEL = 1024
DEPTH = 4
N_MEM = 256
A_GROUPS = ((128, 1), (512, 4), (2048, 16))
A_HEADS_PER_GROUP = 4
A_HEADS = 12
A_HEAD_DIM = 64
A_RADIUS = 64
REL_BUCKETS = 32
REL_MAX_DIST = 1024
S5_GROUP = 16
S5_WIDTH = 768
S5_GROUPS = 48
S5_STATE = 64
RET_HEADS = 6
RET_QK_DIM = 64
RET_V_DIM = 128
RET_CHUNK = 128
ROPE_BASE = 10000.0
MEM_HEADS = 4
MEM_HEAD_DIM = 192
MEM_HEAD_PAD = 256
N_EXPERTS = 32
TOP_K = 4
D_FF = 1024
SWIGLU_LIMIT = 7.0
SWIGLU_ALPHA = 1.702
LN_EPS = 1e-5
DEEPNORM_ALPHA = (2 * DEPTH) ** 0.25
NEG_INF = -1e30

LANE = 128
VMEM_LIMIT = 56 * 1024 * 1024
S5_CHUNK = 128
MOE_BLOCK_ROWS = 512
SC_GATHER_WINDOW = 64

_SEG = {}
_off = 0
for _name, _width, _align in (
        ("gate", 32, 8), ("mq", 8, 8), ("aq", 2, 2), ("ak", 2, 2), ("av", 2, 2), ("pad", 2, 2),
        ("rv", 6, 6), ("rg", 6, 6), ("rq", 3, 3), ("rqs", 3, 3), ("rk", 3, 3), ("rks", 3, 3)):
    assert _off % _align == 0, (_name, _off)
    _SEG[_name] = _off
    _off += _width
NP_UNITS = _off
NP = NP_UNITS * LANE
assert NP % 256 == 0
GW = A_HEADS_PER_GROUP * A_HEAD_DIM
ND = 2 * 3 * GW


def _cparams(sem, vmem=VMEM_LIMIT):
    return pltpu.CompilerParams(dimension_semantics=sem, vmem_limit_bytes=vmem)


def _ln(h, g, b):
    mu = jnp.mean(h, axis=-1, keepdims=True)
    d = h - mu
    var = jnp.mean(d * d, axis=-1, keepdims=True)
    return d * lax.rsqrt(var + LN_EPS) * g + b


def _ln_kernel(x_ref, g_ref, b_ref, o_ref):
    o_ref[...] = _ln(x_ref[...], g_ref[...], b_ref[...])


def _layer_norm_call(x, g, b):
    t = x.shape[0]
    tm = 512
    return pl.pallas_call(
        _ln_kernel, name="input_ln",
        grid=(t // tm,),
        in_specs=[pl.BlockSpec((tm, D_MODEL), lambda i: (i, 0)),
                  pl.BlockSpec((1, D_MODEL), lambda i: (0, 0)),
                  pl.BlockSpec((1, D_MODEL), lambda i: (0, 0))],
        out_specs=pl.BlockSpec((tm, D_MODEL), lambda i: (i, 0)),
        out_shape=jax.ShapeDtypeStruct((t, D_MODEL), F32),
        compiler_params=_cparams(("parallel",)),
    )(x, g.reshape(1, -1), b.reshape(1, -1))


def _proj_kernel(x_ref, w_ref, wsut_ref, wd_ref, p_ref, ut_ref, pd_ref, xb_ref):
    @pl.when(pl.program_id(1) == 0)
    def _():
        xb = x_ref[...].astype(BF16)
        xb_ref[...] = xb
        ut_ref[...] = lax.dot_general(wsut_ref[...], xb, (((1,), (1,)), ((), ())),
                                      preferred_element_type=F32).astype(BF16)
        pd = jnp.dot(xb, wd_ref[...], preferred_element_type=F32)
        for c in range(ND // LANE):
            pd_ref[c] = pd[:, c * LANE:(c + 1) * LANE]

    p_ref[...] = jnp.dot(xb_ref[...], w_ref[...], preferred_element_type=F32).astype(BF16)


def _proj_call(x, w_p, w_su_t, w_d):
    t = x.shape[0]
    tm, tn = 512, NP // 6
    return pl.pallas_call(
        _proj_kernel, name="in_proj",
        grid=(t // tm, NP // tn),
        in_specs=[pl.BlockSpec((tm, D_MODEL), lambda i, j: (i, 0)),
                  pl.BlockSpec((D_MODEL, tn), lambda i, j: (0, j)),
                  pl.BlockSpec((S5_WIDTH, D_MODEL), lambda i, j: (0, 0)),
                  pl.BlockSpec((D_MODEL, ND), lambda i, j: (0, 0))],
        out_specs=[pl.BlockSpec((tm, tn), lambda i, j: (i, j)),
                   pl.BlockSpec((S5_WIDTH, tm), lambda i, j: (0, i)),
                   pl.BlockSpec((ND // LANE, tm, LANE), lambda i, j: (0, i, 0))],
        out_shape=[jax.ShapeDtypeStruct((t, NP), BF16),
                   jax.ShapeDtypeStruct((S5_WIDTH, t), BF16),
                   jax.ShapeDtypeStruct((ND // LANE, t, LANE), F32)],
        scratch_shapes=[pltpu.VMEM((tm, D_MODEL), BF16)],
        compiler_params=_cparams(("parallel", "arbitrary")),
    )(x, w_p, w_su_t, w_d)


ATTN_TQ = 2 * A_RADIUS


def _banded_heads(q, kwin, vwin, bias_ref, first_key, sub_len):
    tq, nk = q.shape[0], kwin.shape[0]
    kidx = first_key + lax.broadcasted_iota(jnp.int32, (tq, nk), 1)
    valid = (kidx >= 0) & (kidx < sub_len)
    lane_head = lax.broadcasted_iota(jnp.int32, (1, q.shape[1]), 1) // A_HEAD_DIM
    scale = jnp.asarray(A_HEAD_DIM ** -0.5, BF16)
    heads = range(A_HEADS_PER_GROUP)
    qs = [jnp.where(lane_head == h, q, jnp.zeros_like(q)) * scale for h in heads]
    ss = [lax.dot_general(qh, kwin, (((1,), (1,)), ((), ())), preferred_element_type=F32) for qh in qs]
    ss = [jnp.where(valid, s + bias_ref[h], NEG_INF) for h, s in zip(heads, ss)]
    ms = [jnp.max(s, axis=-1, keepdims=True) for s in ss]
    ps = [jnp.exp(s - m) for s, m in zip(ss, ms)]
    dens = [jnp.sum(p, axis=-1, keepdims=True) for p in ps]
    ohs = [jnp.dot(p.astype(BF16), vwin, preferred_element_type=F32) for p in ps]
    acc = jnp.zeros((tq, q.shape[1]), F32)
    lse_acc = jnp.zeros((tq, q.shape[1]), F32)
    for h in heads:
        hm = lane_head == h
        acc = jnp.where(hm, ohs[h] / dens[h], acc)
        lse_acc = jnp.where(hm, ms[h] + jnp.log(dens[h]), lse_acc)
    return acc, lse_acc


def _attn_kernel(q_ref, kp_ref, kc_ref, kn_ref, vp_ref, vc_ref, vn_ref, bias_ref, o_ref, lse_ref, *, sub_len):
    tq, half = ATTN_TQ, A_RADIUS
    kwin = jnp.concatenate([kp_ref[tq - half:, :], kc_ref[...], kn_ref[:half, :]], axis=0)
    vwin = jnp.concatenate([vp_ref[tq - half:, :], vc_ref[...], vn_ref[:half, :]], axis=0)
    acc, lse = _banded_heads(q_ref[...], kwin, vwin, bias_ref, pl.program_id(1) * tq - half, sub_len)
    o_ref[...] = acc
    lse_ref[...] = lse


def _attn_call(proj, bias, b, s):
    tq = ATTN_TQ
    nqb = s // tq
    pv = proj.reshape(b, s, NP)
    qo, ko, vo = (_SEG[n] * LANE // GW for n in ("aq", "ak", "av"))
    blk = (None, tq, GW)
    cur = lambda off: (lambda bi, qi: (bi, qi, off))
    prv = lambda off: (lambda bi, qi: (bi, jnp.maximum(qi - 1, 0), off))
    nxt = lambda off: (lambda bi, qi: (bi, jnp.minimum(qi + 1, nqb - 1), off))
    out_map = lambda bi, qi: (bi, qi, 0)
    o, lse = pl.pallas_call(
        functools.partial(_attn_kernel, sub_len=s), name="window_attn_g0",
        grid=(b, nqb),
        in_specs=[pl.BlockSpec(blk, cur(qo)),
                  pl.BlockSpec(blk, prv(ko)), pl.BlockSpec(blk, cur(ko)), pl.BlockSpec(blk, nxt(ko)),
                  pl.BlockSpec(blk, prv(vo)), pl.BlockSpec(blk, cur(vo)), pl.BlockSpec(blk, nxt(vo)),
                  pl.BlockSpec((A_HEADS_PER_GROUP, tq, 2 * tq), lambda bi, qi: (0, 0, 0))],
        out_specs=[pl.BlockSpec(blk, out_map), pl.BlockSpec(blk, out_map)],
        out_shape=[jax.ShapeDtypeStruct((b, s, GW), F32), jax.ShapeDtypeStruct((b, s, GW), F32)],
        compiler_params=_cparams(("parallel", "parallel")),
    )(pv, pv, pv, pv, pv, pv, pv, bias)
    return o.reshape(b * s, GW), lse.reshape(b * s, GW)


def _dil_attn_kernel(q_ref, kp_ref, kc_ref, kn_ref, vp_ref, vc_ref, vn_ref, bias_ref, o_ref, lse_ref, *,
                     dil, sub_len):
    tq, half = ATTN_TQ, A_RADIUS
    first_key = pl.program_id(1) * tq - half

    def residue(r, carry):
        def rows(ref, n):
            return jnp.concatenate([ref[c, pl.ds(r, n, stride=dil), :] for c in range(GW // LANE)], axis=-1)

        q = rows(q_ref, tq).astype(BF16)
        kwin = jnp.concatenate([rows(kp_ref, half), rows(kc_ref, tq), rows(kn_ref, half)], axis=0).astype(BF16)
        vwin = jnp.concatenate([rows(vp_ref, half), rows(vc_ref, tq), rows(vn_ref, half)], axis=0).astype(BF16)
        acc, lse = _banded_heads(q, kwin, vwin, bias_ref, first_key, sub_len)
        for c in range(GW // LANE):
            o_ref[c, pl.ds(r, tq, stride=dil), :] = acc[:, c * LANE:(c + 1) * LANE]
            lse_ref[c, pl.ds(r, tq, stride=dil), :] = lse[:, c * LANE:(c + 1) * LANE]
        return carry

    lax.fori_loop(0, dil, residue, 0)


def _dil_attn_call(pd, bias, b, s, g, dil):
    tq = ATTN_TQ
    rows = tq * dil
    nblk = s // rows
    nslab = GW // LANE
    pv = pd.reshape(ND // LANE, b, s, LANE)
    qo, ko, vo = (3 * (g - 1) + j for j in range(3))
    blk, hblk = (nslab, None, rows, LANE), (nslab, None, rows // 2, LANE)
    cur = lambda off: (lambda bi, i: (off, bi, i, 0))
    prv = lambda off: (lambda bi, i: (off, bi, jnp.maximum(2 * i - 1, 0), 0))
    nxt = lambda off: (lambda bi, i: (off, bi, jnp.minimum(2 * i + 2, 2 * nblk - 1), 0))
    out_map = lambda bi, i: (0, bi, i, 0)
    o, lse = pl.pallas_call(
        functools.partial(_dil_attn_kernel, dil=dil, sub_len=s // dil), name=f"dilated_attn_g{g}",
        grid=(b, nblk),
        in_specs=[pl.BlockSpec(blk, cur(qo)),
                  pl.BlockSpec(hblk, prv(ko)), pl.BlockSpec(blk, cur(ko)), pl.BlockSpec(hblk, nxt(ko)),
                  pl.BlockSpec(hblk, prv(vo)), pl.BlockSpec(blk, cur(vo)), pl.BlockSpec(hblk, nxt(vo)),
                  pl.BlockSpec((A_HEADS_PER_GROUP, tq, 2 * tq), lambda bi, i: (0, 0, 0))],
        out_specs=[pl.BlockSpec(blk, out_map), pl.BlockSpec(blk, out_map)],
        out_shape=[jax.ShapeDtypeStruct((nslab, b, s, LANE), F32)] * 2,
        compiler_params=_cparams(("parallel", "parallel")),
    )(pv, pv, pv, pv, pv, pv, pv, bias)
    return o.reshape(nslab, b * s, LANE), lse.reshape(nslab, b * s, LANE)


def _t5_bucket(rel):
    nb = REL_BUCKETS // 2
    max_exact = nb // 2
    ret = jnp.where(rel > 0, nb, 0)
    n = jnp.abs(rel)
    nf = jnp.maximum(n, 1).astype(F32)
    large = max_exact + (jnp.log(nf / max_exact) / math.log(REL_MAX_DIST / max_exact)
                         * (nb - max_exact)).astype(jnp.int32)
    large = jnp.minimum(large, nb - 1)
    return ret + jnp.where(n < max_exact, n, large)


def _attn_bias_table(rel_bias, g, dil, tq=128):
    qi = jnp.arange(tq)[:, None]
    kj = jnp.arange(2 * tq)[None, :] - A_RADIUS
    rel = kj - qi
    tab = rel_bias[:, g * A_HEADS_PER_GROUP:(g + 1) * A_HEADS_PER_GROUP].astype(F32)
    bias = jnp.transpose(tab[_t5_bucket(rel * dil)], (2, 0, 1))
    return jnp.where((jnp.abs(rel) <= A_RADIUS)[None], bias, NEG_INF)


def _s5_toeplitz_kernel(bcf_ref, pwf_ref, bcb_ref, pwb_ref, d_ref, t_ref, gf_ref, gb_ref):
    gf_ref[...] = jnp.dot(bcf_ref[...], pwf_ref[...], precision=HIGHEST, preferred_element_type=F32)
    gb_ref[...] = jnp.dot(bcb_ref[...], pwb_ref[...], precision=HIGHEST, preferred_element_type=F32)
    c = S5_CHUNK
    row = lax.broadcasted_iota(jnp.int32, (c, c), 0)
    col = lax.broadcasted_iota(jnp.int32, (c, c), 1)

    def body(ci, carry):
        dval = d_ref[pl.ds(ci, 1), :]
        for co in range(S5_GROUP):
            r = ci * S5_GROUP + co
            gf = jnp.broadcast_to(gf_ref[pl.ds(r, 1), :], (c, c))
            gb = jnp.broadcast_to(gb_ref[pl.ds(r, 1), :], (c, c))
            tf = pltpu.roll(gf, 0, 1, stride=1, stride_axis=0)
            tb = pltpu.roll(gb, 1, 1, stride=1, stride_axis=0)
            tile = jnp.where(col >= row, tf, 0.0) + jnp.where(row >= col, tb, 0.0)
            tile = tile + jnp.where((row == col) & (ci == co), dval, 0.0)
            t_ref[pl.ds(pl.multiple_of(ci * c, c), c), co * c:(co + 1) * c] = tile.astype(BF16)
        return carry

    lax.fori_loop(0, S5_GROUP, body, 0)


def _s5_toeplitz_call(bcf, pwf, bcb, pwb, dskip):
    n = S5_GROUP * S5_CHUNK
    return pl.pallas_call(
        _s5_toeplitz_kernel, name="s5_toeplitz",
        grid=(S5_GROUPS,),
        in_specs=[pl.BlockSpec((None, S5_GROUP * S5_GROUP, 2 * S5_STATE), lambda g: (g, 0, 0)),
                  pl.BlockSpec((None, 2 * S5_STATE, S5_CHUNK), lambda g: (g, 0, 0)),
                  pl.BlockSpec((None, S5_GROUP * S5_GROUP, 2 * S5_STATE), lambda g: (g, 0, 0)),
                  pl.BlockSpec((None, 2 * S5_STATE, S5_CHUNK), lambda g: (g, 0, 0)),
                  pl.BlockSpec((None, S5_GROUP, LANE), lambda g: (g, 0, 0))],
        out_specs=pl.BlockSpec((None, n, n), lambda g: (g, 0, 0)),
        out_shape=jax.ShapeDtypeStruct((S5_GROUPS, n, n), BF16),
        scratch_shapes=[pltpu.VMEM((S5_GROUP * S5_GROUP, S5_CHUNK), F32),
                        pltpu.VMEM((S5_GROUP * S5_GROUP, S5_CHUNK), F32)],
        compiler_params=_cparams(("parallel",)),
    )(bcf, pwf, bcb, pwb, dskip)


def _s5_tables(lam_re, lam_im, log_step, b_re, b_im, c_re, c_im, d_skip):
    c = S5_CHUNK
    k = jnp.arange(c, dtype=F32)
    per_dir = []
    for direction in range(2):
        step = jnp.exp(log_step[direction].astype(F32))[:, None]
        lr, li = lam_re[direction].astype(F32), lam_im[direction].astype(F32)
        mag = jnp.exp(lr * step)
        ar, ai = mag * jnp.cos(li * step), mag * jnp.sin(li * step)
        nr, ni = ar - 1.0, ai
        den = lr * lr + li * li
        fr = (nr * lr + ni * li) / den
        fi = (ni * lr - nr * li) / den
        br, bi = b_re[direction].astype(F32), b_im[direction].astype(F32)
        bbr = fr[..., None] * br - fi[..., None] * bi
        bbi = fr[..., None] * bi + fi[..., None] * br
        cr, cim = c_re[direction].astype(F32), c_im[direction].astype(F32)

        def power(e, log_mag=lr * step, phase=li * step):
            m = jnp.exp(e[None, None, :] * log_mag[..., None])
            th = e[None, None, :] * phase[..., None]
            return m * jnp.cos(th), m * jnp.sin(th)

        per_dir.append(dict(bbr=bbr, bbi=bbi, cr=cr, cim=cim, power=power))

    def bc_table(d):
        bbr_t = jnp.transpose(d["bbr"], (0, 2, 1))[:, :, None, :]
        bbi_t = jnp.transpose(d["bbi"], (0, 2, 1))[:, :, None, :]
        cr, cim = d["cr"][:, None], d["cim"][:, None]
        re = bbr_t * cr - bbi_t * cim
        im = bbr_t * cim + bbi_t * cr
        return jnp.concatenate([re, -im], axis=-1).reshape(S5_GROUPS, S5_GROUP * S5_GROUP, 2 * S5_STATE)

    f, bw = per_dir
    pfr, pfi = f["power"](k)
    pbr, pbi = bw["power"](c - 1 - k)
    pwf = jnp.concatenate([pfr, pfi], axis=1)
    pwb = jnp.concatenate([pbr, pbi], axis=1)
    dsk = jnp.broadcast_to(d_skip.astype(F32).reshape(S5_GROUPS, S5_GROUP, 1), (S5_GROUPS, S5_GROUP, LANE))
    toep = _s5_toeplitz_call(bc_table(f), pwf, bc_table(bw), pwb, dsk)

    def w_in(d, e):
        pr, pi = d["power"](e)
        re = d["bbr"][:, :, :, None] * pr[:, :, None, :] - d["bbi"][:, :, :, None] * pi[:, :, None, :]
        im = d["bbr"][:, :, :, None] * pi[:, :, None, :] + d["bbi"][:, :, :, None] * pr[:, :, None, :]
        tr = lambda z: jnp.transpose(z, (0, 2, 3, 1)).reshape(S5_GROUPS, S5_GROUP * c, S5_STATE)
        return tr(re), tr(im)

    fre, fim = w_in(f, c - 1 - k)
    bre, bim = w_in(bw, k)
    win = jnp.concatenate([fre, bre, fim, bim], axis=-1).astype(BF16)

    def w_out(d, e):
        pr, pi = d["power"](e)
        re = d["cr"][:, :, :, None] * pr[:, None] - d["cim"][:, :, :, None] * pi[:, None]
        im = d["cr"][:, :, :, None] * pi[:, None] + d["cim"][:, :, :, None] * pr[:, None]
        tr = lambda z: jnp.transpose(z, (0, 2, 1, 3)).reshape(S5_GROUPS, S5_STATE, S5_GROUP * c)
        return tr(re), tr(-im)

    ofre, ofim = w_out(f, k + 1.0)
    obre, obim = w_out(bw, c - k)
    wout = jnp.concatenate([ofre, obre, ofim, obim], axis=1).astype(BF16)

    cc = jnp.asarray([float(c)], F32)
    afr, afi = f["power"](cc)
    abr, abi = bw["power"](cc)
    a_re = jnp.concatenate([afr[..., 0], abr[..., 0]], axis=-1).reshape(1, -1)
    a_im = jnp.concatenate([afi[..., 0], abi[..., 0]], axis=-1).reshape(1, -1)
    return toep, win, wout, a_re, a_im


def _s5_contrib_kernel(u_ref, win_ref, re_ref, im_ref):
    uc = jnp.concatenate([u_ref[ci] for ci in range(S5_GROUP)], axis=-1)
    res = jnp.dot(uc, win_ref[...], preferred_element_type=F32)
    nc, nb = re_ref.shape[0], re_ref.shape[1]
    half = 2 * S5_STATE
    for bi in range(nb):
        re_ref[:, bi, :] = res[bi * nc:(bi + 1) * nc, :half]
        im_ref[:, bi, :] = res[bi * nc:(bi + 1) * nc, half:]


def _s5_contrib_call(ut, win, b, nc):
    nch = b * nc
    half = 2 * S5_STATE
    uv = ut.reshape(S5_GROUPS, S5_GROUP, nch, S5_CHUNK)
    return pl.pallas_call(
        _s5_contrib_kernel, name="s5_contrib",
        grid=(S5_GROUPS,),
        in_specs=[pl.BlockSpec((None, S5_GROUP, nch, S5_CHUNK), lambda g: (g, 0, 0, 0)),
                  pl.BlockSpec((None, S5_GROUP * S5_CHUNK, 4 * S5_STATE), lambda g: (g, 0, 0))],
        out_specs=[pl.BlockSpec((nc, b, half), lambda g: (0, 0, g))] * 2,
        out_shape=[jax.ShapeDtypeStruct((nc, b, S5_GROUPS * half), F32)] * 2,
        compiler_params=_cparams(("parallel",)),
    )(uv, win)


def _s5_scan_kernel(cre_ref, cim_ref, are_ref, aim_ref, ore_ref, oim_ref):
    nc, nb, width = cre_ref.shape
    a_re, a_im = are_ref[...], aim_ref[...]
    lane = lax.broadcasted_iota(jnp.int32, (1, width), 1)
    fwd_lane = (lane % (2 * S5_STATE)) < S5_STATE
    zero = jnp.zeros((nb, width), F32)

    def advance(s, c):
        sr, si = s
        return a_re * sr - a_im * si + cre_ref[c], a_re * si + a_im * sr + cim_ref[c]

    def up(c, s):
        ore_ref[c] = s[0]
        oim_ref[c] = s[1]
        return advance(s, c)

    lax.fori_loop(0, nc, up, (zero, zero))

    def down(i, s):
        c = nc - 1 - i
        ore_ref[c] = jnp.where(fwd_lane, ore_ref[c], s[0])
        oim_ref[c] = jnp.where(fwd_lane, oim_ref[c], s[1])
        return advance(s, c)

    lax.fori_loop(0, nc, down, (zero, zero))


def _s5_scan_call(cre, cim, a_re, a_im):
    return pl.pallas_call(
        _s5_scan_kernel, name="s5_chunk_scan",
        out_shape=[jax.ShapeDtypeStruct(cre.shape, F32)] * 2,
        compiler_params=pltpu.CompilerParams(vmem_limit_bytes=VMEM_LIMIT),
    )(cre, cim, a_re, a_im)


def _gelu_tanh(y):
    return 0.5 * y * (1.0 + jnp.tanh(math.sqrt(2.0 / math.pi) * (y + 0.044715 * (y * y * y))))


def _s5_out_kernel(u_ref, t_ref, sre_ref, sim_ref, wout_ref, z_ref):
    uc = jnp.concatenate([u_ref[ci] for ci in range(S5_GROUP)], axis=-1)
    y = jnp.dot(uc, t_ref[...], preferred_element_type=F32)
    sp = jnp.concatenate([jnp.concatenate([sre_ref[:, bi, :], sim_ref[:, bi, :]], axis=-1)
                          for bi in range(sre_ref.shape[1])], axis=0)
    y = y + jnp.dot(sp.astype(BF16), wout_ref[...], preferred_element_type=F32)
    z = _gelu_tanh(y)
    for co in range(S5_GROUP):
        z_ref[co] = z[:, co * S5_CHUNK:(co + 1) * S5_CHUNK].astype(BF16)


def _s5_out_call(ut, toep, s_re, s_im, wout, b, nc):
    nch = b * nc
    uv = ut.reshape(S5_GROUPS, S5_GROUP, nch, S5_CHUNK)
    n = S5_GROUP * S5_CHUNK
    half = 2 * S5_STATE
    zt = pl.pallas_call(
        _s5_out_kernel, name="s5_out",
        grid=(S5_GROUPS,),
        in_specs=[pl.BlockSpec((None, S5_GROUP, nch, S5_CHUNK), lambda g: (g, 0, 0, 0)),
                  pl.BlockSpec((None, n, n), lambda g: (g, 0, 0)),
                  pl.BlockSpec((nc, b, half), lambda g: (0, 0, g)),
                  pl.BlockSpec((nc, b, half), lambda g: (0, 0, g)),
                  pl.BlockSpec((None, 4 * S5_STATE, n), lambda g: (g, 0, 0))],
        out_specs=pl.BlockSpec((None, S5_GROUP, nch, S5_CHUNK), lambda g: (g, 0, 0, 0)),
        out_shape=jax.ShapeDtypeStruct((S5_GROUPS, S5_GROUP, nch, S5_CHUNK), BF16),
        compiler_params=_cparams(("parallel",)),
    )(uv, toep, s_re, s_im, wout)
    return zt.reshape(S5_WIDTH, nch * S5_CHUNK)


def _ret_tables(s):
    c = RET_CHUNK
    half = RET_QK_DIM // 2
    inv = ROPE_BASE ** (-jnp.arange(half, dtype=F32) / half)
    ang = jnp.arange(s, dtype=F32)[:, None] * inv[None, :]
    cos = jnp.tile(jnp.cos(ang), (1, 2 * RET_HEADS))
    sin = jnp.tile(jnp.sin(ang), (1, 2 * RET_HEADS))
    hidx = jnp.arange(RET_HEADS, dtype=F32)
    lgf = jnp.log1p(-jnp.exp2(-5.0 - hidx))
    lgb = jnp.log1p(-jnp.exp2(-5.5 - hidx))
    j = jnp.arange(c, dtype=F32)
    rel = j[:, None] - j[None, :]
    intra = jnp.exp(jnp.abs(rel)[None] * jnp.where(rel[None] >= 0, lgf[:, None, None], lgb[:, None, None]))
    qk_f = jnp.repeat(lgf, RET_QK_DIM)[None, :]
    qk_b = jnp.repeat(lgb, RET_QK_DIM)[None, :]
    v_f = jnp.repeat(lgf, RET_V_DIM)[None, :]
    v_b = jnp.repeat(lgb, RET_V_DIM)[None, :]
    tabs = dict(
        cos=cos, sin=sin, intra=intra,
        tail_f=jnp.exp((c - 1 - j)[:, None] * qk_f), head_b=jnp.exp(j[:, None] * qk_b),
        decq_f=jnp.exp((j + 1.0)[:, None] * qk_f), decq_b=jnp.exp((c - j)[:, None] * qk_b),
        chunk_f=jnp.exp(c * v_f), chunk_b=jnp.exp(c * v_b),
        bd=(jnp.arange(RET_HEADS * RET_QK_DIM)[:, None] // RET_QK_DIM
            == jnp.arange(RET_HEADS * RET_V_DIM)[None, :] // RET_V_DIM).astype(F32))
    return tabs


def _rot(x_ref, xs_ref, cos_ref, sin_ref):
    return x_ref[...].astype(F32) * cos_ref[...] + xs_ref[...].astype(F32) * sin_ref[...]


def _ret_bwd_kernel(k_ref, ks_ref, v_ref, cos_ref, sin_ref, headb_ref, chunkb_ref, bd_ref, sb_ref, st_ref):
    @pl.when(pl.program_id(1) == 0)
    def _():
        st_ref[...] = jnp.zeros_like(st_ref)

    sb_ref[...] = st_ref[...].astype(BF16)
    kr = _rot(k_ref, ks_ref, cos_ref, sin_ref) * (RET_QK_DIM ** -0.5)
    kwt = jnp.transpose(kr * headb_ref[...]).astype(BF16)
    kv = jnp.dot(kwt, v_ref[...], preferred_element_type=F32)
    st_ref[...] = st_ref[...] * chunkb_ref[...] + kv * bd_ref[...]


def _ret_fwd_kernel(q_ref, qs_ref, k_ref, ks_ref, v_ref, g_ref, cos_ref, sin_ref, sb_ref, intra_ref,
                    tailf_ref, decqf_ref, decqb_ref, chunkf_ref, bd_ref, o_ref, st_ref):
    @pl.when(pl.program_id(1) == 0)
    def _():
        st_ref[...] = jnp.zeros_like(st_ref)

    qr = _rot(q_ref, qs_ref, cos_ref, sin_ref)
    kr = _rot(k_ref, ks_ref, cos_ref, sin_ref) * (RET_QK_DIM ** -0.5)
    qb, kb = qr.astype(BF16), kr.astype(BF16)
    vb = v_ref[...]
    lane_head = lax.broadcasted_iota(jnp.int32, (1, qb.shape[1]), 1) // RET_QK_DIM
    inter = jnp.dot((qr * decqf_ref[...]).astype(BF16), st_ref[...].astype(BF16), preferred_element_type=F32)
    inter = inter + jnp.dot((qr * decqb_ref[...]).astype(BF16), sb_ref[...], preferred_element_type=F32)
    heads = range(RET_HEADS)
    vcols = [slice(h * RET_V_DIM, (h + 1) * RET_V_DIM) for h in heads]
    qhs = [jnp.where(lane_head == h, qb, jnp.zeros_like(qb)) for h in heads]
    scs = [lax.dot_general(qh, kb, (((1,), (1,)), ((), ())), preferred_element_type=F32) * intra_ref[h]
           for h, qh in zip(heads, qhs)]
    ohs = [jnp.dot(sc.astype(BF16), vb[:, vs], preferred_element_type=F32) + inter[:, vs]
           for sc, vs in zip(scs, vcols)]
    mus = [jnp.mean(oh, axis=-1, keepdims=True) for oh in ohs]
    dds = [oh - mu for oh, mu in zip(ohs, mus)]
    vrs = [jnp.mean(dd * dd, axis=-1, keepdims=True) for dd in dds]
    on = jnp.concatenate([dd * lax.rsqrt(var + LN_EPS) for dd, var in zip(dds, vrs)], axis=-1)
    gg = g_ref[...].astype(F32)
    o_ref[...] = (on * (gg * jax.nn.sigmoid(gg))).astype(BF16)
    kwt = jnp.transpose(kr * tailf_ref[...]).astype(BF16)
    kv = jnp.dot(kwt, vb, preferred_element_type=F32)
    st_ref[...] = st_ref[...] * chunkf_ref[...] + kv * bd_ref[...]


def _retention_call(proj, tabs, b, s):
    c = RET_CHUNK
    nc = s // c
    qkw, vw = RET_HEADS * RET_QK_DIM, RET_HEADS * RET_V_DIM
    pv = proj.reshape(b, s, NP)
    qk_blk, v_blk = (None, c, qkw), (None, c, vw)
    qo, qso, ko, kso = (_SEG[n] * LANE // qkw for n in ("rq", "rqs", "rk", "rks"))
    vo, go = (_SEG[n] * LANE // vw for n in ("rv", "rg"))
    const2 = lambda shape: pl.BlockSpec(shape, lambda bi, i: (0, 0))
    rev = lambda off: (lambda bi, i: (bi, nc - 1 - i, off))
    fwd = lambda off: (lambda bi, i: (bi, i, off))
    sb = pl.pallas_call(
        _ret_bwd_kernel, name="retention_bwd_state",
        grid=(b, nc),
        in_specs=[pl.BlockSpec(qk_blk, rev(ko)), pl.BlockSpec(qk_blk, rev(kso)), pl.BlockSpec(v_blk, rev(vo)),
                  pl.BlockSpec((c, qkw), lambda bi, i: (nc - 1 - i, 0)),
                  pl.BlockSpec((c, qkw), lambda bi, i: (nc - 1 - i, 0)),
                  const2((c, qkw)), const2((1, vw)), const2((qkw, vw))],
        out_specs=pl.BlockSpec((None, None, qkw, vw), lambda bi, i: (bi, nc - 1 - i, 0, 0)),
        out_shape=jax.ShapeDtypeStruct((b, nc, qkw, vw), BF16),
        scratch_shapes=[pltpu.VMEM((qkw, vw), F32)],
        compiler_params=_cparams(("parallel", "arbitrary")),
    )(pv, pv, pv, tabs["cos"], tabs["sin"], tabs["head_b"], tabs["chunk_b"], tabs["bd"])
    yc = pl.pallas_call(
        _ret_fwd_kernel, name="retention_fwd",
        grid=(b, nc),
        in_specs=[pl.BlockSpec(qk_blk, fwd(qo)), pl.BlockSpec(qk_blk, fwd(qso)),
                  pl.BlockSpec(qk_blk, fwd(ko)), pl.BlockSpec(qk_blk, fwd(kso)),
                  pl.BlockSpec(v_blk, fwd(vo)), pl.BlockSpec(v_blk, fwd(go)),
                  pl.BlockSpec((c, qkw), lambda bi, i: (i, 0)), pl.BlockSpec((c, qkw), lambda bi, i: (i, 0)),
                  pl.BlockSpec((None, None, qkw, vw), lambda bi, i: (bi, i, 0, 0)),
                  pl.BlockSpec((RET_HEADS, c, c), lambda bi, i: (0, 0, 0)),
                  const2((c, qkw)), const2((c, qkw)), const2((c, qkw)), const2((1, vw)), const2((qkw, vw))],
        out_specs=pl.BlockSpec((None, c, vw), lambda bi, i: (bi, i, 0)),
        out_shape=jax.ShapeDtypeStruct((b, s, vw), BF16),
        scratch_shapes=[pltpu.VMEM((qkw, vw), F32)],
        compiler_params=_cparams(("parallel", "arbitrary")),
    )(pv, pv, pv, pv, pv, pv, tabs["cos"], tabs["sin"], sb, tabs["intra"],
      tabs["tail_f"], tabs["decq_f"], tabs["decq_b"], tabs["chunk_f"], tabs["bd"])
    return yc.reshape(b * s, vw)


def _mm_kernel(x_ref, w_ref, o_ref):
    o_ref[...] = jnp.dot(x_ref[...].astype(BF16), w_ref[...], preferred_element_type=F32).astype(o_ref.dtype)


def _mem_kv_call(mem2d, w_kv):
    m, n = mem2d.shape[0], w_kv.shape[1]
    tm = 512
    return pl.pallas_call(
        _mm_kernel, name="mem_kv_proj",
        grid=(m // tm,),
        in_specs=[pl.BlockSpec((tm, D_MODEL), lambda i: (i, 0)),
                  pl.BlockSpec((D_MODEL, n), lambda i: (0, 0))],
        out_specs=pl.BlockSpec((tm, n), lambda i: (i, 0)),
        out_shape=jax.ShapeDtypeStruct((m, n), BF16),
        compiler_params=_cparams(("parallel",)),
    )(mem2d, w_kv)


def _mem_attn_kernel(q_ref, k_ref, v_ref, o_ref):
    q = q_ref[...]
    cols = [slice(h * MEM_HEAD_PAD, (h + 1) * MEM_HEAD_PAD) for h in range(MEM_HEADS)]
    ss = [lax.dot_general(q[:, hs], k_ref[:, hs], (((1,), (1,)), ((), ())),
                          preferred_element_type=F32) * (MEM_HEAD_DIM ** -0.5) for hs in cols]
    ms = [jnp.max(s, axis=-1, keepdims=True) for s in ss]
    es = [jnp.exp(s - m) for s, m in zip(ss, ms)]
    ps = [e / jnp.sum(e, axis=-1, keepdims=True) for e in es]
    outs = [jnp.dot(p.astype(BF16), v_ref[:, hs], preferred_element_type=F32) for p, hs in zip(ps, cols)]
    o_ref[...] = jnp.concatenate(outs, axis=-1).astype(BF16)


def _mem_attn_call(proj, kv, b, s):
    tq = 256
    w = MEM_HEADS * MEM_HEAD_PAD
    pv = proj.reshape(b, s, NP)
    kvv = kv.reshape(b, N_MEM, 2 * w)
    qo = _SEG["mq"] * LANE // w
    yd = pl.pallas_call(
        _mem_attn_kernel, name="mem_attn",
        grid=(b, s // tq),
        in_specs=[pl.BlockSpec((None, tq, w), lambda bi, i: (bi, i, qo)),
                  pl.BlockSpec((None, N_MEM, w), lambda bi, i: (bi, 0, 0)),
                  pl.BlockSpec((None, N_MEM, w), lambda bi, i: (bi, 0, 1))],
        out_specs=pl.BlockSpec((None, tq, w), lambda bi, i: (bi, i, 0)),
        out_shape=jax.ShapeDtypeStruct((b, s, w), BF16),
        compiler_params=_cparams(("parallel", "parallel")),
    )(pv, kvv, kvv)
    return yd.reshape(b * s, w)


def _merge_kernel(x_ref, ao0_ref, ao1_ref, ao2_ref, al0_ref, al1_ref, al2_ref, zt_ref, yc_ref, yd_ref, gl_ref,
                  wglut_ref, bglu_ref, wba_ref, wbb_ref, wbc_ref, wbd_ref, wout_ref, g1_ref, b1_ref,
                  rwh_ref, rwl_ref, rb_ref,
                  x1_ref, x1p_ref, route_ref, counts_ref, carry_ref):
    tm = x_ref.shape[0]

    @pl.when(pl.program_id(0) == 0)
    def _():
        carry_ref[...] = jnp.zeros_like(carry_ref)

    slabs = lambda ref: jnp.concatenate([ref[c] for c in range(ref.shape[0])], axis=-1)
    lses = [al0_ref[...], slabs(al1_ref), slabs(al2_ref)]
    aos = [ao0_ref[...], slabs(ao1_ref), slabs(ao2_ref)]
    mx = jnp.maximum(jnp.maximum(lses[0], lses[1]), lses[2])
    es = [jnp.exp(l - mx) for l in lses]
    tot = es[0] + es[1] + es[2]
    ya = jnp.concatenate([o * (e / tot) for o, e in zip(aos, es)], axis=-1).astype(BF16)
    zt = zt_ref[...]
    glu = jnp.dot(wglut_ref[...], zt, preferred_element_type=F32) + bglu_ref[...]
    ybt = zt.astype(F32) * jax.nn.sigmoid(glu)
    yb = jnp.transpose(ybt).astype(BF16)
    merged = jnp.zeros((tm, D_MODEL), F32)
    for n, (y, w_ref) in enumerate(((ya, wba_ref), (yb, wbb_ref), (yc_ref[...], wbc_ref), (yd_ref[...], wbd_ref))):
        bo = jnp.dot(y, w_ref[...], preferred_element_type=F32)
        gate = jax.nn.sigmoid(gl_ref[:, n * D_MODEL:(n + 1) * D_MODEL].astype(F32))
        merged = merged + gate * bo
    h = DEEPNORM_ALPHA * x_ref[...] + jnp.dot(merged.astype(BF16), wout_ref[...], preferred_element_type=F32)
    x1 = _ln(h, g1_ref[...], b1_ref[...])
    x1_ref[...] = x1
    x1p_ref[...] = _pack_bf16_pairs(x1)

    x_hi = x1.astype(BF16)
    x_lo = (x1 - x_hi.astype(F32)).astype(BF16)
    logits = (jnp.dot(x_hi, rwh_ref[...], preferred_element_type=F32)
              + jnp.dot(x_lo, rwh_ref[...], preferred_element_type=F32)
              + jnp.dot(x_hi, rwl_ref[...], preferred_element_type=F32)) + rb_ref[...]
    lane = lax.broadcasted_iota(jnp.int32, (tm, LANE), 1)
    work = logits
    sel, vals, hots = [], [], []
    for _ in range(TOP_K):
        mk = jnp.max(work, axis=-1, keepdims=True)
        ik = jnp.min(jnp.where(work == mk, lane, LANE), axis=-1, keepdims=True)
        hot = lane == ik
        work = jnp.where(hot, -jnp.inf, work)
        sel.append(ik)
        vals.append(mk)
        hots.append(hot)
    ex = [jnp.exp(v - vals[0]) for v in vals]
    den = ex[0] + ex[1] + ex[2] + ex[3]
    chosen = (hots[0] | hots[1] | hots[2] | hots[3]).astype(F32)
    r_i = lax.broadcasted_iota(jnp.int32, (tm, tm), 0)
    c_i = lax.broadcasted_iota(jnp.int32, (tm, tm), 1)
    strict_lower = (c_i < r_i).astype(BF16)
    before = jnp.dot(strict_lower, chosen.astype(BF16), preferred_element_type=F32) + carry_ref[0:1, :]
    route = jnp.zeros((tm, LANE), F32)
    for k in range(TOP_K):
        rank = jnp.sum(jnp.where(hots[k], before, 0.0), axis=-1, keepdims=True)
        route = jnp.where(lane == k, sel[k].astype(F32), route)
        route = jnp.where(lane == TOP_K + k, ex[k] / den, route)
        route = jnp.where(lane == 2 * TOP_K + k, rank, route)
    route_ref[...] = route
    new_carry = carry_ref[0:1, :] + jnp.sum(chosen, axis=0, keepdims=True)
    carry_ref[...] = jnp.broadcast_to(new_carry, carry_ref.shape)
    counts_ref[...] = jnp.broadcast_to(new_carry, counts_ref.shape)


def _merge_call(x, aos, alses, zt, yc, yd, proj, lw):
    t = x.shape[0]
    tm = 256
    gw = GW
    gate_blk = 4 * D_MODEL
    row = lambda w: pl.BlockSpec((tm, w), lambda i: (i, 0))
    slab = pl.BlockSpec((gw // LANE, tm, LANE), lambda i: (0, i, 0))
    full = lambda a: pl.BlockSpec(a.shape, lambda i: (0,) * a.ndim)
    consts = (lw["w_glu_t"], lw["b_glu"], lw["wb_a"], lw["wb_b"], lw["wb_c"], lw["wb_d"], lw["w_out"],
              lw["ln1_g"], lw["ln1_b"], lw["router_w_hi"], lw["router_w_lo"], lw["router_b"])
    return pl.pallas_call(
        _merge_kernel, name="branch_merge_router",
        grid=(t // tm,),
        in_specs=[row(D_MODEL)] + [row(gw), slab, slab] * 2
                 + [pl.BlockSpec((S5_WIDTH, tm), lambda i: (0, i)),
                  row(RET_HEADS * RET_V_DIM), row(MEM_HEADS * MEM_HEAD_PAD),
                  pl.BlockSpec((tm, gate_blk), lambda i: (i, _SEG["gate"] * LANE // gate_blk))]
                 + [full(a) for a in consts],
        out_specs=[row(D_MODEL), row(D_MODEL // 2), row(LANE), pl.BlockSpec((8, LANE), lambda i: (0, 0))],
        out_shape=[jax.ShapeDtypeStruct((t, D_MODEL), F32),
                   jax.ShapeDtypeStruct((t, D_MODEL // 2), jnp.uint32),
                   jax.ShapeDtypeStruct((t, LANE), F32),
                   jax.ShapeDtypeStruct((8, LANE), F32)],
        scratch_shapes=[pltpu.VMEM((8, LANE), F32)],
        compiler_params=_cparams(("arbitrary",)),
    )(x, *aos, *alses, zt, yc, yd, proj, *consts)


def _pack_bf16_pairs(x):
    n = x.shape[1] // 2
    hi = lax.bitcast_convert_type(x[:, :n].astype(BF16).astype(F32), jnp.uint32)
    lo = lax.bitcast_convert_type(x[:, n:].astype(BF16).astype(F32), jnp.uint32)
    return hi | (lo >> 16)


def _unpack_bf16_pairs(w):
    hi = lax.bitcast_convert_type(w & jnp.uint32(0xFFFF0000), F32)
    lo = lax.bitcast_convert_type(w << 16, F32)
    return jnp.concatenate([hi, lo], axis=-1)


def _sc_index_rows(idx):
    window = SC_GATHER_WINDOW
    return jnp.pad(idx.reshape(-1, window), ((0, 0), (0, LANE - window)))


def _sc_scatter_rows(x, idx, n_out):
    n, d = x.shape
    m = idx.shape[0]
    window = SC_GATHER_WINDOW
    src_steps = n // window
    mesh = plsc.VectorSubcoreMesh(core_axis_name="core", subcore_axis_name="subcore")

    @pl.kernel(out_type=jax.ShapeDtypeStruct((n_out, d), x.dtype), mesh=mesh, scratch_types=[])
    def scatter_kernel(x_hbm, i_hbm, o_hbm):
        def body(x_vmem, i_vmem):
            pltpu.sync_copy(x_vmem, o_hbm.at[i_vmem.at[0, pl.ds(0, window)]])

        pltpu.emit_pipeline(
            body,
            grid=(m // window,),
            in_specs=[pl.BlockSpec((window, d), index_map=lambda i: (i % src_steps, 0)),
                      pl.BlockSpec((1, LANE), index_map=lambda i: (i, 0))],
            out_specs=[],
            core_axis_name=("core", "subcore"),
            dimension_semantics=(pltpu.PARALLEL,),
            trace_scopes=False,
        )(x_hbm, i_hbm)

    return scatter_kernel(x, _sc_index_rows(idx))


def _sc_gather_rows(x, idx):
    m, d = idx.shape[0], x.shape[1]
    window = SC_GATHER_WINDOW
    mesh = plsc.VectorSubcoreMesh(core_axis_name="core", subcore_axis_name="subcore")

    @pl.kernel(out_type=jax.ShapeDtypeStruct((m, d), x.dtype), mesh=mesh, scratch_types=[])
    def gather_kernel(x_hbm, i_hbm, o_hbm):
        def body(i_vmem, o_vmem):
            pltpu.sync_copy(x_hbm.at[i_vmem.at[0, pl.ds(0, window)]], o_vmem)

        pltpu.emit_pipeline(
            body,
            grid=(m // window,),
            in_specs=[pl.BlockSpec((1, LANE), index_map=lambda i: (i, 0))],
            out_specs=[pl.BlockSpec((window, d), index_map=lambda i: (i, 0))],
            core_axis_name=("core", "subcore"),
            dimension_semantics=(pltpu.PARALLEL,),
            trace_scopes=False,
        )(i_hbm, o_hbm)

    return gather_kernel(x, _sc_index_rows(idx))


def _moe_expert_kernel(be_ref, nu_ref, nv_ref, x_ref, w1_ref, b1_ref, w2_ref, b2_ref, o_ref, w1s_ref, w2s_ref):
    i = pl.program_id(0)

    @pl.when((i == 0) | (be_ref[i] != be_ref[jnp.maximum(i - 1, 0)]))
    def _():
        w1s_ref[...] = w1_ref[...].astype(BF16)
        w2s_ref[...] = w2_ref[...].astype(BF16)

    @pl.when(i < nu_ref[0])
    def _():
        rows = lax.broadcasted_iota(jnp.int32, (x_ref.shape[0], 1), 0)
        x = jnp.where(rows < nv_ref[i], _unpack_bf16_pairs(x_ref[...]), 0.0)
        h = jnp.dot(x.astype(BF16), w1s_ref[...], preferred_element_type=F32) + b1_ref[...]
        gate = jnp.minimum(h[:, :D_FF], SWIGLU_LIMIT)
        up = jnp.clip(h[:, D_FF:], -SWIGLU_LIMIT, SWIGLU_LIMIT)
        hid = gate * jax.nn.sigmoid(SWIGLU_ALPHA * gate) * (up + 1.0)
        y = jnp.dot(hid.astype(BF16), w2s_ref[...], preferred_element_type=F32) + b2_ref[...]
        o_ref[...] = _pack_bf16_pairs(y)

    @pl.when(i >= nu_ref[0])
    def _():
        o_ref[...] = jnp.zeros_like(o_ref)


def _moe_expert_call(xg, block_expert, n_used, n_valid, layer, w1, b1, w2, b2):
    bm = MOE_BLOCK_ROWS
    n_blocks = xg.shape[0] // bm
    half = D_MODEL // 2
    grid_spec = pltpu.PrefetchScalarGridSpec(
        num_scalar_prefetch=3, grid=(n_blocks,),
        in_specs=[pl.BlockSpec((bm, half), lambda i, be, nu, nv: (i, 0)),
                  pl.BlockSpec((None, None, D_MODEL, 2 * D_FF), lambda i, be, nu, nv: (layer, be[i], 0, 0)),
                  pl.BlockSpec((None, 1, 2 * D_FF), lambda i, be, nu, nv: (be[i], 0, 0)),
                  pl.BlockSpec((None, None, D_FF, D_MODEL), lambda i, be, nu, nv: (layer, be[i], 0, 0)),
                  pl.BlockSpec((None, 1, D_MODEL), lambda i, be, nu, nv: (be[i], 0, 0))],
        out_specs=pl.BlockSpec((bm, half), lambda i, be, nu, nv: (i, 0)),
        scratch_shapes=[pltpu.VMEM((D_MODEL, 2 * D_FF), BF16), pltpu.VMEM((D_FF, D_MODEL), BF16)])
    return pl.pallas_call(
        _moe_expert_kernel, name="moe_experts",
        grid_spec=grid_spec,
        out_shape=jax.ShapeDtypeStruct((n_blocks * bm, half), jnp.uint32),
        compiler_params=_cparams(("arbitrary",)),
    )(block_expert, n_used, n_valid, xg, w1, b1, w2, b2)


def _moe_combine_kernel(x_ref, route_ref, y0_ref, y1_ref, y2_ref, y3_ref, g_ref, b_ref, o_ref):
    route = route_ref[...]
    moe = jnp.zeros(x_ref.shape, F32)
    for k, y_ref in enumerate((y0_ref, y1_ref, y2_ref, y3_ref)):
        moe = moe + route[:, TOP_K + k:TOP_K + k + 1] * _unpack_bf16_pairs(y_ref[...])
    o_ref[...] = _ln(DEEPNORM_ALPHA * x_ref[...] + moe, g_ref[...], b_ref[...])


def _moe_combine_call(x1, route, yg, g2, b2):
    t = x1.shape[0]
    tm = 256
    nt = t // tm
    ysel = lambda k: pl.BlockSpec((tm, D_MODEL // 2), lambda i: (k * nt + i, 0))
    return pl.pallas_call(
        _moe_combine_kernel, name="moe_combine_ln",
        grid=(nt,),
        in_specs=[pl.BlockSpec((tm, D_MODEL), lambda i: (i, 0)),
                  pl.BlockSpec((tm, LANE), lambda i: (i, 0))]
                 + [ysel(k) for k in range(TOP_K)]
                 + [pl.BlockSpec((1, D_MODEL), lambda i: (0, 0)),
                    pl.BlockSpec((1, D_MODEL), lambda i: (0, 0))],
        out_specs=pl.BlockSpec((tm, D_MODEL), lambda i: (i, 0)),
        out_shape=jax.ShapeDtypeStruct((t, D_MODEL), F32),
        compiler_params=_cparams(("parallel",)),
    )(x1, route, yg, yg, yg, yg, g2, b2)


def _moe(x1, x1p, route, counts, lw):
    t = x1.shape[0]
    bm = MOE_BLOCK_ROWS
    n_assign = t * TOP_K
    n_blocks = n_assign // bm + N_EXPERTS
    idx = route[:, :TOP_K].astype(jnp.int32)
    rank = route[:, 2 * TOP_K:3 * TOP_K].astype(jnp.int32)
    cnt = counts[0, :N_EXPERTS].astype(jnp.int32)
    padded = (cnt + bm - 1) // bm * bm
    pad_ends = jnp.cumsum(padded)
    pad_starts = pad_ends - padded
    dest = jnp.transpose(jnp.take(pad_starts, idx) + rank).reshape(-1)
    block_start = jnp.arange(n_blocks, dtype=jnp.int32) * bm
    block_expert = jnp.minimum(jnp.sum(pad_ends[None, :] <= block_start[:, None], axis=1),
                               N_EXPERTS - 1).astype(jnp.int32)
    n_valid = jnp.clip(jnp.take(cnt, block_expert) - (block_start - jnp.take(pad_starts, block_expert)), 0, bm)
    n_used = (pad_ends[-1:] // bm).astype(jnp.int32)
    xg = _sc_scatter_rows(x1p, dest, n_blocks * bm)
    yb = _moe_expert_call(xg, block_expert, n_used, n_valid.astype(jnp.int32), lw["layer"],
                          lw["moe_w1"], lw["moe_b1"], lw["moe_w2"], lw["moe_b2"])
    yg = _sc_gather_rows(yb, dest)
    return _moe_combine_call(x1, route, yg, lw["ln2_g"], lw["ln2_b"])


def _rotate_half_cols(w):
    half = RET_QK_DIM // 2
    w4 = w.reshape(w.shape[0], RET_HEADS, 2, half)
    return jnp.concatenate([-w4[:, :, 1:], w4[:, :, :1]], axis=2).reshape(w.shape)


def _pad_heads(w, axis):
    shp = list(w.shape)
    shp[axis:axis + 1] = [MEM_HEADS, MEM_HEAD_DIM]
    w = w.reshape(shp)
    pad = [(0, 0)] * w.ndim
    pad[axis + 1] = (0, MEM_HEAD_PAD - MEM_HEAD_DIM)
    w = jnp.pad(w, pad)
    shp[axis:axis + 2] = [MEM_HEADS * MEM_HEAD_PAD]
    return w.reshape(shp)


def _layer_weights(l, p):
    w_in = p["w_in"][l]
    aw, sw, qkw, vw, mw = 768, S5_WIDTH, RET_HEADS * RET_QK_DIM, RET_HEADS * RET_V_DIM, 768
    offs = [0]
    for wdt in (aw, aw, aw, sw, qkw, qkw, vw, vw, mw, 4 * D_MODEL):
        offs.append(offs[-1] + wdt)
    aq, ak, av, su, rq, rk, rv, rg, mq, gate = (w_in[:, offs[i]:offs[i + 1]] for i in range(10))
    seg = dict(gate=gate, mq=_pad_heads(mq, 1), aq=aq[:, :GW], ak=ak[:, :GW], av=av[:, :GW],
               pad=jnp.zeros((D_MODEL, 2 * LANE), F32), rv=rv, rg=rg,
               rq=rq, rqs=_rotate_half_cols(rq), rk=rk, rks=_rotate_half_cols(rk))
    order = sorted(_SEG, key=_SEG.get)
    w_p = jnp.concatenate([seg[n] for n in order], axis=1).astype(BF16)
    assert w_p.shape[1] == NP
    w_d = jnp.concatenate([w[:, g * GW:(g + 1) * GW] for g in (1, 2) for w in (aq, ak, av)], axis=1).astype(BF16)
    wkv = p["w_mem_kv"][l]
    w_kv = jnp.concatenate([_pad_heads(wkv[:, :768], 1), _pad_heads(wkv[:, 768:], 1)], axis=1).astype(BF16)
    wb = p["w_branch"][l]
    rw = jnp.pad(p["router_w"][l].astype(F32), ((0, 0), (0, LANE - N_EXPERTS)))
    rb = jnp.pad(p["router_b"][l].astype(F32), (0, LANE - N_EXPERTS), constant_values=NEG_INF)
    return dict(
        w_p=w_p, w_d=w_d, w_su_t=jnp.transpose(su).astype(BF16), w_kv=w_kv,
        w_glu_t=jnp.transpose(p["s5_w_glu"][l]).astype(BF16), b_glu=p["s5_b_glu"][l].astype(F32).reshape(-1, 1),
        wb_a=wb[0].astype(BF16), wb_b=wb[1].astype(BF16), wb_c=wb[2].astype(BF16),
        wb_d=_pad_heads(wb[3], 0).astype(BF16), w_out=p["w_out"][l].astype(BF16),
        ln1_g=p["ln1_g"][l].reshape(1, -1), ln1_b=p["ln1_b"][l].reshape(1, -1),
        router_w_hi=rw.astype(BF16), router_w_lo=(rw - rw.astype(BF16).astype(F32)).astype(BF16),
        router_b=rb.reshape(1, -1),
        layer=l, moe_w1=p["moe_w1"], moe_b1=p["moe_b1"][l].astype(F32).reshape(N_EXPERTS, 1, -1),
        moe_w2=p["moe_w2"], moe_b2=p["moe_b2"][l].astype(F32).reshape(N_EXPERTS, 1, -1),
        ln2_g=p["ln2_g"][l].reshape(1, -1), ln2_b=p["ln2_b"][l].reshape(1, -1),
        s5=_s5_tables(p["s5_lam_re"][l], p["s5_lam_im"][l], p["s5_log_step"][l], p["s5_b_re"][l],
                      p["s5_b_im"][l], p["s5_c_re"][l], p["s5_c_im"][l], p["s5_d"][l]))


def _trunk_layer(x, mem2d, b, s, lw, attn_bias, ret_tabs):
    proj, ut, pd = _proj_call(x, lw["w_p"], lw["w_su_t"], lw["w_d"])
    outs = [_attn_call(proj, attn_bias[0], b, s)]
    outs += [_dil_attn_call(pd, attn_bias[g], b, s, g, A_GROUPS[g][1]) for g in (1, 2)]
    aos, alses = [o for o, _ in outs], [l for _, l in outs]
    toep, win, wout, a_re, a_im = lw["s5"]
    nc = s // S5_CHUNK
    c_re, c_im = _s5_contrib_call(ut, win, b, nc)
    s_re, s_im = _s5_scan_call(c_re, c_im, a_re, a_im)
    zt = _s5_out_call(ut, toep, s_re, s_im, wout, b, nc)
    yc = _retention_call(proj, ret_tabs, b, s)
    kv = _mem_kv_call(mem2d, lw["w_kv"])
    yd = _mem_attn_call(proj, kv, b, s)
    x1, x1p, route, counts = _merge_call(x, aos, alses, zt, yc, yd, proj, lw)
    return _moe(x1, x1p, route, counts, lw)


def kernel(x_prompt, x_sample, mem_prompt, mem_sample, ln_in_g, ln_in_b, rel_bias, w_in, s5_lam_re, s5_lam_im, s5_log_step, s5_b_re, s5_b_im, s5_c_re, s5_c_im, s5_d, s5_w_glu, s5_b_glu, w_mem_kv, w_branch, w_out, ln1_g, ln1_b, router_w, router_b, moe_w1, moe_b1, moe_w2, moe_b2, ln2_g, ln2_b):
    p = dict(w_in=w_in, s5_lam_re=s5_lam_re, s5_lam_im=s5_lam_im, s5_log_step=s5_log_step, s5_b_re=s5_b_re,
             s5_b_im=s5_b_im, s5_c_re=s5_c_re, s5_c_im=s5_c_im, s5_d=s5_d, s5_w_glu=s5_w_glu, s5_b_glu=s5_b_glu,
             w_mem_kv=w_mem_kv, w_branch=w_branch, w_out=w_out, ln1_g=ln1_g, ln1_b=ln1_b, router_w=router_w,
             router_b=router_b, moe_w1=moe_w1, moe_b1=moe_b1, moe_w2=moe_w2, moe_b2=moe_b2, ln2_g=ln2_g,
             ln2_b=ln2_b)
    attn_bias = [_attn_bias_table(rel_bias, g, dil) for g, (_, dil) in enumerate(A_GROUPS)]
    trunks = []
    for x, mem in ((x_prompt, mem_prompt), (x_sample, mem_sample)):
        b, s, _ = x.shape
        trunks.append(dict(x=_layer_norm_call(x.reshape(b * s, D_MODEL), ln_in_g, ln_in_b),
                           mem=mem.reshape(b * N_MEM, D_MODEL), b=b, s=s, tabs=_ret_tables(s)))
    for l in range(DEPTH):
        lw = _layer_weights(l, p)
        for tr in trunks:
            tr["x"] = _trunk_layer(tr["x"], tr["mem"], tr["b"], tr["s"], lw, attn_bias, tr["tabs"])
    return tuple(tr["x"].reshape(tr["b"], tr["s"], D_MODEL) for tr in trunks)
```

```python
import functools
import math

import jax
import jax.numpy as jnp
from jax import lax
from jax.experimental import pallas as pl
from jax.experimental.pallas import tpu as pltpu
from jax.experimental.pallas import tpu_sc as plsc

F32 = jnp.float32
BF16 = jnp.bfloat16
HIGHEST = lax.Precision.HIGHEST

D_MODEL = 1024
DEPTH = 4
N_MEM = 256
A_GROUPS = ((128, 1), (512, 4), (2048, 16))
A_HEADS_PER_GROUP = 4
A_HEADS = 12
A_HEAD_DIM = 64
A_RADIUS = 64
REL_BUCKETS = 32
REL_MAX_DIST = 1024
S5_GROUP = 16
S5_WIDTH = 768
S5_GROUPS = 48
S5_STATE = 64
RET_HEADS = 6
RET_QK_DIM = 64
RET_V_DIM = 128
RET_CHUNK = 128
ROPE_BASE = 10000.0
MEM_HEADS = 4
MEM_HEAD_DIM = 192
MEM_HEAD_PAD = 256
N_EXPERTS = 32
TOP_K = 4
D_FF = 1024
SWIGLU_LIMIT = 7.0
SWIGLU_ALPHA = 1.702
LN_EPS = 1e-5
DEEPNORM_ALPHA = (2 * DEPTH) ** 0.25
NEG_INF = -1e30

LANE = 128
VMEM_LIMIT = 56 * 1024 * 1024
S5_CHUNK = 128
MOE_BLOCK_ROWS = 512
SC_GATHER_WINDOW = 64

_SEG = {}
_off = 0
for _name, _width, _align in (
        ("gate", 32, 8), ("mq", 8, 8), ("aq", 2, 2), ("ak", 2, 2), ("av", 2, 2), ("pad", 2, 2),
        ("rv", 6, 6), ("rg", 6, 6), ("rq", 3, 3), ("rqs", 3, 3), ("rk", 3, 3), ("rks", 3, 3)):
    assert _off % _align == 0, (_name, _off)
    _SEG[_name] = _off
    _off += _width
NP_UNITS = _off
NP = NP_UNITS * LANE
assert NP % 256 == 0
GW = A_HEADS_PER_GROUP * A_HEAD_DIM
ND = 2 * 3 * GW


def _cparams(sem, vmem=VMEM_LIMIT):
    return pltpu.CompilerParams(dimension_semantics=sem, vmem_limit_bytes=vmem)


def _sigmoid(x):
    return 0.5 * jnp.tanh(0.5 * x) + 0.5


def _ln(h, g, b):
    mu = jnp.mean(h, axis=-1, keepdims=True)
    d = h - mu
    var = jnp.mean(d * d, axis=-1, keepdims=True)
    return d * lax.rsqrt(var + LN_EPS) * g + b


def _ln_kernel(x_ref, g_ref, b_ref, o_ref):
    o_ref[...] = _ln(x_ref[...], g_ref[...], b_ref[...])


def _layer_norm_call(x, g, b):
    t = x.shape[0]
    tm = 512
    return pl.pallas_call(
        _ln_kernel, name="input_ln",
        grid=(t // tm,),
        in_specs=[pl.BlockSpec((tm, D_MODEL), lambda i: (i, 0)),
                  pl.BlockSpec((1, D_MODEL), lambda i: (0, 0)),
                  pl.BlockSpec((1, D_MODEL), lambda i: (0, 0))],
        out_specs=pl.BlockSpec((tm, D_MODEL), lambda i: (i, 0)),
        out_shape=jax.ShapeDtypeStruct((t, D_MODEL), F32),
        compiler_params=_cparams(("parallel",)),
    )(x, g.reshape(1, -1), b.reshape(1, -1))


def _proj_kernel(x_ref, w_ref, wsut_ref, wd_ref, p_ref, ut_ref, pd_ref, xb_ref):
    @pl.when(pl.program_id(1) == 0)
    def _():
        xb = x_ref[...].astype(BF16)
        xb_ref[...] = xb
        ut_ref[...] = lax.dot_general(wsut_ref[...], xb, (((1,), (1,)), ((), ())),
                                      preferred_element_type=F32).astype(BF16)
        pd = jnp.dot(xb, wd_ref[...], preferred_element_type=F32)
        for c in range(ND // LANE):
            pd_ref[c] = pd[:, c * LANE:(c + 1) * LANE]

    p_ref[...] = jnp.dot(xb_ref[...], w_ref[...], preferred_element_type=F32).astype(BF16)


def _proj_call(x, w_p, w_su_t, w_d):
    t = x.shape[0]
    tm, tn = 512, NP // 3
    return pl.pallas_call(
        _proj_kernel, name="in_proj",
        grid=(t // tm, NP // tn),
        in_specs=[pl.BlockSpec((tm, D_MODEL), lambda i, j: (i, 0)),
                  pl.BlockSpec((D_MODEL, tn), lambda i, j: (0, j)),
                  pl.BlockSpec((S5_WIDTH, D_MODEL), lambda i, j: (0, 0)),
                  pl.BlockSpec((D_MODEL, ND), lambda i, j: (0, 0))],
        out_specs=[pl.BlockSpec((tm, tn), lambda i, j: (i, j)),
                   pl.BlockSpec((S5_WIDTH, tm), lambda i, j: (0, i)),
                   pl.BlockSpec((ND // LANE, tm, LANE), lambda i, j: (0, i, 0))],
        out_shape=[jax.ShapeDtypeStruct((t, NP), BF16),
                   jax.ShapeDtypeStruct((S5_WIDTH, t), BF16),
                   jax.ShapeDtypeStruct((ND // LANE, t, LANE), F32)],
        scratch_shapes=[pltpu.VMEM((tm, D_MODEL), BF16)],
        compiler_params=_cparams(("parallel", "arbitrary")),
    )(x, w_p, w_su_t, w_d)


ATTN_TQ = 2 * A_RADIUS


def _banded_heads(q, kwin, vwin, bias_ref, first_key, sub_len):
    tq, nk = q.shape[0], kwin.shape[0]
    kidx = first_key + lax.broadcasted_iota(jnp.int32, (tq, nk), 1)
    valid = (kidx >= 0) & (kidx < sub_len)
    lane_head = lax.broadcasted_iota(jnp.int32, (1, q.shape[1]), 1) // A_HEAD_DIM
    scale = jnp.asarray(A_HEAD_DIM ** -0.5, BF16)
    heads = range(A_HEADS_PER_GROUP)
    qs = [jnp.where(lane_head == h, q, jnp.zeros_like(q)) * scale for h in heads]
    ss = [lax.dot_general(qh, kwin, (((1,), (1,)), ((), ())), preferred_element_type=F32) for qh in qs]
    ss = [jnp.where(valid, s + bias_ref[h], NEG_INF) for h, s in zip(heads, ss)]
    ms = [jnp.max(s, axis=-1, keepdims=True) for s in ss]
    ps = [jnp.exp(s - m) for s, m in zip(ss, ms)]
    dens = [jnp.sum(p, axis=-1, keepdims=True) for p in ps]
    ohs = [jnp.dot(p.astype(BF16), vwin, preferred_element_type=F32) for p in ps]
    acc = jnp.zeros((tq, q.shape[1]), F32)
    lse_acc = jnp.zeros((tq, q.shape[1]), F32)
    for h in heads:
        hm = lane_head == h
        acc = jnp.where(hm, ohs[h] / dens[h], acc)
        lse_acc = jnp.where(hm, ms[h] + jnp.log(dens[h]), lse_acc)
    return acc, lse_acc


def _attn_kernel(q_ref, kp_ref, kc_ref, kn_ref, vp_ref, vc_ref, vn_ref, bias_ref, o_ref, lse_ref, *, sub_len):
    tq, half = ATTN_TQ, A_RADIUS
    kwin = jnp.concatenate([kp_ref[tq - half:, :], kc_ref[...], kn_ref[:half, :]], axis=0)
    vwin = jnp.concatenate([vp_ref[tq - half:, :], vc_ref[...], vn_ref[:half, :]], axis=0)
    acc, lse = _banded_heads(q_ref[...], kwin, vwin, bias_ref, pl.program_id(1) * tq - half, sub_len)
    o_ref[...] = acc
    lse_ref[...] = lse


def _attn_call(proj, bias, b, s):
    tq = ATTN_TQ
    nqb = s // tq
    pv = proj.reshape(b, s, NP)
    qo, ko, vo = (_SEG[n] * LANE // GW for n in ("aq", "ak", "av"))
    blk = (None, tq, GW)
    cur = lambda off: (lambda bi, qi: (bi, qi, off))
    prv = lambda off: (lambda bi, qi: (bi, jnp.maximum(qi - 1, 0), off))
    nxt = lambda off: (lambda bi, qi: (bi, jnp.minimum(qi + 1, nqb - 1), off))
    out_map = lambda bi, qi: (bi, qi, 0)
    o, lse = pl.pallas_call(
        functools.partial(_attn_kernel, sub_len=s), name="window_attn_g0",
        grid=(b, nqb),
        in_specs=[pl.BlockSpec(blk, cur(qo)),
                  pl.BlockSpec(blk, prv(ko)), pl.BlockSpec(blk, cur(ko)), pl.BlockSpec(blk, nxt(ko)),
                  pl.BlockSpec(blk, prv(vo)), pl.BlockSpec(blk, cur(vo)), pl.BlockSpec(blk, nxt(vo)),
                  pl.BlockSpec((A_HEADS_PER_GROUP, tq, 2 * tq), lambda bi, qi: (0, 0, 0))],
        out_specs=[pl.BlockSpec(blk, out_map), pl.BlockSpec(blk, out_map)],
        out_shape=[jax.ShapeDtypeStruct((b, s, GW), F32), jax.ShapeDtypeStruct((b, s, GW), F32)],
        compiler_params=_cparams(("parallel", "parallel")),
    )(pv, pv, pv, pv, pv, pv, pv, bias)
    return o.reshape(b * s, GW), lse.reshape(b * s, GW)


def _dil_attn_kernel(q_ref, kp_ref, kc_ref, kn_ref, vp_ref, vc_ref, vn_ref, bias_ref, o_ref, lse_ref, *,
                     dil, sub_len):
    tq, half = ATTN_TQ, A_RADIUS
    first_key = pl.program_id(1) * tq - half

    def residue(r, carry):
        def rows(ref, n):
            return jnp.concatenate([ref[c, pl.ds(r, n, stride=dil), :] for c in range(GW // LANE)], axis=-1)

        q = rows(q_ref, tq).astype(BF16)
        kwin = jnp.concatenate([rows(kp_ref, half), rows(kc_ref, tq), rows(kn_ref, half)], axis=0).astype(BF16)
        vwin = jnp.concatenate([rows(vp_ref, half), rows(vc_ref, tq), rows(vn_ref, half)], axis=0).astype(BF16)
        acc, lse = _banded_heads(q, kwin, vwin, bias_ref, first_key, sub_len)
        for c in range(GW // LANE):
            o_ref[c, pl.ds(r, tq, stride=dil), :] = acc[:, c * LANE:(c + 1) * LANE]
            lse_ref[c, pl.ds(r, tq, stride=dil), :] = lse[:, c * LANE:(c + 1) * LANE]
        return carry

    lax.fori_loop(0, dil, residue, 0)


def _dil_attn_call(pd, bias, b, s, g, dil):
    tq = ATTN_TQ
    rows = tq * dil
    nblk = s // rows
    nslab = GW // LANE
    pv = pd.reshape(ND // LANE, b, s, LANE)
    qo, ko, vo = (3 * (g - 1) + j for j in range(3))
    blk, hblk = (nslab, None, rows, LANE), (nslab, None, rows // 2, LANE)
    cur = lambda off: (lambda bi, i: (off, bi, i, 0))
    prv = lambda off: (lambda bi, i: (off, bi, jnp.maximum(2 * i - 1, 0), 0))
    nxt = lambda off: (lambda bi, i: (off, bi, jnp.minimum(2 * i + 2, 2 * nblk - 1), 0))
    out_map = lambda bi, i: (0, bi, i, 0)
    o, lse = pl.pallas_call(
        functools.partial(_dil_attn_kernel, dil=dil, sub_len=s // dil), name=f"dilated_attn_g{g}",
        grid=(b, nblk),
        in_specs=[pl.BlockSpec(blk, cur(qo)),
                  pl.BlockSpec(hblk, prv(ko)), pl.BlockSpec(blk, cur(ko)), pl.BlockSpec(hblk, nxt(ko)),
                  pl.BlockSpec(hblk, prv(vo)), pl.BlockSpec(blk, cur(vo)), pl.BlockSpec(hblk, nxt(vo)),
                  pl.BlockSpec((A_HEADS_PER_GROUP, tq, 2 * tq), lambda bi, i: (0, 0, 0))],
        out_specs=[pl.BlockSpec(blk, out_map), pl.BlockSpec(blk, out_map)],
        out_shape=[jax.ShapeDtypeStruct((nslab, b, s, LANE), F32)] * 2,
        compiler_params=_cparams(("parallel", "parallel")),
    )(pv, pv, pv, pv, pv, pv, pv, bias)
    return o.reshape(nslab, b * s, LANE), lse.reshape(nslab, b * s, LANE)


def _t5_bucket(rel):
    nb = REL_BUCKETS // 2
    max_exact = nb // 2
    ret = jnp.where(rel > 0, nb, 0)
    n = jnp.abs(rel)
    nf = jnp.maximum(n, 1).astype(F32)
    large = max_exact + (jnp.log(nf / max_exact) / math.log(REL_MAX_DIST / max_exact)
                         * (nb - max_exact)).astype(jnp.int32)
    large = jnp.minimum(large, nb - 1)
    return ret + jnp.where(n < max_exact, n, large)


def _attn_bias_table(rel_bias, g, dil, tq=128):
    qi = jnp.arange(tq)[:, None]
    kj = jnp.arange(2 * tq)[None, :] - A_RADIUS
    rel = kj - qi
    tab = rel_bias[:, g * A_HEADS_PER_GROUP:(g + 1) * A_HEADS_PER_GROUP].astype(F32)
    bias = jnp.transpose(tab[_t5_bucket(rel * dil)], (2, 0, 1))
    return jnp.where((jnp.abs(rel) <= A_RADIUS)[None], bias, NEG_INF)


def _s5_toeplitz_kernel(bcf_ref, pwf_ref, bcb_ref, pwb_ref, d_ref, t_ref, gf_ref, gb_ref):
    gf_ref[...] = jnp.dot(bcf_ref[...], pwf_ref[...], precision=HIGHEST, preferred_element_type=F32)
    gb_ref[...] = jnp.dot(bcb_ref[...], pwb_ref[...], precision=HIGHEST, preferred_element_type=F32)
    c = S5_CHUNK
    row = lax.broadcasted_iota(jnp.int32, (c, c), 0)
    col = lax.broadcasted_iota(jnp.int32, (c, c), 1)

    def body(ci, carry):
        dval = d_ref[pl.ds(ci, 1), :]
        for co in range(S5_GROUP):
            r = ci * S5_GROUP + co
            gf = jnp.broadcast_to(gf_ref[pl.ds(r, 1), :], (c, c))
            gb = jnp.broadcast_to(gb_ref[pl.ds(r, 1), :], (c, c))
            tf = pltpu.roll(gf, 0, 1, stride=1, stride_axis=0)
            tb = pltpu.roll(gb, 1, 1, stride=1, stride_axis=0)
            tile = jnp.where(col >= row, tf, 0.0) + jnp.where(row >= col, tb, 0.0)
            tile = tile + jnp.where((row == col) & (ci == co), dval, 0.0)
            t_ref[pl.ds(pl.multiple_of(ci * c, c), c), co * c:(co + 1) * c] = tile.astype(BF16)
        return carry

    lax.fori_loop(0, S5_GROUP, body, 0)


def _s5_toeplitz_call(bcf, pwf, bcb, pwb, dskip):
    n = S5_GROUP * S5_CHUNK
    return pl.pallas_call(
        _s5_toeplitz_kernel, name="s5_toeplitz",
        grid=(S5_GROUPS,),
        in_specs=[pl.BlockSpec((None, S5_GROUP * S5_GROUP, 2 * S5_STATE), lambda g: (g, 0, 0)),
                  pl.BlockSpec((None, 2 * S5_STATE, S5_CHUNK), lambda g: (g, 0, 0)),
                  pl.BlockSpec((None, S5_GROUP * S5_GROUP, 2 * S5_STATE), lambda g: (g, 0, 0)),
                  pl.BlockSpec((None, 2 * S5_STATE, S5_CHUNK), lambda g: (g, 0, 0)),
                  pl.BlockSpec((None, S5_GROUP, LANE), lambda g: (g, 0, 0))],
        out_specs=pl.BlockSpec((None, n, n), lambda g: (g, 0, 0)),
        out_shape=jax.ShapeDtypeStruct((S5_GROUPS, n, n), BF16),
        scratch_shapes=[pltpu.VMEM((S5_GROUP * S5_GROUP, S5_CHUNK), F32),
                        pltpu.VMEM((S5_GROUP * S5_GROUP, S5_CHUNK), F32)],
        compiler_params=_cparams(("parallel",)),
    )(bcf, pwf, bcb, pwb, dskip)


def _s5_tables(lam_re, lam_im, log_step, b_re, b_im, c_re, c_im, d_skip):
    c = S5_CHUNK
    k = jnp.arange(c, dtype=F32)
    per_dir = []
    for direction in range(2):
        step = jnp.exp(log_step[direction].astype(F32))[:, None]
        lr, li = lam_re[direction].astype(F32), lam_im[direction].astype(F32)
        mag = jnp.exp(lr * step)
        ar, ai = mag * jnp.cos(li * step), mag * jnp.sin(li * step)
        nr, ni = ar - 1.0, ai
        den = lr * lr + li * li
        fr = (nr * lr + ni * li) / den
        fi = (ni * lr - nr * li) / den
        br, bi = b_re[direction].astype(F32), b_im[direction].astype(F32)
        bbr = fr[..., None] * br - fi[..., None] * bi
        bbi = fr[..., None] * bi + fi[..., None] * br
        cr, cim = c_re[direction].astype(F32), c_im[direction].astype(F32)

        def power(e, log_mag=lr * step, phase=li * step):
            m = jnp.exp(e[None, None, :] * log_mag[..., None])
            th = e[None, None, :] * phase[..., None]
            return m * jnp.cos(th), m * jnp.sin(th)

        per_dir.append(dict(bbr=bbr, bbi=bbi, cr=cr, cim=cim, power=power))

    def bc_table(d):
        bbr_t = jnp.transpose(d["bbr"], (0, 2, 1))[:, :, None, :]
        bbi_t = jnp.transpose(d["bbi"], (0, 2, 1))[:, :, None, :]
        cr, cim = d["cr"][:, None], d["cim"][:, None]
        re = bbr_t * cr - bbi_t * cim
        im = bbr_t * cim + bbi_t * cr
        return jnp.concatenate([re, -im], axis=-1).reshape(S5_GROUPS, S5_GROUP * S5_GROUP, 2 * S5_STATE)

    f, bw = per_dir
    pfr, pfi = f["power"](k)
    pbr, pbi = bw["power"](c - 1 - k)
    pwf = jnp.concatenate([pfr, pfi], axis=1)
    pwb = jnp.concatenate([pbr, pbi], axis=1)
    dsk = jnp.broadcast_to(d_skip.astype(F32).reshape(S5_GROUPS, S5_GROUP, 1), (S5_GROUPS, S5_GROUP, LANE))
    toep = _s5_toeplitz_call(bc_table(f), pwf, bc_table(bw), pwb, dsk)

    def w_in(d, e):
        pr, pi = d["power"](e)
        re = d["bbr"][:, :, :, None] * pr[:, :, None, :] - d["bbi"][:, :, :, None] * pi[:, :, None, :]
        im = d["bbr"][:, :, :, None] * pi[:, :, None, :] + d["bbi"][:, :, :, None] * pr[:, :, None, :]
        tr = lambda z: jnp.transpose(z, (0, 2, 3, 1)).reshape(S5_GROUPS, S5_GROUP * c, S5_STATE)
        return tr(re), tr(im)

    fre, fim = w_in(f, c - 1 - k)
    bre, bim = w_in(bw, k)
    win = jnp.concatenate([fre, bre, fim, bim], axis=-1).astype(BF16)

    def w_out(d, e):
        pr, pi = d["power"](e)
        re = d["cr"][:, :, :, None] * pr[:, None] - d["cim"][:, :, :, None] * pi[:, None]
        im = d["cr"][:, :, :, None] * pi[:, None] + d["cim"][:, :, :, None] * pr[:, None]
        tr = lambda z: jnp.transpose(z, (0, 2, 1, 3)).reshape(S5_GROUPS, S5_STATE, S5_GROUP * c)
        return tr(re), tr(-im)

    ofre, ofim = w_out(f, k + 1.0)
    obre, obim = w_out(bw, c - k)
    wout = jnp.concatenate([ofre, obre, ofim, obim], axis=1).astype(BF16)

    cc = jnp.asarray([float(c)], F32)
    afr, afi = f["power"](cc)
    abr, abi = bw["power"](cc)
    a_re = jnp.concatenate([afr[..., 0], abr[..., 0]], axis=-1).reshape(1, -1)
    a_im = jnp.concatenate([afi[..., 0], abi[..., 0]], axis=-1).reshape(1, -1)
    return toep, win, wout, a_re, a_im


def _s5_contrib_kernel(u_ref, win_ref, re_ref, im_ref):
    uc = jnp.concatenate([u_ref[ci] for ci in range(S5_GROUP)], axis=-1)
    res = jnp.dot(uc, win_ref[...], preferred_element_type=F32)
    nc, nb = re_ref.shape[0], re_ref.shape[1]
    half = 2 * S5_STATE
    for bi in range(nb):
        re_ref[:, bi, :] = res[bi * nc:(bi + 1) * nc, :half]
        im_ref[:, bi, :] = res[bi * nc:(bi + 1) * nc, half:]


def _s5_contrib_call(ut, win, b, nc):
    nch = b * nc
    half = 2 * S5_STATE
    uv = ut.reshape(S5_GROUPS, S5_GROUP, nch, S5_CHUNK)
    return pl.pallas_call(
        _s5_contrib_kernel, name="s5_contrib",
        grid=(S5_GROUPS,),
        in_specs=[pl.BlockSpec((None, S5_GROUP, nch, S5_CHUNK), lambda g: (g, 0, 0, 0)),
                  pl.BlockSpec((None, S5_GROUP * S5_CHUNK, 4 * S5_STATE), lambda g: (g, 0, 0))],
        out_specs=[pl.BlockSpec((nc, b, half), lambda g: (0, 0, g))] * 2,
        out_shape=[jax.ShapeDtypeStruct((nc, b, S5_GROUPS * half), F32)] * 2,
        compiler_params=_cparams(("parallel",)),
    )(uv, win)


def _s5_scan_kernel(cre_ref, cim_ref, are_ref, aim_ref, ore_ref, oim_ref):
    nc, nb, width = cre_ref.shape
    a_re, a_im = are_ref[...], aim_ref[...]
    lane = lax.broadcasted_iota(jnp.int32, (1, width), 1)
    fwd_lane = (lane % (2 * S5_STATE)) < S5_STATE
    zero = jnp.zeros((nb, width), F32)

    def advance(s, c):
        sr, si = s
        return a_re * sr - a_im * si + cre_ref[c], a_re * si + a_im * sr + cim_ref[c]

    def up(c, s):
        ore_ref[c] = s[0]
        oim_ref[c] = s[1]
        return advance(s, c)

    lax.fori_loop(0, nc, up, (zero, zero))

    def down(i, s):
        c = nc - 1 - i
        ore_ref[c] = jnp.where(fwd_lane, ore_ref[c], s[0])
        oim_ref[c] = jnp.where(fwd_lane, oim_ref[c], s[1])
        return advance(s, c)

    lax.fori_loop(0, nc, down, (zero, zero))


def _s5_scan_call(cre, cim, a_re, a_im):
    return pl.pallas_call(
        _s5_scan_kernel, name="s5_chunk_scan",
        out_shape=[jax.ShapeDtypeStruct(cre.shape, F32)] * 2,
        compiler_params=pltpu.CompilerParams(vmem_limit_bytes=VMEM_LIMIT),
    )(cre, cim, a_re, a_im)


def _gelu_tanh(y):
    return 0.5 * y * (1.0 + jnp.tanh(math.sqrt(2.0 / math.pi) * (y + 0.044715 * (y * y * y))))


def _s5_out_kernel(u_ref, t_ref, sre_ref, sim_ref, wout_ref, z_ref):
    uc = jnp.concatenate([u_ref[ci] for ci in range(S5_GROUP)], axis=-1)
    y = jnp.dot(uc, t_ref[...], preferred_element_type=F32)
    sp = jnp.concatenate([jnp.concatenate([sre_ref[:, bi, :], sim_ref[:, bi, :]], axis=-1)
                          for bi in range(sre_ref.shape[1])], axis=0)
    y = y + jnp.dot(sp.astype(BF16), wout_ref[...], preferred_element_type=F32)
    z = _gelu_tanh(y)
    for co in range(S5_GROUP):
        z_ref[co] = z[:, co * S5_CHUNK:(co + 1) * S5_CHUNK].astype(BF16)


def _s5_out_call(ut, toep, s_re, s_im, wout, b, nc):
    nch = b * nc
    uv = ut.reshape(S5_GROUPS, S5_GROUP, nch, S5_CHUNK)
    n = S5_GROUP * S5_CHUNK
    half = 2 * S5_STATE
    zt = pl.pallas_call(
        _s5_out_kernel, name="s5_out",
        grid=(S5_GROUPS,),
        in_specs=[pl.BlockSpec((None, S5_GROUP, nch, S5_CHUNK), lambda g: (g, 0, 0, 0)),
                  pl.BlockSpec((None, n, n), lambda g: (g, 0, 0)),
                  pl.BlockSpec((nc, b, half), lambda g: (0, 0, g)),
                  pl.BlockSpec((nc, b, half), lambda g: (0, 0, g)),
                  pl.BlockSpec((None, 4 * S5_STATE, n), lambda g: (g, 0, 0))],
        out_specs=pl.BlockSpec((None, S5_GROUP, nch, S5_CHUNK), lambda g: (g, 0, 0, 0)),
        out_shape=jax.ShapeDtypeStruct((S5_GROUPS, S5_GROUP, nch, S5_CHUNK), BF16),
        compiler_params=_cparams(("parallel",)),
    )(uv, toep, s_re, s_im, wout)
    return zt.reshape(S5_WIDTH, nch * S5_CHUNK)


def _ret_tables(s):
    c = RET_CHUNK
    half = RET_QK_DIM // 2
    inv = ROPE_BASE ** (-jnp.arange(half, dtype=F32) / half)
    ang = jnp.arange(s, dtype=F32)[:, None] * inv[None, :]
    cos = jnp.tile(jnp.cos(ang), (1, 2 * RET_HEADS))
    sin = jnp.tile(jnp.sin(ang), (1, 2 * RET_HEADS))
    hidx = jnp.arange(RET_HEADS, dtype=F32)
    lgf = jnp.log1p(-jnp.exp2(-5.0 - hidx))
    lgb = jnp.log1p(-jnp.exp2(-5.5 - hidx))
    j = jnp.arange(c, dtype=F32)
    rel = j[:, None] - j[None, :]
    intra = jnp.exp(jnp.abs(rel)[None] * jnp.where(rel[None] >= 0, lgf[:, None, None], lgb[:, None, None]))
    qk_f = jnp.repeat(lgf, RET_QK_DIM)[None, :]
    qk_b = jnp.repeat(lgb, RET_QK_DIM)[None, :]
    v_f = jnp.repeat(lgf, RET_V_DIM)[None, :]
    v_b = jnp.repeat(lgb, RET_V_DIM)[None, :]
    tabs = dict(
        cos=cos, sin=sin, intra=intra,
        tail_f=jnp.exp((c - 1 - j)[:, None] * qk_f), head_b=jnp.exp(j[:, None] * qk_b),
        decq_f=jnp.exp((j + 1.0)[:, None] * qk_f), decq_b=jnp.exp((c - j)[:, None] * qk_b),
        chunk_f=jnp.exp(c * v_f), chunk_b=jnp.exp(c * v_b),
        bd=(jnp.arange(RET_HEADS * RET_QK_DIM)[:, None] // RET_QK_DIM
            == jnp.arange(RET_HEADS * RET_V_DIM)[None, :] // RET_V_DIM).astype(F32))
    return tabs


def _rot(x_ref, xs_ref, cos_ref, sin_ref):
    return x_ref[...].astype(F32) * cos_ref[...] + xs_ref[...].astype(F32) * sin_ref[...]


def _ret_bwd_kernel(k_ref, ks_ref, v_ref, cos_ref, sin_ref, headb_ref, chunkb_ref, bd_ref, sb_ref, st_ref):
    @pl.when(pl.program_id(1) == 0)
    def _():
        st_ref[...] = jnp.zeros_like(st_ref)

    for h in range(RET_HEADS):
        sb_ref[h] = st_ref[h * RET_QK_DIM:(h + 1) * RET_QK_DIM, h * RET_V_DIM:(h + 1) * RET_V_DIM].astype(BF16)
    kr = _rot(k_ref, ks_ref, cos_ref, sin_ref) * (RET_QK_DIM ** -0.5)
    kwt = jnp.transpose(kr * headb_ref[...]).astype(BF16)
    kv = jnp.dot(kwt, v_ref[...], preferred_element_type=F32)
    st_ref[...] = st_ref[...] * chunkb_ref[...] + kv * bd_ref[...]


def _ret_fwd_kernel(q_ref, qs_ref, k_ref, ks_ref, v_ref, g_ref, cos_ref, sin_ref, sb_ref, intra_ref,
                    tailf_ref, decqf_ref, decqb_ref, chunkf_ref, bd_ref, o_ref, st_ref, sbd_ref):
    @pl.when(pl.program_id(1) == 0)
    def _():
        st_ref[...] = jnp.zeros_like(st_ref)
        sbd_ref[...] = jnp.zeros_like(sbd_ref)

    for h in range(RET_HEADS):
        sbd_ref[h * RET_QK_DIM:(h + 1) * RET_QK_DIM, h * RET_V_DIM:(h + 1) * RET_V_DIM] = sb_ref[h]
    qr = _rot(q_ref, qs_ref, cos_ref, sin_ref)
    kr = _rot(k_ref, ks_ref, cos_ref, sin_ref) * (RET_QK_DIM ** -0.5)
    qb, kb = qr.astype(BF16), kr.astype(BF16)
    vb = v_ref[...]
    lane_head = lax.broadcasted_iota(jnp.int32, (1, qb.shape[1]), 1) // RET_QK_DIM
    inter = jnp.dot((qr * decqf_ref[...]).astype(BF16), st_ref[...].astype(BF16), preferred_element_type=F32)
    inter = inter + jnp.dot((qr * decqb_ref[...]).astype(BF16), sbd_ref[...], preferred_element_type=F32)
    heads = range(RET_HEADS)
    vcols = [slice(h * RET_V_DIM, (h + 1) * RET_V_DIM) for h in heads]
    qhs = [jnp.where(lane_head == h, qb, jnp.zeros_like(qb)) for h in heads]
    scs = [lax.dot_general(qh, kb, (((1,), (1,)), ((), ())), preferred_element_type=F32) * intra_ref[h]
           for h, qh in zip(heads, qhs)]
    ohs = [jnp.dot(sc.astype(BF16), vb[:, vs], preferred_element_type=F32) + inter[:, vs]
           for sc, vs in zip(scs, vcols)]
    mus = [jnp.mean(oh, axis=-1, keepdims=True) for oh in ohs]
    dds = [oh - mu for oh, mu in zip(ohs, mus)]
    vrs = [jnp.mean(dd * dd, axis=-1, keepdims=True) for dd in dds]
    on = jnp.concatenate([dd * lax.rsqrt(var + LN_EPS) for dd, var in zip(dds, vrs)], axis=-1)
    gg = g_ref[...].astype(F32)
    o_ref[...] = (on * (gg * _sigmoid(gg))).astype(BF16)
    kwt = jnp.transpose(kr * tailf_ref[...]).astype(BF16)
    kv = jnp.dot(kwt, vb, preferred_element_type=F32)
    st_ref[...] = st_ref[...] * chunkf_ref[...] + kv * bd_ref[...]


def _retention_call(proj, tabs, b, s):
    c = RET_CHUNK
    nc = s // c
    qkw, vw = RET_HEADS * RET_QK_DIM, RET_HEADS * RET_V_DIM
    pv = proj.reshape(b, s, NP)
    qk_blk, v_blk = (None, c, qkw), (None, c, vw)
    qo, qso, ko, kso = (_SEG[n] * LANE // qkw for n in ("rq", "rqs", "rk", "rks"))
    vo, go = (_SEG[n] * LANE // vw for n in ("rv", "rg"))
    const2 = lambda shape: pl.BlockSpec(shape, lambda bi, i: (0, 0))
    rev = lambda off: (lambda bi, i: (bi, nc - 1 - i, off))
    fwd = lambda off: (lambda bi, i: (bi, i, off))
    sb = pl.pallas_call(
        _ret_bwd_kernel, name="retention_bwd_state",
        grid=(b, nc),
        in_specs=[pl.BlockSpec(qk_blk, rev(ko)), pl.BlockSpec(qk_blk, rev(kso)), pl.BlockSpec(v_blk, rev(vo)),
                  pl.BlockSpec((c, qkw), lambda bi, i: (nc - 1 - i, 0)),
                  pl.BlockSpec((c, qkw), lambda bi, i: (nc - 1 - i, 0)),
                  const2((c, qkw)), const2((1, vw)), const2((qkw, vw))],
        out_specs=pl.BlockSpec((None, None, RET_HEADS, RET_QK_DIM, RET_V_DIM),
                               lambda bi, i: (bi, nc - 1 - i, 0, 0, 0)),
        out_shape=jax.ShapeDtypeStruct((b, nc, RET_HEADS, RET_QK_DIM, RET_V_DIM), BF16),
        scratch_shapes=[pltpu.VMEM((qkw, vw), F32)],
        compiler_params=_cparams(("parallel", "arbitrary")),
    )(pv, pv, pv, tabs["cos"], tabs["sin"], tabs["head_b"], tabs["chunk_b"], tabs["bd"])
    yc = pl.pallas_call(
        _ret_fwd_kernel, name="retention_fwd",
        grid=(b, nc),
        in_specs=[pl.BlockSpec(qk_blk, fwd(qo)), pl.BlockSpec(qk_blk, fwd(qso)),
                  pl.BlockSpec(qk_blk, fwd(ko)), pl.BlockSpec(qk_blk, fwd(kso)),
                  pl.BlockSpec(v_blk, fwd(vo)), pl.BlockSpec(v_blk, fwd(go)),
                  pl.BlockSpec((c, qkw), lambda bi, i: (i, 0)), pl.BlockSpec((c, qkw), lambda bi, i: (i, 0)),
                  pl.BlockSpec((None, None, RET_HEADS, RET_QK_DIM, RET_V_DIM), lambda bi, i: (bi, i, 0, 0, 0)),
                  pl.BlockSpec((RET_HEADS, c, c), lambda bi, i: (0, 0, 0)),
                  const2((c, qkw)), const2((c, qkw)), const2((c, qkw)), const2((1, vw)), const2((qkw, vw))],
        out_specs=pl.BlockSpec((None, c, vw), lambda bi, i: (bi, i, 0)),
        out_shape=jax.ShapeDtypeStruct((b, s, vw), BF16),
        scratch_shapes=[pltpu.VMEM((qkw, vw), F32), pltpu.VMEM((qkw, vw), BF16)],
        compiler_params=_cparams(("parallel", "arbitrary")),
    )(pv, pv, pv, pv, pv, pv, tabs["cos"], tabs["sin"], sb, tabs["intra"],
      tabs["tail_f"], tabs["decq_f"], tabs["decq_b"], tabs["chunk_f"], tabs["bd"])
    return yc.reshape(b * s, vw)


def _mm_kernel(x_ref, w_ref, o_ref):
    o_ref[...] = jnp.dot(x_ref[...].astype(BF16), w_ref[...], preferred_element_type=F32).astype(o_ref.dtype)


def _mem_kv_call(mem2d, w_kv):
    m, n = mem2d.shape[0], w_kv.shape[1]
    tm = 512
    return pl.pallas_call(
        _mm_kernel, name="mem_kv_proj",
        grid=(m // tm,),
        in_specs=[pl.BlockSpec((tm, D_MODEL), lambda i: (i, 0)),
                  pl.BlockSpec((D_MODEL, n), lambda i: (0, 0))],
        out_specs=pl.BlockSpec((tm, n), lambda i: (i, 0)),
        out_shape=jax.ShapeDtypeStruct((m, n), BF16),
        compiler_params=_cparams(("parallel",)),
    )(mem2d, w_kv)


def _mem_attn_kernel(q_ref, k_ref, v_ref, o_ref):
    q = q_ref[...]
    cols = [slice(h * MEM_HEAD_PAD, (h + 1) * MEM_HEAD_PAD) for h in range(MEM_HEADS)]
    ss = [lax.dot_general(q[:, hs], k_ref[:, hs], (((1,), (1,)), ((), ())),
                          preferred_element_type=F32) * (MEM_HEAD_DIM ** -0.5) for hs in cols]
    ms = [jnp.max(s, axis=-1, keepdims=True) for s in ss]
    es = [jnp.exp(s - m) for s, m in zip(ss, ms)]
    ps = [e / jnp.sum(e, axis=-1, keepdims=True) for e in es]
    outs = [jnp.dot(p.astype(BF16), v_ref[:, hs], preferred_element_type=F32) for p, hs in zip(ps, cols)]
    o_ref[...] = jnp.concatenate(outs, axis=-1).astype(BF16)


def _mem_attn_call(proj, kv, b, s):
    tq = 256
    w = MEM_HEADS * MEM_HEAD_PAD
    pv = proj.reshape(b, s, NP)
    kvv = kv.reshape(b, N_MEM, 2 * w)
    qo = _SEG["mq"] * LANE // w
    yd = pl.pallas_call(
        _mem_attn_kernel, name="mem_attn",
        grid=(b, s // tq),
        in_specs=[pl.BlockSpec((None, tq, w), lambda bi, i: (bi, i, qo)),
                  pl.BlockSpec((None, N_MEM, w), lambda bi, i: (bi, 0, 0)),
                  pl.BlockSpec((None, N_MEM, w), lambda bi, i: (bi, 0, 1))],
        out_specs=pl.BlockSpec((None, tq, w), lambda bi, i: (bi, i, 0)),
        out_shape=jax.ShapeDtypeStruct((b, s, w), BF16),
        compiler_params=_cparams(("parallel", "parallel")),
    )(pv, kvv, kvv)
    return yd.reshape(b * s, w)


def _merge_kernel(x_ref, ao0_ref, ao1_ref, ao2_ref, al0_ref, al1_ref, al2_ref, zt_ref, yc_ref, yd_ref, gl_ref,
                  wglut_ref, bglu_ref, wba_ref, wbb_ref, wbc_ref, wbd_ref, wout_ref, g1_ref, b1_ref,
                  rwh_ref, rwl_ref, rb_ref,
                  x1_ref, x1p_ref, route_ref, counts_ref, carry_ref):
    tm = x_ref.shape[0]

    @pl.when(pl.program_id(0) == 0)
    def _():
        carry_ref[...] = jnp.zeros_like(carry_ref)

    slabs = lambda ref: jnp.concatenate([ref[c] for c in range(ref.shape[0])], axis=-1)
    lses = [al0_ref[...], slabs(al1_ref), slabs(al2_ref)]
    aos = [ao0_ref[...], slabs(ao1_ref), slabs(ao2_ref)]
    mx = jnp.maximum(jnp.maximum(lses[0], lses[1]), lses[2])
    es = [jnp.exp(l - mx) for l in lses]
    tot = es[0] + es[1] + es[2]
    ya = jnp.concatenate([o * (e / tot) for o, e in zip(aos, es)], axis=-1).astype(BF16)
    zt = zt_ref[...]
    glu = jnp.dot(wglut_ref[...], zt, preferred_element_type=F32) + bglu_ref[...]
    ybt = zt.astype(F32) * _sigmoid(glu)
    yb = jnp.transpose(ybt).astype(BF16)
    merged = jnp.zeros((tm, D_MODEL), F32)
    for n, (y, w_ref) in enumerate(((ya, wba_ref), (yb, wbb_ref), (yc_ref[...], wbc_ref), (yd_ref[...], wbd_ref))):
        bo = jnp.dot(y, w_ref[...], preferred_element_type=F32)
        merged = merged + (jnp.tanh(gl_ref[:, n * D_MODEL:(n + 1) * D_MODEL].astype(F32)) + 1.0) * bo
    h = DEEPNORM_ALPHA * x_ref[...] + jnp.dot(merged.astype(BF16), wout_ref[...], preferred_element_type=F32)
    x1 = _ln(h, g1_ref[...], b1_ref[...])
    x1_ref[...] = x1
    x1p_ref[...] = _pack_bf16_pairs(x1)

    x_hi = x1.astype(BF16)
    x_lo = (x1 - x_hi.astype(F32)).astype(BF16)
    logits = (jnp.dot(x_hi, rwh_ref[...], preferred_element_type=F32)
              + jnp.dot(x_lo, rwh_ref[...], preferred_element_type=F32)
              + jnp.dot(x_hi, rwl_ref[...], preferred_element_type=F32)) + rb_ref[...]
    lane = lax.broadcasted_iota(jnp.int32, (tm, LANE), 1)
    work = logits
    sel, vals, hots = [], [], []
    for _ in range(TOP_K):
        mk = jnp.max(work, axis=-1, keepdims=True)
        ik = jnp.min(jnp.where(work == mk, lane, LANE), axis=-1, keepdims=True)
        hot = lane == ik
        work = jnp.where(hot, -jnp.inf, work)
        sel.append(ik)
        vals.append(mk)
        hots.append(hot)
    ex = [jnp.exp(v - vals[0]) for v in vals]
    den = ex[0] + ex[1] + ex[2] + ex[3]
    chosen = (hots[0] | hots[1] | hots[2] | hots[3]).astype(F32)
    r_i = lax.broadcasted_iota(jnp.int32, (tm, tm), 0)
    c_i = lax.broadcasted_iota(jnp.int32, (tm, tm), 1)
    strict_lower = (c_i < r_i).astype(BF16)
    before = jnp.dot(strict_lower, chosen.astype(BF16), preferred_element_type=F32) + carry_ref[0:1, :]
    route = jnp.zeros((tm, LANE), F32)
    for k in range(TOP_K):
        rank = jnp.sum(jnp.where(hots[k], before, 0.0), axis=-1, keepdims=True)
        route = jnp.where(lane == k, sel[k].astype(F32), route)
        route = jnp.where(lane == TOP_K + k, ex[k] / den, route)
        route = jnp.where(lane == 2 * TOP_K + k, rank, route)
    route_ref[...] = route
    new_carry = carry_ref[0:1, :] + jnp.sum(chosen, axis=0, keepdims=True)
    carry_ref[...] = jnp.broadcast_to(new_carry, carry_ref.shape)
    counts_ref[...] = jnp.broadcast_to(new_carry, counts_ref.shape)


def _merge_call(x, aos, alses, zt, yc, yd, proj, lw):
    t = x.shape[0]
    tm = 256
    gw = GW
    gate_blk = 4 * D_MODEL
    row = lambda w: pl.BlockSpec((tm, w), lambda i: (i, 0))
    slab = pl.BlockSpec((gw // LANE, tm, LANE), lambda i: (0, i, 0))
    full = lambda a: pl.BlockSpec(a.shape, lambda i: (0,) * a.ndim)
    consts = (lw["w_glu_t"], lw["b_glu"], lw["wb_a"], lw["wb_b"], lw["wb_c"], lw["wb_d"], lw["w_out"],
              lw["ln1_g"], lw["ln1_b"], lw["router_w_hi"], lw["router_w_lo"], lw["router_b"])
    return pl.pallas_call(
        _merge_kernel, name="branch_merge_router",
        grid=(t // tm,),
        in_specs=[row(D_MODEL)] + [row(gw), slab, slab] * 2
                 + [pl.BlockSpec((S5_WIDTH, tm), lambda i: (0, i)),
                  row(RET_HEADS * RET_V_DIM), row(MEM_HEADS * MEM_HEAD_PAD),
                  pl.BlockSpec((tm, gate_blk), lambda i: (i, _SEG["gate"] * LANE // gate_blk))]
                 + [full(a) for a in consts],
        out_specs=[row(D_MODEL), row(D_MODEL // 2), row(LANE), pl.BlockSpec((8, LANE), lambda i: (0, 0))],
        out_shape=[jax.ShapeDtypeStruct((t, D_MODEL), F32),
                   jax.ShapeDtypeStruct((t, D_MODEL // 2), jnp.uint32),
                   jax.ShapeDtypeStruct((t, LANE), F32),
                   jax.ShapeDtypeStruct((8, LANE), F32)],
        scratch_shapes=[pltpu.VMEM((8, LANE), F32)],
        compiler_params=_cparams(("arbitrary",)),
    )(x, *aos, *alses, zt, yc, yd, proj, *consts)


def _pack_bf16_pairs(x):
    n = x.shape[1] // 2
    hi = lax.bitcast_convert_type(x[:, :n].astype(BF16).astype(F32), jnp.uint32)
    lo = lax.bitcast_convert_type(x[:, n:].astype(BF16).astype(F32), jnp.uint32)
    return hi | (lo >> 16)


def _unpack_bf16_pairs(w):
    hi = lax.bitcast_convert_type(w & jnp.uint32(0xFFFF0000), F32)
    lo = lax.bitcast_convert_type(w << 16, F32)
    return jnp.concatenate([hi, lo], axis=-1)


def _sc_index_rows(idx):
    window = SC_GATHER_WINDOW
    return jnp.pad(idx.reshape(-1, window), ((0, 0), (0, LANE - window)))


def _sc_scatter_rows(x, idx, n_out):
    n, d = x.shape
    m = idx.shape[0]
    window = SC_GATHER_WINDOW
    src_steps = n // window
    mesh = plsc.VectorSubcoreMesh(core_axis_name="core", subcore_axis_name="subcore")

    @pl.kernel(out_type=jax.ShapeDtypeStruct((n_out, d), x.dtype), mesh=mesh, scratch_types=[])
    def scatter_kernel(x_hbm, i_hbm, o_hbm):
        def body(x_vmem, i_vmem):
            pltpu.sync_copy(x_vmem, o_hbm.at[i_vmem.at[0, pl.ds(0, window)]])

        pltpu.emit_pipeline(
            body,
            grid=(m // window,),
            in_specs=[pl.BlockSpec((window, d), index_map=lambda i: (i % src_steps, 0)),
                      pl.BlockSpec((1, LANE), index_map=lambda i: (i, 0))],
            out_specs=[],
            core_axis_name=("core", "subcore"),
            dimension_semantics=(pltpu.PARALLEL,),
            trace_scopes=False,
        )(x_hbm, i_hbm)

    return scatter_kernel(x, _sc_index_rows(idx))


def _sc_gather_rows(x, idx):
    m, d = idx.shape[0], x.shape[1]
    window = SC_GATHER_WINDOW
    mesh = plsc.VectorSubcoreMesh(core_axis_name="core", subcore_axis_name="subcore")

    @pl.kernel(out_type=jax.ShapeDtypeStruct((m, d), x.dtype), mesh=mesh, scratch_types=[])
    def gather_kernel(x_hbm, i_hbm, o_hbm):
        def body(i_vmem, o_vmem):
            pltpu.sync_copy(x_hbm.at[i_vmem.at[0, pl.ds(0, window)]], o_vmem)

        pltpu.emit_pipeline(
            body,
            grid=(m // window,),
            in_specs=[pl.BlockSpec((1, LANE), index_map=lambda i: (i, 0))],
            out_specs=[pl.BlockSpec((window, d), index_map=lambda i: (i, 0))],
            core_axis_name=("core", "subcore"),
            dimension_semantics=(pltpu.PARALLEL,),
            trace_scopes=False,
        )(i_hbm, o_hbm)

    return gather_kernel(x, _sc_index_rows(idx))


def _moe_expert_kernel(be_ref, nu_ref, nv_ref, x_ref, w1_ref, b1_ref, w2_ref, b2_ref, o_ref, w1s_ref, w2s_ref):
    i = pl.program_id(0)

    @pl.when((i == 0) | (be_ref[i] != be_ref[jnp.maximum(i - 1, 0)]))
    def _():
        w1s_ref[...] = w1_ref[...].astype(BF16)
        w2s_ref[...] = w2_ref[...].astype(BF16)

    @pl.when(i < nu_ref[0])
    def _():
        rows = lax.broadcasted_iota(jnp.int32, (x_ref.shape[0], 1), 0)
        x = jnp.where(rows < nv_ref[i], _unpack_bf16_pairs(x_ref[...]), 0.0)
        h = jnp.dot(x.astype(BF16), w1s_ref[...], preferred_element_type=F32) + b1_ref[...]
        gate = jnp.minimum(h[:, :D_FF], SWIGLU_LIMIT)
        up = jnp.clip(h[:, D_FF:], -SWIGLU_LIMIT, SWIGLU_LIMIT)
        hid = gate * _sigmoid(SWIGLU_ALPHA * gate) * (up + 1.0)
        y = jnp.dot(hid.astype(BF16), w2s_ref[...], preferred_element_type=F32) + b2_ref[...]
        o_ref[...] = _pack_bf16_pairs(y)

    @pl.when(i >= nu_ref[0])
    def _():
        o_ref[...] = jnp.zeros_like(o_ref)


def _moe_expert_call(xg, block_expert, n_used, n_valid, layer, w1, b1, w2, b2):
    bm = MOE_BLOCK_ROWS
    n_blocks = xg.shape[0] // bm
    half = D_MODEL // 2
    grid_spec = pltpu.PrefetchScalarGridSpec(
        num_scalar_prefetch=3, grid=(n_blocks,),
        in_specs=[pl.BlockSpec((bm, half), lambda i, be, nu, nv: (i, 0)),
                  pl.BlockSpec((None, None, D_MODEL, 2 * D_FF), lambda i, be, nu, nv: (layer, be[i], 0, 0)),
                  pl.BlockSpec((None, 1, 2 * D_FF), lambda i, be, nu, nv: (be[i], 0, 0)),
                  pl.BlockSpec((None, None, D_FF, D_MODEL), lambda i, be, nu, nv: (layer, be[i], 0, 0)),
                  pl.BlockSpec((None, 1, D_MODEL), lambda i, be, nu, nv: (be[i], 0, 0))],
        out_specs=pl.BlockSpec((bm, half), lambda i, be, nu, nv: (i, 0)),
        scratch_shapes=[pltpu.VMEM((D_MODEL, 2 * D_FF), BF16), pltpu.VMEM((D_FF, D_MODEL), BF16)])
    return pl.pallas_call(
        _moe_expert_kernel, name="moe_experts",
        grid_spec=grid_spec,
        out_shape=jax.ShapeDtypeStruct((n_blocks * bm, half), jnp.uint32),
        compiler_params=_cparams(("arbitrary",)),
    )(block_expert, n_used, n_valid, xg, w1, b1, w2, b2)


def _moe_combine_kernel(x_ref, route_ref, y0_ref, y1_ref, y2_ref, y3_ref, g_ref, b_ref, o_ref):
    route = route_ref[...]
    moe = jnp.zeros(x_ref.shape, F32)
    for k, y_ref in enumerate((y0_ref, y1_ref, y2_ref, y3_ref)):
        moe = moe + route[:, TOP_K + k:TOP_K + k + 1] * _unpack_bf16_pairs(y_ref[...])
    o_ref[...] = _ln(DEEPNORM_ALPHA * x_ref[...] + moe, g_ref[...], b_ref[...])


def _moe_combine_call(x1, route, yg, g2, b2):
    t = x1.shape[0]
    tm = 256
    nt = t // tm
    ysel = lambda k: pl.BlockSpec((tm, D_MODEL // 2), lambda i: (k * nt + i, 0))
    return pl.pallas_call(
        _moe_combine_kernel, name="moe_combine_ln",
        grid=(nt,),
        in_specs=[pl.BlockSpec((tm, D_MODEL), lambda i: (i, 0)),
                  pl.BlockSpec((tm, LANE), lambda i: (i, 0))]
                 + [ysel(k) for k in range(TOP_K)]
                 + [pl.BlockSpec((1, D_MODEL), lambda i: (0, 0)),
                    pl.BlockSpec((1, D_MODEL), lambda i: (0, 0))],
        out_specs=pl.BlockSpec((tm, D_MODEL), lambda i: (i, 0)),
        out_shape=jax.ShapeDtypeStruct((t, D_MODEL), F32),
        compiler_params=_cparams(("parallel",)),
    )(x1, route, yg, yg, yg, yg, g2, b2)


def _moe(x1, x1p, route, counts, lw):
    t = x1.shape[0]
    bm = MOE_BLOCK_ROWS
    n_assign = t * TOP_K
    n_blocks = n_assign // bm + N_EXPERTS
    idx = route[:, :TOP_K].astype(jnp.int32)
    rank = route[:, 2 * TOP_K:3 * TOP_K].astype(jnp.int32)
    cnt = counts[0, :N_EXPERTS].astype(jnp.int32)
    padded = (cnt + bm - 1) // bm * bm
    pad_ends = jnp.cumsum(padded)
    pad_starts = pad_ends - padded
    dest = jnp.transpose(jnp.take(pad_starts, idx) + rank).reshape(-1)
    block_start = jnp.arange(n_blocks, dtype=jnp.int32) * bm
    block_expert = jnp.minimum(jnp.sum(pad_ends[None, :] <= block_start[:, None], axis=1),
                               N_EXPERTS - 1).astype(jnp.int32)
    n_valid = jnp.clip(jnp.take(cnt, block_expert) - (block_start - jnp.take(pad_starts, block_expert)), 0, bm)
    n_used = (pad_ends[-1:] // bm).astype(jnp.int32)
    xg = _sc_scatter_rows(x1p, dest, n_blocks * bm)
    yb = _moe_expert_call(xg, block_expert, n_used, n_valid.astype(jnp.int32), lw["layer"],
                          lw["moe_w1"], lw["moe_b1"], lw["moe_w2"], lw["moe_b2"])
    yg = _sc_gather_rows(yb, dest)
    return _moe_combine_call(x1, route, yg, lw["ln2_g"], lw["ln2_b"])


def _rotate_half_cols(w):
    half = RET_QK_DIM // 2
    w4 = w.reshape(w.shape[0], RET_HEADS, 2, half)
    return jnp.concatenate([-w4[:, :, 1:], w4[:, :, :1]], axis=2).reshape(w.shape)


def _pad_heads(w, axis):
    shp = list(w.shape)
    shp[axis:axis + 1] = [MEM_HEADS, MEM_HEAD_DIM]
    w = w.reshape(shp)
    pad = [(0, 0)] * w.ndim
    pad[axis + 1] = (0, MEM_HEAD_PAD - MEM_HEAD_DIM)
    w = jnp.pad(w, pad)
    shp[axis:axis + 2] = [MEM_HEADS * MEM_HEAD_PAD]
    return w.reshape(shp)


def _layer_weights(l, p):
    w_in = p["w_in"][l]
    aw, sw, qkw, vw, mw = 768, S5_WIDTH, RET_HEADS * RET_QK_DIM, RET_HEADS * RET_V_DIM, 768
    offs = [0]
    for wdt in (aw, aw, aw, sw, qkw, qkw, vw, vw, mw, 4 * D_MODEL):
        offs.append(offs[-1] + wdt)
    aq, ak, av, su, rq, rk, rv, rg, mq, gate = (w_in[:, offs[i]:offs[i + 1]] for i in range(10))
    seg = dict(gate=0.5 * gate, mq=_pad_heads(mq, 1), aq=aq[:, :GW], ak=ak[:, :GW], av=av[:, :GW],
               pad=jnp.zeros((D_MODEL, 2 * LANE), F32), rv=rv, rg=rg,
               rq=rq, rqs=_rotate_half_cols(rq), rk=rk, rks=_rotate_half_cols(rk))
    order = sorted(_SEG, key=_SEG.get)
    w_p = jnp.concatenate([seg[n] for n in order], axis=1).astype(BF16)
    assert w_p.shape[1] == NP
    w_d = jnp.concatenate([w[:, g * GW:(g + 1) * GW] for g in (1, 2) for w in (aq, ak, av)], axis=1).astype(BF16)
    wkv = p["w_mem_kv"][l]
    w_kv = jnp.concatenate([_pad_heads(wkv[:, :768], 1), _pad_heads(wkv[:, 768:], 1)], axis=1).astype(BF16)
    wb = p["w_branch"][l]
    rw = jnp.pad(p["router_w"][l].astype(F32), ((0, 0), (0, LANE - N_EXPERTS)))
    rb = jnp.pad(p["router_b"][l].astype(F32), (0, LANE - N_EXPERTS), constant_values=NEG_INF)
    return dict(
        w_p=w_p, w_d=w_d, w_su_t=jnp.transpose(su).astype(BF16), w_kv=w_kv,
        w_glu_t=jnp.transpose(p["s5_w_glu"][l]).astype(BF16), b_glu=p["s5_b_glu"][l].astype(F32).reshape(-1, 1),
        wb_a=wb[0].astype(BF16), wb_b=wb[1].astype(BF16), wb_c=wb[2].astype(BF16),
        wb_d=_pad_heads(wb[3], 0).astype(BF16), w_out=(0.5 * p["w_out"][l]).astype(BF16),
        ln1_g=p["ln1_g"][l].reshape(1, -1), ln1_b=p["ln1_b"][l].reshape(1, -1),
        router_w_hi=rw.astype(BF16), router_w_lo=(rw - rw.astype(BF16).astype(F32)).astype(BF16),
        router_b=rb.reshape(1, -1),
        layer=l, moe_w1=p["moe_w1"], moe_b1=p["moe_b1"][l].astype(F32).reshape(N_EXPERTS, 1, -1),
        moe_w2=p["moe_w2"], moe_b2=p["moe_b2"][l].astype(F32).reshape(N_EXPERTS, 1, -1),
        ln2_g=p["ln2_g"][l].reshape(1, -1), ln2_b=p["ln2_b"][l].reshape(1, -1),
        s5=_s5_tables(p["s5_lam_re"][l], p["s5_lam_im"][l], p["s5_log_step"][l], p["s5_b_re"][l],
                      p["s5_b_im"][l], p["s5_c_re"][l], p["s5_c_im"][l], p["s5_d"][l]))


def _trunk_layer(x, mem2d, b, s, lw, attn_bias, ret_tabs):
    proj, ut, pd = _proj_call(x, lw["w_p"], lw["w_su_t"], lw["w_d"])
    outs = [_attn_call(proj, attn_bias[0], b, s)]
    outs += [_dil_attn_call(pd, attn_bias[g], b, s, g, A_GROUPS[g][1]) for g in (1, 2)]
    aos, alses = [o for o, _ in outs], [l for _, l in outs]
    toep, win, wout, a_re, a_im = lw["s5"]
    nc = s // S5_CHUNK
    c_re, c_im = _s5_contrib_call(ut, win, b, nc)
    s_re, s_im = _s5_scan_call(c_re, c_im, a_re, a_im)
    zt = _s5_out_call(ut, toep, s_re, s_im, wout, b, nc)
    yc = _retention_call(proj, ret_tabs, b, s)
    kv = _mem_kv_call(mem2d, lw["w_kv"])
    yd = _mem_attn_call(proj, kv, b, s)
    x1, x1p, route, counts = _merge_call(x, aos, alses, zt, yc, yd, proj, lw)
    return _moe(x1, x1p, route, counts, lw)


def kernel(x_prompt, x_sample, mem_prompt, mem_sample, ln_in_g, ln_in_b, rel_bias, w_in, s5_lam_re, s5_lam_im, s5_log_step, s5_b_re, s5_b_im, s5_c_re, s5_c_im, s5_d, s5_w_glu, s5_b_glu, w_mem_kv, w_branch, w_out, ln1_g, ln1_b, router_w, router_b, moe_w1, moe_b1, moe_w2, moe_b2, ln2_g, ln2_b):
    p = dict(w_in=w_in, s5_lam_re=s5_lam_re, s5_lam_im=s5_lam_im, s5_log_step=s5_log_step, s5_b_re=s5_b_re,
             s5_b_im=s5_b_im, s5_c_re=s5_c_re, s5_c_im=s5_c_im, s5_d=s5_d, s5_w_glu=s5_w_glu, s5_b_glu=s5_b_glu,
             w_mem_kv=w_mem_kv, w_branch=w_branch, w_out=w_out, ln1_g=ln1_g, ln1_b=ln1_b, router_w=router_w,
             router_b=router_b, moe_w1=moe_w1, moe_b1=moe_b1, moe_w2=moe_w2, moe_b2=moe_b2, ln2_g=ln2_g,
             ln2_b=ln2_b)
    attn_bias = [_attn_bias_table(rel_bias, g, dil) for g, (_, dil) in enumerate(A_GROUPS)]
    trunks = []
    for x, mem in ((x_prompt, mem_prompt), (x_sample, mem_sample)):
        b, s, _ = x.shape
        trunks.append(dict(x=_layer_norm_call(x.reshape(b * s, D_MODEL), ln_in_g, ln_in_b),
                           mem=mem.reshape(b * N_MEM, D_MODEL), b=b, s=s, tabs=_ret_tables(s)))
    for l in range(DEPTH):
        lw = _layer_weights(l, p)
        for tr in trunks:
            tr["x"] = _trunk_layer(tr["x"], tr["mem"], tr["b"], tr["s"], lw, attn_bias, tr["tabs"])
    return tuple(tr["x"].reshape(tr["b"], tr["s"], D_MODEL) for tr in trunks)
```

```python
import functools
import math

import jax
import jax.numpy as jnp
from jax import lax
from jax.experimental import pallas as pl
from jax.experimental.pallas import tpu as pltpu
from jax.experimental.pallas import tpu_sc as plsc

F32 = jnp.float32
BF16 = jnp.bfloat16
HIGHEST = lax.Precision.HIGHEST

D_MODEL = 1024
DEPTH = 4
N_MEM = 256
A_GROUPS = ((128, 1), (512, 4), (2048, 16))
A_HEADS_PER_GROUP = 4
A_HEADS = 12
A_HEAD_DIM = 64
A_RADIUS = 64
REL_BUCKETS = 32
REL_MAX_DIST = 1024
S5_GROUP = 16
S5_WIDTH = 768
S5_GROUPS = 48
S5_STATE = 64
RET_HEADS = 6
RET_QK_DIM = 64
RET_V_DIM = 128
RET_CHUNK = 128
ROPE_BASE = 10000.0
MEM_HEADS = 4
MEM_HEAD_DIM = 192
MEM_HEAD_PAD = 256
N_EXPERTS = 32
TOP_K = 4
D_FF = 1024
SWIGLU_LIMIT = 7.0
SWIGLU_ALPHA = 1.702
LN_EPS = 1e-5
DEEPNORM_ALPHA = (2 * DEPTH) ** 0.25
NEG_INF = -1e30

LANE = 128
VMEM_LIMIT = 56 * 1024 * 1024
S5_CHUNK = 128
MOE_BLOCK_ROWS = 512
SC_GATHER_WINDOW = 64

_SEG = {}
_off = 0
for _name, _width, _align in (
        ("gate", 32, 8), ("mq", 8, 8), ("aq", 2, 2), ("ak", 2, 2), ("av", 2, 2), ("pad", 2, 2),
        ("rv", 6, 6), ("rg", 6, 6), ("rq", 3, 3), ("rqs", 3, 3), ("rk", 3, 3), ("rks", 3, 3)):
    assert _off % _align == 0, (_name, _off)
    _SEG[_name] = _off
    _off += _width
NP_UNITS = _off
NP = NP_UNITS * LANE
assert NP % 256 == 0
GW = A_HEADS_PER_GROUP * A_HEAD_DIM
ND = 2 * 3 * GW


def _cparams(sem, vmem=VMEM_LIMIT):
    return pltpu.CompilerParams(dimension_semantics=sem, vmem_limit_bytes=vmem)


def _sigmoid(x):
    return 0.5 * jnp.tanh(0.5 * x) + 0.5


def _ln(h, g, b):
    mu = jnp.mean(h, axis=-1, keepdims=True)
    d = h - mu
    var = jnp.mean(d * d, axis=-1, keepdims=True)
    return d * lax.rsqrt(var + LN_EPS) * g + b


def _ln_kernel(x_ref, g_ref, b_ref, o_ref):
    o_ref[...] = _ln(x_ref[...], g_ref[...], b_ref[...])


def _layer_norm_call(x, g, b):
    t = x.shape[0]
    tm = 512
    return pl.pallas_call(
        _ln_kernel, name="input_ln",
        grid=(t // tm,),
        in_specs=[pl.BlockSpec((tm, D_MODEL), lambda i: (i, 0)),
                  pl.BlockSpec((1, D_MODEL), lambda i: (0, 0)),
                  pl.BlockSpec((1, D_MODEL), lambda i: (0, 0))],
        out_specs=pl.BlockSpec((tm, D_MODEL), lambda i: (i, 0)),
        out_shape=jax.ShapeDtypeStruct((t, D_MODEL), F32),
        compiler_params=_cparams(("parallel",)),
    )(x, g.reshape(1, -1), b.reshape(1, -1))


def _proj_kernel(x_ref, w_ref, wsut_ref, wd_ref, p_ref, ut_ref, pd_ref, xb_ref):
    @pl.when(pl.program_id(1) == 0)
    def _():
        xb = x_ref[...].astype(BF16)
        xb_ref[...] = xb
        ut_ref[...] = lax.dot_general(wsut_ref[...], xb, (((1,), (1,)), ((), ())),
                                      preferred_element_type=F32).astype(BF16)
        pd = jnp.dot(xb, wd_ref[...], preferred_element_type=F32)
        for c in range(ND // LANE):
            pd_ref[c] = pd[:, c * LANE:(c + 1) * LANE]

    p_ref[...] = jnp.dot(xb_ref[...], w_ref[...], preferred_element_type=F32).astype(BF16)


def _proj_call(x, w_p, w_su_t, w_d):
    t = x.shape[0]
    tm, tn = 512, NP // 3
    return pl.pallas_call(
        _proj_kernel, name="in_proj",
        grid=(t // tm, NP // tn),
        in_specs=[pl.BlockSpec((tm, D_MODEL), lambda i, j: (i, 0)),
                  pl.BlockSpec((D_MODEL, tn), lambda i, j: (0, j)),
                  pl.BlockSpec((S5_WIDTH, D_MODEL), lambda i, j: (0, 0)),
                  pl.BlockSpec((D_MODEL, ND), lambda i, j: (0, 0))],
        out_specs=[pl.BlockSpec((tm, tn), lambda i, j: (i, j)),
                   pl.BlockSpec((S5_WIDTH, tm), lambda i, j: (0, i)),
                   pl.BlockSpec((ND // LANE, tm, LANE), lambda i, j: (0, i, 0))],
        out_shape=[jax.ShapeDtypeStruct((t, NP), BF16),
                   jax.ShapeDtypeStruct((S5_WIDTH, t), BF16),
                   jax.ShapeDtypeStruct((ND // LANE, t, LANE), F32)],
        scratch_shapes=[pltpu.VMEM((tm, D_MODEL), BF16)],
        compiler_params=_cparams(("parallel", "arbitrary")),
    )(x, w_p, w_su_t, w_d)


ATTN_TQ = 2 * A_RADIUS


def _banded_heads(q, kwin, vwin, bias_ref, first_key, sub_len):
    tq, nk = q.shape[0], kwin.shape[0]
    kidx = first_key + lax.broadcasted_iota(jnp.int32, (tq, nk), 1)
    valid = (kidx >= 0) & (kidx < sub_len)
    lane_head = lax.broadcasted_iota(jnp.int32, (1, q.shape[1]), 1) // A_HEAD_DIM
    scale = jnp.asarray(A_HEAD_DIM ** -0.5, BF16)
    heads = range(A_HEADS_PER_GROUP)
    qs = [jnp.where(lane_head == h, q, jnp.zeros_like(q)) * scale for h in heads]
    ss = [lax.dot_general(qh, kwin, (((1,), (1,)), ((), ())), preferred_element_type=F32) for qh in qs]
    ss = [jnp.where(valid, s + bias_ref[h], NEG_INF) for h, s in zip(heads, ss)]
    ms = [jnp.max(s, axis=-1, keepdims=True) for s in ss]
    ps = [jnp.exp(s - m) for s, m in zip(ss, ms)]
    dens = [jnp.sum(p, axis=-1, keepdims=True) for p in ps]
    ohs = [jnp.dot(p.astype(BF16), vwin, preferred_element_type=F32) for p in ps]
    acc = jnp.zeros((tq, q.shape[1]), F32)
    lse_acc = jnp.zeros((tq, q.shape[1]), F32)
    for h in heads:
        hm = lane_head == h
        acc = jnp.where(hm, ohs[h] / dens[h], acc)
        lse_acc = jnp.where(hm, ms[h] + jnp.log(dens[h]), lse_acc)
    return acc, lse_acc


def _attn_kernel(q_ref, kp_ref, kc_ref, kn_ref, vp_ref, vc_ref, vn_ref, bias_ref, o_ref, lse_ref, *, sub_len):
    tq, half = ATTN_TQ, A_RADIUS
    kwin = jnp.concatenate([kp_ref[tq - half:, :], kc_ref[...], kn_ref[:half, :]], axis=0)
    vwin = jnp.concatenate([vp_ref[tq - half:, :], vc_ref[...], vn_ref[:half, :]], axis=0)
    acc, lse = _banded_heads(q_ref[...], kwin, vwin, bias_ref, pl.program_id(1) * tq - half, sub_len)
    o_ref[...] = acc
    lse_ref[...] = lse


def _attn_call(proj, bias, b, s):
    tq = ATTN_TQ
    nqb = s // tq
    pv = proj.reshape(b, s, NP)
    qo, ko, vo = (_SEG[n] * LANE // GW for n in ("aq", "ak", "av"))
    blk = (None, tq, GW)
    cur = lambda off: (lambda bi, qi: (bi, qi, off))
    prv = lambda off: (lambda bi, qi: (bi, jnp.maximum(qi - 1, 0), off))
    nxt = lambda off: (lambda bi, qi: (bi, jnp.minimum(qi + 1, nqb - 1), off))
    out_map = lambda bi, qi: (bi, qi, 0)
    o, lse = pl.pallas_call(
        functools.partial(_attn_kernel, sub_len=s), name="window_attn_g0",
        grid=(b, nqb),
        in_specs=[pl.BlockSpec(blk, cur(qo)),
                  pl.BlockSpec(blk, prv(ko)), pl.BlockSpec(blk, cur(ko)), pl.BlockSpec(blk, nxt(ko)),
                  pl.BlockSpec(blk, prv(vo)), pl.BlockSpec(blk, cur(vo)), pl.BlockSpec(blk, nxt(vo)),
                  pl.BlockSpec((A_HEADS_PER_GROUP, tq, 2 * tq), lambda bi, qi: (0, 0, 0))],
        out_specs=[pl.BlockSpec(blk, out_map), pl.BlockSpec(blk, out_map)],
        out_shape=[jax.ShapeDtypeStruct((b, s, GW), F32), jax.ShapeDtypeStruct((b, s, GW), F32)],
        compiler_params=_cparams(("parallel", "parallel")),
    )(pv, pv, pv, pv, pv, pv, pv, bias)
    return o.reshape(b * s, GW), lse.reshape(b * s, GW)


def _dil_attn_kernel(q_ref, kp_ref, kc_ref, kn_ref, vp_ref, vc_ref, vn_ref, bias_ref, o_ref, lse_ref, *,
                     dil, sub_len):
    tq, half = ATTN_TQ, A_RADIUS
    first_key = pl.program_id(1) * tq - half

    def residue(r, carry):
        def rows(ref, n):
            return jnp.concatenate([ref[c, pl.ds(r, n, stride=dil), :] for c in range(GW // LANE)], axis=-1)

        q = rows(q_ref, tq).astype(BF16)
        kwin = jnp.concatenate([rows(kp_ref, half), rows(kc_ref, tq), rows(kn_ref, half)], axis=0).astype(BF16)
        vwin = jnp.concatenate([rows(vp_ref, half), rows(vc_ref, tq), rows(vn_ref, half)], axis=0).astype(BF16)
        acc, lse = _banded_heads(q, kwin, vwin, bias_ref, first_key, sub_len)
        for c in range(GW // LANE):
            o_ref[c, pl.ds(r, tq, stride=dil), :] = acc[:, c * LANE:(c + 1) * LANE]
            lse_ref[c, pl.ds(r, tq, stride=dil), :] = lse[:, c * LANE:(c + 1) * LANE]
        return carry

    lax.fori_loop(0, dil, residue, 0)


def _dil_attn_call(pd, bias, b, s, g, dil):
    tq = ATTN_TQ
    rows = tq * dil
    nblk = s // rows
    nslab = GW // LANE
    pv = pd.reshape(ND // LANE, b, s, LANE)
    qo, ko, vo = (3 * (g - 1) + j for j in range(3))
    blk, hblk = (nslab, None, rows, LANE), (nslab, None, rows // 2, LANE)
    cur = lambda off: (lambda bi, i: (off, bi, i, 0))
    prv = lambda off: (lambda bi, i: (off, bi, jnp.maximum(2 * i - 1, 0), 0))
    nxt = lambda off: (lambda bi, i: (off, bi, jnp.minimum(2 * i + 2, 2 * nblk - 1), 0))
    out_map = lambda bi, i: (0, bi, i, 0)
    o, lse = pl.pallas_call(
        functools.partial(_dil_attn_kernel, dil=dil, sub_len=s // dil), name=f"dilated_attn_g{g}",
        grid=(b, nblk),
        in_specs=[pl.BlockSpec(blk, cur(qo)),
                  pl.BlockSpec(hblk, prv(ko)), pl.BlockSpec(blk, cur(ko)), pl.BlockSpec(hblk, nxt(ko)),
                  pl.BlockSpec(hblk, prv(vo)), pl.BlockSpec(blk, cur(vo)), pl.BlockSpec(hblk, nxt(vo)),
                  pl.BlockSpec((A_HEADS_PER_GROUP, tq, 2 * tq), lambda bi, i: (0, 0, 0))],
        out_specs=[pl.BlockSpec(blk, out_map), pl.BlockSpec(blk, out_map)],
        out_shape=[jax.ShapeDtypeStruct((nslab, b, s, LANE), F32)] * 2,
        compiler_params=_cparams(("parallel", "parallel")),
    )(pv, pv, pv, pv, pv, pv, pv, bias)
    return o.reshape(nslab, b * s, LANE), lse.reshape(nslab, b * s, LANE)


def _t5_bucket(rel):
    nb = REL_BUCKETS // 2
    max_exact = nb // 2
    ret = jnp.where(rel > 0, nb, 0)
    n = jnp.abs(rel)
    nf = jnp.maximum(n, 1).astype(F32)
    large = max_exact + (jnp.log(nf / max_exact) / math.log(REL_MAX_DIST / max_exact)
                         * (nb - max_exact)).astype(jnp.int32)
    large = jnp.minimum(large, nb - 1)
    return ret + jnp.where(n < max_exact, n, large)


def _attn_bias_table(rel_bias, g, dil, tq=128):
    qi = jnp.arange(tq)[:, None]
    kj = jnp.arange(2 * tq)[None, :] - A_RADIUS
    rel = kj - qi
    tab = rel_bias[:, g * A_HEADS_PER_GROUP:(g + 1) * A_HEADS_PER_GROUP].astype(F32)
    bucket = _t5_bucket(rel * dil)[None]
    bias = jnp.zeros((A_HEADS_PER_GROUP,) + rel.shape, F32)
    for bkt in range(REL_BUCKETS):
        bias = jnp.where(bucket == bkt, tab[bkt][:, None, None], bias)
    return jnp.where((jnp.abs(rel) <= A_RADIUS)[None], bias, NEG_INF)


def _s5_toeplitz_kernel(bcf_ref, pwf_ref, bcb_ref, pwb_ref, d_ref, t_ref, gf_ref, gb_ref):
    gf_ref[...] = jnp.dot(bcf_ref[...], pwf_ref[...], precision=HIGHEST, preferred_element_type=F32)
    gb_ref[...] = jnp.dot(bcb_ref[...], pwb_ref[...], precision=HIGHEST, preferred_element_type=F32)
    c = S5_CHUNK
    row = lax.broadcasted_iota(jnp.int32, (c, c), 0)
    col = lax.broadcasted_iota(jnp.int32, (c, c), 1)

    def body(ci, carry):
        dval = d_ref[pl.ds(ci, 1), :]
        for co in range(S5_GROUP):
            r = ci * S5_GROUP + co
            gf = jnp.broadcast_to(gf_ref[pl.ds(r, 1), :], (c, c))
            gb = jnp.broadcast_to(gb_ref[pl.ds(r, 1), :], (c, c))
            tf = pltpu.roll(gf, 0, 1, stride=1, stride_axis=0)
            tb = pltpu.roll(gb, 1, 1, stride=1, stride_axis=0)
            tile = jnp.where(col >= row, tf, 0.0) + jnp.where(row >= col, tb, 0.0)
            tile = tile + jnp.where((row == col) & (ci == co), dval, 0.0)
            t_ref[pl.ds(pl.multiple_of(ci * c, c), c), co * c:(co + 1) * c] = tile.astype(BF16)
        return carry

    lax.fori_loop(0, S5_GROUP, body, 0)


def _s5_toeplitz_call(bcf, pwf, bcb, pwb, dskip):
    n = S5_GROUP * S5_CHUNK
    return pl.pallas_call(
        _s5_toeplitz_kernel, name="s5_toeplitz",
        grid=(S5_GROUPS,),
        in_specs=[pl.BlockSpec((None, S5_GROUP * S5_GROUP, 2 * S5_STATE), lambda g: (g, 0, 0)),
                  pl.BlockSpec((None, 2 * S5_STATE, S5_CHUNK), lambda g: (g, 0, 0)),
                  pl.BlockSpec((None, S5_GROUP * S5_GROUP, 2 * S5_STATE), lambda g: (g, 0, 0)),
                  pl.BlockSpec((None, 2 * S5_STATE, S5_CHUNK), lambda g: (g, 0, 0)),
                  pl.BlockSpec((None, S5_GROUP, LANE), lambda g: (g, 0, 0))],
        out_specs=pl.BlockSpec((None, n, n), lambda g: (g, 0, 0)),
        out_shape=jax.ShapeDtypeStruct((S5_GROUPS, n, n), BF16),
        scratch_shapes=[pltpu.VMEM((S5_GROUP * S5_GROUP, S5_CHUNK), F32),
                        pltpu.VMEM((S5_GROUP * S5_GROUP, S5_CHUNK), F32)],
        compiler_params=_cparams(("parallel",)),
    )(bcf, pwf, bcb, pwb, dskip)


def _s5_tables(lam_re, lam_im, log_step, b_re, b_im, c_re, c_im, d_skip):
    c = S5_CHUNK
    k = jnp.arange(c, dtype=F32)
    per_dir = []
    for direction in range(2):
        step = jnp.exp(log_step[direction].astype(F32))[:, None]
        lr, li = lam_re[direction].astype(F32), lam_im[direction].astype(F32)
        mag = jnp.exp(lr * step)
        ar, ai = mag * jnp.cos(li * step), mag * jnp.sin(li * step)
        nr, ni = ar - 1.0, ai
        den = lr * lr + li * li
        fr = (nr * lr + ni * li) / den
        fi = (ni * lr - nr * li) / den
        br, bi = b_re[direction].astype(F32), b_im[direction].astype(F32)
        bbr = fr[..., None] * br - fi[..., None] * bi
        bbi = fr[..., None] * bi + fi[..., None] * br
        cr, cim = c_re[direction].astype(F32), c_im[direction].astype(F32)

        def power(e, log_mag=lr * step, phase=li * step):
            m = jnp.exp(e[None, None, :] * log_mag[..., None])
            th = e[None, None, :] * phase[..., None]
            return m * jnp.cos(th), m * jnp.sin(th)

        per_dir.append(dict(bbr=bbr, bbi=bbi, cr=cr, cim=cim, power=power))

    def bc_table(d):
        bbr_t = jnp.transpose(d["bbr"], (0, 2, 1))[:, :, None, :]
        bbi_t = jnp.transpose(d["bbi"], (0, 2, 1))[:, :, None, :]
        cr, cim = d["cr"][:, None], d["cim"][:, None]
        re = bbr_t * cr - bbi_t * cim
        im = bbr_t * cim + bbi_t * cr
        return jnp.concatenate([re, -im], axis=-1).reshape(S5_GROUPS, S5_GROUP * S5_GROUP, 2 * S5_STATE)

    f, bw = per_dir
    pfr, pfi = f["power"](k)
    pbr, pbi = bw["power"](c - 1 - k)
    pwf = jnp.concatenate([pfr, pfi], axis=1)
    pwb = jnp.concatenate([pbr, pbi], axis=1)
    dsk = jnp.broadcast_to(d_skip.astype(F32).reshape(S5_GROUPS, S5_GROUP, 1), (S5_GROUPS, S5_GROUP, LANE))
    toep = _s5_toeplitz_call(bc_table(f), pwf, bc_table(bw), pwb, dsk)

    def w_in(d, e):
        pr, pi = d["power"](e)
        re = d["bbr"][:, :, :, None] * pr[:, :, None, :] - d["bbi"][:, :, :, None] * pi[:, :, None, :]
        im = d["bbr"][:, :, :, None] * pi[:, :, None, :] + d["bbi"][:, :, :, None] * pr[:, :, None, :]
        tr = lambda z: jnp.transpose(z, (0, 2, 3, 1)).reshape(S5_GROUPS, S5_GROUP * c, S5_STATE)
        return tr(re), tr(im)

    fre, fim = w_in(f, c - 1 - k)
    bre, bim = w_in(bw, k)
    win = jnp.concatenate([fre, bre, fim, bim], axis=-1).astype(BF16)

    def w_out(d, e):
        pr, pi = d["power"](e)
        re = d["cr"][:, :, :, None] * pr[:, None] - d["cim"][:, :, :, None] * pi[:, None]
        im = d["cr"][:, :, :, None] * pi[:, None] + d["cim"][:, :, :, None] * pr[:, None]
        tr = lambda z: jnp.transpose(z, (0, 2, 1, 3)).reshape(S5_GROUPS, S5_STATE, S5_GROUP * c)
        return tr(re), tr(-im)

    ofre, ofim = w_out(f, k + 1.0)
    obre, obim = w_out(bw, c - k)
    wout = jnp.concatenate([ofre, obre, ofim, obim], axis=1).astype(BF16)

    cc = jnp.asarray([float(c)], F32)
    afr, afi = f["power"](cc)
    abr, abi = bw["power"](cc)
    a_re = jnp.concatenate([afr[..., 0], abr[..., 0]], axis=-1).reshape(1, -1)
    a_im = jnp.concatenate([afi[..., 0], abi[..., 0]], axis=-1).reshape(1, -1)
    return toep, win, wout, a_re, a_im


def _s5_contrib_kernel(u_ref, win_ref, re_ref, im_ref):
    uc = jnp.concatenate([u_ref[ci] for ci in range(S5_GROUP)], axis=-1)
    res = jnp.dot(uc, win_ref[...], preferred_element_type=F32)
    nc, nb = re_ref.shape[0], re_ref.shape[1]
    half = 2 * S5_STATE
    for bi in range(nb):
        re_ref[:, bi, :] = res[bi * nc:(bi + 1) * nc, :half]
        im_ref[:, bi, :] = res[bi * nc:(bi + 1) * nc, half:]


def _s5_contrib_call(ut, win, b, nc):
    nch = b * nc
    half = 2 * S5_STATE
    uv = ut.reshape(S5_GROUPS, S5_GROUP, nch, S5_CHUNK)
    return pl.pallas_call(
        _s5_contrib_kernel, name="s5_contrib",
        grid=(S5_GROUPS,),
        in_specs=[pl.BlockSpec((None, S5_GROUP, nch, S5_CHUNK), lambda g: (g, 0, 0, 0)),
                  pl.BlockSpec((None, S5_GROUP * S5_CHUNK, 4 * S5_STATE), lambda g: (g, 0, 0))],
        out_specs=[pl.BlockSpec((nc, b, half), lambda g: (0, 0, g))] * 2,
        out_shape=[jax.ShapeDtypeStruct((nc, b, S5_GROUPS * half), F32)] * 2,
        compiler_params=_cparams(("parallel",)),
    )(uv, win)


def _s5_scan_kernel(cre_ref, cim_ref, are_ref, aim_ref, ore_ref, oim_ref):
    nc, nb, width = cre_ref.shape
    a_re, a_im = are_ref[...], aim_ref[...]
    lane = lax.broadcasted_iota(jnp.int32, (1, width), 1)
    fwd_lane = (lane % (2 * S5_STATE)) < S5_STATE
    zero = jnp.zeros((nb, width), F32)

    def advance(s, c):
        sr, si = s
        return a_re * sr - a_im * si + cre_ref[c], a_re * si + a_im * sr + cim_ref[c]

    def up(c, s):
        ore_ref[c] = s[0]
        oim_ref[c] = s[1]
        return advance(s, c)

    lax.fori_loop(0, nc, up, (zero, zero))

    def down(i, s):
        c = nc - 1 - i
        ore_ref[c] = jnp.where(fwd_lane, ore_ref[c], s[0])
        oim_ref[c] = jnp.where(fwd_lane, oim_ref[c], s[1])
        return advance(s, c)

    lax.fori_loop(0, nc, down, (zero, zero))


def _s5_scan_call(cre, cim, a_re, a_im):
    return pl.pallas_call(
        _s5_scan_kernel, name="s5_chunk_scan",
        out_shape=[jax.ShapeDtypeStruct(cre.shape, F32)] * 2,
        compiler_params=pltpu.CompilerParams(vmem_limit_bytes=VMEM_LIMIT),
    )(cre, cim, a_re, a_im)


def _gelu_tanh(y):
    return 0.5 * y * (1.0 + jnp.tanh(math.sqrt(2.0 / math.pi) * (y + 0.044715 * (y * y * y))))


def _s5_out_kernel(u_ref, t_ref, sre_ref, sim_ref, wout_ref, z_ref):
    uc = jnp.concatenate([u_ref[ci] for ci in range(S5_GROUP)], axis=-1)
    y = jnp.dot(uc, t_ref[...], preferred_element_type=F32)
    sp = jnp.concatenate([jnp.concatenate([sre_ref[:, bi, :], sim_ref[:, bi, :]], axis=-1)
                          for bi in range(sre_ref.shape[1])], axis=0)
    y = y + jnp.dot(sp.astype(BF16), wout_ref[...], preferred_element_type=F32)
    z = _gelu_tanh(y)
    for co in range(S5_GROUP):
        z_ref[co] = z[:, co * S5_CHUNK:(co + 1) * S5_CHUNK].astype(BF16)


def _s5_out_call(ut, toep, s_re, s_im, wout, b, nc):
    nch = b * nc
    uv = ut.reshape(S5_GROUPS, S5_GROUP, nch, S5_CHUNK)
    n = S5_GROUP * S5_CHUNK
    half = 2 * S5_STATE
    zt = pl.pallas_call(
        _s5_out_kernel, name="s5_out",
        grid=(S5_GROUPS,),
        in_specs=[pl.BlockSpec((None, S5_GROUP, nch, S5_CHUNK), lambda g: (g, 0, 0, 0)),
                  pl.BlockSpec((None, n, n), lambda g: (g, 0, 0)),
                  pl.BlockSpec((nc, b, half), lambda g: (0, 0, g)),
                  pl.BlockSpec((nc, b, half), lambda g: (0, 0, g)),
                  pl.BlockSpec((None, 4 * S5_STATE, n), lambda g: (g, 0, 0))],
        out_specs=pl.BlockSpec((None, S5_GROUP, nch, S5_CHUNK), lambda g: (g, 0, 0, 0)),
        out_shape=jax.ShapeDtypeStruct((S5_GROUPS, S5_GROUP, nch, S5_CHUNK), BF16),
        compiler_params=_cparams(("parallel",)),
    )(uv, toep, s_re, s_im, wout)
    return zt.reshape(S5_WIDTH, nch * S5_CHUNK)


def _ret_tables(s):
    c = RET_CHUNK
    half = RET_QK_DIM // 2
    inv = ROPE_BASE ** (-jnp.arange(half, dtype=F32) / half)
    ang = jnp.arange(s, dtype=F32)[:, None] * inv[None, :]
    cos = jnp.tile(jnp.cos(ang), (1, 2 * RET_HEADS))
    sin = jnp.tile(jnp.sin(ang), (1, 2 * RET_HEADS))
    hidx = jnp.arange(RET_HEADS, dtype=F32)
    lgf = jnp.log1p(-jnp.exp2(-5.0 - hidx))
    lgb = jnp.log1p(-jnp.exp2(-5.5 - hidx))
    j = jnp.arange(c, dtype=F32)
    rel = j[:, None] - j[None, :]
    intra = jnp.exp(jnp.abs(rel)[None] * jnp.where(rel[None] >= 0, lgf[:, None, None], lgb[:, None, None]))
    qk_f = jnp.repeat(lgf, RET_QK_DIM)[None, :]
    qk_b = jnp.repeat(lgb, RET_QK_DIM)[None, :]
    v_f = jnp.repeat(lgf, RET_V_DIM)[None, :]
    v_b = jnp.repeat(lgb, RET_V_DIM)[None, :]
    tabs = dict(
        cos=cos, sin=sin, intra=intra,
        tail_f=jnp.exp((c - 1 - j)[:, None] * qk_f), head_b=jnp.exp(j[:, None] * qk_b),
        decq_f=jnp.exp((j + 1.0)[:, None] * qk_f), decq_b=jnp.exp((c - j)[:, None] * qk_b),
        chunk_f=jnp.exp(c * v_f), chunk_b=jnp.exp(c * v_b),
        bd=(jnp.arange(RET_HEADS * RET_QK_DIM)[:, None] // RET_QK_DIM
            == jnp.arange(RET_HEADS * RET_V_DIM)[None, :] // RET_V_DIM).astype(F32))
    return tabs


def _rot(x_ref, xs_ref, cos_ref, sin_ref):
    return x_ref[...].astype(F32) * cos_ref[...] + xs_ref[...].astype(F32) * sin_ref[...]


def _ret_bwd_kernel(k_ref, ks_ref, v_ref, cos_ref, sin_ref, headb_ref, chunkb_ref, bd_ref, sb_ref, st_ref):
    @pl.when(pl.program_id(1) == 0)
    def _():
        st_ref[...] = jnp.zeros_like(st_ref)

    for h in range(RET_HEADS):
        sb_ref[h] = st_ref[h * RET_QK_DIM:(h + 1) * RET_QK_DIM, h * RET_V_DIM:(h + 1) * RET_V_DIM].astype(BF16)
    kr = _rot(k_ref, ks_ref, cos_ref, sin_ref) * (RET_QK_DIM ** -0.5)
    kwt = jnp.transpose(kr * headb_ref[...]).astype(BF16)
    kv = jnp.dot(kwt, v_ref[...], preferred_element_type=F32)
    st_ref[...] = st_ref[...] * chunkb_ref[...] + kv * bd_ref[...]


def _ret_fwd_kernel(q_ref, qs_ref, k_ref, ks_ref, v_ref, g_ref, cos_ref, sin_ref, sb_ref, intra_ref,
                    tailf_ref, decqf_ref, decqb_ref, chunkf_ref, bd_ref, o_ref, st_ref, sbd_ref):
    @pl.when(pl.program_id(1) == 0)
    def _():
        st_ref[...] = jnp.zeros_like(st_ref)
        sbd_ref[...] = jnp.zeros_like(sbd_ref)

    for h in range(RET_HEADS):
        sbd_ref[h * RET_QK_DIM:(h + 1) * RET_QK_DIM, h * RET_V_DIM:(h + 1) * RET_V_DIM] = sb_ref[h]
    qr = _rot(q_ref, qs_ref, cos_ref, sin_ref)
    kr = _rot(k_ref, ks_ref, cos_ref, sin_ref) * (RET_QK_DIM ** -0.5)
    qb, kb = qr.astype(BF16), kr.astype(BF16)
    vb = v_ref[...]
    lane_head = lax.broadcasted_iota(jnp.int32, (1, qb.shape[1]), 1) // RET_QK_DIM
    inter = jnp.dot((qr * decqf_ref[...]).astype(BF16), st_ref[...].astype(BF16), preferred_element_type=F32)
    inter = inter + jnp.dot((qr * decqb_ref[...]).astype(BF16), sbd_ref[...], preferred_element_type=F32)
    heads = range(RET_HEADS)
    vcols = [slice(h * RET_V_DIM, (h + 1) * RET_V_DIM) for h in heads]
    qhs = [jnp.where(lane_head == h, qb, jnp.zeros_like(qb)) for h in heads]
    scs = [lax.dot_general(qh, kb, (((1,), (1,)), ((), ())), preferred_element_type=F32) * intra_ref[h]
           for h, qh in zip(heads, qhs)]
    ohs = [jnp.dot(sc.astype(BF16), vb[:, vs], preferred_element_type=F32) + inter[:, vs]
           for sc, vs in zip(scs, vcols)]
    mus = [jnp.mean(oh, axis=-1, keepdims=True) for oh in ohs]
    dds = [oh - mu for oh, mu in zip(ohs, mus)]
    vrs = [jnp.mean(dd * dd, axis=-1, keepdims=True) for dd in dds]
    on = jnp.concatenate([dd * lax.rsqrt(var + LN_EPS) for dd, var in zip(dds, vrs)], axis=-1)
    gg = g_ref[...].astype(F32)
    o_ref[...] = (on * (gg * _sigmoid(gg))).astype(BF16)
    kwt = jnp.transpose(kr * tailf_ref[...]).astype(BF16)
    kv = jnp.dot(kwt, vb, preferred_element_type=F32)
    st_ref[...] = st_ref[...] * chunkf_ref[...] + kv * bd_ref[...]


def _retention_call(proj, tabs, b, s):
    c = RET_CHUNK
    nc = s // c
    qkw, vw = RET_HEADS * RET_QK_DIM, RET_HEADS * RET_V_DIM
    pv = proj.reshape(b, s, NP)
    qk_blk, v_blk = (None, c, qkw), (None, c, vw)
    qo, qso, ko, kso = (_SEG[n] * LANE // qkw for n in ("rq", "rqs", "rk", "rks"))
    vo, go = (_SEG[n] * LANE // vw for n in ("rv", "rg"))
    const2 = lambda shape: pl.BlockSpec(shape, lambda bi, i: (0, 0))
    rev = lambda off: (lambda bi, i: (bi, nc - 1 - i, off))
    fwd = lambda off: (lambda bi, i: (bi, i, off))
    sb = pl.pallas_call(
        _ret_bwd_kernel, name="retention_bwd_state",
        grid=(b, nc),
        in_specs=[pl.BlockSpec(qk_blk, rev(ko)), pl.BlockSpec(qk_blk, rev(kso)), pl.BlockSpec(v_blk, rev(vo)),
                  pl.BlockSpec((c, qkw), lambda bi, i: (nc - 1 - i, 0)),
                  pl.BlockSpec((c, qkw), lambda bi, i: (nc - 1 - i, 0)),
                  const2((c, qkw)), const2((1, vw)), const2((qkw, vw))],
        out_specs=pl.BlockSpec((None, None, RET_HEADS, RET_QK_DIM, RET_V_DIM),
                               lambda bi, i: (bi, nc - 1 - i, 0, 0, 0)),
        out_shape=jax.ShapeDtypeStruct((b, nc, RET_HEADS, RET_QK_DIM, RET_V_DIM), BF16),
        scratch_shapes=[pltpu.VMEM((qkw, vw), F32)],
        compiler_params=_cparams(("parallel", "arbitrary")),
    )(pv, pv, pv, tabs["cos"], tabs["sin"], tabs["head_b"], tabs["chunk_b"], tabs["bd"])
    yc = pl.pallas_call(
        _ret_fwd_kernel, name="retention_fwd",
        grid=(b, nc),
        in_specs=[pl.BlockSpec(qk_blk, fwd(qo)), pl.BlockSpec(qk_blk, fwd(qso)),
                  pl.BlockSpec(qk_blk, fwd(ko)), pl.BlockSpec(qk_blk, fwd(kso)),
                  pl.BlockSpec(v_blk, fwd(vo)), pl.BlockSpec(v_blk, fwd(go)),
                  pl.BlockSpec((c, qkw), lambda bi, i: (i, 0)), pl.BlockSpec((c, qkw), lambda bi, i: (i, 0)),
                  pl.BlockSpec((None, None, RET_HEADS, RET_QK_DIM, RET_V_DIM), lambda bi, i: (bi, i, 0, 0, 0)),
                  pl.BlockSpec((RET_HEADS, c, c), lambda bi, i: (0, 0, 0)),
                  const2((c, qkw)), const2((c, qkw)), const2((c, qkw)), const2((1, vw)), const2((qkw, vw))],
        out_specs=pl.BlockSpec((None, c, vw), lambda bi, i: (bi, i, 0)),
        out_shape=jax.ShapeDtypeStruct((b, s, vw), BF16),
        scratch_shapes=[pltpu.VMEM((qkw, vw), F32), pltpu.VMEM((qkw, vw), BF16)],
        compiler_params=_cparams(("parallel", "arbitrary")),
    )(pv, pv, pv, pv, pv, pv, tabs["cos"], tabs["sin"], sb, tabs["intra"],
      tabs["tail_f"], tabs["decq_f"], tabs["decq_b"], tabs["chunk_f"], tabs["bd"])
    return yc.reshape(b * s, vw)


def _mm_kernel(x_ref, w_ref, o_ref):
    o_ref[...] = jnp.dot(x_ref[...].astype(BF16), w_ref[...], preferred_element_type=F32).astype(o_ref.dtype)


def _mem_kv_call(mem2d, w_kv):
    m, n = mem2d.shape[0], w_kv.shape[1]
    tm = 512
    return pl.pallas_call(
        _mm_kernel, name="mem_kv_proj",
        grid=(m // tm,),
        in_specs=[pl.BlockSpec((tm, D_MODEL), lambda i: (i, 0)),
                  pl.BlockSpec((D_MODEL, n), lambda i: (0, 0))],
        out_specs=pl.BlockSpec((tm, n), lambda i: (i, 0)),
        out_shape=jax.ShapeDtypeStruct((m, n), BF16),
        compiler_params=_cparams(("parallel",)),
    )(mem2d, w_kv)


def _mem_attn_kernel(q_ref, k_ref, v_ref, o_ref):
    q = q_ref[...]
    cols = [slice(h * MEM_HEAD_PAD, (h + 1) * MEM_HEAD_PAD) for h in range(MEM_HEADS)]
    ss = [lax.dot_general(q[:, hs], k_ref[:, hs], (((1,), (1,)), ((), ())),
                          preferred_element_type=F32) * (MEM_HEAD_DIM ** -0.5) for hs in cols]
    ms = [jnp.max(s, axis=-1, keepdims=True) for s in ss]
    es = [jnp.exp(s - m) for s, m in zip(ss, ms)]
    ps = [e / jnp.sum(e, axis=-1, keepdims=True) for e in es]
    outs = [jnp.dot(p.astype(BF16), v_ref[:, hs], preferred_element_type=F32) for p, hs in zip(ps, cols)]
    o_ref[...] = jnp.concatenate(outs, axis=-1).astype(BF16)


def _mem_attn_call(proj, kv, b, s):
    tq = 256
    w = MEM_HEADS * MEM_HEAD_PAD
    pv = proj.reshape(b, s, NP)
    kvv = kv.reshape(b, N_MEM, 2 * w)
    qo = _SEG["mq"] * LANE // w
    yd = pl.pallas_call(
        _mem_attn_kernel, name="mem_attn",
        grid=(b, s // tq),
        in_specs=[pl.BlockSpec((None, tq, w), lambda bi, i: (bi, i, qo)),
                  pl.BlockSpec((None, N_MEM, w), lambda bi, i: (bi, 0, 0)),
                  pl.BlockSpec((None, N_MEM, w), lambda bi, i: (bi, 0, 1))],
        out_specs=pl.BlockSpec((None, tq, w), lambda bi, i: (bi, i, 0)),
        out_shape=jax.ShapeDtypeStruct((b, s, w), BF16),
        compiler_params=_cparams(("parallel", "parallel")),
    )(pv, kvv, kvv)
    return yd.reshape(b * s, w)


def _merge_kernel(x_ref, ao0_ref, ao1_ref, ao2_ref, al0_ref, al1_ref, al2_ref, zt_ref, yc_ref, yd_ref, gl_ref,
                  wglut_ref, bglu_ref, wba_ref, wbb_ref, wbc_ref, wbd_ref, wout_ref, g1_ref, b1_ref,
                  x1_ref):
    tm = x_ref.shape[0]
    slabs = lambda ref: jnp.concatenate([ref[c] for c in range(ref.shape[0])], axis=-1)
    lses = [al0_ref[...], slabs(al1_ref), slabs(al2_ref)]
    aos = [ao0_ref[...], slabs(ao1_ref), slabs(ao2_ref)]
    mx = jnp.maximum(jnp.maximum(lses[0], lses[1]), lses[2])
    es = [jnp.exp(l - mx) for l in lses]
    tot = es[0] + es[1] + es[2]
    ya = jnp.concatenate([o * (e / tot) for o, e in zip(aos, es)], axis=-1).astype(BF16)
    zt = zt_ref[...]
    glu = jnp.dot(wglut_ref[...], zt, preferred_element_type=F32) + bglu_ref[...]
    ybt = zt.astype(F32) * _sigmoid(glu)
    yb = jnp.transpose(ybt).astype(BF16)
    merged = jnp.zeros((tm, D_MODEL), F32)
    for n, (y, w_ref) in enumerate(((ya, wba_ref), (yb, wbb_ref), (yc_ref[...], wbc_ref), (yd_ref[...], wbd_ref))):
        bo = jnp.dot(y, w_ref[...], preferred_element_type=F32)
        merged = merged + (jnp.tanh(gl_ref[:, n * D_MODEL:(n + 1) * D_MODEL].astype(F32)) + 1.0) * bo
    h = DEEPNORM_ALPHA * x_ref[...] + jnp.dot(merged.astype(BF16), wout_ref[...], preferred_element_type=F32)
    x1_ref[...] = _ln(h, g1_ref[...], b1_ref[...])


def _router_kernel(x1_ref, rwh_ref, rwl_ref, rb_ref, tri_ref, x1p_ref, route_ref, counts_ref, carry_ref):
    tm = x1_ref.shape[0]

    @pl.when(pl.program_id(0) == 0)
    def _():
        carry_ref[...] = jnp.zeros_like(carry_ref)

    x1 = x1_ref[...]
    x1p_ref[...] = _pack_bf16_pairs(x1)
    x_hi = x1.astype(BF16)
    x_lo = (x1 - x_hi.astype(F32)).astype(BF16)
    logits = (jnp.dot(x_hi, rwh_ref[...], preferred_element_type=F32)
              + jnp.dot(x_lo, rwh_ref[...], preferred_element_type=F32)
              + jnp.dot(x_hi, rwl_ref[...], preferred_element_type=F32)) + rb_ref[...]
    lane = lax.broadcasted_iota(jnp.int32, (tm, LANE), 1)
    work = logits
    sel, vals, hots = [], [], []
    for _ in range(TOP_K):
        mk = jnp.max(work, axis=-1, keepdims=True)
        ik = jnp.min(jnp.where(work == mk, lane, LANE), axis=-1, keepdims=True)
        hot = lane == ik
        work = jnp.where(hot, -jnp.inf, work)
        sel.append(ik)
        vals.append(mk)
        hots.append(hot)
    ex = [jnp.exp(v - vals[0]) for v in vals]
    den = ex[0] + ex[1] + ex[2] + ex[3]
    chosen = (hots[0] | hots[1] | hots[2] | hots[3]).astype(F32)
    before = jnp.dot(tri_ref[...], chosen.astype(BF16), preferred_element_type=F32) + carry_ref[0:1, :]
    route = jnp.zeros((tm, LANE), F32)
    for k in range(TOP_K):
        rank = jnp.sum(jnp.where(hots[k], before, 0.0), axis=-1, keepdims=True)
        route = jnp.where(lane == k, sel[k].astype(F32), route)
        route = jnp.where(lane == TOP_K + k, ex[k] / den, route)
        route = jnp.where(lane == 2 * TOP_K + k, rank, route)
    route_ref[...] = route
    new_carry = carry_ref[0:1, :] + jnp.sum(chosen, axis=0, keepdims=True)
    carry_ref[...] = jnp.broadcast_to(new_carry, carry_ref.shape)
    counts_ref[...] = jnp.broadcast_to(new_carry, counts_ref.shape)


def _merge_call(x, aos, alses, zt, yc, yd, proj, lw):
    t = x.shape[0]
    tm = 256
    gw = GW
    gate_blk = 4 * D_MODEL
    row = lambda w: pl.BlockSpec((tm, w), lambda i: (i, 0))
    slab = pl.BlockSpec((gw // LANE, tm, LANE), lambda i: (0, i, 0))
    full = lambda a: pl.BlockSpec(a.shape, lambda i: (0,) * a.ndim)
    consts = (lw["w_glu_t"], lw["b_glu"], lw["wb_a"], lw["wb_b"], lw["wb_c"], lw["wb_d"], lw["w_out"],
              lw["ln1_g"], lw["ln1_b"])
    return pl.pallas_call(
        _merge_kernel, name="branch_merge",
        grid=(t // tm,),
        in_specs=[row(D_MODEL)] + [row(gw), slab, slab] * 2
                 + [pl.BlockSpec((S5_WIDTH, tm), lambda i: (0, i)),
                  row(RET_HEADS * RET_V_DIM), row(MEM_HEADS * MEM_HEAD_PAD),
                  pl.BlockSpec((tm, gate_blk), lambda i: (i, _SEG["gate"] * LANE // gate_blk))]
                 + [full(a) for a in consts],
        out_specs=row(D_MODEL),
        out_shape=jax.ShapeDtypeStruct((t, D_MODEL), F32),
        compiler_params=_cparams(("parallel",)),
    )(x, *aos, *alses, zt, yc, yd, proj, *consts)


def _router_call(x1, lw):
    t = x1.shape[0]
    tm = 1024
    row = lambda w: pl.BlockSpec((tm, w), lambda i: (i, 0))
    full = lambda a: pl.BlockSpec(a.shape, lambda i: (0,) * a.ndim)
    tri = jnp.tril(jnp.ones((tm, tm), BF16), -1)
    consts = (lw["router_w_hi"], lw["router_w_lo"], lw["router_b"], tri)
    return pl.pallas_call(
        _router_kernel, name="moe_router",
        grid=(t // tm,),
        in_specs=[row(D_MODEL)] + [full(a) for a in consts],
        out_specs=[row(D_MODEL // 2), row(LANE), pl.BlockSpec((8, LANE), lambda i: (0, 0))],
        out_shape=[jax.ShapeDtypeStruct((t, D_MODEL // 2), jnp.uint32),
                   jax.ShapeDtypeStruct((t, LANE), F32),
                   jax.ShapeDtypeStruct((8, LANE), F32)],
        scratch_shapes=[pltpu.VMEM((8, LANE), F32)],
        compiler_params=_cparams(("arbitrary",)),
    )(x1, *consts)


def _pack_bf16_pairs(x):
    n = x.shape[1] // 2
    hi = lax.bitcast_convert_type(x[:, :n].astype(BF16).astype(F32), jnp.uint32)
    lo = lax.bitcast_convert_type(x[:, n:].astype(BF16).astype(F32), jnp.uint32)
    return hi | (lo >> 16)


def _unpack_bf16_pairs(w):
    hi = lax.bitcast_convert_type(w & jnp.uint32(0xFFFF0000), F32)
    lo = lax.bitcast_convert_type(w << 16, F32)
    return jnp.concatenate([hi, lo], axis=-1)


def _sc_index_rows(idx):
    window = SC_GATHER_WINDOW
    return jnp.pad(idx.reshape(-1, window), ((0, 0), (0, LANE - window)))


def _sc_scatter_rows(x, idx, n_out):
    n, d = x.shape
    m = idx.shape[0]
    window = SC_GATHER_WINDOW
    src_steps = n // window
    mesh = plsc.VectorSubcoreMesh(core_axis_name="core", subcore_axis_name="subcore")

    @pl.kernel(out_type=jax.ShapeDtypeStruct((n_out, d), x.dtype), mesh=mesh, scratch_types=[])
    def scatter_kernel(x_hbm, i_hbm, o_hbm):
        def body(x_vmem, i_vmem):
            pltpu.sync_copy(x_vmem, o_hbm.at[i_vmem.at[0, pl.ds(0, window)]])

        pltpu.emit_pipeline(
            body,
            grid=(m // window,),
            in_specs=[pl.BlockSpec((window, d), index_map=lambda i: (i % src_steps, 0)),
                      pl.BlockSpec((1, LANE), index_map=lambda i: (i, 0))],
            out_specs=[],
            core_axis_name=("core", "subcore"),
            dimension_semantics=(pltpu.PARALLEL,),
            trace_scopes=False,
        )(x_hbm, i_hbm)

    return scatter_kernel(x, _sc_index_rows(idx))


def _sc_gather_rows(x, idx):
    m, d = idx.shape[0], x.shape[1]
    window = SC_GATHER_WINDOW
    mesh = plsc.VectorSubcoreMesh(core_axis_name="core", subcore_axis_name="subcore")

    @pl.kernel(out_type=jax.ShapeDtypeStruct((m, d), x.dtype), mesh=mesh, scratch_types=[])
    def gather_kernel(x_hbm, i_hbm, o_hbm):
        def body(i_vmem, o_vmem):
            pltpu.sync_copy(x_hbm.at[i_vmem.at[0, pl.ds(0, window)]], o_vmem)

        pltpu.emit_pipeline(
            body,
            grid=(m // window,),
            in_specs=[pl.BlockSpec((1, LANE), index_map=lambda i: (i, 0))],
            out_specs=[pl.BlockSpec((window, d), index_map=lambda i: (i, 0))],
            core_axis_name=("core", "subcore"),
            dimension_semantics=(pltpu.PARALLEL,),
            trace_scopes=False,
        )(i_hbm, o_hbm)

    return gather_kernel(x, _sc_index_rows(idx))


def _moe_expert_kernel(be_ref, nu_ref, nv_ref, x_ref, w1_ref, b1_ref, w2_ref, b2_ref, o_ref, w1s_ref, w2s_ref):
    i = pl.program_id(0)

    @pl.when((i == 0) | (be_ref[i] != be_ref[jnp.maximum(i - 1, 0)]))
    def _():
        w1s_ref[...] = w1_ref[...].astype(BF16)
        w2s_ref[...] = w2_ref[...].astype(BF16)

    @pl.when(i < nu_ref[0])
    def _():
        rows = lax.broadcasted_iota(jnp.int32, (x_ref.shape[0], 1), 0)
        x = jnp.where(rows < nv_ref[i], _unpack_bf16_pairs(x_ref[...]), 0.0)
        h = jnp.dot(x.astype(BF16), w1s_ref[...], preferred_element_type=F32) + b1_ref[...]
        gate = jnp.minimum(h[:, :D_FF], SWIGLU_LIMIT)
        up = jnp.clip(h[:, D_FF:], -SWIGLU_LIMIT, SWIGLU_LIMIT)
        hid = gate * _sigmoid(SWIGLU_ALPHA * gate) * (up + 1.0)
        y = jnp.dot(hid.astype(BF16), w2s_ref[...], preferred_element_type=F32) + b2_ref[...]
        o_ref[...] = _pack_bf16_pairs(y)

    @pl.when(i >= nu_ref[0])
    def _():
        o_ref[...] = jnp.zeros_like(o_ref)


def _moe_expert_call(xg, block_expert, n_used, n_valid, layer, w1, b1, w2, b2):
    bm = MOE_BLOCK_ROWS
    n_blocks = xg.shape[0] // bm
    half = D_MODEL // 2
    grid_spec = pltpu.PrefetchScalarGridSpec(
        num_scalar_prefetch=3, grid=(n_blocks,),
        in_specs=[pl.BlockSpec((bm, half), lambda i, be, nu, nv: (i, 0)),
                  pl.BlockSpec((None, None, D_MODEL, 2 * D_FF), lambda i, be, nu, nv: (layer, be[i], 0, 0)),
                  pl.BlockSpec((None, 1, 2 * D_FF), lambda i, be, nu, nv: (be[i], 0, 0)),
                  pl.BlockSpec((None, None, D_FF, D_MODEL), lambda i, be, nu, nv: (layer, be[i], 0, 0)),
                  pl.BlockSpec((None, 1, D_MODEL), lambda i, be, nu, nv: (be[i], 0, 0))],
        out_specs=pl.BlockSpec((bm, half), lambda i, be, nu, nv: (i, 0)),
        scratch_shapes=[pltpu.VMEM((D_MODEL, 2 * D_FF), BF16), pltpu.VMEM((D_FF, D_MODEL), BF16)])
    return pl.pallas_call(
        _moe_expert_kernel, name="moe_experts",
        grid_spec=grid_spec,
        out_shape=jax.ShapeDtypeStruct((n_blocks * bm, half), jnp.uint32),
        compiler_params=_cparams(("arbitrary",)),
    )(block_expert, n_used, n_valid, xg, w1, b1, w2, b2)


def _moe_combine_kernel(x_ref, route_ref, y0_ref, y1_ref, y2_ref, y3_ref, g_ref, b_ref, o_ref):
    route = route_ref[...]
    moe = jnp.zeros(x_ref.shape, F32)
    for k, y_ref in enumerate((y0_ref, y1_ref, y2_ref, y3_ref)):
        moe = moe + route[:, TOP_K + k:TOP_K + k + 1] * _unpack_bf16_pairs(y_ref[...])
    o_ref[...] = _ln(DEEPNORM_ALPHA * x_ref[...] + moe, g_ref[...], b_ref[...])


def _moe_combine_call(x1, route, yg, g2, b2):
    t = x1.shape[0]
    tm = 512
    nt = t // tm
    ysel = lambda k: pl.BlockSpec((tm, D_MODEL // 2), lambda i: (k * nt + i, 0))
    return pl.pallas_call(
        _moe_combine_kernel, name="moe_combine_ln",
        grid=(nt,),
        in_specs=[pl.BlockSpec((tm, D_MODEL), lambda i: (i, 0)),
                  pl.BlockSpec((tm, LANE), lambda i: (i, 0))]
                 + [ysel(k) for k in range(TOP_K)]
                 + [pl.BlockSpec((1, D_MODEL), lambda i: (0, 0)),
                    pl.BlockSpec((1, D_MODEL), lambda i: (0, 0))],
        out_specs=pl.BlockSpec((tm, D_MODEL), lambda i: (i, 0)),
        out_shape=jax.ShapeDtypeStruct((t, D_MODEL), F32),
        compiler_params=_cparams(("parallel",)),
    )(x1, route, yg, yg, yg, yg, g2, b2)


def _moe(x1, x1p, route, counts, lw):
    t = x1.shape[0]
    bm = MOE_BLOCK_ROWS
    n_assign = t * TOP_K
    n_blocks = n_assign // bm + N_EXPERTS
    idx = route[:, :TOP_K].astype(jnp.int32)
    rank = route[:, 2 * TOP_K:3 * TOP_K].astype(jnp.int32)
    cnt = counts[0, :N_EXPERTS].astype(jnp.int32)
    padded = (cnt + bm - 1) // bm * bm
    pad_ends = jnp.cumsum(padded)
    pad_starts = pad_ends - padded
    dest = jnp.transpose(jnp.take(pad_starts, idx) + rank).reshape(-1)
    block_start = jnp.arange(n_blocks, dtype=jnp.int32) * bm
    block_expert = jnp.minimum(jnp.sum(pad_ends[None, :] <= block_start[:, None], axis=1),
                               N_EXPERTS - 1).astype(jnp.int32)
    n_valid = jnp.clip(jnp.take(cnt, block_expert) - (block_start - jnp.take(pad_starts, block_expert)), 0, bm)
    n_used = (pad_ends[-1:] // bm).astype(jnp.int32)
    xg = _sc_scatter_rows(x1p, dest, n_blocks * bm)
    yb = _moe_expert_call(xg, block_expert, n_used, n_valid.astype(jnp.int32), lw["layer"],
                          lw["moe_w1"], lw["moe_b1"], lw["moe_w2"], lw["moe_b2"])
    yg = _sc_gather_rows(yb, dest)
    return _moe_combine_call(x1, route, yg, lw["ln2_g"], lw["ln2_b"])


def _rotate_half_cols(w):
    half = RET_QK_DIM // 2
    w4 = w.reshape(w.shape[0], RET_HEADS, 2, half)
    return jnp.concatenate([-w4[:, :, 1:], w4[:, :, :1]], axis=2).reshape(w.shape)


def _pad_heads(w, axis):
    shp = list(w.shape)
    shp[axis:axis + 1] = [MEM_HEADS, MEM_HEAD_DIM]
    w = w.reshape(shp)
    pad = [(0, 0)] * w.ndim
    pad[axis + 1] = (0, MEM_HEAD_PAD - MEM_HEAD_DIM)
    w = jnp.pad(w, pad)
    shp[axis:axis + 2] = [MEM_HEADS * MEM_HEAD_PAD]
    return w.reshape(shp)


def _layer_weights(l, p):
    w_in = p["w_in"][l]
    aw, sw, qkw, vw, mw = 768, S5_WIDTH, RET_HEADS * RET_QK_DIM, RET_HEADS * RET_V_DIM, 768
    offs = [0]
    for wdt in (aw, aw, aw, sw, qkw, qkw, vw, vw, mw, 4 * D_MODEL):
        offs.append(offs[-1] + wdt)
    aq, ak, av, su, rq, rk, rv, rg, mq, gate = (w_in[:, offs[i]:offs[i + 1]] for i in range(10))
    seg = dict(gate=0.5 * gate, mq=_pad_heads(mq, 1), aq=aq[:, :GW], ak=ak[:, :GW], av=av[:, :GW],
               pad=jnp.zeros((D_MODEL, 2 * LANE), F32), rv=rv, rg=rg,
               rq=rq, rqs=_rotate_half_cols(rq), rk=rk, rks=_rotate_half_cols(rk))
    order = sorted(_SEG, key=_SEG.get)
    w_p = jnp.concatenate([seg[n] for n in order], axis=1).astype(BF16)
    assert w_p.shape[1] == NP
    w_d = jnp.concatenate([w[:, g * GW:(g + 1) * GW] for g in (1, 2) for w in (aq, ak, av)], axis=1).astype(BF16)
    wkv = p["w_mem_kv"][l]
    w_kv = jnp.concatenate([_pad_heads(wkv[:, :768], 1), _pad_heads(wkv[:, 768:], 1)], axis=1).astype(BF16)
    wb = p["w_branch"][l]
    rw = jnp.pad(p["router_w"][l].astype(F32), ((0, 0), (0, LANE - N_EXPERTS)))
    rb = jnp.pad(p["router_b"][l].astype(F32), (0, LANE - N_EXPERTS), constant_values=NEG_INF)
    return dict(
        w_p=w_p, w_d=w_d, w_su_t=jnp.transpose(su).astype(BF16), w_kv=w_kv,
        w_glu_t=jnp.transpose(p["s5_w_glu"][l]).astype(BF16), b_glu=p["s5_b_glu"][l].astype(F32).reshape(-1, 1),
        wb_a=wb[0].astype(BF16), wb_b=wb[1].astype(BF16), wb_c=wb[2].astype(BF16),
        wb_d=_pad_heads(wb[3], 0).astype(BF16), w_out=(0.5 * p["w_out"][l]).astype(BF16),
        ln1_g=p["ln1_g"][l].reshape(1, -1), ln1_b=p["ln1_b"][l].reshape(1, -1),
        router_w_hi=rw.astype(BF16), router_w_lo=(rw - rw.astype(BF16).astype(F32)).astype(BF16),
        router_b=rb.reshape(1, -1),
        layer=l, moe_w1=p["moe_w1"], moe_b1=p["moe_b1"][l].astype(F32).reshape(N_EXPERTS, 1, -1),
        moe_w2=p["moe_w2"], moe_b2=p["moe_b2"][l].astype(F32).reshape(N_EXPERTS, 1, -1),
        ln2_g=p["ln2_g"][l].reshape(1, -1), ln2_b=p["ln2_b"][l].reshape(1, -1),
        s5=_s5_tables(p["s5_lam_re"][l], p["s5_lam_im"][l], p["s5_log_step"][l], p["s5_b_re"][l],
                      p["s5_b_im"][l], p["s5_c_re"][l], p["s5_c_im"][l], p["s5_d"][l]))


def _trunk_layer(x, mem2d, b, s, lw, attn_bias, ret_tabs):
    proj, ut, pd = _proj_call(x, lw["w_p"], lw["w_su_t"], lw["w_d"])
    outs = [_attn_call(proj, attn_bias[0], b, s)]
    outs += [_dil_attn_call(pd, attn_bias[g], b, s, g, A_GROUPS[g][1]) for g in (1, 2)]
    aos, alses = [o for o, _ in outs], [l for _, l in outs]
    toep, win, wout, a_re, a_im = lw["s5"]
    nc = s // S5_CHUNK
    c_re, c_im = _s5_contrib_call(ut, win, b, nc)
    s_re, s_im = _s5_scan_call(c_re, c_im, a_re, a_im)
    zt = _s5_out_call(ut, toep, s_re, s_im, wout, b, nc)
    yc = _retention_call(proj, ret_tabs, b, s)
    kv = _mem_kv_call(mem2d, lw["w_kv"])
    yd = _mem_attn_call(proj, kv, b, s)
    x1 = _merge_call(x, aos, alses, zt, yc, yd, proj, lw)
    x1p, route, counts = _router_call(x1, lw)
    return _moe(x1, x1p, route, counts, lw)


def kernel(x_prompt, x_sample, mem_prompt, mem_sample, ln_in_g, ln_in_b, rel_bias, w_in, s5_lam_re, s5_lam_im, s5_log_step, s5_b_re, s5_b_im, s5_c_re, s5_c_im, s5_d, s5_w_glu, s5_b_glu, w_mem_kv, w_branch, w_out, ln1_g, ln1_b, router_w, router_b, moe_w1, moe_b1, moe_w2, moe_b2, ln2_g, ln2_b):
    p = dict(w_in=w_in, s5_lam_re=s5_lam_re, s5_lam_im=s5_lam_im, s5_log_step=s5_log_step, s5_b_re=s5_b_re,
             s5_b_im=s5_b_im, s5_c_re=s5_c_re, s5_c_im=s5_c_im, s5_d=s5_d, s5_w_glu=s5_w_glu, s5_b_glu=s5_b_glu,
             w_mem_kv=w_mem_kv, w_branch=w_branch, w_out=w_out, ln1_g=ln1_g, ln1_b=ln1_b, router_w=router_w,
             router_b=router_b, moe_w1=moe_w1, moe_b1=moe_b1, moe_w2=moe_w2, moe_b2=moe_b2, ln2_g=ln2_g,
             ln2_b=ln2_b)
    attn_bias = [_attn_bias_table(rel_bias, g, dil) for g, (_, dil) in enumerate(A_GROUPS)]
    trunks = []
    for x, mem in ((x_prompt, mem_prompt), (x_sample, mem_sample)):
        b, s, _ = x.shape
        trunks.append(dict(x=_layer_norm_call(x.reshape(b * s, D_MODEL), ln_in_g, ln_in_b),
                           mem=mem.reshape(b * N_MEM, D_MODEL), b=b, s=s, tabs=_ret_tables(s)))
    for l in range(DEPTH):
        lw = _layer_weights(l, p)
        for tr in trunks:
            tr["x"] = _trunk_layer(tr["x"], tr["mem"], tr["b"], tr["s"], lw, attn_bias, tr["tabs"])
    return tuple(tr["x"].reshape(tr["b"], tr["s"], D_MODEL) for tr in trunks)
```

```python
import functools
import math

import jax
import jax.numpy as jnp
from jax import lax
from jax.experimental import pallas as pl
from jax.experimental.pallas import tpu as pltpu
from jax.experimental.pallas import tpu_sc as plsc

F32 = jnp.float32
BF16 = jnp.bfloat16
HIGHEST = lax.Precision.HIGHEST

D_MODEL = 1024
DEPTH = 4
N_MEM = 256
A_GROUPS = ((128, 1), (512, 4), (2048, 16))
A_HEADS_PER_GROUP = 4
A_HEADS = 12
A_HEAD_DIM = 64
A_RADIUS = 64
REL_BUCKETS = 32
REL_MAX_DIST = 1024
S5_GROUP = 16
S5_WIDTH = 768
S5_GROUPS = 48
S5_STATE = 64
RET_HEADS = 6
RET_QK_DIM = 64
RET_V_DIM = 128
RET_CHUNK = 128
ROPE_BASE = 10000.0
MEM_HEADS = 4
MEM_HEAD_DIM = 192
MEM_HEAD_PAD = 256
N_EXPERTS = 32
TOP_K = 4
D_FF = 1024
SWIGLU_LIMIT = 7.0
SWIGLU_ALPHA = 1.702
LN_EPS = 1e-5
DEEPNORM_ALPHA = (2 * DEPTH) ** 0.25
NEG_INF = -1e30

LANE = 128
VMEM_LIMIT = 56 * 1024 * 1024
S5_CHUNK = 128
MOE_BLOCK_ROWS = 512
SC_GATHER_WINDOW = 64

_SEG = {}
_off = 0
for _name, _width, _align in (
        ("gate", 32, 8), ("mq", 8, 8), ("aq", 2, 2), ("ak", 2, 2), ("av", 2, 2), ("pad", 2, 2),
        ("rv", 6, 6), ("rg", 6, 6), ("rq", 3, 3), ("rqs", 3, 3), ("rk", 3, 3), ("rks", 3, 3)):
    assert _off % _align == 0, (_name, _off)
    _SEG[_name] = _off
    _off += _width
NP_UNITS = _off
NP = NP_UNITS * LANE
assert NP % 256 == 0
GW = A_HEADS_PER_GROUP * A_HEAD_DIM
ND = 2 * 3 * GW


def _cparams(sem, vmem=VMEM_LIMIT):
    return pltpu.CompilerParams(dimension_semantics=sem, vmem_limit_bytes=vmem)


def _sigmoid(x):
    return 0.5 * jnp.tanh(0.5 * x) + 0.5


def _ln(h, g, b):
    mu = jnp.mean(h, axis=-1, keepdims=True)
    d = h - mu
    var = jnp.mean(d * d, axis=-1, keepdims=True)
    return d * lax.rsqrt(var + LN_EPS) * g + b


def _ln_kernel(x_ref, g_ref, b_ref, o_ref):
    o_ref[...] = _ln(x_ref[...], g_ref[...], b_ref[...])


def _layer_norm_call(x, g, b):
    t = x.shape[0]
    tm = 512
    return pl.pallas_call(
        _ln_kernel, name="input_ln",
        grid=(t // tm,),
        in_specs=[pl.BlockSpec((tm, D_MODEL), lambda i: (i, 0)),
                  pl.BlockSpec((1, D_MODEL), lambda i: (0, 0)),
                  pl.BlockSpec((1, D_MODEL), lambda i: (0, 0))],
        out_specs=pl.BlockSpec((tm, D_MODEL), lambda i: (i, 0)),
        out_shape=jax.ShapeDtypeStruct((t, D_MODEL), F32),
        compiler_params=_cparams(("parallel",)),
    )(x, g.reshape(1, -1), b.reshape(1, -1))


def _proj_kernel(x_ref, w_ref, wsut_ref, wd_ref, p_ref, ut_ref, pd_ref, xb_ref):
    @pl.when(pl.program_id(1) == 0)
    def _():
        xb = x_ref[...].astype(BF16)
        xb_ref[...] = xb
        ut_ref[...] = lax.dot_general(wsut_ref[...], xb, (((1,), (1,)), ((), ())),
                                      preferred_element_type=F32).astype(BF16)
        pd = jnp.dot(xb, wd_ref[...], preferred_element_type=F32)
        for c in range(ND // LANE):
            pd_ref[c] = pd[:, c * LANE:(c + 1) * LANE]

    p_ref[...] = jnp.dot(xb_ref[...], w_ref[...], preferred_element_type=F32).astype(BF16)


def _proj_call(x, w_p, w_su_t, w_d):
    t = x.shape[0]
    tm, tn = 512, NP // 3
    return pl.pallas_call(
        _proj_kernel, name="in_proj",
        grid=(t // tm, NP // tn),
        in_specs=[pl.BlockSpec((tm, D_MODEL), lambda i, j: (i, 0)),
                  pl.BlockSpec((D_MODEL, tn), lambda i, j: (0, j)),
                  pl.BlockSpec((S5_WIDTH, D_MODEL), lambda i, j: (0, 0)),
                  pl.BlockSpec((D_MODEL, ND), lambda i, j: (0, 0))],
        out_specs=[pl.BlockSpec((tm, tn), lambda i, j: (i, j)),
                   pl.BlockSpec((S5_WIDTH, tm), lambda i, j: (0, i)),
                   pl.BlockSpec((ND // LANE, tm, LANE), lambda i, j: (0, i, 0))],
        out_shape=[jax.ShapeDtypeStruct((t, NP), BF16),
                   jax.ShapeDtypeStruct((S5_WIDTH, t), BF16),
                   jax.ShapeDtypeStruct((ND // LANE, t, LANE), F32)],
        scratch_shapes=[pltpu.VMEM((tm, D_MODEL), BF16)],
        compiler_params=_cparams(("parallel", "arbitrary")),
    )(x, w_p, w_su_t, w_d)


ATTN_TQ = 2 * A_RADIUS


def _banded_heads(q, kwin, vwin, bias_ref, first_key, sub_len):
    tq, nk = q.shape[0], kwin.shape[0]
    kidx = first_key + lax.broadcasted_iota(jnp.int32, (tq, nk), 1)
    valid = (kidx >= 0) & (kidx < sub_len)
    lane_head = lax.broadcasted_iota(jnp.int32, (1, q.shape[1]), 1) // A_HEAD_DIM
    scale = jnp.asarray(A_HEAD_DIM ** -0.5, BF16)
    heads = range(A_HEADS_PER_GROUP)
    qs = [jnp.where(lane_head == h, q, jnp.zeros_like(q)) * scale for h in heads]
    ss = [lax.dot_general(qh, kwin, (((1,), (1,)), ((), ())), preferred_element_type=F32) for qh in qs]
    ss = [jnp.where(valid, s + bias_ref[h], NEG_INF) for h, s in zip(heads, ss)]
    ms = [jnp.max(s, axis=-1, keepdims=True) for s in ss]
    ps = [jnp.exp(s - m) for s, m in zip(ss, ms)]
    dens = [jnp.sum(p, axis=-1, keepdims=True) for p in ps]
    ohs = [jnp.dot(p.astype(BF16), vwin, preferred_element_type=F32) for p in ps]
    acc = jnp.zeros((tq, q.shape[1]), F32)
    lse_acc = jnp.zeros((tq, q.shape[1]), F32)
    for h in heads:
        hm = lane_head == h
        acc = jnp.where(hm, ohs[h] / dens[h], acc)
        lse_acc = jnp.where(hm, ms[h] + jnp.log(dens[h]), lse_acc)
    return acc, lse_acc


def _attn_kernel(q_ref, kp_ref, kc_ref, kn_ref, vp_ref, vc_ref, vn_ref, bias_ref, o_ref, lse_ref, *, sub_len):
    tq, half = ATTN_TQ, A_RADIUS
    kwin = jnp.concatenate([kp_ref[tq - half:, :], kc_ref[...], kn_ref[:half, :]], axis=0)
    vwin = jnp.concatenate([vp_ref[tq - half:, :], vc_ref[...], vn_ref[:half, :]], axis=0)
    acc, lse = _banded_heads(q_ref[...], kwin, vwin, bias_ref, pl.program_id(1) * tq - half, sub_len)
    o_ref[...] = acc
    lse_ref[...] = lse


def _attn_call(proj, bias, b, s):
    tq = ATTN_TQ
    nqb = s // tq
    pv = proj.reshape(b, s, NP)
    qo, ko, vo = (_SEG[n] * LANE // GW for n in ("aq", "ak", "av"))
    blk = (None, tq, GW)
    cur = lambda off: (lambda bi, qi: (bi, qi, off))
    prv = lambda off: (lambda bi, qi: (bi, jnp.maximum(qi - 1, 0), off))
    nxt = lambda off: (lambda bi, qi: (bi, jnp.minimum(qi + 1, nqb - 1), off))
    out_map = lambda bi, qi: (bi, qi, 0)
    o, lse = pl.pallas_call(
        functools.partial(_attn_kernel, sub_len=s), name="window_attn_g0",
        grid=(b, nqb),
        in_specs=[pl.BlockSpec(blk, cur(qo)),
                  pl.BlockSpec(blk, prv(ko)), pl.BlockSpec(blk, cur(ko)), pl.BlockSpec(blk, nxt(ko)),
                  pl.BlockSpec(blk, prv(vo)), pl.BlockSpec(blk, cur(vo)), pl.BlockSpec(blk, nxt(vo)),
                  pl.BlockSpec((A_HEADS_PER_GROUP, tq, 2 * tq), lambda bi, qi: (0, 0, 0))],
        out_specs=[pl.BlockSpec(blk, out_map), pl.BlockSpec(blk, out_map)],
        out_shape=[jax.ShapeDtypeStruct((b, s, GW), F32), jax.ShapeDtypeStruct((b, s, GW), F32)],
        compiler_params=_cparams(("parallel", "parallel")),
    )(pv, pv, pv, pv, pv, pv, pv, bias)
    return o.reshape(b * s, GW), lse.reshape(b * s, GW)


def _dil_attn_kernel(q_ref, kp_ref, kc_ref, kn_ref, vp_ref, vc_ref, vn_ref, bias_ref, o_ref, lse_ref, *,
                     dil, sub_len):
    tq, half = ATTN_TQ, A_RADIUS
    first_key = pl.program_id(1) * tq - half

    def residue(r, carry):
        def rows(ref, n):
            return jnp.concatenate([ref[c, pl.ds(r, n, stride=dil), :] for c in range(GW // LANE)], axis=-1)

        q = rows(q_ref, tq).astype(BF16)
        kwin = jnp.concatenate([rows(kp_ref, half), rows(kc_ref, tq), rows(kn_ref, half)], axis=0).astype(BF16)
        vwin = jnp.concatenate([rows(vp_ref, half), rows(vc_ref, tq), rows(vn_ref, half)], axis=0).astype(BF16)
        acc, lse = _banded_heads(q, kwin, vwin, bias_ref, first_key, sub_len)
        for c in range(GW // LANE):
            o_ref[c, pl.ds(r, tq, stride=dil), :] = acc[:, c * LANE:(c + 1) * LANE]
            lse_ref[c, pl.ds(r, tq, stride=dil), :] = lse[:, c * LANE:(c + 1) * LANE]
        return carry

    lax.fori_loop(0, dil, residue, 0)


def _dil_attn_call(pd, bias, b, s, g, dil):
    tq = ATTN_TQ
    rows = tq * dil
    nblk = s // rows
    nslab = GW // LANE
    pv = pd.reshape(ND // LANE, b, s, LANE)
    qo, ko, vo = (3 * (g - 1) + j for j in range(3))
    blk, hblk = (nslab, None, rows, LANE), (nslab, None, rows // 2, LANE)
    cur = lambda off: (lambda bi, i: (off, bi, i, 0))
    prv = lambda off: (lambda bi, i: (off, bi, jnp.maximum(2 * i - 1, 0), 0))
    nxt = lambda off: (lambda bi, i: (off, bi, jnp.minimum(2 * i + 2, 2 * nblk - 1), 0))
    out_map = lambda bi, i: (0, bi, i, 0)
    o, lse = pl.pallas_call(
        functools.partial(_dil_attn_kernel, dil=dil, sub_len=s // dil), name=f"dilated_attn_g{g}",
        grid=(b, nblk),
        in_specs=[pl.BlockSpec(blk, cur(qo)),
                  pl.BlockSpec(hblk, prv(ko)), pl.BlockSpec(blk, cur(ko)), pl.BlockSpec(hblk, nxt(ko)),
                  pl.BlockSpec(hblk, prv(vo)), pl.BlockSpec(blk, cur(vo)), pl.BlockSpec(hblk, nxt(vo)),
                  pl.BlockSpec((A_HEADS_PER_GROUP, tq, 2 * tq), lambda bi, i: (0, 0, 0))],
        out_specs=[pl.BlockSpec(blk, out_map), pl.BlockSpec(blk, out_map)],
        out_shape=[jax.ShapeDtypeStruct((nslab, b, s, LANE), F32)] * 2,
        compiler_params=_cparams(("parallel", "parallel")),
    )(pv, pv, pv, pv, pv, pv, pv, bias)
    return o.reshape(nslab, b * s, LANE), lse.reshape(nslab, b * s, LANE)


def _t5_bucket(rel):
    nb = REL_BUCKETS // 2
    max_exact = nb // 2
    ret = jnp.where(rel > 0, nb, 0)
    n = jnp.abs(rel)
    nf = jnp.maximum(n, 1).astype(F32)
    large = max_exact + (jnp.log(nf / max_exact) / math.log(REL_MAX_DIST / max_exact)
                         * (nb - max_exact)).astype(jnp.int32)
    large = jnp.minimum(large, nb - 1)
    return ret + jnp.where(n < max_exact, n, large)


def _attn_bias_table(rel_bias, g, dil, tq=128):
    qi = jnp.arange(tq)[:, None]
    kj = jnp.arange(2 * tq)[None, :] - A_RADIUS
    rel = kj - qi
    tab = rel_bias[:, g * A_HEADS_PER_GROUP:(g + 1) * A_HEADS_PER_GROUP].astype(F32)
    bucket = _t5_bucket(rel * dil)[None]
    bias = jnp.zeros((A_HEADS_PER_GROUP,) + rel.shape, F32)
    for bkt in range(REL_BUCKETS):
        bias = jnp.where(bucket == bkt, tab[bkt][:, None, None], bias)
    return jnp.where((jnp.abs(rel) <= A_RADIUS)[None], bias, NEG_INF)


def _s5_toeplitz_kernel(bcf_ref, pwf_ref, bcb_ref, pwb_ref, d_ref, t_ref, gf_ref, gb_ref):
    gf_ref[...] = jnp.dot(bcf_ref[...], pwf_ref[...], precision=HIGHEST, preferred_element_type=F32)
    gb_ref[...] = jnp.dot(bcb_ref[...], pwb_ref[...], precision=HIGHEST, preferred_element_type=F32)
    c = S5_CHUNK
    row = lax.broadcasted_iota(jnp.int32, (c, c), 0)
    col = lax.broadcasted_iota(jnp.int32, (c, c), 1)

    def body(ci, carry):
        dval = d_ref[pl.ds(ci, 1), :]
        for co in range(S5_GROUP):
            r = ci * S5_GROUP + co
            gf = jnp.broadcast_to(gf_ref[pl.ds(r, 1), :], (c, c))
            gb = jnp.broadcast_to(gb_ref[pl.ds(r, 1), :], (c, c))
            tf = pltpu.roll(gf, 0, 1, stride=1, stride_axis=0)
            tb = pltpu.roll(gb, 1, 1, stride=1, stride_axis=0)
            tile = jnp.where(col >= row, tf, 0.0) + jnp.where(row >= col, tb, 0.0)
            tile = tile + jnp.where((row == col) & (ci == co), dval, 0.0)
            t_ref[pl.ds(pl.multiple_of(ci * c, c), c), co * c:(co + 1) * c] = tile.astype(BF16)
        return carry

    lax.fori_loop(0, S5_GROUP, body, 0)


def _s5_toeplitz_call(bcf, pwf, bcb, pwb, dskip):
    n = S5_GROUP * S5_CHUNK
    return pl.pallas_call(
        _s5_toeplitz_kernel, name="s5_toeplitz",
        grid=(S5_GROUPS,),
        in_specs=[pl.BlockSpec((None, S5_GROUP * S5_GROUP, 2 * S5_STATE), lambda g: (g, 0, 0)),
                  pl.BlockSpec((None, 2 * S5_STATE, S5_CHUNK), lambda g: (g, 0, 0)),
                  pl.BlockSpec((None, S5_GROUP * S5_GROUP, 2 * S5_STATE), lambda g: (g, 0, 0)),
                  pl.BlockSpec((None, 2 * S5_STATE, S5_CHUNK), lambda g: (g, 0, 0)),
                  pl.BlockSpec((None, S5_GROUP, LANE), lambda g: (g, 0, 0))],
        out_specs=pl.BlockSpec((None, n, n), lambda g: (g, 0, 0)),
        out_shape=jax.ShapeDtypeStruct((S5_GROUPS, n, n), BF16),
        scratch_shapes=[pltpu.VMEM((S5_GROUP * S5_GROUP, S5_CHUNK), F32),
                        pltpu.VMEM((S5_GROUP * S5_GROUP, S5_CHUNK), F32)],
        compiler_params=_cparams(("parallel",)),
    )(bcf, pwf, bcb, pwb, dskip)


def _s5_tables(lam_re, lam_im, log_step, b_re, b_im, c_re, c_im, d_skip):
    c = S5_CHUNK
    k = jnp.arange(c, dtype=F32)
    per_dir = []
    for direction in range(2):
        step = jnp.exp(log_step[direction].astype(F32))[:, None]
        lr, li = lam_re[direction].astype(F32), lam_im[direction].astype(F32)
        mag = jnp.exp(lr * step)
        ar, ai = mag * jnp.cos(li * step), mag * jnp.sin(li * step)
        nr, ni = ar - 1.0, ai
        den = lr * lr + li * li
        fr = (nr * lr + ni * li) / den
        fi = (ni * lr - nr * li) / den
        br, bi = b_re[direction].astype(F32), b_im[direction].astype(F32)
        bbr = fr[..., None] * br - fi[..., None] * bi
        bbi = fr[..., None] * bi + fi[..., None] * br
        cr, cim = c_re[direction].astype(F32), c_im[direction].astype(F32)

        def power(e, log_mag=lr * step, phase=li * step):
            m = jnp.exp(e[None, None, :] * log_mag[..., None])
            th = e[None, None, :] * phase[..., None]
            return m * jnp.cos(th), m * jnp.sin(th)

        per_dir.append(dict(bbr=bbr, bbi=bbi, cr=cr, cim=cim, power=power))

    def bc_table(d):
        bbr_t = jnp.transpose(d["bbr"], (0, 2, 1))[:, :, None, :]
        bbi_t = jnp.transpose(d["bbi"], (0, 2, 1))[:, :, None, :]
        cr, cim = d["cr"][:, None], d["cim"][:, None]
        re = bbr_t * cr - bbi_t * cim
        im = bbr_t * cim + bbi_t * cr
        return jnp.concatenate([re, -im], axis=-1).reshape(S5_GROUPS, S5_GROUP * S5_GROUP, 2 * S5_STATE)

    f, bw = per_dir
    pfr, pfi = f["power"](k)
    pbr, pbi = bw["power"](c - 1 - k)
    pwf = jnp.concatenate([pfr, pfi], axis=1)
    pwb = jnp.concatenate([pbr, pbi], axis=1)
    dsk = jnp.broadcast_to(d_skip.astype(F32).reshape(S5_GROUPS, S5_GROUP, 1), (S5_GROUPS, S5_GROUP, LANE))
    toep = _s5_toeplitz_call(bc_table(f), pwf, bc_table(bw), pwb, dsk)

    def w_in(d, e):
        pr, pi = d["power"](e)
        re = d["bbr"][:, :, :, None] * pr[:, :, None, :] - d["bbi"][:, :, :, None] * pi[:, :, None, :]
        im = d["bbr"][:, :, :, None] * pi[:, :, None, :] + d["bbi"][:, :, :, None] * pr[:, :, None, :]
        tr = lambda z: jnp.transpose(z, (0, 2, 3, 1)).reshape(S5_GROUPS, S5_GROUP * c, S5_STATE)
        return tr(re), tr(im)

    fre, fim = w_in(f, c - 1 - k)
    bre, bim = w_in(bw, k)
    win = jnp.concatenate([fre, bre, fim, bim], axis=-1).astype(BF16)

    def w_out(d, e):
        pr, pi = d["power"](e)
        re = d["cr"][:, :, :, None] * pr[:, None] - d["cim"][:, :, :, None] * pi[:, None]
        im = d["cr"][:, :, :, None] * pi[:, None] + d["cim"][:, :, :, None] * pr[:, None]
        tr = lambda z: jnp.transpose(z, (0, 2, 1, 3)).reshape(S5_GROUPS, S5_STATE, S5_GROUP * c)
        return tr(re), tr(-im)

    ofre, ofim = w_out(f, k + 1.0)
    obre, obim = w_out(bw, c - k)
    wout = jnp.concatenate([ofre, obre, ofim, obim], axis=1).astype(BF16)

    cc = jnp.asarray([float(c)], F32)
    afr, afi = f["power"](cc)
    abr, abi = bw["power"](cc)
    a_re = jnp.concatenate([afr[..., 0], abr[..., 0]], axis=-1).reshape(1, -1)
    a_im = jnp.concatenate([afi[..., 0], abi[..., 0]], axis=-1).reshape(1, -1)
    return toep, win, wout, a_re, a_im


def _s5_contrib_kernel(u_ref, win_ref, re_ref, im_ref):
    uc = jnp.concatenate([u_ref[ci] for ci in range(S5_GROUP)], axis=-1)
    res = jnp.dot(uc, win_ref[...], preferred_element_type=F32)
    nc, nb = re_ref.shape[0], re_ref.shape[1]
    half = 2 * S5_STATE
    for bi in range(nb):
        re_ref[:, bi, :] = res[bi * nc:(bi + 1) * nc, :half]
        im_ref[:, bi, :] = res[bi * nc:(bi + 1) * nc, half:]


def _s5_contrib_call(ut, win, b, nc):
    nch = b * nc
    half = 2 * S5_STATE
    uv = ut.reshape(S5_GROUPS, S5_GROUP, nch, S5_CHUNK)
    return pl.pallas_call(
        _s5_contrib_kernel, name="s5_contrib",
        grid=(S5_GROUPS,),
        in_specs=[pl.BlockSpec((None, S5_GROUP, nch, S5_CHUNK), lambda g: (g, 0, 0, 0)),
                  pl.BlockSpec((None, S5_GROUP * S5_CHUNK, 4 * S5_STATE), lambda g: (g, 0, 0))],
        out_specs=[pl.BlockSpec((nc, b, half), lambda g: (0, 0, g))] * 2,
        out_shape=[jax.ShapeDtypeStruct((nc, b, S5_GROUPS * half), F32)] * 2,
        compiler_params=_cparams(("parallel",)),
    )(uv, win)


def _s5_scan_kernel(cre_ref, cim_ref, are_ref, aim_ref, ore_ref, oim_ref):
    nc, nb, width = cre_ref.shape
    a_re, a_im = are_ref[...], aim_ref[...]
    lane = lax.broadcasted_iota(jnp.int32, (1, width), 1)
    fwd_lane = (lane % (2 * S5_STATE)) < S5_STATE
    zero = jnp.zeros((nb, width), F32)

    def advance(s, c):
        sr, si = s
        return a_re * sr - a_im * si + cre_ref[c], a_re * si + a_im * sr + cim_ref[c]

    def up(c, s):
        ore_ref[c] = s[0]
        oim_ref[c] = s[1]
        return advance(s, c)

    lax.fori_loop(0, nc, up, (zero, zero))

    def down(i, s):
        c = nc - 1 - i
        ore_ref[c] = jnp.where(fwd_lane, ore_ref[c], s[0])
        oim_ref[c] = jnp.where(fwd_lane, oim_ref[c], s[1])
        return advance(s, c)

    lax.fori_loop(0, nc, down, (zero, zero))


def _s5_scan_call(cre, cim, a_re, a_im):
    return pl.pallas_call(
        _s5_scan_kernel, name="s5_chunk_scan",
        out_shape=[jax.ShapeDtypeStruct(cre.shape, F32)] * 2,
        compiler_params=pltpu.CompilerParams(vmem_limit_bytes=VMEM_LIMIT),
    )(cre, cim, a_re, a_im)


def _gelu_tanh(y):
    return 0.5 * y * (1.0 + jnp.tanh(math.sqrt(2.0 / math.pi) * (y + 0.044715 * (y * y * y))))


def _s5_out_kernel(u_ref, t_ref, sre_ref, sim_ref, wout_ref, z_ref):
    uc = jnp.concatenate([u_ref[ci] for ci in range(S5_GROUP)], axis=-1)
    y = jnp.dot(uc, t_ref[...], preferred_element_type=F32)
    sp = jnp.concatenate([jnp.concatenate([sre_ref[:, bi, :], sim_ref[:, bi, :]], axis=-1)
                          for bi in range(sre_ref.shape[1])], axis=0)
    y = y + jnp.dot(sp.astype(BF16), wout_ref[...], preferred_element_type=F32)
    z = _gelu_tanh(y)
    for co in range(S5_GROUP):
        z_ref[co] = z[:, co * S5_CHUNK:(co + 1) * S5_CHUNK].astype(BF16)


def _s5_out_call(ut, toep, s_re, s_im, wout, b, nc):
    nch = b * nc
    uv = ut.reshape(S5_GROUPS, S5_GROUP, nch, S5_CHUNK)
    n = S5_GROUP * S5_CHUNK
    half = 2 * S5_STATE
    zt = pl.pallas_call(
        _s5_out_kernel, name="s5_out",
        grid=(S5_GROUPS,),
        in_specs=[pl.BlockSpec((None, S5_GROUP, nch, S5_CHUNK), lambda g: (g, 0, 0, 0)),
                  pl.BlockSpec((None, n, n), lambda g: (g, 0, 0)),
                  pl.BlockSpec((nc, b, half), lambda g: (0, 0, g)),
                  pl.BlockSpec((nc, b, half), lambda g: (0, 0, g)),
                  pl.BlockSpec((None, 4 * S5_STATE, n), lambda g: (g, 0, 0))],
        out_specs=pl.BlockSpec((None, S5_GROUP, nch, S5_CHUNK), lambda g: (g, 0, 0, 0)),
        out_shape=jax.ShapeDtypeStruct((S5_GROUPS, S5_GROUP, nch, S5_CHUNK), BF16),
        compiler_params=_cparams(("parallel",)),
    )(uv, toep, s_re, s_im, wout)
    return zt.reshape(S5_WIDTH, nch * S5_CHUNK)


def _ret_tables(s):
    c = RET_CHUNK
    half = RET_QK_DIM // 2
    inv = ROPE_BASE ** (-jnp.arange(half, dtype=F32) / half)
    ang = jnp.arange(s, dtype=F32)[:, None] * inv[None, :]
    cos = jnp.tile(jnp.cos(ang), (1, 2 * RET_HEADS))
    sin = jnp.tile(jnp.sin(ang), (1, 2 * RET_HEADS))
    hidx = jnp.arange(RET_HEADS, dtype=F32)
    lgf = jnp.log1p(-jnp.exp2(-5.0 - hidx))
    lgb = jnp.log1p(-jnp.exp2(-5.5 - hidx))
    j = jnp.arange(c, dtype=F32)
    rel = j[:, None] - j[None, :]
    intra = jnp.exp(jnp.abs(rel)[None] * jnp.where(rel[None] >= 0, lgf[:, None, None], lgb[:, None, None]))
    qk_f = jnp.repeat(lgf, RET_QK_DIM)[None, :]
    qk_b = jnp.repeat(lgb, RET_QK_DIM)[None, :]
    v_f = jnp.repeat(lgf, RET_V_DIM)[None, :]
    v_b = jnp.repeat(lgb, RET_V_DIM)[None, :]
    tabs = dict(
        cos=cos, sin=sin, intra=intra,
        tail_f=jnp.exp((c - 1 - j)[:, None] * qk_f), head_b=jnp.exp(j[:, None] * qk_b),
        decq_f=jnp.exp((j + 1.0)[:, None] * qk_f), decq_b=jnp.exp((c - j)[:, None] * qk_b),
        chunk_f=jnp.exp(c * v_f), chunk_b=jnp.exp(c * v_b),
        bd=(jnp.arange(RET_HEADS * RET_QK_DIM)[:, None] // RET_QK_DIM
            == jnp.arange(RET_HEADS * RET_V_DIM)[None, :] // RET_V_DIM).astype(F32))
    return tabs


def _rot(x_ref, xs_ref, cos_ref, sin_ref):
    return x_ref[...].astype(F32) * cos_ref[...] + xs_ref[...].astype(F32) * sin_ref[...]


def _ret_kernel(q_ref, qs_ref, k_ref, ks_ref, v_ref, g_ref, cos_ref, sin_ref, intra_ref, tailf_ref, headb_ref,
                decqf_ref, decqb_ref, chunkf_ref, chunkb_ref, bd_ref, o_ref, st_ref, sbd_ref, sball_ref, *, nc):
    i = pl.program_id(1)
    qk, dv = RET_QK_DIM, RET_V_DIM
    vb = v_ref[...]
    kr = _rot(k_ref, ks_ref, cos_ref, sin_ref) * (RET_QK_DIM ** -0.5)

    @pl.when((i == 0) | (i == nc))
    def _():
        st_ref[...] = jnp.zeros_like(st_ref)

    @pl.when(i == 0)
    def _():
        sbd_ref[...] = jnp.zeros_like(sbd_ref)

    @pl.when(i < nc)
    def _():
        c = nc - 1 - i
        for h in range(RET_HEADS):
            sball_ref[c, h] = st_ref[h * qk:(h + 1) * qk, h * dv:(h + 1) * dv].astype(BF16)
        kwt = jnp.transpose(kr * headb_ref[...]).astype(BF16)
        kv = jnp.dot(kwt, vb, preferred_element_type=F32)
        st_ref[...] = st_ref[...] * chunkb_ref[...] + kv * bd_ref[...]

    @pl.when(i >= nc)
    def _():
        c = i - nc
        for h in range(RET_HEADS):
            sbd_ref[h * qk:(h + 1) * qk, h * dv:(h + 1) * dv] = sball_ref[c, h]
        qr = _rot(q_ref, qs_ref, cos_ref, sin_ref)
        qb, kb = qr.astype(BF16), kr.astype(BF16)
        lane_head = lax.broadcasted_iota(jnp.int32, (1, qb.shape[1]), 1) // qk
        inter = jnp.dot((qr * decqf_ref[...]).astype(BF16), st_ref[...].astype(BF16), preferred_element_type=F32)
        inter = inter + jnp.dot((qr * decqb_ref[...]).astype(BF16), sbd_ref[...], preferred_element_type=F32)
        heads = range(RET_HEADS)
        vcols = [slice(h * dv, (h + 1) * dv) for h in heads]
        qhs = [jnp.where(lane_head == h, qb, jnp.zeros_like(qb)) for h in heads]
        scs = [lax.dot_general(qh, kb, (((1,), (1,)), ((), ())), preferred_element_type=F32) * intra_ref[h]
               for h, qh in zip(heads, qhs)]
        ohs = [jnp.dot(sc.astype(BF16), vb[:, vs], preferred_element_type=F32) + inter[:, vs]
               for sc, vs in zip(scs, vcols)]
        mus = [jnp.mean(oh, axis=-1, keepdims=True) for oh in ohs]
        dds = [oh - mu for oh, mu in zip(ohs, mus)]
        vrs = [jnp.mean(dd * dd, axis=-1, keepdims=True) for dd in dds]
        on = jnp.concatenate([dd * lax.rsqrt(var + LN_EPS) for dd, var in zip(dds, vrs)], axis=-1)
        gg = g_ref[...].astype(F32)
        o_ref[...] = (on * (gg * _sigmoid(gg))).astype(BF16)
        kwt = jnp.transpose(kr * tailf_ref[...]).astype(BF16)
        kv = jnp.dot(kwt, vb, preferred_element_type=F32)
        st_ref[...] = st_ref[...] * chunkf_ref[...] + kv * bd_ref[...]


def _retention_call(proj, tabs, b, s):
    c = RET_CHUNK
    nc = s // c
    qkw, vw = RET_HEADS * RET_QK_DIM, RET_HEADS * RET_V_DIM
    pv = proj.reshape(b, s, NP)
    qk_blk, v_blk = (None, c, qkw), (None, c, vw)
    qo, qso, ko, kso = (_SEG[n] * LANE // qkw for n in ("rq", "rqs", "rk", "rks"))
    vo, go = (_SEG[n] * LANE // vw for n in ("rv", "rg"))
    const2 = lambda shape: pl.BlockSpec(shape, lambda bi, i: (0, 0))
    chunk = lambda i: jnp.where(i < nc, nc - 1 - i, i - nc)
    fchunk = lambda i: jnp.maximum(i - nc, 0)
    both = lambda off: (lambda bi, i: (bi, chunk(i), off))
    fonly = lambda off: (lambda bi, i: (bi, fchunk(i), off))
    yc = pl.pallas_call(
        functools.partial(_ret_kernel, nc=nc), name="retention",
        grid=(b, 2 * nc),
        in_specs=[pl.BlockSpec(qk_blk, fonly(qo)), pl.BlockSpec(qk_blk, fonly(qso)),
                  pl.BlockSpec(qk_blk, both(ko)), pl.BlockSpec(qk_blk, both(kso)),
                  pl.BlockSpec(v_blk, both(vo)), pl.BlockSpec(v_blk, fonly(go)),
                  pl.BlockSpec((c, qkw), lambda bi, i: (chunk(i), 0)),
                  pl.BlockSpec((c, qkw), lambda bi, i: (chunk(i), 0)),
                  pl.BlockSpec((RET_HEADS, c, c), lambda bi, i: (0, 0, 0)),
                  const2((c, qkw)), const2((c, qkw)), const2((c, qkw)), const2((c, qkw)),
                  const2((1, vw)), const2((1, vw)), const2((qkw, vw))],
        out_specs=pl.BlockSpec((None, c, vw), lambda bi, i: (bi, fchunk(i), 0)),
        out_shape=jax.ShapeDtypeStruct((b, s, vw), BF16),
        scratch_shapes=[pltpu.VMEM((qkw, vw), F32), pltpu.VMEM((qkw, vw), BF16),
                        pltpu.VMEM((nc, RET_HEADS, RET_QK_DIM, RET_V_DIM), BF16)],
        compiler_params=_cparams(("parallel", "arbitrary")),
    )(pv, pv, pv, pv, pv, pv, tabs["cos"], tabs["sin"], tabs["intra"], tabs["tail_f"], tabs["head_b"],
      tabs["decq_f"], tabs["decq_b"], tabs["chunk_f"], tabs["chunk_b"], tabs["bd"])
    return yc.reshape(b * s, vw)


def _mm_kernel(x_ref, w_ref, o_ref):
    o_ref[...] = jnp.dot(x_ref[...].astype(BF16), w_ref[...], preferred_element_type=F32).astype(o_ref.dtype)


def _mem_kv_call(mem2d, w_kv):
    m, n = mem2d.shape[0], w_kv.shape[1]
    tm = 512
    return pl.pallas_call(
        _mm_kernel, name="mem_kv_proj",
        grid=(m // tm,),
        in_specs=[pl.BlockSpec((tm, D_MODEL), lambda i: (i, 0)),
                  pl.BlockSpec((D_MODEL, n), lambda i: (0, 0))],
        out_specs=pl.BlockSpec((tm, n), lambda i: (i, 0)),
        out_shape=jax.ShapeDtypeStruct((m, n), BF16),
        compiler_params=_cparams(("parallel",)),
    )(mem2d, w_kv)


def _mem_attn_kernel(q_ref, k_ref, v_ref, o_ref):
    q = q_ref[...]
    cols = [slice(h * MEM_HEAD_PAD, (h + 1) * MEM_HEAD_PAD) for h in range(MEM_HEADS)]
    ss = [lax.dot_general(q[:, hs], k_ref[:, hs], (((1,), (1,)), ((), ())),
                          preferred_element_type=F32) * (MEM_HEAD_DIM ** -0.5) for hs in cols]
    ms = [jnp.max(s, axis=-1, keepdims=True) for s in ss]
    es = [jnp.exp(s - m) for s, m in zip(ss, ms)]
    ps = [e / jnp.sum(e, axis=-1, keepdims=True) for e in es]
    outs = [jnp.dot(p.astype(BF16), v_ref[:, hs], preferred_element_type=F32) for p, hs in zip(ps, cols)]
    o_ref[...] = jnp.concatenate(outs, axis=-1).astype(BF16)


def _mem_attn_call(proj, kv, b, s):
    tq = 256
    w = MEM_HEADS * MEM_HEAD_PAD
    pv = proj.reshape(b, s, NP)
    kvv = kv.reshape(b, N_MEM, 2 * w)
    qo = _SEG["mq"] * LANE // w
    yd = pl.pallas_call(
        _mem_attn_kernel, name="mem_attn",
        grid=(b, s // tq),
        in_specs=[pl.BlockSpec((None, tq, w), lambda bi, i: (bi, i, qo)),
                  pl.BlockSpec((None, N_MEM, w), lambda bi, i: (bi, 0, 0)),
                  pl.BlockSpec((None, N_MEM, w), lambda bi, i: (bi, 0, 1))],
        out_specs=pl.BlockSpec((None, tq, w), lambda bi, i: (bi, i, 0)),
        out_shape=jax.ShapeDtypeStruct((b, s, w), BF16),
        compiler_params=_cparams(("parallel", "parallel")),
    )(pv, kvv, kvv)
    return yd.reshape(b * s, w)


def _merge_kernel(x_ref, ao0_ref, ao1_ref, ao2_ref, al0_ref, al1_ref, al2_ref, zt_ref, yc_ref, yd_ref, gl_ref,
                  wglut_ref, bglu_ref, wba_ref, wbb_ref, wbc_ref, wbd_ref, wout_ref, g1_ref, b1_ref,
                  x1_ref):
    tm = x_ref.shape[0]
    slabs = lambda ref: jnp.concatenate([ref[c] for c in range(ref.shape[0])], axis=-1)
    lses = [al0_ref[...], slabs(al1_ref), slabs(al2_ref)]
    aos = [ao0_ref[...], slabs(ao1_ref), slabs(ao2_ref)]
    mx = jnp.maximum(jnp.maximum(lses[0], lses[1]), lses[2])
    es = [jnp.exp(l - mx) for l in lses]
    tot = es[0] + es[1] + es[2]
    ya = jnp.concatenate([o * (e / tot) for o, e in zip(aos, es)], axis=-1).astype(BF16)
    zt = zt_ref[...]
    glu = jnp.dot(wglut_ref[...], zt, preferred_element_type=F32) + bglu_ref[...]
    ybt = zt.astype(F32) * _sigmoid(glu)
    yb = jnp.transpose(ybt).astype(BF16)
    merged = jnp.zeros((tm, D_MODEL), F32)
    for n, (y, w_ref) in enumerate(((ya, wba_ref), (yb, wbb_ref), (yc_ref[...], wbc_ref), (yd_ref[...], wbd_ref))):
        bo = jnp.dot(y, w_ref[...], preferred_element_type=F32)
        merged = merged + (jnp.tanh(gl_ref[:, n * D_MODEL:(n + 1) * D_MODEL].astype(F32)) + 1.0) * bo
    h = DEEPNORM_ALPHA * x_ref[...] + jnp.dot(merged.astype(BF16), wout_ref[...], preferred_element_type=F32)
    x1_ref[...] = _ln(h, g1_ref[...], b1_ref[...])


def _router_kernel(x1_ref, rwh_ref, rwl_ref, rb_ref, tri_ref, x1p_ref, route_ref, counts_ref, carry_ref):
    tm = x1_ref.shape[0]

    @pl.when(pl.program_id(0) == 0)
    def _():
        carry_ref[...] = jnp.zeros_like(carry_ref)

    x1 = x1_ref[...]
    x1p_ref[...] = _pack_bf16_pairs(x1)
    x_hi = x1.astype(BF16)
    x_lo = (x1 - x_hi.astype(F32)).astype(BF16)
    logits = (jnp.dot(x_hi, rwh_ref[...], preferred_element_type=F32)
              + jnp.dot(x_lo, rwh_ref[...], preferred_element_type=F32)
              + jnp.dot(x_hi, rwl_ref[...], preferred_element_type=F32)) + rb_ref[...]
    lane = lax.broadcasted_iota(jnp.int32, (tm, LANE), 1)
    work = logits
    sel, vals, hots = [], [], []
    for _ in range(TOP_K):
        mk = jnp.max(work, axis=-1, keepdims=True)
        ik = jnp.min(jnp.where(work == mk, lane, LANE), axis=-1, keepdims=True)
        hot = lane == ik
        work = jnp.where(hot, -jnp.inf, work)
        sel.append(ik)
        vals.append(mk)
        hots.append(hot)
    ex = [jnp.exp(v - vals[0]) for v in vals]
    den = ex[0] + ex[1] + ex[2] + ex[3]
    chosen = (hots[0] | hots[1] | hots[2] | hots[3]).astype(F32)
    before = jnp.dot(tri_ref[...], chosen.astype(BF16), preferred_element_type=F32) + carry_ref[0:1, :]
    route = jnp.zeros((tm, LANE), F32)
    for k in range(TOP_K):
        rank = jnp.sum(jnp.where(hots[k], before, 0.0), axis=-1, keepdims=True)
        route = jnp.where(lane == k, sel[k].astype(F32), route)
        route = jnp.where(lane == TOP_K + k, ex[k] / den, route)
        route = jnp.where(lane == 2 * TOP_K + k, rank, route)
    route_ref[...] = route
    new_carry = carry_ref[0:1, :] + jnp.sum(chosen, axis=0, keepdims=True)
    carry_ref[...] = jnp.broadcast_to(new_carry, carry_ref.shape)
    counts_ref[...] = jnp.broadcast_to(new_carry, counts_ref.shape)


def _merge_call(x, aos, alses, zt, yc, yd, proj, lw):
    t = x.shape[0]
    tm = 256
    gw = GW
    gate_blk = 4 * D_MODEL
    row = lambda w: pl.BlockSpec((tm, w), lambda i: (i, 0))
    slab = pl.BlockSpec((gw // LANE, tm, LANE), lambda i: (0, i, 0))
    full = lambda a: pl.BlockSpec(a.shape, lambda i: (0,) * a.ndim)
    consts = (lw["w_glu_t"], lw["b_glu"], lw["wb_a"], lw["wb_b"], lw["wb_c"], lw["wb_d"], lw["w_out"],
              lw["ln1_g"], lw["ln1_b"])
    return pl.pallas_call(
        _merge_kernel, name="branch_merge",
        grid=(t // tm,),
        in_specs=[row(D_MODEL)] + [row(gw), slab, slab] * 2
                 + [pl.BlockSpec((S5_WIDTH, tm), lambda i: (0, i)),
                  row(RET_HEADS * RET_V_DIM), row(MEM_HEADS * MEM_HEAD_PAD),
                  pl.BlockSpec((tm, gate_blk), lambda i: (i, _SEG["gate"] * LANE // gate_blk))]
                 + [full(a) for a in consts],
        out_specs=row(D_MODEL),
        out_shape=jax.ShapeDtypeStruct((t, D_MODEL), F32),
        compiler_params=_cparams(("parallel",)),
    )(x, *aos, *alses, zt, yc, yd, proj, *consts)


def _router_call(x1, lw):
    t = x1.shape[0]
    tm = 1024
    row = lambda w: pl.BlockSpec((tm, w), lambda i: (i, 0))
    full = lambda a: pl.BlockSpec(a.shape, lambda i: (0,) * a.ndim)
    tri = jnp.tril(jnp.ones((tm, tm), BF16), -1)
    consts = (lw["router_w_hi"], lw["router_w_lo"], lw["router_b"], tri)
    return pl.pallas_call(
        _router_kernel, name="moe_router",
        grid=(t // tm,),
        in_specs=[row(D_MODEL)] + [full(a) for a in consts],
        out_specs=[row(D_MODEL // 2), row(LANE), pl.BlockSpec((8, LANE), lambda i: (0, 0))],
        out_shape=[jax.ShapeDtypeStruct((t, D_MODEL // 2), jnp.uint32),
                   jax.ShapeDtypeStruct((t, LANE), F32),
                   jax.ShapeDtypeStruct((8, LANE), F32)],
        scratch_shapes=[pltpu.VMEM((8, LANE), F32)],
        compiler_params=_cparams(("arbitrary",)),
    )(x1, *consts)


def _pack_bf16_pairs(x):
    n = x.shape[1] // 2
    hi = lax.bitcast_convert_type(x[:, :n].astype(BF16).astype(F32), jnp.uint32)
    lo = lax.bitcast_convert_type(x[:, n:].astype(BF16).astype(F32), jnp.uint32)
    return hi | (lo >> 16)


def _unpack_bf16_pairs(w):
    hi = lax.bitcast_convert_type(w & jnp.uint32(0xFFFF0000), F32)
    lo = lax.bitcast_convert_type(w << 16, F32)
    return jnp.concatenate([hi, lo], axis=-1)


def _sc_index_rows(idx):
    window = SC_GATHER_WINDOW
    return jnp.pad(idx.reshape(-1, window), ((0, 0), (0, LANE - window)))


def _sc_scatter_rows(x, idx, n_out):
    n, d = x.shape
    m = idx.shape[0]
    window = SC_GATHER_WINDOW
    src_steps = n // window
    mesh = plsc.VectorSubcoreMesh(core_axis_name="core", subcore_axis_name="subcore")

    @pl.kernel(out_type=jax.ShapeDtypeStruct((n_out, d), x.dtype), mesh=mesh, scratch_types=[])
    def scatter_kernel(x_hbm, i_hbm, o_hbm):
        def body(x_vmem, i_vmem):
            pltpu.sync_copy(x_vmem, o_hbm.at[i_vmem.at[0, pl.ds(0, window)]])

        pltpu.emit_pipeline(
            body,
            grid=(m // window,),
            in_specs=[pl.BlockSpec((window, d), index_map=lambda i: (i % src_steps, 0)),
                      pl.BlockSpec((1, LANE), index_map=lambda i: (i, 0))],
            out_specs=[],
            core_axis_name=("core", "subcore"),
            dimension_semantics=(pltpu.PARALLEL,),
            trace_scopes=False,
        )(x_hbm, i_hbm)

    return scatter_kernel(x, _sc_index_rows(idx))


def _sc_gather_rows(x, idx):
    m, d = idx.shape[0], x.shape[1]
    window = SC_GATHER_WINDOW
    mesh = plsc.VectorSubcoreMesh(core_axis_name="core", subcore_axis_name="subcore")

    @pl.kernel(out_type=jax.ShapeDtypeStruct((m, d), x.dtype), mesh=mesh, scratch_types=[])
    def gather_kernel(x_hbm, i_hbm, o_hbm):
        def body(i_vmem, o_vmem):
            pltpu.sync_copy(x_hbm.at[i_vmem.at[0, pl.ds(0, window)]], o_vmem)

        pltpu.emit_pipeline(
            body,
            grid=(m // window,),
            in_specs=[pl.BlockSpec((1, LANE), index_map=lambda i: (i, 0))],
            out_specs=[pl.BlockSpec((window, d), index_map=lambda i: (i, 0))],
            core_axis_name=("core", "subcore"),
            dimension_semantics=(pltpu.PARALLEL,),
            trace_scopes=False,
        )(i_hbm, o_hbm)

    return gather_kernel(x, _sc_index_rows(idx))


def _moe_expert_kernel(be_ref, nu_ref, nv_ref, x_ref, w1_ref, b1_ref, w2_ref, b2_ref, o_ref, w1s_ref, w2s_ref):
    i = pl.program_id(0)

    @pl.when((i == 0) | (be_ref[i] != be_ref[jnp.maximum(i - 1, 0)]))
    def _():
        w1s_ref[...] = w1_ref[...].astype(BF16)
        w2s_ref[...] = w2_ref[...].astype(BF16)

    @pl.when(i < nu_ref[0])
    def _():
        rows = lax.broadcasted_iota(jnp.int32, (x_ref.shape[0], 1), 0)
        x = jnp.where(rows < nv_ref[i], _unpack_bf16_pairs(x_ref[...]), 0.0)
        h = jnp.dot(x.astype(BF16), w1s_ref[...], preferred_element_type=F32) + b1_ref[...]
        gate = jnp.minimum(h[:, :D_FF], SWIGLU_LIMIT)
        up = jnp.clip(h[:, D_FF:], -SWIGLU_LIMIT, SWIGLU_LIMIT)
        hid = gate * _sigmoid(SWIGLU_ALPHA * gate) * (up + 1.0)
        y = jnp.dot(hid.astype(BF16), w2s_ref[...], preferred_element_type=F32) + b2_ref[...]
        o_ref[...] = _pack_bf16_pairs(y)

    @pl.when(i >= nu_ref[0])
    def _():
        o_ref[...] = jnp.zeros_like(o_ref)


def _moe_expert_call(xg, block_expert, n_used, n_valid, layer, w1, b1, w2, b2):
    bm = MOE_BLOCK_ROWS
    n_blocks = xg.shape[0] // bm
    half = D_MODEL // 2
    grid_spec = pltpu.PrefetchScalarGridSpec(
        num_scalar_prefetch=3, grid=(n_blocks,),
        in_specs=[pl.BlockSpec((bm, half), lambda i, be, nu, nv: (i, 0)),
                  pl.BlockSpec((None, None, D_MODEL, 2 * D_FF), lambda i, be, nu, nv: (layer, be[i], 0, 0)),
                  pl.BlockSpec((None, 1, 2 * D_FF), lambda i, be, nu, nv: (be[i], 0, 0)),
                  pl.BlockSpec((None, None, D_FF, D_MODEL), lambda i, be, nu, nv: (layer, be[i], 0, 0)),
                  pl.BlockSpec((None, 1, D_MODEL), lambda i, be, nu, nv: (be[i], 0, 0))],
        out_specs=pl.BlockSpec((bm, half), lambda i, be, nu, nv: (i, 0)),
        scratch_shapes=[pltpu.VMEM((D_MODEL, 2 * D_FF), BF16), pltpu.VMEM((D_FF, D_MODEL), BF16)])
    return pl.pallas_call(
        _moe_expert_kernel, name="moe_experts",
        grid_spec=grid_spec,
        out_shape=jax.ShapeDtypeStruct((n_blocks * bm, half), jnp.uint32),
        compiler_params=_cparams(("arbitrary",)),
    )(block_expert, n_used, n_valid, xg, w1, b1, w2, b2)


def _moe_combine_kernel(x_ref, route_ref, y0_ref, y1_ref, y2_ref, y3_ref, g_ref, b_ref, o_ref):
    route = route_ref[...]
    moe = jnp.zeros(x_ref.shape, F32)
    for k, y_ref in enumerate((y0_ref, y1_ref, y2_ref, y3_ref)):
        moe = moe + route[:, TOP_K + k:TOP_K + k + 1] * _unpack_bf16_pairs(y_ref[...])
    o_ref[...] = _ln(DEEPNORM_ALPHA * x_ref[...] + moe, g_ref[...], b_ref[...])


def _moe_combine_call(x1, route, yg, g2, b2):
    t = x1.shape[0]
    tm = 512
    nt = t // tm
    ysel = lambda k: pl.BlockSpec((tm, D_MODEL // 2), lambda i: (k * nt + i, 0))
    return pl.pallas_call(
        _moe_combine_kernel, name="moe_combine_ln",
        grid=(nt,),
        in_specs=[pl.BlockSpec((tm, D_MODEL), lambda i: (i, 0)),
                  pl.BlockSpec((tm, LANE), lambda i: (i, 0))]
                 + [ysel(k) for k in range(TOP_K)]
                 + [pl.BlockSpec((1, D_MODEL), lambda i: (0, 0)),
                    pl.BlockSpec((1, D_MODEL), lambda i: (0, 0))],
        out_specs=pl.BlockSpec((tm, D_MODEL), lambda i: (i, 0)),
        out_shape=jax.ShapeDtypeStruct((t, D_MODEL), F32),
        compiler_params=_cparams(("parallel",)),
    )(x1, route, yg, yg, yg, yg, g2, b2)


def _moe(x1, x1p, route, counts, lw):
    t = x1.shape[0]
    bm = MOE_BLOCK_ROWS
    n_assign = t * TOP_K
    n_blocks = n_assign // bm + N_EXPERTS
    idx = route[:, :TOP_K].astype(jnp.int32)
    rank = route[:, 2 * TOP_K:3 * TOP_K].astype(jnp.int32)
    cnt = counts[0, :N_EXPERTS].astype(jnp.int32)
    padded = (cnt + bm - 1) // bm * bm
    pad_ends = jnp.cumsum(padded)
    pad_starts = pad_ends - padded
    dest = jnp.transpose(jnp.take(pad_starts, idx) + rank).reshape(-1)
    block_start = jnp.arange(n_blocks, dtype=jnp.int32) * bm
    block_expert = jnp.minimum(jnp.sum(pad_ends[None, :] <= block_start[:, None], axis=1),
                               N_EXPERTS - 1).astype(jnp.int32)
    n_valid = jnp.clip(jnp.take(cnt, block_expert) - (block_start - jnp.take(pad_starts, block_expert)), 0, bm)
    n_used = (pad_ends[-1:] // bm).astype(jnp.int32)
    xg = _sc_scatter_rows(x1p, dest, n_blocks * bm)
    yb = _moe_expert_call(xg, block_expert, n_used, n_valid.astype(jnp.int32), lw["layer"],
                          lw["moe_w1"], lw["moe_b1"], lw["moe_w2"], lw["moe_b2"])
    yg = _sc_gather_rows(yb, dest)
    return _moe_combine_call(x1, route, yg, lw["ln2_g"], lw["ln2_b"])


def _rotate_half_cols(w):
    half = RET_QK_DIM // 2
    w4 = w.reshape(w.shape[0], RET_HEADS, 2, half)
    return jnp.concatenate([-w4[:, :, 1:], w4[:, :, :1]], axis=2).reshape(w.shape)


def _pad_heads(w, axis):
    shp = list(w.shape)
    shp[axis:axis + 1] = [MEM_HEADS, MEM_HEAD_DIM]
    w = w.reshape(shp)
    pad = [(0, 0)] * w.ndim
    pad[axis + 1] = (0, MEM_HEAD_PAD - MEM_HEAD_DIM)
    w = jnp.pad(w, pad)
    shp[axis:axis + 2] = [MEM_HEADS * MEM_HEAD_PAD]
    return w.reshape(shp)


def _layer_weights(l, p):
    w_in = p["w_in"][l]
    aw, sw, qkw, vw, mw = 768, S5_WIDTH, RET_HEADS * RET_QK_DIM, RET_HEADS * RET_V_DIM, 768
    offs = [0]
    for wdt in (aw, aw, aw, sw, qkw, qkw, vw, vw, mw, 4 * D_MODEL):
        offs.append(offs[-1] + wdt)
    aq, ak, av, su, rq, rk, rv, rg, mq, gate = (w_in[:, offs[i]:offs[i + 1]] for i in range(10))
    seg = dict(gate=0.5 * gate, mq=_pad_heads(mq, 1), aq=aq[:, :GW], ak=ak[:, :GW], av=av[:, :GW],
               pad=jnp.zeros((D_MODEL, 2 * LANE), F32), rv=rv, rg=rg,
               rq=rq, rqs=_rotate_half_cols(rq), rk=rk, rks=_rotate_half_cols(rk))
    order = sorted(_SEG, key=_SEG.get)
    w_p = jnp.concatenate([seg[n] for n in order], axis=1).astype(BF16)
    assert w_p.shape[1] == NP
    w_d = jnp.concatenate([w[:, g * GW:(g + 1) * GW] for g in (1, 2) for w in (aq, ak, av)], axis=1).astype(BF16)
    wkv = p["w_mem_kv"][l]
    w_kv = jnp.concatenate([_pad_heads(wkv[:, :768], 1), _pad_heads(wkv[:, 768:], 1)], axis=1).astype(BF16)
    wb = p["w_branch"][l]
    rw = jnp.pad(p["router_w"][l].astype(F32), ((0, 0), (0, LANE - N_EXPERTS)))
    rb = jnp.pad(p["router_b"][l].astype(F32), (0, LANE - N_EXPERTS), constant_values=NEG_INF)
    return dict(
        w_p=w_p, w_d=w_d, w_su_t=jnp.transpose(su).astype(BF16), w_kv=w_kv,
        w_glu_t=jnp.transpose(p["s5_w_glu"][l]).astype(BF16), b_glu=p["s5_b_glu"][l].astype(F32).reshape(-1, 1),
        wb_a=wb[0].astype(BF16), wb_b=wb[1].astype(BF16), wb_c=wb[2].astype(BF16),
        wb_d=_pad_heads(wb[3], 0).astype(BF16), w_out=(0.5 * p["w_out"][l]).astype(BF16),
        ln1_g=p["ln1_g"][l].reshape(1, -1), ln1_b=p["ln1_b"][l].reshape(1, -1),
        router_w_hi=rw.astype(BF16), router_w_lo=(rw - rw.astype(BF16).astype(F32)).astype(BF16),
        router_b=rb.reshape(1, -1),
        layer=l, moe_w1=p["moe_w1"], moe_b1=p["moe_b1"][l].astype(F32).reshape(N_EXPERTS, 1, -1),
        moe_w2=p["moe_w2"], moe_b2=p["moe_b2"][l].astype(F32).reshape(N_EXPERTS, 1, -1),
        ln2_g=p["ln2_g"][l].reshape(1, -1), ln2_b=p["ln2_b"][l].reshape(1, -1),
        s5=_s5_tables(p["s5_lam_re"][l], p["s5_lam_im"][l], p["s5_log_step"][l], p["s5_b_re"][l],
                      p["s5_b_im"][l], p["s5_c_re"][l], p["s5_c_im"][l], p["s5_d"][l]))


def _trunk_layer(x, mem2d, b, s, lw, attn_bias, ret_tabs):
    proj, ut, pd = _proj_call(x, lw["w_p"], lw["w_su_t"], lw["w_d"])
    outs = [_attn_call(proj, attn_bias[0], b, s)]
    outs += [_dil_attn_call(pd, attn_bias[g], b, s, g, A_GROUPS[g][1]) for g in (1, 2)]
    aos, alses = [o for o, _ in outs], [l for _, l in outs]
    toep, win, wout, a_re, a_im = lw["s5"]
    nc = s // S5_CHUNK
    c_re, c_im = _s5_contrib_call(ut, win, b, nc)
    s_re, s_im = _s5_scan_call(c_re, c_im, a_re, a_im)
    zt = _s5_out_call(ut, toep, s_re, s_im, wout, b, nc)
    yc = _retention_call(proj, ret_tabs, b, s)
    kv = _mem_kv_call(mem2d, lw["w_kv"])
    yd = _mem_attn_call(proj, kv, b, s)
    x1 = _merge_call(x, aos, alses, zt, yc, yd, proj, lw)
    x1p, route, counts = _router_call(x1, lw)
    return _moe(x1, x1p, route, counts, lw)


def kernel(x_prompt, x_sample, mem_prompt, mem_sample, ln_in_g, ln_in_b, rel_bias, w_in, s5_lam_re, s5_lam_im, s5_log_step, s5_b_re, s5_b_im, s5_c_re, s5_c_im, s5_d, s5_w_glu, s5_b_glu, w_mem_kv, w_branch, w_out, ln1_g, ln1_b, router_w, router_b, moe_w1, moe_b1, moe_w2, moe_b2, ln2_g, ln2_b):
    p = dict(w_in=w_in, s5_lam_re=s5_lam_re, s5_lam_im=s5_lam_im, s5_log_step=s5_log_step, s5_b_re=s5_b_re,
             s5_b_im=s5_b_im, s5_c_re=s5_c_re, s5_c_im=s5_c_im, s5_d=s5_d, s5_w_glu=s5_w_glu, s5_b_glu=s5_b_glu,
             w_mem_kv=w_mem_kv, w_branch=w_branch, w_out=w_out, ln1_g=ln1_g, ln1_b=ln1_b, router_w=router_w,
             router_b=router_b, moe_w1=moe_w1, moe_b1=moe_b1, moe_w2=moe_w2, moe_b2=moe_b2, ln2_g=ln2_g,
             ln2_b=ln2_b)
    attn_bias = [_attn_bias_table(rel_bias, g, dil) for g, (_, dil) in enumerate(A_GROUPS)]
    trunks = []
    for x, mem in ((x_prompt, mem_prompt), (x_sample, mem_sample)):
        b, s, _ = x.shape
        trunks.append(dict(x=_layer_norm_call(x.reshape(b * s, D_MODEL), ln_in_g, ln_in_b),
                           mem=mem.reshape(b * N_MEM, D_MODEL), b=b, s=s, tabs=_ret_tables(s)))
    for l in range(DEPTH):
        lw = _layer_weights(l, p)
        for tr in trunks:
            tr["x"] = _trunk_layer(tr["x"], tr["mem"], tr["b"], tr["s"], lw, attn_bias, tr["tabs"])
    return tuple(tr["x"].reshape(tr["b"], tr["s"], D_MODEL) for tr in trunks)
```

```python
import functools
import math

import jax
import jax.numpy as jnp
from jax import lax
from jax.experimental import pallas as pl
from jax.experimental.pallas import tpu as pltpu
from jax.experimental.pallas import tpu_sc as plsc

F32 = jnp.float32
BF16 = jnp.bfloat16
HIGHEST = lax.Precision.HIGHEST

D_MODEL = 1024
DEPTH = 4
N_MEM = 256
A_GROUPS = ((128, 1), (512, 4), (2048, 16))
A_HEADS_PER_GROUP = 4
A_HEADS = 12
A_HEAD_DIM = 64
A_RADIUS = 64
REL_BUCKETS = 32
REL_MAX_DIST = 1024
S5_GROUP = 16
S5_WIDTH = 768
S5_GROUPS = 48
S5_STATE = 64
RET_HEADS = 6
RET_QK_DIM = 64
RET_V_DIM = 128
RET_CHUNK = 128
ROPE_BASE = 10000.0
MEM_HEADS = 4
MEM_HEAD_DIM = 192
MEM_HEAD_PAD = 256
N_EXPERTS = 32
TOP_K = 4
D_FF = 1024
SWIGLU_LIMIT = 7.0
SWIGLU_ALPHA = 1.702
LN_EPS = 1e-5
DEEPNORM_ALPHA = (2 * DEPTH) ** 0.25
NEG_INF = -1e30

LANE = 128
VMEM_LIMIT = 56 * 1024 * 1024
S5_CHUNK = 128
MOE_BLOCK_ROWS = 512
SC_GATHER_WINDOW = 64

_SEG = {}
_off = 0
for _name, _width, _align in (
        ("gate", 32, 8), ("mq", 8, 8), ("aq", 2, 2), ("ak", 2, 2), ("av", 2, 2), ("pad", 2, 2),
        ("rv", 6, 6), ("rg", 6, 6), ("rq", 3, 3), ("rqs", 3, 3), ("rk", 3, 3), ("rks", 3, 3)):
    assert _off % _align == 0, (_name, _off)
    _SEG[_name] = _off
    _off += _width
NP_UNITS = _off
NP = NP_UNITS * LANE
assert NP % 256 == 0
GW = A_HEADS_PER_GROUP * A_HEAD_DIM
ND = 2 * 3 * GW


def _cparams(sem, vmem=VMEM_LIMIT):
    return pltpu.CompilerParams(dimension_semantics=sem, vmem_limit_bytes=vmem)


def _sigmoid(x):
    return 0.5 * jnp.tanh(0.5 * x) + 0.5


def _ln(h, g, b):
    mu = jnp.mean(h, axis=-1, keepdims=True)
    d = h - mu
    var = jnp.mean(d * d, axis=-1, keepdims=True)
    return d * lax.rsqrt(var + LN_EPS) * g + b


def _ln_kernel(x_ref, g_ref, b_ref, o_ref):
    o_ref[...] = _ln(x_ref[...], g_ref[...], b_ref[...])


def _layer_norm_call(x, g, b):
    t = x.shape[0]
    tm = 512
    return pl.pallas_call(
        _ln_kernel, name="input_ln",
        grid=(t // tm,),
        in_specs=[pl.BlockSpec((tm, D_MODEL), lambda i: (i, 0)),
                  pl.BlockSpec((1, D_MODEL), lambda i: (0, 0)),
                  pl.BlockSpec((1, D_MODEL), lambda i: (0, 0))],
        out_specs=pl.BlockSpec((tm, D_MODEL), lambda i: (i, 0)),
        out_shape=jax.ShapeDtypeStruct((t, D_MODEL), F32),
        compiler_params=_cparams(("parallel",)),
    )(x, g.reshape(1, -1), b.reshape(1, -1))


def _proj_kernel(x_ref, w_ref, wsut_ref, wd_ref, p_ref, ut_ref, pd_ref, xb_ref):
    @pl.when(pl.program_id(1) == 0)
    def _():
        xb = x_ref[...].astype(BF16)
        xb_ref[...] = xb
        ut_ref[...] = lax.dot_general(wsut_ref[...], xb, (((1,), (1,)), ((), ())),
                                      preferred_element_type=F32).astype(BF16)
        pd = jnp.dot(xb, wd_ref[...], preferred_element_type=F32)
        for c in range(ND // LANE):
            pd_ref[c] = pd[:, c * LANE:(c + 1) * LANE]

    p_ref[...] = jnp.dot(xb_ref[...], w_ref[...], preferred_element_type=F32).astype(BF16)


def _proj_call(x, w_p, w_su_t, w_d):
    t = x.shape[0]
    tm, tn = 512, NP // 3
    return pl.pallas_call(
        _proj_kernel, name="in_proj",
        grid=(t // tm, NP // tn),
        in_specs=[pl.BlockSpec((tm, D_MODEL), lambda i, j: (i, 0)),
                  pl.BlockSpec((D_MODEL, tn), lambda i, j: (0, j)),
                  pl.BlockSpec((S5_WIDTH, D_MODEL), lambda i, j: (0, 0)),
                  pl.BlockSpec((D_MODEL, ND), lambda i, j: (0, 0))],
        out_specs=[pl.BlockSpec((tm, tn), lambda i, j: (i, j)),
                   pl.BlockSpec((S5_WIDTH, tm), lambda i, j: (0, i)),
                   pl.BlockSpec((ND // LANE, tm, LANE), lambda i, j: (0, i, 0))],
        out_shape=[jax.ShapeDtypeStruct((t, NP), BF16),
                   jax.ShapeDtypeStruct((S5_WIDTH, t), BF16),
                   jax.ShapeDtypeStruct((ND // LANE, t, LANE), F32)],
        scratch_shapes=[pltpu.VMEM((tm, D_MODEL), BF16)],
        compiler_params=_cparams(("parallel", "arbitrary")),
    )(x, w_p, w_su_t, w_d)


ATTN_TQ = 2 * A_RADIUS
ATTN_PAIR = 4


def _banded_heads(blocks, bias_ref, sub_len):
    tq, nk = blocks[0][0].shape[0], blocks[0][1].shape[0]
    width = blocks[0][0].shape[1]
    col = lax.broadcasted_iota(jnp.int32, (tq, nk), 1)
    valids = [(blk[3] + col >= 0) & (blk[3] + col < sub_len) for blk in blocks]
    lane_head = lax.broadcasted_iota(jnp.int32, (1, width), 1) // A_HEAD_DIM
    scale = jnp.asarray(A_HEAD_DIM ** -0.5, BF16)
    pairs = [(bk, h) for bk in range(len(blocks)) for h in range(A_HEADS_PER_GROUP)]
    qs = [jnp.where(lane_head == h, blocks[bk][0], jnp.zeros_like(blocks[bk][0])) * scale for bk, h in pairs]
    ss = [lax.dot_general(qh, blocks[bk][1], (((1,), (1,)), ((), ())), preferred_element_type=F32)
          for qh, (bk, h) in zip(qs, pairs)]
    ss = [jnp.where(valids[bk], s + bias_ref[h], NEG_INF) for s, (bk, h) in zip(ss, pairs)]
    ms = [jnp.max(s, axis=-1, keepdims=True) for s in ss]
    ps = [jnp.exp(s - m) for s, m in zip(ss, ms)]
    dens = [jnp.sum(p, axis=-1, keepdims=True) for p in ps]
    ohs = [jnp.dot(p.astype(BF16), blocks[bk][2], preferred_element_type=F32) for p, (bk, h) in zip(ps, pairs)]
    outs = []
    for bk in range(len(blocks)):
        acc = jnp.zeros((tq, width), F32)
        lse_acc = jnp.zeros((tq, width), F32)
        for j, (pb, h) in enumerate(pairs):
            if pb == bk:
                hm = lane_head == h
                acc = jnp.where(hm, ohs[j] / dens[j], acc)
                lse_acc = jnp.where(hm, ms[j] + jnp.log(dens[j]), lse_acc)
        outs.append((acc, lse_acc))
    return outs


def _attn_kernel(q_ref, kp_ref, kc_ref, kn_ref, vp_ref, vc_ref, vn_ref, bias_ref, o_ref, lse_ref, *, sub_len):
    tq = ATTN_TQ
    kcat = jnp.concatenate([kp_ref[...], kc_ref[...], kn_ref[...]], axis=0)
    vcat = jnp.concatenate([vp_ref[...], vc_ref[...], vn_ref[...]], axis=0)
    first = pl.program_id(1) * (ATTN_PAIR * tq) - A_RADIUS
    blocks = [(q_ref[j * tq:(j + 1) * tq, :], kcat[j * tq:j * tq + 2 * tq, :], vcat[j * tq:j * tq + 2 * tq, :],
               first + j * tq) for j in range(ATTN_PAIR)]
    for j, (acc, lse) in enumerate(_banded_heads(blocks, bias_ref, sub_len)):
        o_ref[j * tq:(j + 1) * tq, :] = acc
        lse_ref[j * tq:(j + 1) * tq, :] = lse


def _attn_call(proj, bias, b, s):
    rows = ATTN_PAIR * ATTN_TQ
    nstep = s // rows
    edge = rows // A_RADIUS
    pv = proj.reshape(b, s, NP)
    qo, ko, vo = (_SEG[n] * LANE // GW for n in ("aq", "ak", "av"))
    blk, eblk = (None, rows, GW), (None, A_RADIUS, GW)
    cur = lambda off: (lambda bi, i: (bi, i, off))
    prv = lambda off: (lambda bi, i: (bi, jnp.maximum(i * edge - 1, 0), off))
    nxt = lambda off: (lambda bi, i: (bi, jnp.minimum((i + 1) * edge, nstep * edge - 1), off))
    out_map = lambda bi, i: (bi, i, 0)
    o, lse = pl.pallas_call(
        functools.partial(_attn_kernel, sub_len=s), name="window_attn_g0",
        grid=(b, nstep),
        in_specs=[pl.BlockSpec(blk, cur(qo)),
                  pl.BlockSpec(eblk, prv(ko)), pl.BlockSpec(blk, cur(ko)), pl.BlockSpec(eblk, nxt(ko)),
                  pl.BlockSpec(eblk, prv(vo)), pl.BlockSpec(blk, cur(vo)), pl.BlockSpec(eblk, nxt(vo)),
                  pl.BlockSpec((A_HEADS_PER_GROUP, ATTN_TQ, 2 * ATTN_TQ), lambda bi, i: (0, 0, 0))],
        out_specs=[pl.BlockSpec(blk, out_map), pl.BlockSpec(blk, out_map)],
        out_shape=[jax.ShapeDtypeStruct((b, s, GW), F32), jax.ShapeDtypeStruct((b, s, GW), F32)],
        compiler_params=_cparams(("parallel", "parallel")),
    )(pv, pv, pv, pv, pv, pv, pv, bias)
    return o.reshape(b * s, GW), lse.reshape(b * s, GW)


def _dil_attn_kernel(q_ref, kp_ref, kc_ref, kn_ref, vp_ref, vc_ref, vn_ref, bias_ref, o_ref, lse_ref, *,
                     dil, sub_len):
    tq, half = ATTN_TQ, A_RADIUS
    first_key = pl.program_id(1) * tq - half

    def residues(it, carry):
        def rows(ref, r, n):
            return jnp.concatenate([ref[c, pl.ds(r, n, stride=dil), :] for c in range(GW // LANE)], axis=-1)

        rs = [it * ATTN_PAIR + j for j in range(ATTN_PAIR)]
        blocks = []
        for r in rs:
            q = rows(q_ref, r, tq).astype(BF16)
            kwin = jnp.concatenate([rows(kp_ref, r, half), rows(kc_ref, r, tq), rows(kn_ref, r, half)],
                                   axis=0).astype(BF16)
            vwin = jnp.concatenate([rows(vp_ref, r, half), rows(vc_ref, r, tq), rows(vn_ref, r, half)],
                                   axis=0).astype(BF16)
            blocks.append((q, kwin, vwin, first_key))
        for r, (acc, lse) in zip(rs, _banded_heads(blocks, bias_ref, sub_len)):
            for c in range(GW // LANE):
                o_ref[c, pl.ds(r, tq, stride=dil), :] = acc[:, c * LANE:(c + 1) * LANE]
                lse_ref[c, pl.ds(r, tq, stride=dil), :] = lse[:, c * LANE:(c + 1) * LANE]
        return carry

    lax.fori_loop(0, dil // ATTN_PAIR, residues, 0)


def _dil_attn_call(pd, bias, b, s, g, dil):
    tq = ATTN_TQ
    rows = tq * dil
    nblk = s // rows
    nslab = GW // LANE
    pv = pd.reshape(ND // LANE, b, s, LANE)
    qo, ko, vo = (3 * (g - 1) + j for j in range(3))
    blk, hblk = (nslab, None, rows, LANE), (nslab, None, rows // 2, LANE)
    cur = lambda off: (lambda bi, i: (off, bi, i, 0))
    prv = lambda off: (lambda bi, i: (off, bi, jnp.maximum(2 * i - 1, 0), 0))
    nxt = lambda off: (lambda bi, i: (off, bi, jnp.minimum(2 * i + 2, 2 * nblk - 1), 0))
    out_map = lambda bi, i: (0, bi, i, 0)
    o, lse = pl.pallas_call(
        functools.partial(_dil_attn_kernel, dil=dil, sub_len=s // dil), name=f"dilated_attn_g{g}",
        grid=(b, nblk),
        in_specs=[pl.BlockSpec(blk, cur(qo)),
                  pl.BlockSpec(hblk, prv(ko)), pl.BlockSpec(blk, cur(ko)), pl.BlockSpec(hblk, nxt(ko)),
                  pl.BlockSpec(hblk, prv(vo)), pl.BlockSpec(blk, cur(vo)), pl.BlockSpec(hblk, nxt(vo)),
                  pl.BlockSpec((A_HEADS_PER_GROUP, tq, 2 * tq), lambda bi, i: (0, 0, 0))],
        out_specs=[pl.BlockSpec(blk, out_map), pl.BlockSpec(blk, out_map)],
        out_shape=[jax.ShapeDtypeStruct((nslab, b, s, LANE), F32)] * 2,
        compiler_params=_cparams(("parallel", "parallel")),
    )(pv, pv, pv, pv, pv, pv, pv, bias)
    return o.reshape(nslab, b * s, LANE), lse.reshape(nslab, b * s, LANE)


def _t5_bucket(rel):
    nb = REL_BUCKETS // 2
    max_exact = nb // 2
    ret = jnp.where(rel > 0, nb, 0)
    n = jnp.abs(rel)
    nf = jnp.maximum(n, 1).astype(F32)
    large = max_exact + (jnp.log(nf / max_exact) / math.log(REL_MAX_DIST / max_exact)
                         * (nb - max_exact)).astype(jnp.int32)
    large = jnp.minimum(large, nb - 1)
    return ret + jnp.where(n < max_exact, n, large)


def _attn_bias_table(rel_bias, g, dil, tq=128):
    qi = jnp.arange(tq)[:, None]
    kj = jnp.arange(2 * tq)[None, :] - A_RADIUS
    rel = kj - qi
    tab = rel_bias[:, g * A_HEADS_PER_GROUP:(g + 1) * A_HEADS_PER_GROUP].astype(F32)
    bucket = _t5_bucket(rel * dil)[None]
    bias = jnp.zeros((A_HEADS_PER_GROUP,) + rel.shape, F32)
    for bkt in range(REL_BUCKETS):
        bias = jnp.where(bucket == bkt, tab[bkt][:, None, None], bias)
    return jnp.where((jnp.abs(rel) <= A_RADIUS)[None], bias, NEG_INF)


def _s5_toeplitz_kernel(bcf_ref, pwf_ref, bcb_ref, pwb_ref, d_ref, t_ref, gf_ref, gb_ref):
    gf_ref[...] = jnp.dot(bcf_ref[...], pwf_ref[...], precision=HIGHEST, preferred_element_type=F32)
    gb_ref[...] = jnp.dot(bcb_ref[...], pwb_ref[...], precision=HIGHEST, preferred_element_type=F32)
    c = S5_CHUNK
    row = lax.broadcasted_iota(jnp.int32, (c, c), 0)
    col = lax.broadcasted_iota(jnp.int32, (c, c), 1)

    def body(ci, carry):
        dval = d_ref[pl.ds(ci, 1), :]
        for co in range(S5_GROUP):
            r = ci * S5_GROUP + co
            gf = jnp.broadcast_to(gf_ref[pl.ds(r, 1), :], (c, c))
            gb = jnp.broadcast_to(gb_ref[pl.ds(r, 1), :], (c, c))
            tf = pltpu.roll(gf, 0, 1, stride=1, stride_axis=0)
            tb = pltpu.roll(gb, 1, 1, stride=1, stride_axis=0)
            tile = jnp.where(col >= row, tf, 0.0) + jnp.where(row >= col, tb, 0.0)
            tile = tile + jnp.where((row == col) & (ci == co), dval, 0.0)
            t_ref[pl.ds(pl.multiple_of(ci * c, c), c), co * c:(co + 1) * c] = tile.astype(BF16)
        return carry

    lax.fori_loop(0, S5_GROUP, body, 0)


def _s5_toeplitz_call(bcf, pwf, bcb, pwb, dskip):
    n = S5_GROUP * S5_CHUNK
    return pl.pallas_call(
        _s5_toeplitz_kernel, name="s5_toeplitz",
        grid=(S5_GROUPS,),
        in_specs=[pl.BlockSpec((None, S5_GROUP * S5_GROUP, 2 * S5_STATE), lambda g: (g, 0, 0)),
                  pl.BlockSpec((None, 2 * S5_STATE, S5_CHUNK), lambda g: (g, 0, 0)),
                  pl.BlockSpec((None, S5_GROUP * S5_GROUP, 2 * S5_STATE), lambda g: (g, 0, 0)),
                  pl.BlockSpec((None, 2 * S5_STATE, S5_CHUNK), lambda g: (g, 0, 0)),
                  pl.BlockSpec((None, S5_GROUP, LANE), lambda g: (g, 0, 0))],
        out_specs=pl.BlockSpec((None, n, n), lambda g: (g, 0, 0)),
        out_shape=jax.ShapeDtypeStruct((S5_GROUPS, n, n), BF16),
        scratch_shapes=[pltpu.VMEM((S5_GROUP * S5_GROUP, S5_CHUNK), F32),
                        pltpu.VMEM((S5_GROUP * S5_GROUP, S5_CHUNK), F32)],
        compiler_params=_cparams(("parallel",)),
    )(bcf, pwf, bcb, pwb, dskip)


def _s5_tables(lam_re, lam_im, log_step, b_re, b_im, c_re, c_im, d_skip):
    c = S5_CHUNK
    k = jnp.arange(c, dtype=F32)
    per_dir = []
    for direction in range(2):
        step = jnp.exp(log_step[direction].astype(F32))[:, None]
        lr, li = lam_re[direction].astype(F32), lam_im[direction].astype(F32)
        mag = jnp.exp(lr * step)
        ar, ai = mag * jnp.cos(li * step), mag * jnp.sin(li * step)
        nr, ni = ar - 1.0, ai
        den = lr * lr + li * li
        fr = (nr * lr + ni * li) / den
        fi = (ni * lr - nr * li) / den
        br, bi = b_re[direction].astype(F32), b_im[direction].astype(F32)
        bbr = fr[..., None] * br - fi[..., None] * bi
        bbi = fr[..., None] * bi + fi[..., None] * br
        cr, cim = c_re[direction].astype(F32), c_im[direction].astype(F32)

        def power(e, log_mag=lr * step, phase=li * step):
            m = jnp.exp(e[None, None, :] * log_mag[..., None])
            th = e[None, None, :] * phase[..., None]
            return m * jnp.cos(th), m * jnp.sin(th)

        per_dir.append(dict(bbr=bbr, bbi=bbi, cr=cr, cim=cim, power=power))

    def bc_table(d):
        bbr_t = jnp.transpose(d["bbr"], (0, 2, 1))[:, :, None, :]
        bbi_t = jnp.transpose(d["bbi"], (0, 2, 1))[:, :, None, :]
        cr, cim = d["cr"][:, None], d["cim"][:, None]
        re = bbr_t * cr - bbi_t * cim
        im = bbr_t * cim + bbi_t * cr
        return jnp.concatenate([re, -im], axis=-1).reshape(S5_GROUPS, S5_GROUP * S5_GROUP, 2 * S5_STATE)

    f, bw = per_dir
    pfr, pfi = f["power"](k)
    pbr, pbi = bw["power"](c - 1 - k)
    pwf = jnp.concatenate([pfr, pfi], axis=1)
    pwb = jnp.concatenate([pbr, pbi], axis=1)
    dsk = jnp.broadcast_to(d_skip.astype(F32).reshape(S5_GROUPS, S5_GROUP, 1), (S5_GROUPS, S5_GROUP, LANE))
    toep = _s5_toeplitz_call(bc_table(f), pwf, bc_table(bw), pwb, dsk)

    def w_in(d, e):
        pr, pi = d["power"](e)
        re = d["bbr"][:, :, :, None] * pr[:, :, None, :] - d["bbi"][:, :, :, None] * pi[:, :, None, :]
        im = d["bbr"][:, :, :, None] * pi[:, :, None, :] + d["bbi"][:, :, :, None] * pr[:, :, None, :]
        tr = lambda z: jnp.transpose(z, (0, 2, 3, 1)).reshape(S5_GROUPS, S5_GROUP * c, S5_STATE)
        return tr(re), tr(im)

    fre, fim = w_in(f, c - 1 - k)
    bre, bim = w_in(bw, k)
    win = jnp.concatenate([fre, bre, fim, bim], axis=-1).astype(BF16)

    def w_out(d, e):
        pr, pi = d["power"](e)
        re = d["cr"][:, :, :, None] * pr[:, None] - d["cim"][:, :, :, None] * pi[:, None]
        im = d["cr"][:, :, :, None] * pi[:, None] + d["cim"][:, :, :, None] * pr[:, None]
        tr = lambda z: jnp.transpose(z, (0, 2, 1, 3)).reshape(S5_GROUPS, S5_STATE, S5_GROUP * c)
        return tr(re), tr(-im)

    ofre, ofim = w_out(f, k + 1.0)
    obre, obim = w_out(bw, c - k)
    wout = jnp.concatenate([ofre, obre, ofim, obim], axis=1).astype(BF16)

    cc = jnp.asarray([float(c)], F32)
    afr, afi = f["power"](cc)
    abr, abi = bw["power"](cc)
    a_re = jnp.concatenate([afr[..., 0], abr[..., 0]], axis=-1).reshape(1, -1)
    a_im = jnp.concatenate([afi[..., 0], abi[..., 0]], axis=-1).reshape(1, -1)
    return toep, win, wout, a_re, a_im


def _s5_contrib_kernel(u_ref, win_ref, re_ref, im_ref):
    uc = jnp.concatenate([u_ref[ci] for ci in range(S5_GROUP)], axis=-1)
    res = jnp.dot(uc, win_ref[...], preferred_element_type=F32)
    nc, nb = re_ref.shape[0], re_ref.shape[1]
    half = 2 * S5_STATE
    for bi in range(nb):
        re_ref[:, bi, :] = res[bi * nc:(bi + 1) * nc, :half]
        im_ref[:, bi, :] = res[bi * nc:(bi + 1) * nc, half:]


def _s5_contrib_call(ut, win, b, nc):
    nch = b * nc
    half = 2 * S5_STATE
    uv = ut.reshape(S5_GROUPS, S5_GROUP, nch, S5_CHUNK)
    return pl.pallas_call(
        _s5_contrib_kernel, name="s5_contrib",
        grid=(S5_GROUPS,),
        in_specs=[pl.BlockSpec((None, S5_GROUP, nch, S5_CHUNK), lambda g: (g, 0, 0, 0)),
                  pl.BlockSpec((None, S5_GROUP * S5_CHUNK, 4 * S5_STATE), lambda g: (g, 0, 0))],
        out_specs=[pl.BlockSpec((nc, b, half), lambda g: (0, 0, g))] * 2,
        out_shape=[jax.ShapeDtypeStruct((nc, b, S5_GROUPS * half), F32)] * 2,
        compiler_params=_cparams(("parallel",)),
    )(uv, win)


def _s5_scan_kernel(cre_ref, cim_ref, are_ref, aim_ref, ore_ref, oim_ref):
    nc, nb, width = cre_ref.shape
    a_re, a_im = are_ref[...], aim_ref[...]
    lane = lax.broadcasted_iota(jnp.int32, (1, width), 1)
    fwd_lane = (lane % (2 * S5_STATE)) < S5_STATE
    zero = jnp.zeros((nb, width), F32)

    def advance(s, c):
        sr, si = s
        return a_re * sr - a_im * si + cre_ref[c], a_re * si + a_im * sr + cim_ref[c]

    def up(c, s):
        ore_ref[c] = s[0]
        oim_ref[c] = s[1]
        return advance(s, c)

    lax.fori_loop(0, nc, up, (zero, zero))

    def down(i, s):
        c = nc - 1 - i
        ore_ref[c] = jnp.where(fwd_lane, ore_ref[c], s[0])
        oim_ref[c] = jnp.where(fwd_lane, oim_ref[c], s[1])
        return advance(s, c)

    lax.fori_loop(0, nc, down, (zero, zero))


def _s5_scan_call(cre, cim, a_re, a_im):
    return pl.pallas_call(
        _s5_scan_kernel, name="s5_chunk_scan",
        out_shape=[jax.ShapeDtypeStruct(cre.shape, F32)] * 2,
        compiler_params=pltpu.CompilerParams(vmem_limit_bytes=VMEM_LIMIT),
    )(cre, cim, a_re, a_im)


def _gelu_tanh(y):
    return 0.5 * y * (1.0 + jnp.tanh(math.sqrt(2.0 / math.pi) * (y + 0.044715 * (y * y * y))))


def _s5_out_kernel(u_ref, t_ref, sre_ref, sim_ref, wout_ref, z_ref):
    uc = jnp.concatenate([u_ref[ci] for ci in range(S5_GROUP)], axis=-1)
    y = jnp.dot(uc, t_ref[...], preferred_element_type=F32)
    sp = jnp.concatenate([jnp.concatenate([sre_ref[:, bi, :], sim_ref[:, bi, :]], axis=-1)
                          for bi in range(sre_ref.shape[1])], axis=0)
    y = y + jnp.dot(sp.astype(BF16), wout_ref[...], preferred_element_type=F32)
    z = _gelu_tanh(y)
    for co in range(S5_GROUP):
        z_ref[co] = z[:, co * S5_CHUNK:(co + 1) * S5_CHUNK].astype(BF16)


def _s5_out_call(ut, toep, s_re, s_im, wout, b, nc):
    nch = b * nc
    uv = ut.reshape(S5_GROUPS, S5_GROUP, nch, S5_CHUNK)
    n = S5_GROUP * S5_CHUNK
    half = 2 * S5_STATE
    zt = pl.pallas_call(
        _s5_out_kernel, name="s5_out",
        grid=(S5_GROUPS,),
        in_specs=[pl.BlockSpec((None, S5_GROUP, nch, S5_CHUNK), lambda g: (g, 0, 0, 0)),
                  pl.BlockSpec((None, n, n), lambda g: (g, 0, 0)),
                  pl.BlockSpec((nc, b, half), lambda g: (0, 0, g)),
                  pl.BlockSpec((nc, b, half), lambda g: (0, 0, g)),
                  pl.BlockSpec((None, 4 * S5_STATE, n), lambda g: (g, 0, 0))],
        out_specs=pl.BlockSpec((None, S5_GROUP, nch, S5_CHUNK), lambda g: (g, 0, 0, 0)),
        out_shape=jax.ShapeDtypeStruct((S5_GROUPS, S5_GROUP, nch, S5_CHUNK), BF16),
        compiler_params=_cparams(("parallel",)),
    )(uv, toep, s_re, s_im, wout)
    return zt.reshape(S5_WIDTH, nch * S5_CHUNK)


def _ret_tables(s):
    c = RET_CHUNK
    half = RET_QK_DIM // 2
    inv = ROPE_BASE ** (-jnp.arange(half, dtype=F32) / half)
    ang = jnp.arange(s, dtype=F32)[:, None] * inv[None, :]
    cos = jnp.tile(jnp.cos(ang), (1, 2 * RET_HEADS))
    sin = jnp.tile(jnp.sin(ang), (1, 2 * RET_HEADS))
    hidx = jnp.arange(RET_HEADS, dtype=F32)
    lgf = jnp.log1p(-jnp.exp2(-5.0 - hidx))
    lgb = jnp.log1p(-jnp.exp2(-5.5 - hidx))
    j = jnp.arange(c, dtype=F32)
    rel = j[:, None] - j[None, :]
    intra = jnp.exp(jnp.abs(rel)[None] * jnp.where(rel[None] >= 0, lgf[:, None, None], lgb[:, None, None]))
    qk_f = jnp.repeat(lgf, RET_QK_DIM)[None, :]
    qk_b = jnp.repeat(lgb, RET_QK_DIM)[None, :]
    v_f = jnp.repeat(lgf, RET_V_DIM)[None, :]
    v_b = jnp.repeat(lgb, RET_V_DIM)[None, :]
    tabs = dict(
        cos=cos, sin=sin, intra=intra,
        tail_f=jnp.exp((c - 1 - j)[:, None] * qk_f), head_b=jnp.exp(j[:, None] * qk_b),
        decq_f=jnp.exp((j + 1.0)[:, None] * qk_f), decq_b=jnp.exp((c - j)[:, None] * qk_b),
        chunk_f=jnp.exp(c * v_f), chunk_b=jnp.exp(c * v_b),
        bd=(jnp.arange(RET_HEADS * RET_QK_DIM)[:, None] // RET_QK_DIM
            == jnp.arange(RET_HEADS * RET_V_DIM)[None, :] // RET_V_DIM).astype(F32))
    return tabs


def _rot(x_ref, xs_ref, cos_ref, sin_ref):
    return x_ref[...].astype(F32) * cos_ref[...] + xs_ref[...].astype(F32) * sin_ref[...]


def _ret_kernel(q_ref, qs_ref, k_ref, ks_ref, v_ref, g_ref, cos_ref, sin_ref, intra_ref, tailf_ref, headb_ref,
                decqf_ref, decqb_ref, chunkf_ref, chunkb_ref, bd_ref, o_ref, st_ref, sbd_ref, sball_ref, *, nc):
    i = pl.program_id(1)
    qk, dv = RET_QK_DIM, RET_V_DIM
    vb = v_ref[...]
    kr = _rot(k_ref, ks_ref, cos_ref, sin_ref) * (RET_QK_DIM ** -0.5)

    @pl.when((i == 0) | (i == nc))
    def _():
        st_ref[...] = jnp.zeros_like(st_ref)

    @pl.when(i == 0)
    def _():
        sbd_ref[...] = jnp.zeros_like(sbd_ref)

    @pl.when(i < nc)
    def _():
        c = nc - 1 - i
        for h in range(RET_HEADS):
            sball_ref[c, h] = st_ref[h * qk:(h + 1) * qk, h * dv:(h + 1) * dv].astype(BF16)
        kwt = jnp.transpose(kr * headb_ref[...]).astype(BF16)
        kv = jnp.dot(kwt, vb, preferred_element_type=F32)
        st_ref[...] = st_ref[...] * chunkb_ref[...] + kv * bd_ref[...]

    @pl.when(i >= nc)
    def _():
        c = i - nc
        for h in range(RET_HEADS):
            sbd_ref[h * qk:(h + 1) * qk, h * dv:(h + 1) * dv] = sball_ref[c, h]
        qr = _rot(q_ref, qs_ref, cos_ref, sin_ref)
        qb, kb = qr.astype(BF16), kr.astype(BF16)
        lane_head = lax.broadcasted_iota(jnp.int32, (1, qb.shape[1]), 1) // qk
        inter = jnp.dot((qr * decqf_ref[...]).astype(BF16), st_ref[...].astype(BF16), preferred_element_type=F32)
        inter = inter + jnp.dot((qr * decqb_ref[...]).astype(BF16), sbd_ref[...], preferred_element_type=F32)
        heads = range(RET_HEADS)
        vcols = [slice(h * dv, (h + 1) * dv) for h in heads]
        qhs = [jnp.where(lane_head == h, qb, jnp.zeros_like(qb)) for h in heads]
        scs = [lax.dot_general(qh, kb, (((1,), (1,)), ((), ())), preferred_element_type=F32) * intra_ref[h]
               for h, qh in zip(heads, qhs)]
        ohs = [jnp.dot(sc.astype(BF16), vb[:, vs], preferred_element_type=F32) + inter[:, vs]
               for sc, vs in zip(scs, vcols)]
        mus = [jnp.mean(oh, axis=-1, keepdims=True) for oh in ohs]
        dds = [oh - mu for oh, mu in zip(ohs, mus)]
        vrs = [jnp.mean(dd * dd, axis=-1, keepdims=True) for dd in dds]
        on = jnp.concatenate([dd * lax.rsqrt(var + LN_EPS) for dd, var in zip(dds, vrs)], axis=-1)
        gg = g_ref[...].astype(F32)
        o_ref[...] = (on * (gg * _sigmoid(gg))).astype(BF16)
        kwt = jnp.transpose(kr * tailf_ref[...]).astype(BF16)
        kv = jnp.dot(kwt, vb, preferred_element_type=F32)
        st_ref[...] = st_ref[...] * chunkf_ref[...] + kv * bd_ref[...]


def _retention_call(proj, tabs, b, s):
    c = RET_CHUNK
    nc = s // c
    qkw, vw = RET_HEADS * RET_QK_DIM, RET_HEADS * RET_V_DIM
    pv = proj.reshape(b, s, NP)
    qk_blk, v_blk = (None, c, qkw), (None, c, vw)
    qo, qso, ko, kso = (_SEG[n] * LANE // qkw for n in ("rq", "rqs", "rk", "rks"))
    vo, go = (_SEG[n] * LANE // vw for n in ("rv", "rg"))
    const2 = lambda shape: pl.BlockSpec(shape, lambda bi, i: (0, 0))
    chunk = lambda i: jnp.where(i < nc, nc - 1 - i, i - nc)
    fchunk = lambda i: jnp.maximum(i - nc, 0)
    both = lambda off: (lambda bi, i: (bi, chunk(i), off))
    fonly = lambda off: (lambda bi, i: (bi, fchunk(i), off))
    yc = pl.pallas_call(
        functools.partial(_ret_kernel, nc=nc), name="retention",
        grid=(b, 2 * nc),
        in_specs=[pl.BlockSpec(qk_blk, fonly(qo)), pl.BlockSpec(qk_blk, fonly(qso)),
                  pl.BlockSpec(qk_blk, both(ko)), pl.BlockSpec(qk_blk, both(kso)),
                  pl.BlockSpec(v_blk, both(vo)), pl.BlockSpec(v_blk, fonly(go)),
                  pl.BlockSpec((c, qkw), lambda bi, i: (chunk(i), 0)),
                  pl.BlockSpec((c, qkw), lambda bi, i: (chunk(i), 0)),
                  pl.BlockSpec((RET_HEADS, c, c), lambda bi, i: (0, 0, 0)),
                  const2((c, qkw)), const2((c, qkw)), const2((c, qkw)), const2((c, qkw)),
                  const2((1, vw)), const2((1, vw)), const2((qkw, vw))],
        out_specs=pl.BlockSpec((None, c, vw), lambda bi, i: (bi, fchunk(i), 0)),
        out_shape=jax.ShapeDtypeStruct((b, s, vw), BF16),
        scratch_shapes=[pltpu.VMEM((qkw, vw), F32), pltpu.VMEM((qkw, vw), BF16),
                        pltpu.VMEM((nc, RET_HEADS, RET_QK_DIM, RET_V_DIM), BF16)],
        compiler_params=_cparams(("parallel", "arbitrary")),
    )(pv, pv, pv, pv, pv, pv, tabs["cos"], tabs["sin"], tabs["intra"], tabs["tail_f"], tabs["head_b"],
      tabs["decq_f"], tabs["decq_b"], tabs["chunk_f"], tabs["chunk_b"], tabs["bd"])
    return yc.reshape(b * s, vw)


def _mm_kernel(x_ref, w_ref, o_ref):
    o_ref[...] = jnp.dot(x_ref[...].astype(BF16), w_ref[...], preferred_element_type=F32).astype(o_ref.dtype)


def _mem_kv_call(mem2d, w_kv):
    m, n = mem2d.shape[0], w_kv.shape[1]
    tm = 512
    return pl.pallas_call(
        _mm_kernel, name="mem_kv_proj",
        grid=(m // tm,),
        in_specs=[pl.BlockSpec((tm, D_MODEL), lambda i: (i, 0)),
                  pl.BlockSpec((D_MODEL, n), lambda i: (0, 0))],
        out_specs=pl.BlockSpec((tm, n), lambda i: (i, 0)),
        out_shape=jax.ShapeDtypeStruct((m, n), BF16),
        compiler_params=_cparams(("parallel",)),
    )(mem2d, w_kv)


def _mem_attn_kernel(q_ref, k_ref, v_ref, o_ref):
    q = q_ref[...]
    cols = [slice(h * MEM_HEAD_PAD, (h + 1) * MEM_HEAD_PAD) for h in range(MEM_HEADS)]
    ss = [lax.dot_general(q[:, hs], k_ref[:, hs], (((1,), (1,)), ((), ())),
                          preferred_element_type=F32) * (MEM_HEAD_DIM ** -0.5) for hs in cols]
    ms = [jnp.max(s, axis=-1, keepdims=True) for s in ss]
    es = [jnp.exp(s - m) for s, m in zip(ss, ms)]
    ps = [e / jnp.sum(e, axis=-1, keepdims=True) for e in es]
    outs = [jnp.dot(p.astype(BF16), v_ref[:, hs], preferred_element_type=F32) for p, hs in zip(ps, cols)]
    o_ref[...] = jnp.concatenate(outs, axis=-1).astype(BF16)


def _mem_attn_call(proj, kv, b, s):
    tq = 256
    w = MEM_HEADS * MEM_HEAD_PAD
    pv = proj.reshape(b, s, NP)
    kvv = kv.reshape(b, N_MEM, 2 * w)
    qo = _SEG["mq"] * LANE // w
    yd = pl.pallas_call(
        _mem_attn_kernel, name="mem_attn",
        grid=(b, s // tq),
        in_specs=[pl.BlockSpec((None, tq, w), lambda bi, i: (bi, i, qo)),
                  pl.BlockSpec((None, N_MEM, w), lambda bi, i: (bi, 0, 0)),
                  pl.BlockSpec((None, N_MEM, w), lambda bi, i: (bi, 0, 1))],
        out_specs=pl.BlockSpec((None, tq, w), lambda bi, i: (bi, i, 0)),
        out_shape=jax.ShapeDtypeStruct((b, s, w), BF16),
        compiler_params=_cparams(("parallel", "parallel")),
    )(pv, kvv, kvv)
    return yd.reshape(b * s, w)


def _merge_kernel(x_ref, ao0_ref, ao1_ref, ao2_ref, al0_ref, al1_ref, al2_ref, zt_ref, yc_ref, yd_ref, gl_ref,
                  wglut_ref, bglu_ref, wba_ref, wbb_ref, wbc_ref, wbd_ref, wout_ref, g1_ref, b1_ref,
                  x1_ref):
    tm = x_ref.shape[0]
    slabs = lambda ref: jnp.concatenate([ref[c] for c in range(ref.shape[0])], axis=-1)
    lses = [al0_ref[...], slabs(al1_ref), slabs(al2_ref)]
    aos = [ao0_ref[...], slabs(ao1_ref), slabs(ao2_ref)]
    mx = jnp.maximum(jnp.maximum(lses[0], lses[1]), lses[2])
    es = [jnp.exp(l - mx) for l in lses]
    tot = es[0] + es[1] + es[2]
    ya = jnp.concatenate([o * (e / tot) for o, e in zip(aos, es)], axis=-1).astype(BF16)
    zt = zt_ref[...]
    glu = jnp.dot(wglut_ref[...], zt, preferred_element_type=F32) + bglu_ref[...]
    ybt = zt.astype(F32) * _sigmoid(glu)
    yb = jnp.transpose(ybt).astype(BF16)
    merged = jnp.zeros((tm, D_MODEL), F32)
    for n, (y, w_ref) in enumerate(((ya, wba_ref), (yb, wbb_ref), (yc_ref[...], wbc_ref), (yd_ref[...], wbd_ref))):
        bo = jnp.dot(y, w_ref[...], preferred_element_type=F32)
        merged = merged + (jnp.tanh(gl_ref[:, n * D_MODEL:(n + 1) * D_MODEL].astype(F32)) + 1.0) * bo
    h = DEEPNORM_ALPHA * x_ref[...] + jnp.dot(merged.astype(BF16), wout_ref[...], preferred_element_type=F32)
    x1_ref[...] = _ln(h, g1_ref[...], b1_ref[...])


def _router_kernel(x1_ref, rwh_ref, rwl_ref, rb_ref, tri_ref, x1p_ref, route_ref, counts_ref, carry_ref):
    tm = x1_ref.shape[0]

    @pl.when(pl.program_id(0) == 0)
    def _():
        carry_ref[...] = jnp.zeros_like(carry_ref)

    x1 = x1_ref[...]
    x1p_ref[...] = _pack_bf16_pairs(x1)
    x_hi = x1.astype(BF16)
    x_lo = (x1 - x_hi.astype(F32)).astype(BF16)
    logits = (jnp.dot(x_hi, rwh_ref[...], preferred_element_type=F32)
              + jnp.dot(x_lo, rwh_ref[...], preferred_element_type=F32)
              + jnp.dot(x_hi, rwl_ref[...], preferred_element_type=F32)) + rb_ref[...]
    lane = lax.broadcasted_iota(jnp.int32, (tm, LANE), 1)
    work = logits
    sel, vals, hots = [], [], []
    for _ in range(TOP_K):
        mk = jnp.max(work, axis=-1, keepdims=True)
        ik = jnp.min(jnp.where(work == mk, lane, LANE), axis=-1, keepdims=True)
        hot = lane == ik
        work = jnp.where(hot, -jnp.inf, work)
        sel.append(ik)
        vals.append(mk)
        hots.append(hot)
    ex = [jnp.exp(v - vals[0]) for v in vals]
    den = ex[0] + ex[1] + ex[2] + ex[3]
    chosen = (hots[0] | hots[1] | hots[2] | hots[3]).astype(F32)
    before = jnp.dot(tri_ref[...], chosen.astype(BF16), preferred_element_type=F32) + carry_ref[0:1, :]
    route = jnp.zeros((tm, LANE), F32)
    for k in range(TOP_K):
        rank = jnp.sum(jnp.where(hots[k], before, 0.0), axis=-1, keepdims=True)
        route = jnp.where(lane == k, sel[k].astype(F32), route)
        route = jnp.where(lane == TOP_K + k, ex[k] / den, route)
        route = jnp.where(lane == 2 * TOP_K + k, rank, route)
    route_ref[...] = route
    new_carry = carry_ref[0:1, :] + jnp.sum(chosen, axis=0, keepdims=True)
    carry_ref[...] = jnp.broadcast_to(new_carry, carry_ref.shape)
    counts_ref[...] = jnp.broadcast_to(new_carry, counts_ref.shape)


def _merge_call(x, aos, alses, zt, yc, yd, proj, lw):
    t = x.shape[0]
    tm = 256
    gw = GW
    gate_blk = 4 * D_MODEL
    row = lambda w: pl.BlockSpec((tm, w), lambda i: (i, 0))
    slab = pl.BlockSpec((gw // LANE, tm, LANE), lambda i: (0, i, 0))
    full = lambda a: pl.BlockSpec(a.shape, lambda i: (0,) * a.ndim)
    consts = (lw["w_glu_t"], lw["b_glu"], lw["wb_a"], lw["wb_b"], lw["wb_c"], lw["wb_d"], lw["w_out"],
              lw["ln1_g"], lw["ln1_b"])
    return pl.pallas_call(
        _merge_kernel, name="branch_merge",
        grid=(t // tm,),
        in_specs=[row(D_MODEL)] + [row(gw), slab, slab] * 2
                 + [pl.BlockSpec((S5_WIDTH, tm), lambda i: (0, i)),
                  row(RET_HEADS * RET_V_DIM), row(MEM_HEADS * MEM_HEAD_PAD),
                  pl.BlockSpec((tm, gate_blk), lambda i: (i, _SEG["gate"] * LANE // gate_blk))]
                 + [full(a) for a in consts],
        out_specs=row(D_MODEL),
        out_shape=jax.ShapeDtypeStruct((t, D_MODEL), F32),
        compiler_params=_cparams(("parallel",)),
    )(x, *aos, *alses, zt, yc, yd, proj, *consts)


def _router_call(x1, lw):
    t = x1.shape[0]
    tm = 1024
    row = lambda w: pl.BlockSpec((tm, w), lambda i: (i, 0))
    full = lambda a: pl.BlockSpec(a.shape, lambda i: (0,) * a.ndim)
    tri = jnp.tril(jnp.ones((tm, tm), BF16), -1)
    consts = (lw["router_w_hi"], lw["router_w_lo"], lw["router_b"], tri)
    return pl.pallas_call(
        _router_kernel, name="moe_router",
        grid=(t // tm,),
        in_specs=[row(D_MODEL)] + [full(a) for a in consts],
        out_specs=[row(D_MODEL // 2), row(LANE), pl.BlockSpec((8, LANE), lambda i: (0, 0))],
        out_shape=[jax.ShapeDtypeStruct((t, D_MODEL // 2), jnp.uint32),
                   jax.ShapeDtypeStruct((t, LANE), F32),
                   jax.ShapeDtypeStruct((8, LANE), F32)],
        scratch_shapes=[pltpu.VMEM((8, LANE), F32)],
        compiler_params=_cparams(("arbitrary",)),
    )(x1, *consts)


def _pack_bf16_pairs(x):
    n = x.shape[1] // 2
    hi = lax.bitcast_convert_type(x[:, :n].astype(BF16).astype(F32), jnp.uint32)
    lo = lax.bitcast_convert_type(x[:, n:].astype(BF16).astype(F32), jnp.uint32)
    return hi | (lo >> 16)


def _unpack_bf16_pairs(w):
    hi = lax.bitcast_convert_type(w & jnp.uint32(0xFFFF0000), F32)
    lo = lax.bitcast_convert_type(w << 16, F32)
    return jnp.concatenate([hi, lo], axis=-1)


def _sc_index_rows(idx):
    window = SC_GATHER_WINDOW
    return jnp.pad(idx.reshape(-1, window), ((0, 0), (0, LANE - window)))


def _sc_scatter_rows(x, idx, n_out):
    n, d = x.shape
    m = idx.shape[0]
    window = SC_GATHER_WINDOW
    src_steps = n // window
    mesh = plsc.VectorSubcoreMesh(core_axis_name="core", subcore_axis_name="subcore")

    @pl.kernel(out_type=jax.ShapeDtypeStruct((n_out, d), x.dtype), mesh=mesh, scratch_types=[])
    def scatter_kernel(x_hbm, i_hbm, o_hbm):
        def body(x_vmem, i_vmem):
            pltpu.sync_copy(x_vmem, o_hbm.at[i_vmem.at[0, pl.ds(0, window)]])

        pltpu.emit_pipeline(
            body,
            grid=(m // window,),
            in_specs=[pl.BlockSpec((window, d), index_map=lambda i: (i % src_steps, 0)),
                      pl.BlockSpec((1, LANE), index_map=lambda i: (i, 0))],
            out_specs=[],
            core_axis_name=("core", "subcore"),
            dimension_semantics=(pltpu.PARALLEL,),
            trace_scopes=False,
        )(x_hbm, i_hbm)

    return scatter_kernel(x, _sc_index_rows(idx))


def _sc_gather_rows(x, idx):
    m, d = idx.shape[0], x.shape[1]
    window = SC_GATHER_WINDOW
    mesh = plsc.VectorSubcoreMesh(core_axis_name="core", subcore_axis_name="subcore")

    @pl.kernel(out_type=jax.ShapeDtypeStruct((m, d), x.dtype), mesh=mesh, scratch_types=[])
    def gather_kernel(x_hbm, i_hbm, o_hbm):
        def body(i_vmem, o_vmem):
            pltpu.sync_copy(x_hbm.at[i_vmem.at[0, pl.ds(0, window)]], o_vmem)

        pltpu.emit_pipeline(
            body,
            grid=(m // window,),
            in_specs=[pl.BlockSpec((1, LANE), index_map=lambda i: (i, 0))],
            out_specs=[pl.BlockSpec((window, d), index_map=lambda i: (i, 0))],
            core_axis_name=("core", "subcore"),
            dimension_semantics=(pltpu.PARALLEL,),
            trace_scopes=False,
        )(i_hbm, o_hbm)

    return gather_kernel(x, _sc_index_rows(idx))


def _moe_expert_kernel(be_ref, nu_ref, nv_ref, x_ref, w1_ref, b1_ref, w2_ref, b2_ref, o_ref, w1s_ref, w2s_ref):
    i = pl.program_id(0)

    @pl.when((i == 0) | (be_ref[i] != be_ref[jnp.maximum(i - 1, 0)]))
    def _():
        w1s_ref[...] = w1_ref[...].astype(BF16)
        w2s_ref[...] = w2_ref[...].astype(BF16)

    @pl.when(i < nu_ref[0])
    def _():
        rows = lax.broadcasted_iota(jnp.int32, (x_ref.shape[0], 1), 0)
        x = jnp.where(rows < nv_ref[i], _unpack_bf16_pairs(x_ref[...]), 0.0)
        h = jnp.dot(x.astype(BF16), w1s_ref[...], preferred_element_type=F32) + b1_ref[...]
        gate = jnp.minimum(h[:, :D_FF], SWIGLU_LIMIT)
        up = jnp.clip(h[:, D_FF:], -SWIGLU_LIMIT, SWIGLU_LIMIT)
        hid = gate * _sigmoid(SWIGLU_ALPHA * gate) * (up + 1.0)
        y = jnp.dot(hid.astype(BF16), w2s_ref[...], preferred_element_type=F32) + b2_ref[...]
        o_ref[...] = _pack_bf16_pairs(y)

    @pl.when(i >= nu_ref[0])
    def _():
        o_ref[...] = jnp.zeros_like(o_ref)


def _moe_expert_call(xg, block_expert, n_used, n_valid, layer, w1, b1, w2, b2):
    bm = MOE_BLOCK_ROWS
    n_blocks = xg.shape[0] // bm
    half = D_MODEL // 2
    grid_spec = pltpu.PrefetchScalarGridSpec(
        num_scalar_prefetch=3, grid=(n_blocks,),
        in_specs=[pl.BlockSpec((bm, half), lambda i, be, nu, nv: (i, 0)),
                  pl.BlockSpec((None, None, D_MODEL, 2 * D_FF), lambda i, be, nu, nv: (layer, be[i], 0, 0)),
                  pl.BlockSpec((None, 1, 2 * D_FF), lambda i, be, nu, nv: (be[i], 0, 0)),
                  pl.BlockSpec((None, None, D_FF, D_MODEL), lambda i, be, nu, nv: (layer, be[i], 0, 0)),
                  pl.BlockSpec((None, 1, D_MODEL), lambda i, be, nu, nv: (be[i], 0, 0))],
        out_specs=pl.BlockSpec((bm, half), lambda i, be, nu, nv: (i, 0)),
        scratch_shapes=[pltpu.VMEM((D_MODEL, 2 * D_FF), BF16), pltpu.VMEM((D_FF, D_MODEL), BF16)])
    return pl.pallas_call(
        _moe_expert_kernel, name="moe_experts",
        grid_spec=grid_spec,
        out_shape=jax.ShapeDtypeStruct((n_blocks * bm, half), jnp.uint32),
        compiler_params=_cparams(("arbitrary",)),
    )(block_expert, n_used, n_valid, xg, w1, b1, w2, b2)


def _moe_combine_kernel(x_ref, route_ref, y0_ref, y1_ref, y2_ref, y3_ref, g_ref, b_ref, o_ref):
    route = route_ref[...]
    moe = jnp.zeros(x_ref.shape, F32)
    for k, y_ref in enumerate((y0_ref, y1_ref, y2_ref, y3_ref)):
        moe = moe + route[:, TOP_K + k:TOP_K + k + 1] * _unpack_bf16_pairs(y_ref[...])
    o_ref[...] = _ln(DEEPNORM_ALPHA * x_ref[...] + moe, g_ref[...], b_ref[...])


def _moe_combine_call(x1, route, yg, g2, b2):
    t = x1.shape[0]
    tm = 512
    nt = t // tm
    ysel = lambda k: pl.BlockSpec((tm, D_MODEL // 2), lambda i: (k * nt + i, 0))
    return pl.pallas_call(
        _moe_combine_kernel, name="moe_combine_ln",
        grid=(nt,),
        in_specs=[pl.BlockSpec((tm, D_MODEL), lambda i: (i, 0)),
                  pl.BlockSpec((tm, LANE), lambda i: (i, 0))]
                 + [ysel(k) for k in range(TOP_K)]
                 + [pl.BlockSpec((1, D_MODEL), lambda i: (0, 0)),
                    pl.BlockSpec((1, D_MODEL), lambda i: (0, 0))],
        out_specs=pl.BlockSpec((tm, D_MODEL), lambda i: (i, 0)),
        out_shape=jax.ShapeDtypeStruct((t, D_MODEL), F32),
        compiler_params=_cparams(("parallel",)),
    )(x1, route, yg, yg, yg, yg, g2, b2)


def _moe(x1, x1p, route, counts, lw):
    t = x1.shape[0]
    bm = MOE_BLOCK_ROWS
    n_assign = t * TOP_K
    n_blocks = n_assign // bm + N_EXPERTS
    idx = route[:, :TOP_K].astype(jnp.int32)
    rank = route[:, 2 * TOP_K:3 * TOP_K].astype(jnp.int32)
    cnt = counts[0, :N_EXPERTS].astype(jnp.int32)
    padded = (cnt + bm - 1) // bm * bm
    pad_ends = jnp.cumsum(padded)
    pad_starts = pad_ends - padded
    dest = jnp.transpose(jnp.take(pad_starts, idx) + rank).reshape(-1)
    block_start = jnp.arange(n_blocks, dtype=jnp.int32) * bm
    block_expert = jnp.minimum(jnp.sum(pad_ends[None, :] <= block_start[:, None], axis=1),
                               N_EXPERTS - 1).astype(jnp.int32)
    n_valid = jnp.clip(jnp.take(cnt, block_expert) - (block_start - jnp.take(pad_starts, block_expert)), 0, bm)
    n_used = (pad_ends[-1:] // bm).astype(jnp.int32)
    xg = _sc_scatter_rows(x1p, dest, n_blocks * bm)
    yb = _moe_expert_call(xg, block_expert, n_used, n_valid.astype(jnp.int32), lw["layer"],
                          lw["moe_w1"], lw["moe_b1"], lw["moe_w2"], lw["moe_b2"])
    yg = _sc_gather_rows(yb, dest)
    return _moe_combine_call(x1, route, yg, lw["ln2_g"], lw["ln2_b"])


def _rotate_half_cols(w):
    half = RET_QK_DIM // 2
    w4 = w.reshape(w.shape[0], RET_HEADS, 2, half)
    return jnp.concatenate([-w4[:, :, 1:], w4[:, :, :1]], axis=2).reshape(w.shape)


def _pad_heads(w, axis):
    shp = list(w.shape)
    shp[axis:axis + 1] = [MEM_HEADS, MEM_HEAD_DIM]
    w = w.reshape(shp)
    pad = [(0, 0)] * w.ndim
    pad[axis + 1] = (0, MEM_HEAD_PAD - MEM_HEAD_DIM)
    w = jnp.pad(w, pad)
    shp[axis:axis + 2] = [MEM_HEADS * MEM_HEAD_PAD]
    return w.reshape(shp)


def _layer_weights(l, p):
    w_in = p["w_in"][l]
    aw, sw, qkw, vw, mw = 768, S5_WIDTH, RET_HEADS * RET_QK_DIM, RET_HEADS * RET_V_DIM, 768
    offs = [0]
    for wdt in (aw, aw, aw, sw, qkw, qkw, vw, vw, mw, 4 * D_MODEL):
        offs.append(offs[-1] + wdt)
    aq, ak, av, su, rq, rk, rv, rg, mq, gate = (w_in[:, offs[i]:offs[i + 1]] for i in range(10))
    seg = dict(gate=0.5 * gate, mq=_pad_heads(mq, 1), aq=aq[:, :GW], ak=ak[:, :GW], av=av[:, :GW],
               pad=jnp.zeros((D_MODEL, 2 * LANE), F32), rv=rv, rg=rg,
               rq=rq, rqs=_rotate_half_cols(rq), rk=rk, rks=_rotate_half_cols(rk))
    order = sorted(_SEG, key=_SEG.get)
    w_p = jnp.concatenate([seg[n] for n in order], axis=1).astype(BF16)
    assert w_p.shape[1] == NP
    w_d = jnp.concatenate([w[:, g * GW:(g + 1) * GW] for g in (1, 2) for w in (aq, ak, av)], axis=1).astype(BF16)
    wkv = p["w_mem_kv"][l]
    w_kv = jnp.concatenate([_pad_heads(wkv[:, :768], 1), _pad_heads(wkv[:, 768:], 1)], axis=1).astype(BF16)
    wb = p["w_branch"][l]
    rw = jnp.pad(p["router_w"][l].astype(F32), ((0, 0), (0, LANE - N_EXPERTS)))
    rb = jnp.pad(p["router_b"][l].astype(F32), (0, LANE - N_EXPERTS), constant_values=NEG_INF)
    return dict(
        w_p=w_p, w_d=w_d, w_su_t=jnp.transpose(su).astype(BF16), w_kv=w_kv,
        w_glu_t=jnp.transpose(p["s5_w_glu"][l]).astype(BF16), b_glu=p["s5_b_glu"][l].astype(F32).reshape(-1, 1),
        wb_a=wb[0].astype(BF16), wb_b=wb[1].astype(BF16), wb_c=wb[2].astype(BF16),
        wb_d=_pad_heads(wb[3], 0).astype(BF16), w_out=(0.5 * p["w_out"][l]).astype(BF16),
        ln1_g=p["ln1_g"][l].reshape(1, -1), ln1_b=p["ln1_b"][l].reshape(1, -1),
        router_w_hi=rw.astype(BF16), router_w_lo=(rw - rw.astype(BF16).astype(F32)).astype(BF16),
        router_b=rb.reshape(1, -1),
        layer=l, moe_w1=p["moe_w1"], moe_b1=p["moe_b1"][l].astype(F32).reshape(N_EXPERTS, 1, -1),
        moe_w2=p["moe_w2"], moe_b2=p["moe_b2"][l].astype(F32).reshape(N_EXPERTS, 1, -1),
        ln2_g=p["ln2_g"][l].reshape(1, -1), ln2_b=p["ln2_b"][l].reshape(1, -1),
        s5=_s5_tables(p["s5_lam_re"][l], p["s5_lam_im"][l], p["s5_log_step"][l], p["s5_b_re"][l],
                      p["s5_b_im"][l], p["s5_c_re"][l], p["s5_c_im"][l], p["s5_d"][l]))


def _trunk_layer(x, mem2d, b, s, lw, attn_bias, ret_tabs):
    proj, ut, pd = _proj_call(x, lw["w_p"], lw["w_su_t"], lw["w_d"])
    outs = [_attn_call(proj, attn_bias[0], b, s)]
    outs += [_dil_attn_call(pd, attn_bias[g], b, s, g, A_GROUPS[g][1]) for g in (1, 2)]
    aos, alses = [o for o, _ in outs], [l for _, l in outs]
    toep, win, wout, a_re, a_im = lw["s5"]
    nc = s // S5_CHUNK
    c_re, c_im = _s5_contrib_call(ut, win, b, nc)
    s_re, s_im = _s5_scan_call(c_re, c_im, a_re, a_im)
    zt = _s5_out_call(ut, toep, s_re, s_im, wout, b, nc)
    yc = _retention_call(proj, ret_tabs, b, s)
    kv = _mem_kv_call(mem2d, lw["w_kv"])
    yd = _mem_attn_call(proj, kv, b, s)
    x1 = _merge_call(x, aos, alses, zt, yc, yd, proj, lw)
    x1p, route, counts = _router_call(x1, lw)
    return _moe(x1, x1p, route, counts, lw)


def kernel(x_prompt, x_sample, mem_prompt, mem_sample, ln_in_g, ln_in_b, rel_bias, w_in, s5_lam_re, s5_lam_im, s5_log_step, s5_b_re, s5_b_im, s5_c_re, s5_c_im, s5_d, s5_w_glu, s5_b_glu, w_mem_kv, w_branch, w_out, ln1_g, ln1_b, router_w, router_b, moe_w1, moe_b1, moe_w2, moe_b2, ln2_g, ln2_b):
    p = dict(w_in=w_in, s5_lam_re=s5_lam_re, s5_lam_im=s5_lam_im, s5_log_step=s5_log_step, s5_b_re=s5_b_re,
             s5_b_im=s5_b_im, s5_c_re=s5_c_re, s5_c_im=s5_c_im, s5_d=s5_d, s5_w_glu=s5_w_glu, s5_b_glu=s5_b_glu,
             w_mem_kv=w_mem_kv, w_branch=w_branch, w_out=w_out, ln1_g=ln1_g, ln1_b=ln1_b, router_w=router_w,
             router_b=router_b, moe_w1=moe_w1, moe_b1=moe_b1, moe_w2=moe_w2, moe_b2=moe_b2, ln2_g=ln2_g,
             ln2_b=ln2_b)
    attn_bias = [_attn_bias_table(rel_bias, g, dil) for g, (_, dil) in enumerate(A_GROUPS)]
    trunks = []
    for x, mem in ((x_prompt, mem_prompt), (x_sample, mem_sample)):
        b, s, _ = x.shape
        trunks.append(dict(x=_layer_norm_call(x.reshape(b * s, D_MODEL), ln_in_g, ln_in_b),
                           mem=mem.reshape(b * N_MEM, D_MODEL), b=b, s=s, tabs=_ret_tables(s)))
    for l in range(DEPTH):
        lw = _layer_weights(l, p)
        for tr in trunks:
            tr["x"] = _trunk_layer(tr["x"], tr["mem"], tr["b"], tr["s"], lw, attn_bias, tr["tabs"])
    return tuple(tr["x"].reshape(tr["b"], tr["s"], D_MODEL) for tr in trunks)
```

```python
import functools
import math

import jax
import jax.numpy as jnp
from jax import lax
from jax.experimental import pallas as pl
from jax.experimental.pallas import tpu as pltpu
from jax.experimental.pallas import tpu_sc as plsc

F32 = jnp.float32
BF16 = jnp.bfloat16
HIGHEST = lax.Precision.HIGHEST

D_MODEL = 1024
DEPTH = 4
N_MEM = 256
A_GROUPS = ((128, 1), (512, 4), (2048, 16))
A_HEADS_PER_GROUP = 4
A_HEADS = 12
A_HEAD_DIM = 64
A_RADIUS = 64
REL_BUCKETS = 32
REL_MAX_DIST = 1024
S5_GROUP = 16
S5_WIDTH = 768
S5_GROUPS = 48
S5_STATE = 64
RET_HEADS = 6
RET_QK_DIM = 64
RET_V_DIM = 128
RET_CHUNK = 128
ROPE_BASE = 10000.0
MEM_HEADS = 4
MEM_HEAD_DIM = 192
MEM_HEAD_PAD = 256
N_EXPERTS = 32
TOP_K = 4
D_FF = 1024
SWIGLU_LIMIT = 7.0
SWIGLU_ALPHA = 1.702
LN_EPS = 1e-5
DEEPNORM_ALPHA = (2 * DEPTH) ** 0.25
NEG_INF = -1e30

LANE = 128
VMEM_LIMIT = 56 * 1024 * 1024
S5_CHUNK = 128
MOE_BLOCK_ROWS = 512
SC_GATHER_WINDOW = 64

_SEG = {}
_off = 0
for _name, _width, _align in (
        ("gate", 32, 8), ("mq", 8, 8), ("aq", 2, 2), ("ak", 2, 2), ("av", 2, 2), ("pad", 2, 2),
        ("rv", 6, 6), ("rg", 6, 6), ("rq", 3, 3), ("rqs", 3, 3), ("rk", 3, 3), ("rks", 3, 3)):
    assert _off % _align == 0, (_name, _off)
    _SEG[_name] = _off
    _off += _width
NP_UNITS = _off
NP = NP_UNITS * LANE
assert NP % 256 == 0
GW = A_HEADS_PER_GROUP * A_HEAD_DIM
ND = 2 * 3 * GW


def _cparams(sem, vmem=VMEM_LIMIT):
    return pltpu.CompilerParams(dimension_semantics=sem, vmem_limit_bytes=vmem)


def _sigmoid(x):
    return 0.5 * jnp.tanh(0.5 * x) + 0.5


def _ln(h, g, b):
    mu = jnp.mean(h, axis=-1, keepdims=True)
    d = h - mu
    var = jnp.mean(d * d, axis=-1, keepdims=True)
    return d * lax.rsqrt(var + LN_EPS) * g + b


def _ln_kernel(x_ref, g_ref, b_ref, o_ref):
    o_ref[...] = _ln(x_ref[...], g_ref[...], b_ref[...])


def _layer_norm_call(x, g, b):
    t = x.shape[0]
    tm = 512
    return pl.pallas_call(
        _ln_kernel, name="input_ln",
        grid=(t // tm,),
        in_specs=[pl.BlockSpec((tm, D_MODEL), lambda i: (i, 0)),
                  pl.BlockSpec((1, D_MODEL), lambda i: (0, 0)),
                  pl.BlockSpec((1, D_MODEL), lambda i: (0, 0))],
        out_specs=pl.BlockSpec((tm, D_MODEL), lambda i: (i, 0)),
        out_shape=jax.ShapeDtypeStruct((t, D_MODEL), F32),
        compiler_params=_cparams(("parallel",)),
    )(x, g.reshape(1, -1), b.reshape(1, -1))


def _proj_kernel(x_ref, w_ref, wsut_ref, wd_ref, p_ref, ut_ref, pd_ref, xb_ref):
    @pl.when(pl.program_id(1) == 0)
    def _():
        xb = x_ref[...].astype(BF16)
        xb_ref[...] = xb
        ut_ref[...] = lax.dot_general(wsut_ref[...], xb, (((1,), (1,)), ((), ())),
                                      preferred_element_type=F32).astype(BF16)
        pd = jnp.dot(xb, wd_ref[...], preferred_element_type=F32)
        for c in range(ND // LANE):
            pd_ref[c] = pd[:, c * LANE:(c + 1) * LANE]

    p_ref[...] = jnp.dot(xb_ref[...], w_ref[...], preferred_element_type=F32).astype(BF16)


def _proj_call(x, w_p, w_su_t, w_d):
    t = x.shape[0]
    tm, tn = 512, NP // 3
    return pl.pallas_call(
        _proj_kernel, name="in_proj",
        grid=(t // tm, NP // tn),
        in_specs=[pl.BlockSpec((tm, D_MODEL), lambda i, j: (i, 0)),
                  pl.BlockSpec((D_MODEL, tn), lambda i, j: (0, j)),
                  pl.BlockSpec((S5_WIDTH, D_MODEL), lambda i, j: (0, 0)),
                  pl.BlockSpec((D_MODEL, ND), lambda i, j: (0, 0))],
        out_specs=[pl.BlockSpec((tm, tn), lambda i, j: (i, j)),
                   pl.BlockSpec((S5_WIDTH, tm), lambda i, j: (0, i)),
                   pl.BlockSpec((ND // LANE, tm, LANE), lambda i, j: (0, i, 0))],
        out_shape=[jax.ShapeDtypeStruct((t, NP), BF16),
                   jax.ShapeDtypeStruct((S5_WIDTH, t), BF16),
                   jax.ShapeDtypeStruct((ND // LANE, t, LANE), F32)],
        scratch_shapes=[pltpu.VMEM((tm, D_MODEL), BF16)],
        compiler_params=_cparams(("parallel", "arbitrary")),
    )(x, w_p, w_su_t, w_d)


ATTN_TQ = 2 * A_RADIUS
ATTN_PAIR = 4


def _banded_heads(blocks, bias_ref, sub_len):
    tq, nk = blocks[0][0].shape[0], blocks[0][1].shape[0]
    width = blocks[0][0].shape[1]
    col = lax.broadcasted_iota(jnp.int32, (tq, nk), 1)
    valids = [(blk[3] + col >= 0) & (blk[3] + col < sub_len) for blk in blocks]
    lane_head = lax.broadcasted_iota(jnp.int32, (1, width), 1) // A_HEAD_DIM
    scale = jnp.asarray(A_HEAD_DIM ** -0.5, BF16)
    pairs = [(bk, h) for bk in range(len(blocks)) for h in range(A_HEADS_PER_GROUP)]
    qs = [jnp.where(lane_head == h, blocks[bk][0], jnp.zeros_like(blocks[bk][0])) * scale for bk, h in pairs]
    ss = [lax.dot_general(qh, blocks[bk][1], (((1,), (1,)), ((), ())), preferred_element_type=F32)
          for qh, (bk, h) in zip(qs, pairs)]
    ss = [jnp.where(valids[bk], s + bias_ref[h], NEG_INF) for s, (bk, h) in zip(ss, pairs)]
    ms = [jnp.max(s, axis=-1, keepdims=True) for s in ss]
    ps = [jnp.exp(s - m) for s, m in zip(ss, ms)]
    dens = [jnp.sum(p, axis=-1, keepdims=True) for p in ps]
    ohs = [jnp.dot(p.astype(BF16), blocks[bk][2], preferred_element_type=F32) for p, (bk, h) in zip(ps, pairs)]
    outs = []
    for bk in range(len(blocks)):
        acc = jnp.zeros((tq, width), F32)
        lse_acc = jnp.zeros((tq, width), F32)
        for j, (pb, h) in enumerate(pairs):
            if pb == bk:
                hm = lane_head == h
                acc = jnp.where(hm, ohs[j] / dens[j], acc)
                lse_acc = jnp.where(hm, ms[j] + jnp.log(dens[j]), lse_acc)
        outs.append((acc, lse_acc))
    return outs


def _attn_kernel(q_ref, kp_ref, kc_ref, kn_ref, vp_ref, vc_ref, vn_ref, bias_ref, o_ref, lse_ref, *, sub_len):
    tq = ATTN_TQ
    kcat = jnp.concatenate([kp_ref[...], kc_ref[...], kn_ref[...]], axis=0)
    vcat = jnp.concatenate([vp_ref[...], vc_ref[...], vn_ref[...]], axis=0)
    first = pl.program_id(1) * (ATTN_PAIR * tq) - A_RADIUS
    blocks = [(q_ref[j * tq:(j + 1) * tq, :], kcat[j * tq:j * tq + 2 * tq, :], vcat[j * tq:j * tq + 2 * tq, :],
               first + j * tq) for j in range(ATTN_PAIR)]
    for j, (acc, lse) in enumerate(_banded_heads(blocks, bias_ref, sub_len)):
        o_ref[j * tq:(j + 1) * tq, :] = acc
        lse_ref[j * tq:(j + 1) * tq, :] = lse


def _attn_call(proj, bias, b, s):
    rows = ATTN_PAIR * ATTN_TQ
    nstep = s // rows
    edge = rows // A_RADIUS
    pv = proj.reshape(b, s, NP)
    qo, ko, vo = (_SEG[n] * LANE // GW for n in ("aq", "ak", "av"))
    blk, eblk = (None, rows, GW), (None, A_RADIUS, GW)
    cur = lambda off: (lambda bi, i: (bi, i, off))
    prv = lambda off: (lambda bi, i: (bi, jnp.maximum(i * edge - 1, 0), off))
    nxt = lambda off: (lambda bi, i: (bi, jnp.minimum((i + 1) * edge, nstep * edge - 1), off))
    out_map = lambda bi, i: (bi, i, 0)
    o, lse = pl.pallas_call(
        functools.partial(_attn_kernel, sub_len=s), name="window_attn_g0",
        grid=(b, nstep),
        in_specs=[pl.BlockSpec(blk, cur(qo)),
                  pl.BlockSpec(eblk, prv(ko)), pl.BlockSpec(blk, cur(ko)), pl.BlockSpec(eblk, nxt(ko)),
                  pl.BlockSpec(eblk, prv(vo)), pl.BlockSpec(blk, cur(vo)), pl.BlockSpec(eblk, nxt(vo)),
                  pl.BlockSpec((A_HEADS_PER_GROUP, ATTN_TQ, 2 * ATTN_TQ), lambda bi, i: (0, 0, 0))],
        out_specs=[pl.BlockSpec(blk, out_map), pl.BlockSpec(blk, out_map)],
        out_shape=[jax.ShapeDtypeStruct((b, s, GW), F32), jax.ShapeDtypeStruct((b, s, GW), F32)],
        compiler_params=_cparams(("parallel", "parallel")),
    )(pv, pv, pv, pv, pv, pv, pv, bias)
    return o.reshape(b * s, GW), lse.reshape(b * s, GW)


def _dil_attn_kernel(q_ref, kp_ref, kc_ref, kn_ref, vp_ref, vc_ref, vn_ref, bias_ref, o_ref, lse_ref, *,
                     dil, sub_len):
    tq, half = ATTN_TQ, A_RADIUS
    first_key = pl.program_id(1) * tq - half

    def residues(it, carry):
        def rows(ref, r, n):
            return jnp.concatenate([ref[c, pl.ds(r, n, stride=dil), :] for c in range(GW // LANE)], axis=-1)

        rs = [it * ATTN_PAIR + j for j in range(ATTN_PAIR)]
        blocks = []
        for r in rs:
            q = rows(q_ref, r, tq).astype(BF16)
            kwin = jnp.concatenate([rows(kp_ref, r, half), rows(kc_ref, r, tq), rows(kn_ref, r, half)],
                                   axis=0).astype(BF16)
            vwin = jnp.concatenate([rows(vp_ref, r, half), rows(vc_ref, r, tq), rows(vn_ref, r, half)],
                                   axis=0).astype(BF16)
            blocks.append((q, kwin, vwin, first_key))
        for r, (acc, lse) in zip(rs, _banded_heads(blocks, bias_ref, sub_len)):
            for c in range(GW // LANE):
                o_ref[c, pl.ds(r, tq, stride=dil), :] = acc[:, c * LANE:(c + 1) * LANE]
                lse_ref[c, pl.ds(r, tq, stride=dil), :] = lse[:, c * LANE:(c + 1) * LANE]
        return carry

    lax.fori_loop(0, dil // ATTN_PAIR, residues, 0)


def _dil_attn_call(pd, bias, b, s, g, dil):
    tq = ATTN_TQ
    rows = tq * dil
    nblk = s // rows
    nslab = GW // LANE
    pv = pd.reshape(ND // LANE, b, s, LANE)
    qo, ko, vo = (3 * (g - 1) + j for j in range(3))
    blk, hblk = (nslab, None, rows, LANE), (nslab, None, rows // 2, LANE)
    cur = lambda off: (lambda bi, i: (off, bi, i, 0))
    prv = lambda off: (lambda bi, i: (off, bi, jnp.maximum(2 * i - 1, 0), 0))
    nxt = lambda off: (lambda bi, i: (off, bi, jnp.minimum(2 * i + 2, 2 * nblk - 1), 0))
    out_map = lambda bi, i: (0, bi, i, 0)
    o, lse = pl.pallas_call(
        functools.partial(_dil_attn_kernel, dil=dil, sub_len=s // dil), name=f"dilated_attn_g{g}",
        grid=(b, nblk),
        in_specs=[pl.BlockSpec(blk, cur(qo)),
                  pl.BlockSpec(hblk, prv(ko)), pl.BlockSpec(blk, cur(ko)), pl.BlockSpec(hblk, nxt(ko)),
                  pl.BlockSpec(hblk, prv(vo)), pl.BlockSpec(blk, cur(vo)), pl.BlockSpec(hblk, nxt(vo)),
                  pl.BlockSpec((A_HEADS_PER_GROUP, tq, 2 * tq), lambda bi, i: (0, 0, 0))],
        out_specs=[pl.BlockSpec(blk, out_map), pl.BlockSpec(blk, out_map)],
        out_shape=[jax.ShapeDtypeStruct((nslab, b, s, LANE), F32)] * 2,
        compiler_params=_cparams(("parallel", "parallel")),
    )(pv, pv, pv, pv, pv, pv, pv, bias)
    return o.reshape(nslab, b * s, LANE), lse.reshape(nslab, b * s, LANE)


def _t5_bucket(rel):
    nb = REL_BUCKETS // 2
    max_exact = nb // 2
    ret = jnp.where(rel > 0, nb, 0)
    n = jnp.abs(rel)
    nf = jnp.maximum(n, 1).astype(F32)
    large = max_exact + (jnp.log(nf / max_exact) / math.log(REL_MAX_DIST / max_exact)
                         * (nb - max_exact)).astype(jnp.int32)
    large = jnp.minimum(large, nb - 1)
    return ret + jnp.where(n < max_exact, n, large)


def _attn_bias_table(rel_bias, g, dil, tq=128):
    qi = jnp.arange(tq)[:, None]
    kj = jnp.arange(2 * tq)[None, :] - A_RADIUS
    rel = kj - qi
    tab = rel_bias[:, g * A_HEADS_PER_GROUP:(g + 1) * A_HEADS_PER_GROUP].astype(F32)
    bucket = _t5_bucket(rel * dil)[None]
    bias = jnp.zeros((A_HEADS_PER_GROUP,) + rel.shape, F32)
    for bkt in range(REL_BUCKETS):
        bias = jnp.where(bucket == bkt, tab[bkt][:, None, None], bias)
    return jnp.where((jnp.abs(rel) <= A_RADIUS)[None], bias, NEG_INF)


def _s5_toeplitz_kernel(bcf_ref, pwf_ref, bcb_ref, pwb_ref, d_ref, t_ref, gf_ref, gb_ref):
    gf_ref[...] = jnp.dot(bcf_ref[...], pwf_ref[...], precision=HIGHEST, preferred_element_type=F32)
    gb_ref[...] = jnp.dot(bcb_ref[...], pwb_ref[...], precision=HIGHEST, preferred_element_type=F32)
    c = S5_CHUNK
    row = lax.broadcasted_iota(jnp.int32, (c, c), 0)
    col = lax.broadcasted_iota(jnp.int32, (c, c), 1)

    def body(ci, carry):
        dval = d_ref[pl.ds(ci, 1), :]
        for co in range(S5_GROUP):
            r = ci * S5_GROUP + co
            gf = jnp.broadcast_to(gf_ref[pl.ds(r, 1), :], (c, c))
            gb = jnp.broadcast_to(gb_ref[pl.ds(r, 1), :], (c, c))
            tf = pltpu.roll(gf, 0, 1, stride=1, stride_axis=0)
            tb = pltpu.roll(gb, 1, 1, stride=1, stride_axis=0)
            tile = jnp.where(col >= row, tf, 0.0) + jnp.where(row >= col, tb, 0.0)
            tile = tile + jnp.where((row == col) & (ci == co), dval, 0.0)
            t_ref[pl.ds(pl.multiple_of(ci * c, c), c), co * c:(co + 1) * c] = tile.astype(BF16)
        return carry

    lax.fori_loop(0, S5_GROUP, body, 0)


def _s5_toeplitz_call(bcf, pwf, bcb, pwb, dskip):
    n = S5_GROUP * S5_CHUNK
    return pl.pallas_call(
        _s5_toeplitz_kernel, name="s5_toeplitz",
        grid=(S5_GROUPS,),
        in_specs=[pl.BlockSpec((None, S5_GROUP * S5_GROUP, 2 * S5_STATE), lambda g: (g, 0, 0)),
                  pl.BlockSpec((None, 2 * S5_STATE, S5_CHUNK), lambda g: (g, 0, 0)),
                  pl.BlockSpec((None, S5_GROUP * S5_GROUP, 2 * S5_STATE), lambda g: (g, 0, 0)),
                  pl.BlockSpec((None, 2 * S5_STATE, S5_CHUNK), lambda g: (g, 0, 0)),
                  pl.BlockSpec((None, S5_GROUP, LANE), lambda g: (g, 0, 0))],
        out_specs=pl.BlockSpec((None, n, n), lambda g: (g, 0, 0)),
        out_shape=jax.ShapeDtypeStruct((S5_GROUPS, n, n), BF16),
        scratch_shapes=[pltpu.VMEM((S5_GROUP * S5_GROUP, S5_CHUNK), F32),
                        pltpu.VMEM((S5_GROUP * S5_GROUP, S5_CHUNK), F32)],
        compiler_params=_cparams(("parallel",)),
    )(bcf, pwf, bcb, pwb, dskip)


def _s5_tables(lam_re, lam_im, log_step, b_re, b_im, c_re, c_im, d_skip):
    c = S5_CHUNK
    k = jnp.arange(c, dtype=F32)
    per_dir = []
    for direction in range(2):
        step = jnp.exp(log_step[direction].astype(F32))[:, None]
        lr, li = lam_re[direction].astype(F32), lam_im[direction].astype(F32)
        mag = jnp.exp(lr * step)
        ar, ai = mag * jnp.cos(li * step), mag * jnp.sin(li * step)
        nr, ni = ar - 1.0, ai
        den = lr * lr + li * li
        fr = (nr * lr + ni * li) / den
        fi = (ni * lr - nr * li) / den
        br, bi = b_re[direction].astype(F32), b_im[direction].astype(F32)
        bbr = fr[..., None] * br - fi[..., None] * bi
        bbi = fr[..., None] * bi + fi[..., None] * br
        cr, cim = c_re[direction].astype(F32), c_im[direction].astype(F32)

        def power(e, log_mag=lr * step, phase=li * step):
            m = jnp.exp(e[None, None, :] * log_mag[..., None])
            th = e[None, None, :] * phase[..., None]
            return m * jnp.cos(th), m * jnp.sin(th)

        per_dir.append(dict(bbr=bbr, bbi=bbi, cr=cr, cim=cim, power=power))

    def bc_table(d):
        bbr_t = jnp.transpose(d["bbr"], (0, 2, 1))[:, :, None, :]
        bbi_t = jnp.transpose(d["bbi"], (0, 2, 1))[:, :, None, :]
        cr, cim = d["cr"][:, None], d["cim"][:, None]
        re = bbr_t * cr - bbi_t * cim
        im = bbr_t * cim + bbi_t * cr
        return jnp.concatenate([re, -im], axis=-1).reshape(S5_GROUPS, S5_GROUP * S5_GROUP, 2 * S5_STATE)

    f, bw = per_dir
    pfr, pfi = f["power"](k)
    pbr, pbi = bw["power"](c - 1 - k)
    pwf = jnp.concatenate([pfr, pfi], axis=1)
    pwb = jnp.concatenate([pbr, pbi], axis=1)
    dsk = jnp.broadcast_to(d_skip.astype(F32).reshape(S5_GROUPS, S5_GROUP, 1), (S5_GROUPS, S5_GROUP, LANE))
    toep = _s5_toeplitz_call(bc_table(f), pwf, bc_table(bw), pwb, dsk)

    def w_in(d, e):
        pr, pi = d["power"](e)
        re = d["bbr"][:, :, :, None] * pr[:, :, None, :] - d["bbi"][:, :, :, None] * pi[:, :, None, :]
        im = d["bbr"][:, :, :, None] * pi[:, :, None, :] + d["bbi"][:, :, :, None] * pr[:, :, None, :]
        tr = lambda z: jnp.transpose(z, (0, 2, 3, 1)).reshape(S5_GROUPS, S5_GROUP * c, S5_STATE)
        return tr(re), tr(im)

    fre, fim = w_in(f, c - 1 - k)
    bre, bim = w_in(bw, k)
    win = jnp.concatenate([fre, bre, fim, bim], axis=-1).astype(BF16)

    def w_out(d, e):
        pr, pi = d["power"](e)
        re = d["cr"][:, :, :, None] * pr[:, None] - d["cim"][:, :, :, None] * pi[:, None]
        im = d["cr"][:, :, :, None] * pi[:, None] + d["cim"][:, :, :, None] * pr[:, None]
        tr = lambda z: jnp.transpose(z, (0, 2, 1, 3)).reshape(S5_GROUPS, S5_STATE, S5_GROUP * c)
        return tr(re), tr(-im)

    ofre, ofim = w_out(f, k + 1.0)
    obre, obim = w_out(bw, c - k)
    wout = jnp.concatenate([ofre, obre, ofim, obim], axis=1).astype(BF16)

    cc = jnp.asarray([float(c)], F32)
    afr, afi = f["power"](cc)
    abr, abi = bw["power"](cc)
    a_re = jnp.concatenate([afr[..., 0], abr[..., 0]], axis=-1).reshape(1, -1)
    a_im = jnp.concatenate([afi[..., 0], abi[..., 0]], axis=-1).reshape(1, -1)
    return toep, win, wout, a_re, a_im


def _s5_contrib_kernel(u_ref, win_ref, re_ref, im_ref):
    uc = jnp.concatenate([u_ref[ci] for ci in range(S5_GROUP)], axis=-1)
    res = jnp.dot(uc, win_ref[...], preferred_element_type=F32)
    nc, nb = re_ref.shape[0], re_ref.shape[1]
    half = 2 * S5_STATE
    for bi in range(nb):
        re_ref[:, bi, :] = res[bi * nc:(bi + 1) * nc, :half]
        im_ref[:, bi, :] = res[bi * nc:(bi + 1) * nc, half:]


def _s5_contrib_call(ut, win, b, nc):
    nch = b * nc
    half = 2 * S5_STATE
    uv = ut.reshape(S5_GROUPS, S5_GROUP, nch, S5_CHUNK)
    return pl.pallas_call(
        _s5_contrib_kernel, name="s5_contrib",
        grid=(S5_GROUPS,),
        in_specs=[pl.BlockSpec((None, S5_GROUP, nch, S5_CHUNK), lambda g: (g, 0, 0, 0)),
                  pl.BlockSpec((None, S5_GROUP * S5_CHUNK, 4 * S5_STATE), lambda g: (g, 0, 0))],
        out_specs=[pl.BlockSpec((nc, b, half), lambda g: (0, 0, g))] * 2,
        out_shape=[jax.ShapeDtypeStruct((nc, b, S5_GROUPS * half), F32)] * 2,
        compiler_params=_cparams(("parallel",)),
    )(uv, win)


def _s5_scan_kernel(cre_ref, cim_ref, are_ref, aim_ref, ore_ref, oim_ref):
    nc, nb, width = cre_ref.shape
    a_re, a_im = are_ref[...], aim_ref[...]
    lane = lax.broadcasted_iota(jnp.int32, (1, width), 1)
    fwd_lane = (lane % (2 * S5_STATE)) < S5_STATE
    zero = jnp.zeros((nb, width), F32)

    def advance(s, c):
        sr, si = s
        return a_re * sr - a_im * si + cre_ref[c], a_re * si + a_im * sr + cim_ref[c]

    def up(c, s):
        ore_ref[c] = s[0]
        oim_ref[c] = s[1]
        return advance(s, c)

    lax.fori_loop(0, nc, up, (zero, zero))

    def down(i, s):
        c = nc - 1 - i
        ore_ref[c] = jnp.where(fwd_lane, ore_ref[c], s[0])
        oim_ref[c] = jnp.where(fwd_lane, oim_ref[c], s[1])
        return advance(s, c)

    lax.fori_loop(0, nc, down, (zero, zero))


def _s5_scan_call(cre, cim, a_re, a_im):
    return pl.pallas_call(
        _s5_scan_kernel, name="s5_chunk_scan",
        out_shape=[jax.ShapeDtypeStruct(cre.shape, F32)] * 2,
        compiler_params=pltpu.CompilerParams(vmem_limit_bytes=VMEM_LIMIT),
    )(cre, cim, a_re, a_im)


def _gelu_tanh(y):
    return 0.5 * y * (1.0 + jnp.tanh(math.sqrt(2.0 / math.pi) * (y + 0.044715 * (y * y * y))))


def _s5_out_kernel(u_ref, t_ref, sre_ref, sim_ref, wout_ref, z_ref):
    uc = jnp.concatenate([u_ref[ci] for ci in range(S5_GROUP)], axis=-1)
    y = jnp.dot(uc, t_ref[...], preferred_element_type=F32)
    sp = jnp.concatenate([jnp.concatenate([sre_ref[:, bi, :], sim_ref[:, bi, :]], axis=-1)
                          for bi in range(sre_ref.shape[1])], axis=0)
    y = y + jnp.dot(sp.astype(BF16), wout_ref[...], preferred_element_type=F32)
    z = _gelu_tanh(y)
    for co in range(S5_GROUP):
        z_ref[co] = z[:, co * S5_CHUNK:(co + 1) * S5_CHUNK].astype(BF16)


def _s5_out_call(ut, toep, s_re, s_im, wout, b, nc):
    nch = b * nc
    uv = ut.reshape(S5_GROUPS, S5_GROUP, nch, S5_CHUNK)
    n = S5_GROUP * S5_CHUNK
    half = 2 * S5_STATE
    zt = pl.pallas_call(
        _s5_out_kernel, name="s5_out",
        grid=(S5_GROUPS,),
        in_specs=[pl.BlockSpec((None, S5_GROUP, nch, S5_CHUNK), lambda g: (g, 0, 0, 0)),
                  pl.BlockSpec((None, n, n), lambda g: (g, 0, 0)),
                  pl.BlockSpec((nc, b, half), lambda g: (0, 0, g)),
                  pl.BlockSpec((nc, b, half), lambda g: (0, 0, g)),
                  pl.BlockSpec((None, 4 * S5_STATE, n), lambda g: (g, 0, 0))],
        out_specs=pl.BlockSpec((None, S5_GROUP, nch, S5_CHUNK), lambda g: (g, 0, 0, 0)),
        out_shape=jax.ShapeDtypeStruct((S5_GROUPS, S5_GROUP, nch, S5_CHUNK), BF16),
        compiler_params=_cparams(("parallel",)),
    )(uv, toep, s_re, s_im, wout)
    return zt.reshape(S5_WIDTH, nch * S5_CHUNK)


def _ret_tables(s):
    c = RET_CHUNK
    half = RET_QK_DIM // 2
    inv = ROPE_BASE ** (-jnp.arange(half, dtype=F32) / half)
    ang = jnp.arange(s, dtype=F32)[:, None] * inv[None, :]
    cos = jnp.tile(jnp.cos(ang), (1, 2 * RET_HEADS))
    sin = jnp.tile(jnp.sin(ang), (1, 2 * RET_HEADS))
    hidx = jnp.arange(RET_HEADS, dtype=F32)
    lgf = jnp.log1p(-jnp.exp2(-5.0 - hidx))
    lgb = jnp.log1p(-jnp.exp2(-5.5 - hidx))
    j = jnp.arange(c, dtype=F32)
    rel = j[:, None] - j[None, :]
    intra = jnp.exp(jnp.abs(rel)[None] * jnp.where(rel[None] >= 0, lgf[:, None, None], lgb[:, None, None]))
    qk_f = jnp.repeat(lgf, RET_QK_DIM)[None, :]
    qk_b = jnp.repeat(lgb, RET_QK_DIM)[None, :]
    v_f = jnp.repeat(lgf, RET_V_DIM)[None, :]
    v_b = jnp.repeat(lgb, RET_V_DIM)[None, :]
    tabs = dict(
        cos=cos, sin=sin, intra=intra,
        tail_f=jnp.exp((c - 1 - j)[:, None] * qk_f), head_b=jnp.exp(j[:, None] * qk_b),
        decq_f=jnp.exp((j + 1.0)[:, None] * qk_f), decq_b=jnp.exp((c - j)[:, None] * qk_b),
        chunk_f=jnp.exp(c * v_f), chunk_b=jnp.exp(c * v_b),
        bd=(jnp.arange(RET_HEADS * RET_QK_DIM)[:, None] // RET_QK_DIM
            == jnp.arange(RET_HEADS * RET_V_DIM)[None, :] // RET_V_DIM).astype(F32))
    return tabs


def _rot(x_ref, xs_ref, cos_ref, sin_ref):
    return x_ref[...].astype(F32) * cos_ref[...] + xs_ref[...].astype(F32) * sin_ref[...]


def _ret_kernel(q_ref, qs_ref, k_ref, ks_ref, v_ref, g_ref, cos_ref, sin_ref, intra_ref, tailf_ref, headb_ref,
                decqf_ref, decqb_ref, chunkf_ref, chunkb_ref, o_ref, st_ref, sfd_ref, sbd_ref, sball_ref, *, nc):
    i = pl.program_id(1)
    qk, dv = RET_QK_DIM, RET_V_DIM
    vb = v_ref[...]
    kr = _rot(k_ref, ks_ref, cos_ref, sin_ref) * (RET_QK_DIM ** -0.5)

    def update_state(weight_ref, decay_ref):
        kwt = jnp.transpose(kr * weight_ref[...])
        for h in range(RET_HEADS):
            kv = jnp.dot(kwt[h * qk:(h + 1) * qk, :].astype(BF16), vb[:, h * dv:(h + 1) * dv],
                         preferred_element_type=F32)
            st_ref[h] = st_ref[h] * decay_ref[:, h * dv:(h + 1) * dv] + kv

    @pl.when((i == 0) | (i == nc))
    def _():
        st_ref[...] = jnp.zeros_like(st_ref)

    @pl.when(i == 0)
    def _():
        sfd_ref[...] = jnp.zeros_like(sfd_ref)
        sbd_ref[...] = jnp.zeros_like(sbd_ref)

    @pl.when(i < nc)
    def _():
        c = nc - 1 - i
        sball_ref[c] = st_ref[...].astype(BF16)
        update_state(headb_ref, chunkb_ref)

    @pl.when(i >= nc)
    def _():
        c = i - nc
        for h in range(RET_HEADS):
            sfd_ref[h * qk:(h + 1) * qk, h * dv:(h + 1) * dv] = st_ref[h].astype(BF16)
            sbd_ref[h * qk:(h + 1) * qk, h * dv:(h + 1) * dv] = sball_ref[c, h]
        qr = _rot(q_ref, qs_ref, cos_ref, sin_ref)
        qb, kb = qr.astype(BF16), kr.astype(BF16)
        lane_head = lax.broadcasted_iota(jnp.int32, (1, qb.shape[1]), 1) // qk
        inter = jnp.dot((qr * decqf_ref[...]).astype(BF16), sfd_ref[...], preferred_element_type=F32)
        inter = inter + jnp.dot((qr * decqb_ref[...]).astype(BF16), sbd_ref[...], preferred_element_type=F32)
        heads = range(RET_HEADS)
        vcols = [slice(h * dv, (h + 1) * dv) for h in heads]
        qhs = [jnp.where(lane_head == h, qb, jnp.zeros_like(qb)) for h in heads]
        scs = [lax.dot_general(qh, kb, (((1,), (1,)), ((), ())), preferred_element_type=F32) * intra_ref[h]
               for h, qh in zip(heads, qhs)]
        ohs = [jnp.dot(sc.astype(BF16), vb[:, vs], preferred_element_type=F32) + inter[:, vs]
               for sc, vs in zip(scs, vcols)]
        mus = [jnp.mean(oh, axis=-1, keepdims=True) for oh in ohs]
        dds = [oh - mu for oh, mu in zip(ohs, mus)]
        vrs = [jnp.mean(dd * dd, axis=-1, keepdims=True) for dd in dds]
        on = jnp.concatenate([dd * lax.rsqrt(var + LN_EPS) for dd, var in zip(dds, vrs)], axis=-1)
        gg = g_ref[...].astype(F32)
        o_ref[...] = (on * (gg * _sigmoid(gg))).astype(BF16)
        update_state(tailf_ref, chunkf_ref)


def _retention_call(proj, tabs, b, s):
    c = RET_CHUNK
    nc = s // c
    qkw, vw = RET_HEADS * RET_QK_DIM, RET_HEADS * RET_V_DIM
    pv = proj.reshape(b, s, NP)
    qk_blk, v_blk = (None, c, qkw), (None, c, vw)
    qo, qso, ko, kso = (_SEG[n] * LANE // qkw for n in ("rq", "rqs", "rk", "rks"))
    vo, go = (_SEG[n] * LANE // vw for n in ("rv", "rg"))
    const2 = lambda shape: pl.BlockSpec(shape, lambda bi, i: (0, 0))
    chunk = lambda i: jnp.where(i < nc, nc - 1 - i, i - nc)
    fchunk = lambda i: jnp.maximum(i - nc, 0)
    both = lambda off: (lambda bi, i: (bi, chunk(i), off))
    fonly = lambda off: (lambda bi, i: (bi, fchunk(i), off))
    yc = pl.pallas_call(
        functools.partial(_ret_kernel, nc=nc), name="retention",
        grid=(b, 2 * nc),
        in_specs=[pl.BlockSpec(qk_blk, fonly(qo)), pl.BlockSpec(qk_blk, fonly(qso)),
                  pl.BlockSpec(qk_blk, both(ko)), pl.BlockSpec(qk_blk, both(kso)),
                  pl.BlockSpec(v_blk, both(vo)), pl.BlockSpec(v_blk, fonly(go)),
                  pl.BlockSpec((c, qkw), lambda bi, i: (chunk(i), 0)),
                  pl.BlockSpec((c, qkw), lambda bi, i: (chunk(i), 0)),
                  pl.BlockSpec((RET_HEADS, c, c), lambda bi, i: (0, 0, 0)),
                  const2((c, qkw)), const2((c, qkw)), const2((c, qkw)), const2((c, qkw)),
                  const2((1, vw)), const2((1, vw))],
        out_specs=pl.BlockSpec((None, c, vw), lambda bi, i: (bi, fchunk(i), 0)),
        out_shape=jax.ShapeDtypeStruct((b, s, vw), BF16),
        scratch_shapes=[pltpu.VMEM((RET_HEADS, RET_QK_DIM, RET_V_DIM), F32),
                        pltpu.VMEM((qkw, vw), BF16), pltpu.VMEM((qkw, vw), BF16),
                        pltpu.VMEM((nc, RET_HEADS, RET_QK_DIM, RET_V_DIM), BF16)],
        compiler_params=_cparams(("parallel", "arbitrary")),
    )(pv, pv, pv, pv, pv, pv, tabs["cos"], tabs["sin"], tabs["intra"], tabs["tail_f"], tabs["head_b"],
      tabs["decq_f"], tabs["decq_b"], tabs["chunk_f"], tabs["chunk_b"])
    return yc.reshape(b * s, vw)


def _mm_kernel(x_ref, w_ref, o_ref):
    o_ref[...] = jnp.dot(x_ref[...].astype(BF16), w_ref[...], preferred_element_type=F32).astype(o_ref.dtype)


def _mem_kv_call(mem2d, w_kv):
    m, n = mem2d.shape[0], w_kv.shape[1]
    tm = 512
    return pl.pallas_call(
        _mm_kernel, name="mem_kv_proj",
        grid=(m // tm,),
        in_specs=[pl.BlockSpec((tm, D_MODEL), lambda i: (i, 0)),
                  pl.BlockSpec((D_MODEL, n), lambda i: (0, 0))],
        out_specs=pl.BlockSpec((tm, n), lambda i: (i, 0)),
        out_shape=jax.ShapeDtypeStruct((m, n), BF16),
        compiler_params=_cparams(("parallel",)),
    )(mem2d, w_kv)


def _mem_attn_kernel(q_ref, k_ref, v_ref, o_ref):
    q = q_ref[...]
    cols = [slice(h * MEM_HEAD_PAD, (h + 1) * MEM_HEAD_PAD) for h in range(MEM_HEADS)]
    ss = [lax.dot_general(q[:, hs], k_ref[:, hs], (((1,), (1,)), ((), ())),
                          preferred_element_type=F32) * (MEM_HEAD_DIM ** -0.5) for hs in cols]
    ms = [jnp.max(s, axis=-1, keepdims=True) for s in ss]
    es = [jnp.exp(s - m) for s, m in zip(ss, ms)]
    ps = [e / jnp.sum(e, axis=-1, keepdims=True) for e in es]
    outs = [jnp.dot(p.astype(BF16), v_ref[:, hs], preferred_element_type=F32) for p, hs in zip(ps, cols)]
    o_ref[...] = jnp.concatenate(outs, axis=-1).astype(BF16)


def _mem_attn_call(proj, kv, b, s):
    tq = 256
    w = MEM_HEADS * MEM_HEAD_PAD
    pv = proj.reshape(b, s, NP)
    kvv = kv.reshape(b, N_MEM, 2 * w)
    qo = _SEG["mq"] * LANE // w
    yd = pl.pallas_call(
        _mem_attn_kernel, name="mem_attn",
        grid=(b, s // tq),
        in_specs=[pl.BlockSpec((None, tq, w), lambda bi, i: (bi, i, qo)),
                  pl.BlockSpec((None, N_MEM, w), lambda bi, i: (bi, 0, 0)),
                  pl.BlockSpec((None, N_MEM, w), lambda bi, i: (bi, 0, 1))],
        out_specs=pl.BlockSpec((None, tq, w), lambda bi, i: (bi, i, 0)),
        out_shape=jax.ShapeDtypeStruct((b, s, w), BF16),
        compiler_params=_cparams(("parallel", "parallel")),
    )(pv, kvv, kvv)
    return yd.reshape(b * s, w)


def _merge_kernel(x_ref, ao0_ref, ao1_ref, ao2_ref, al0_ref, al1_ref, al2_ref, zt_ref, yc_ref, yd_ref, gl_ref,
                  wglut_ref, bglu_ref, wba_ref, wbb_ref, wbc_ref, wbd_ref, wout_ref, g1_ref, b1_ref,
                  x1_ref):
    tm = x_ref.shape[0]
    slabs = lambda ref: jnp.concatenate([ref[c] for c in range(ref.shape[0])], axis=-1)
    lses = [al0_ref[...], slabs(al1_ref), slabs(al2_ref)]
    aos = [ao0_ref[...], slabs(ao1_ref), slabs(ao2_ref)]
    mx = jnp.maximum(jnp.maximum(lses[0], lses[1]), lses[2])
    es = [jnp.exp(l - mx) for l in lses]
    tot = es[0] + es[1] + es[2]
    ya = jnp.concatenate([o * (e / tot) for o, e in zip(aos, es)], axis=-1).astype(BF16)
    zt = zt_ref[...]
    glu = jnp.dot(wglut_ref[...], zt, preferred_element_type=F32) + bglu_ref[...]
    ybt = zt.astype(F32) * _sigmoid(glu)
    yb = jnp.transpose(ybt).astype(BF16)
    merged = jnp.zeros((tm, D_MODEL), F32)
    for n, (y, w_ref) in enumerate(((ya, wba_ref), (yb, wbb_ref), (yc_ref[...], wbc_ref), (yd_ref[...], wbd_ref))):
        bo = jnp.dot(y, w_ref[...], preferred_element_type=F32)
        merged = merged + (jnp.tanh(gl_ref[:, n * D_MODEL:(n + 1) * D_MODEL].astype(F32)) + 1.0) * bo
    h = DEEPNORM_ALPHA * x_ref[...] + jnp.dot(merged.astype(BF16), wout_ref[...], preferred_element_type=F32)
    x1_ref[...] = _ln(h, g1_ref[...], b1_ref[...])


def _router_kernel(x1_ref, rwh_ref, rwl_ref, rb_ref, tri_ref, x1p_ref, route_ref, counts_ref, carry_ref):
    tm = x1_ref.shape[0]

    @pl.when(pl.program_id(0) == 0)
    def _():
        carry_ref[...] = jnp.zeros_like(carry_ref)

    x1 = x1_ref[...]
    x1p_ref[...] = _pack_bf16_pairs(x1)
    x_hi = x1.astype(BF16)
    x_lo = (x1 - x_hi.astype(F32)).astype(BF16)
    logits = (jnp.dot(x_hi, rwh_ref[...], preferred_element_type=F32)
              + jnp.dot(x_lo, rwh_ref[...], preferred_element_type=F32)
              + jnp.dot(x_hi, rwl_ref[...], preferred_element_type=F32)) + rb_ref[...]
    lane = lax.broadcasted_iota(jnp.int32, (tm, LANE), 1)
    work = logits
    sel, vals, hots = [], [], []
    for _ in range(TOP_K):
        mk = jnp.max(work, axis=-1, keepdims=True)
        ik = jnp.min(jnp.where(work == mk, lane, LANE), axis=-1, keepdims=True)
        hot = lane == ik
        work = jnp.where(hot, -jnp.inf, work)
        sel.append(ik)
        vals.append(mk)
        hots.append(hot)
    ex = [jnp.exp(v - vals[0]) for v in vals]
    den = ex[0] + ex[1] + ex[2] + ex[3]
    chosen = (hots[0] | hots[1] | hots[2] | hots[3]).astype(F32)
    before = jnp.dot(tri_ref[...], chosen.astype(BF16), preferred_element_type=F32) + carry_ref[0:1, :]
    route = jnp.zeros((tm, LANE), F32)
    for k in range(TOP_K):
        rank = jnp.sum(jnp.where(hots[k], before, 0.0), axis=-1, keepdims=True)
        route = jnp.where(lane == k, sel[k].astype(F32), route)
        route = jnp.where(lane == TOP_K + k, ex[k] / den, route)
        route = jnp.where(lane == 2 * TOP_K + k, rank, route)
    route_ref[...] = route
    new_carry = carry_ref[0:1, :] + jnp.sum(chosen, axis=0, keepdims=True)
    carry_ref[...] = jnp.broadcast_to(new_carry, carry_ref.shape)
    counts_ref[...] = jnp.broadcast_to(new_carry, counts_ref.shape)


def _merge_call(x, aos, alses, zt, yc, yd, proj, lw):
    t = x.shape[0]
    tm = 256
    gw = GW
    gate_blk = 4 * D_MODEL
    row = lambda w: pl.BlockSpec((tm, w), lambda i: (i, 0))
    slab = pl.BlockSpec((gw // LANE, tm, LANE), lambda i: (0, i, 0))
    full = lambda a: pl.BlockSpec(a.shape, lambda i: (0,) * a.ndim)
    consts = (lw["w_glu_t"], lw["b_glu"], lw["wb_a"], lw["wb_b"], lw["wb_c"], lw["wb_d"], lw["w_out"],
              lw["ln1_g"], lw["ln1_b"])
    return pl.pallas_call(
        _merge_kernel, name="branch_merge",
        grid=(t // tm,),
        in_specs=[row(D_MODEL)] + [row(gw), slab, slab] * 2
                 + [pl.BlockSpec((S5_WIDTH, tm), lambda i: (0, i)),
                  row(RET_HEADS * RET_V_DIM), row(MEM_HEADS * MEM_HEAD_PAD),
                  pl.BlockSpec((tm, gate_blk), lambda i: (i, _SEG["gate"] * LANE // gate_blk))]
                 + [full(a) for a in consts],
        out_specs=row(D_MODEL),
        out_shape=jax.ShapeDtypeStruct((t, D_MODEL), F32),
        compiler_params=_cparams(("parallel",)),
    )(x, *aos, *alses, zt, yc, yd, proj, *consts)


def _router_call(x1, lw):
    t = x1.shape[0]
    tm = 1024
    row = lambda w: pl.BlockSpec((tm, w), lambda i: (i, 0))
    full = lambda a: pl.BlockSpec(a.shape, lambda i: (0,) * a.ndim)
    tri = jnp.tril(jnp.ones((tm, tm), BF16), -1)
    consts = (lw["router_w_hi"], lw["router_w_lo"], lw["router_b"], tri)
    return pl.pallas_call(
        _router_kernel, name="moe_router",
        grid=(t // tm,),
        in_specs=[row(D_MODEL)] + [full(a) for a in consts],
        out_specs=[row(D_MODEL // 2), row(LANE), pl.BlockSpec((8, LANE), lambda i: (0, 0))],
        out_shape=[jax.ShapeDtypeStruct((t, D_MODEL // 2), jnp.uint32),
                   jax.ShapeDtypeStruct((t, LANE), F32),
                   jax.ShapeDtypeStruct((8, LANE), F32)],
        scratch_shapes=[pltpu.VMEM((8, LANE), F32)],
        compiler_params=_cparams(("arbitrary",)),
    )(x1, *consts)


def _pack_bf16_pairs(x):
    n = x.shape[1] // 2
    hi = lax.bitcast_convert_type(x[:, :n].astype(BF16).astype(F32), jnp.uint32)
    lo = lax.bitcast_convert_type(x[:, n:].astype(BF16).astype(F32), jnp.uint32)
    return hi | (lo >> 16)


def _unpack_bf16_pairs(w):
    hi = lax.bitcast_convert_type(w & jnp.uint32(0xFFFF0000), F32)
    lo = lax.bitcast_convert_type(w << 16, F32)
    return jnp.concatenate([hi, lo], axis=-1)


def _sc_index_rows(idx):
    window = SC_GATHER_WINDOW
    return jnp.pad(idx.reshape(-1, window), ((0, 0), (0, LANE - window)))


def _sc_scatter_rows(x, idx, n_out):
    n, d = x.shape
    m = idx.shape[0]
    window = SC_GATHER_WINDOW
    src_steps = n // window
    mesh = plsc.VectorSubcoreMesh(core_axis_name="core", subcore_axis_name="subcore")

    @pl.kernel(out_type=jax.ShapeDtypeStruct((n_out, d), x.dtype), mesh=mesh, scratch_types=[])
    def scatter_kernel(x_hbm, i_hbm, o_hbm):
        def body(x_vmem, i_vmem):
            pltpu.sync_copy(x_vmem, o_hbm.at[i_vmem.at[0, pl.ds(0, window)]])

        pltpu.emit_pipeline(
            body,
            grid=(m // window,),
            in_specs=[pl.BlockSpec((window, d), index_map=lambda i: (i % src_steps, 0)),
                      pl.BlockSpec((1, LANE), index_map=lambda i: (i, 0))],
            out_specs=[],
            core_axis_name=("core", "subcore"),
            dimension_semantics=(pltpu.PARALLEL,),
            trace_scopes=False,
        )(x_hbm, i_hbm)

    return scatter_kernel(x, _sc_index_rows(idx))


def _sc_gather_rows(x, idx):
    m, d = idx.shape[0], x.shape[1]
    window = SC_GATHER_WINDOW
    mesh = plsc.VectorSubcoreMesh(core_axis_name="core", subcore_axis_name="subcore")

    @pl.kernel(out_type=jax.ShapeDtypeStruct((m, d), x.dtype), mesh=mesh, scratch_types=[])
    def gather_kernel(x_hbm, i_hbm, o_hbm):
        def body(i_vmem, o_vmem):
            pltpu.sync_copy(x_hbm.at[i_vmem.at[0, pl.ds(0, window)]], o_vmem)

        pltpu.emit_pipeline(
            body,
            grid=(m // window,),
            in_specs=[pl.BlockSpec((1, LANE), index_map=lambda i: (i, 0))],
            out_specs=[pl.BlockSpec((window, d), index_map=lambda i: (i, 0))],
            core_axis_name=("core", "subcore"),
            dimension_semantics=(pltpu.PARALLEL,),
            trace_scopes=False,
        )(i_hbm, o_hbm)

    return gather_kernel(x, _sc_index_rows(idx))


def _moe_expert_kernel(be_ref, nu_ref, nv_ref, x_ref, w1_ref, b1_ref, w2_ref, b2_ref, o_ref, w1s_ref, w2s_ref):
    i = pl.program_id(0)

    @pl.when((i == 0) | (be_ref[i] != be_ref[jnp.maximum(i - 1, 0)]))
    def _():
        w1s_ref[...] = w1_ref[...].astype(BF16)
        w2s_ref[...] = w2_ref[...].astype(BF16)

    @pl.when(i < nu_ref[0])
    def _():
        rows = lax.broadcasted_iota(jnp.int32, (x_ref.shape[0], 1), 0)
        x = jnp.where(rows < nv_ref[i], _unpack_bf16_pairs(x_ref[...]), 0.0)
        h = jnp.dot(x.astype(BF16), w1s_ref[...], preferred_element_type=F32) + b1_ref[...]
        gate = jnp.minimum(h[:, :D_FF], SWIGLU_LIMIT)
        up = jnp.clip(h[:, D_FF:], -SWIGLU_LIMIT, SWIGLU_LIMIT)
        hid = gate * _sigmoid(SWIGLU_ALPHA * gate) * (up + 1.0)
        y = jnp.dot(hid.astype(BF16), w2s_ref[...], preferred_element_type=F32) + b2_ref[...]
        o_ref[...] = _pack_bf16_pairs(y)

    @pl.when(i >= nu_ref[0])
    def _():
        o_ref[...] = jnp.zeros_like(o_ref)


def _moe_expert_call(xg, block_expert, n_used, n_valid, layer, w1, b1, w2, b2):
    bm = MOE_BLOCK_ROWS
    n_blocks = xg.shape[0] // bm
    half = D_MODEL // 2
    grid_spec = pltpu.PrefetchScalarGridSpec(
        num_scalar_prefetch=3, grid=(n_blocks,),
        in_specs=[pl.BlockSpec((bm, half), lambda i, be, nu, nv: (i, 0)),
                  pl.BlockSpec((None, None, D_MODEL, 2 * D_FF), lambda i, be, nu, nv: (layer, be[i], 0, 0)),
                  pl.BlockSpec((None, 1, 2 * D_FF), lambda i, be, nu, nv: (be[i], 0, 0)),
                  pl.BlockSpec((None, None, D_FF, D_MODEL), lambda i, be, nu, nv: (layer, be[i], 0, 0)),
                  pl.BlockSpec((None, 1, D_MODEL), lambda i, be, nu, nv: (be[i], 0, 0))],
        out_specs=pl.BlockSpec((bm, half), lambda i, be, nu, nv: (i, 0)),
        scratch_shapes=[pltpu.VMEM((D_MODEL, 2 * D_FF), BF16), pltpu.VMEM((D_FF, D_MODEL), BF16)])
    return pl.pallas_call(
        _moe_expert_kernel, name="moe_experts",
        grid_spec=grid_spec,
        out_shape=jax.ShapeDtypeStruct((n_blocks * bm, half), jnp.uint32),
        compiler_params=_cparams(("arbitrary",)),
    )(block_expert, n_used, n_valid, xg, w1, b1, w2, b2)


def _moe_combine_kernel(x_ref, route_ref, y0_ref, y1_ref, y2_ref, y3_ref, g_ref, b_ref, o_ref):
    route = route_ref[...]
    moe = jnp.zeros(x_ref.shape, F32)
    for k, y_ref in enumerate((y0_ref, y1_ref, y2_ref, y3_ref)):
        moe = moe + route[:, TOP_K + k:TOP_K + k + 1] * _unpack_bf16_pairs(y_ref[...])
    o_ref[...] = _ln(DEEPNORM_ALPHA * x_ref[...] + moe, g_ref[...], b_ref[...])


def _moe_combine_call(x1, route, yg, g2, b2):
    t = x1.shape[0]
    tm = 512
    nt = t // tm
    ysel = lambda k: pl.BlockSpec((tm, D_MODEL // 2), lambda i: (k * nt + i, 0))
    return pl.pallas_call(
        _moe_combine_kernel, name="moe_combine_ln",
        grid=(nt,),
        in_specs=[pl.BlockSpec((tm, D_MODEL), lambda i: (i, 0)),
                  pl.BlockSpec((tm, LANE), lambda i: (i, 0))]
                 + [ysel(k) for k in range(TOP_K)]
                 + [pl.BlockSpec((1, D_MODEL), lambda i: (0, 0)),
                    pl.BlockSpec((1, D_MODEL), lambda i: (0, 0))],
        out_specs=pl.BlockSpec((tm, D_MODEL), lambda i: (i, 0)),
        out_shape=jax.ShapeDtypeStruct((t, D_MODEL), F32),
        compiler_params=_cparams(("parallel",)),
    )(x1, route, yg, yg, yg, yg, g2, b2)


def _moe(x1, x1p, route, counts, lw):
    t = x1.shape[0]
    bm = MOE_BLOCK_ROWS
    n_assign = t * TOP_K
    n_blocks = n_assign // bm + N_EXPERTS
    idx = route[:, :TOP_K].astype(jnp.int32)
    rank = route[:, 2 * TOP_K:3 * TOP_K].astype(jnp.int32)
    cnt = counts[0, :N_EXPERTS].astype(jnp.int32)
    padded = (cnt + bm - 1) // bm * bm
    pad_ends = jnp.cumsum(padded)
    pad_starts = pad_ends - padded
    dest = jnp.transpose(jnp.take(pad_starts, idx) + rank).reshape(-1)
    block_start = jnp.arange(n_blocks, dtype=jnp.int32) * bm
    block_expert = jnp.minimum(jnp.sum(pad_ends[None, :] <= block_start[:, None], axis=1),
                               N_EXPERTS - 1).astype(jnp.int32)
    n_valid = jnp.clip(jnp.take(cnt, block_expert) - (block_start - jnp.take(pad_starts, block_expert)), 0, bm)
    n_used = (pad_ends[-1:] // bm).astype(jnp.int32)
    xg = _sc_scatter_rows(x1p, dest, n_blocks * bm)
    yb = _moe_expert_call(xg, block_expert, n_used, n_valid.astype(jnp.int32), lw["layer"],
                          lw["moe_w1"], lw["moe_b1"], lw["moe_w2"], lw["moe_b2"])
    yg = _sc_gather_rows(yb, dest)
    return _moe_combine_call(x1, route, yg, lw["ln2_g"], lw["ln2_b"])


def _rotate_half_cols(w):
    half = RET_QK_DIM // 2
    w4 = w.reshape(w.shape[0], RET_HEADS, 2, half)
    return jnp.concatenate([-w4[:, :, 1:], w4[:, :, :1]], axis=2).reshape(w.shape)


def _pad_heads(w, axis):
    shp = list(w.shape)
    shp[axis:axis + 1] = [MEM_HEADS, MEM_HEAD_DIM]
    w = w.reshape(shp)
    pad = [(0, 0)] * w.ndim
    pad[axis + 1] = (0, MEM_HEAD_PAD - MEM_HEAD_DIM)
    w = jnp.pad(w, pad)
    shp[axis:axis + 2] = [MEM_HEADS * MEM_HEAD_PAD]
    return w.reshape(shp)


def _layer_weights(l, p):
    w_in = p["w_in"][l]
    aw, sw, qkw, vw, mw = 768, S5_WIDTH, RET_HEADS * RET_QK_DIM, RET_HEADS * RET_V_DIM, 768
    offs = [0]
    for wdt in (aw, aw, aw, sw, qkw, qkw, vw, vw, mw, 4 * D_MODEL):
        offs.append(offs[-1] + wdt)
    aq, ak, av, su, rq, rk, rv, rg, mq, gate = (w_in[:, offs[i]:offs[i + 1]] for i in range(10))
    seg = dict(gate=0.5 * gate, mq=_pad_heads(mq, 1), aq=aq[:, :GW], ak=ak[:, :GW], av=av[:, :GW],
               pad=jnp.zeros((D_MODEL, 2 * LANE), F32), rv=rv, rg=rg,
               rq=rq, rqs=_rotate_half_cols(rq), rk=rk, rks=_rotate_half_cols(rk))
    order = sorted(_SEG, key=_SEG.get)
    w_p = jnp.concatenate([seg[n] for n in order], axis=1).astype(BF16)
    assert w_p.shape[1] == NP
    w_d = jnp.concatenate([w[:, g * GW:(g + 1) * GW] for g in (1, 2) for w in (aq, ak, av)], axis=1).astype(BF16)
    wkv = p["w_mem_kv"][l]
    w_kv = jnp.concatenate([_pad_heads(wkv[:, :768], 1), _pad_heads(wkv[:, 768:], 1)], axis=1).astype(BF16)
    wb = p["w_branch"][l]
    rw = jnp.pad(p["router_w"][l].astype(F32), ((0, 0), (0, LANE - N_EXPERTS)))
    rb = jnp.pad(p["router_b"][l].astype(F32), (0, LANE - N_EXPERTS), constant_values=NEG_INF)
    return dict(
        w_p=w_p, w_d=w_d, w_su_t=jnp.transpose(su).astype(BF16), w_kv=w_kv,
        w_glu_t=jnp.transpose(p["s5_w_glu"][l]).astype(BF16), b_glu=p["s5_b_glu"][l].astype(F32).reshape(-1, 1),
        wb_a=wb[0].astype(BF16), wb_b=wb[1].astype(BF16), wb_c=wb[2].astype(BF16),
        wb_d=_pad_heads(wb[3], 0).astype(BF16), w_out=(0.5 * p["w_out"][l]).astype(BF16),
        ln1_g=p["ln1_g"][l].reshape(1, -1), ln1_b=p["ln1_b"][l].reshape(1, -1),
        router_w_hi=rw.astype(BF16), router_w_lo=(rw - rw.astype(BF16).astype(F32)).astype(BF16),
        router_b=rb.reshape(1, -1),
        layer=l, moe_w1=p["moe_w1"], moe_b1=p["moe_b1"][l].astype(F32).reshape(N_EXPERTS, 1, -1),
        moe_w2=p["moe_w2"], moe_b2=p["moe_b2"][l].astype(F32).reshape(N_EXPERTS, 1, -1),
        ln2_g=p["ln2_g"][l].reshape(1, -1), ln2_b=p["ln2_b"][l].reshape(1, -1),
        s5=_s5_tables(p["s5_lam_re"][l], p["s5_lam_im"][l], p["s5_log_step"][l], p["s5_b_re"][l],
                      p["s5_b_im"][l], p["s5_c_re"][l], p["s5_c_im"][l], p["s5_d"][l]))


def _trunk_layer(x, mem2d, b, s, lw, attn_bias, ret_tabs):
    proj, ut, pd = _proj_call(x, lw["w_p"], lw["w_su_t"], lw["w_d"])
    outs = [_attn_call(proj, attn_bias[0], b, s)]
    outs += [_dil_attn_call(pd, attn_bias[g], b, s, g, A_GROUPS[g][1]) for g in (1, 2)]
    aos, alses = [o for o, _ in outs], [l for _, l in outs]
    toep, win, wout, a_re, a_im = lw["s5"]
    nc = s // S5_CHUNK
    c_re, c_im = _s5_contrib_call(ut, win, b, nc)
    s_re, s_im = _s5_scan_call(c_re, c_im, a_re, a_im)
    zt = _s5_out_call(ut, toep, s_re, s_im, wout, b, nc)
    yc = _retention_call(proj, ret_tabs, b, s)
    kv = _mem_kv_call(mem2d, lw["w_kv"])
    yd = _mem_attn_call(proj, kv, b, s)
    x1 = _merge_call(x, aos, alses, zt, yc, yd, proj, lw)
    x1p, route, counts = _router_call(x1, lw)
    return _moe(x1, x1p, route, counts, lw)


def kernel(x_prompt, x_sample, mem_prompt, mem_sample, ln_in_g, ln_in_b, rel_bias, w_in, s5_lam_re, s5_lam_im, s5_log_step, s5_b_re, s5_b_im, s5_c_re, s5_c_im, s5_d, s5_w_glu, s5_b_glu, w_mem_kv, w_branch, w_out, ln1_g, ln1_b, router_w, router_b, moe_w1, moe_b1, moe_w2, moe_b2, ln2_g, ln2_b):
    p = dict(w_in=w_in, s5_lam_re=s5_lam_re, s5_lam_im=s5_lam_im, s5_log_step=s5_log_step, s5_b_re=s5_b_re,
             s5_b_im=s5_b_im, s5_c_re=s5_c_re, s5_c_im=s5_c_im, s5_d=s5_d, s5_w_glu=s5_w_glu, s5_b_glu=s5_b_glu,
             w_mem_kv=w_mem_kv, w_branch=w_branch, w_out=w_out, ln1_g=ln1_g, ln1_b=ln1_b, router_w=router_w,
             router_b=router_b, moe_w1=moe_w1, moe_b1=moe_b1, moe_w2=moe_w2, moe_b2=moe_b2, ln2_g=ln2_g,
             ln2_b=ln2_b)
    attn_bias = [_attn_bias_table(rel_bias, g, dil) for g, (_, dil) in enumerate(A_GROUPS)]
    trunks = []
    for x, mem in ((x_prompt, mem_prompt), (x_sample, mem_sample)):
        b, s, _ = x.shape
        trunks.append(dict(x=_layer_norm_call(x.reshape(b * s, D_MODEL), ln_in_g, ln_in_b),
                           mem=mem.reshape(b * N_MEM, D_MODEL), b=b, s=s, tabs=_ret_tables(s)))
    for l in range(DEPTH):
        lw = _layer_weights(l, p)
        for tr in trunks:
            tr["x"] = _trunk_layer(tr["x"], tr["mem"], tr["b"], tr["s"], lw, attn_bias, tr["tabs"])
    return tuple(tr["x"].reshape(tr["b"], tr["s"], D_MODEL) for tr in trunks)
```

```python
import functools
import math

import jax
import jax.numpy as jnp
from jax import lax
from jax.experimental import pallas as pl
from jax.experimental.pallas import tpu as pltpu
from jax.experimental.pallas import tpu_sc as plsc

F32 = jnp.float32
BF16 = jnp.bfloat16
HIGHEST = lax.Precision.HIGHEST

D_MODEL = 1024
DEPTH = 4
N_MEM = 256
A_GROUPS = ((128, 1), (512, 4), (2048, 16))
A_HEADS_PER_GROUP = 4
A_HEADS = 12
A_HEAD_DIM = 64
A_RADIUS = 64
REL_BUCKETS = 32
REL_MAX_DIST = 1024
S5_GROUP = 16
S5_WIDTH = 768
S5_GROUPS = 48
S5_STATE = 64
RET_HEADS = 6
RET_QK_DIM = 64
RET_V_DIM = 128
RET_CHUNK = 128
ROPE_BASE = 10000.0
MEM_HEADS = 4
MEM_HEAD_DIM = 192
MEM_HEAD_PAD = 256
N_EXPERTS = 32
TOP_K = 4
D_FF = 1024
SWIGLU_LIMIT = 7.0
SWIGLU_ALPHA = 1.702
LN_EPS = 1e-5
DEEPNORM_ALPHA = (2 * DEPTH) ** 0.25
NEG_INF = -1e30

LANE = 128
VMEM_LIMIT = 56 * 1024 * 1024
S5_CHUNK = 128
MOE_BLOCK_ROWS = 512
SC_GATHER_WINDOW = 64

_SEG = {}
_off = 0
for _name, _width, _align in (
        ("gate", 32, 8), ("mq", 8, 8), ("aq", 2, 2), ("ak", 2, 2), ("av", 2, 2), ("pad", 2, 2),
        ("rv", 6, 6), ("rg", 6, 6), ("rq", 3, 3), ("rqs", 3, 3), ("rk", 3, 3), ("rks", 3, 3)):
    assert _off % _align == 0, (_name, _off)
    _SEG[_name] = _off
    _off += _width
NP_UNITS = _off
NP = NP_UNITS * LANE
assert NP % 256 == 0
GW = A_HEADS_PER_GROUP * A_HEAD_DIM
ND = 2 * 3 * GW


def _cparams(sem, vmem=VMEM_LIMIT):
    return pltpu.CompilerParams(dimension_semantics=sem, vmem_limit_bytes=vmem)


def _sigmoid(x):
    return 0.5 * jnp.tanh(0.5 * x) + 0.5


def _ln(h, g, b):
    mu = jnp.mean(h, axis=-1, keepdims=True)
    d = h - mu
    var = jnp.mean(d * d, axis=-1, keepdims=True)
    return d * lax.rsqrt(var + LN_EPS) * g + b


def _ln_kernel(x_ref, g_ref, b_ref, o_ref):
    o_ref[...] = _ln(x_ref[...], g_ref[...], b_ref[...])


def _layer_norm_call(x, g, b):
    t = x.shape[0]
    tm = 512
    return pl.pallas_call(
        _ln_kernel, name="input_ln",
        grid=(t // tm,),
        in_specs=[pl.BlockSpec((tm, D_MODEL), lambda i: (i, 0)),
                  pl.BlockSpec((1, D_MODEL), lambda i: (0, 0)),
                  pl.BlockSpec((1, D_MODEL), lambda i: (0, 0))],
        out_specs=pl.BlockSpec((tm, D_MODEL), lambda i: (i, 0)),
        out_shape=jax.ShapeDtypeStruct((t, D_MODEL), F32),
        compiler_params=_cparams(("parallel",)),
    )(x, g.reshape(1, -1), b.reshape(1, -1))


def _proj_kernel(x_ref, w_ref, p_ref):
    p_ref[...] = jnp.dot(x_ref[...].astype(BF16), w_ref[...], preferred_element_type=F32).astype(BF16)


def _proj_aux_kernel(x_ref, wsut_ref, wd_ref, ut_ref, pd_ref):
    xb = x_ref[...].astype(BF16)
    ut_ref[...] = lax.dot_general(wsut_ref[...], xb, (((1,), (1,)), ((), ())),
                                  preferred_element_type=F32).astype(BF16)
    pd = jnp.dot(xb, wd_ref[...], preferred_element_type=F32)
    for c in range(ND // LANE):
        pd_ref[c] = pd[:, c * LANE:(c + 1) * LANE]


def _proj_call(x, w_p, w_su_t, w_d):
    t = x.shape[0]
    tm, tn = 512, NP // 3
    proj = pl.pallas_call(
        _proj_kernel, name="in_proj",
        grid=(NP // tn, t // tm),
        in_specs=[pl.BlockSpec((tm, D_MODEL), lambda j, i: (i, 0)),
                  pl.BlockSpec((D_MODEL, tn), lambda j, i: (0, j))],
        out_specs=pl.BlockSpec((tm, tn), lambda j, i: (i, j)),
        out_shape=jax.ShapeDtypeStruct((t, NP), BF16),
        compiler_params=_cparams(("parallel", "parallel")),
    )(x, w_p)
    ut, pd = pl.pallas_call(
        _proj_aux_kernel, name="in_proj_s5_dilated",
        grid=(t // tm,),
        in_specs=[pl.BlockSpec((tm, D_MODEL), lambda i: (i, 0)),
                  pl.BlockSpec((S5_WIDTH, D_MODEL), lambda i: (0, 0)),
                  pl.BlockSpec((D_MODEL, ND), lambda i: (0, 0))],
        out_specs=[pl.BlockSpec((S5_WIDTH, tm), lambda i: (0, i)),
                   pl.BlockSpec((ND // LANE, tm, LANE), lambda i: (0, i, 0))],
        out_shape=[jax.ShapeDtypeStruct((S5_WIDTH, t), BF16),
                   jax.ShapeDtypeStruct((ND // LANE, t, LANE), F32)],
        compiler_params=_cparams(("parallel",)),
    )(x, w_su_t, w_d)
    return proj, ut, pd


ATTN_TQ = 2 * A_RADIUS
ATTN_PAIR = 4


def _banded_heads(blocks, bias_ref, sub_len):
    tq, nk = blocks[0][0].shape[0], blocks[0][1].shape[0]
    width = blocks[0][0].shape[1]
    col = lax.broadcasted_iota(jnp.int32, (tq, nk), 1)
    valids = [(blk[3] + col >= 0) & (blk[3] + col < sub_len) for blk in blocks]
    lane_head = lax.broadcasted_iota(jnp.int32, (1, width), 1) // A_HEAD_DIM
    scale = jnp.asarray(A_HEAD_DIM ** -0.5, BF16)
    pairs = [(bk, h) for bk in range(len(blocks)) for h in range(A_HEADS_PER_GROUP)]
    qs = [jnp.where(lane_head == h, blocks[bk][0], jnp.zeros_like(blocks[bk][0])) * scale for bk, h in pairs]
    ss = [lax.dot_general(qh, blocks[bk][1], (((1,), (1,)), ((), ())), preferred_element_type=F32)
          for qh, (bk, h) in zip(qs, pairs)]
    ss = [jnp.where(valids[bk], s + bias_ref[h], NEG_INF) for s, (bk, h) in zip(ss, pairs)]
    ms = [jnp.max(s, axis=-1, keepdims=True) for s in ss]
    ps = [jnp.exp(s - m) for s, m in zip(ss, ms)]
    dens = [jnp.sum(p, axis=-1, keepdims=True) for p in ps]
    ohs = [jnp.dot(p.astype(BF16), blocks[bk][2], preferred_element_type=F32) for p, (bk, h) in zip(ps, pairs)]
    outs = []
    for bk in range(len(blocks)):
        acc = jnp.zeros((tq, width), F32)
        lse_acc = jnp.zeros((tq, width), F32)
        for j, (pb, h) in enumerate(pairs):
            if pb == bk:
                hm = lane_head == h
                acc = jnp.where(hm, ohs[j] / dens[j], acc)
                lse_acc = jnp.where(hm, ms[j] + jnp.log(dens[j]), lse_acc)
        outs.append((acc, lse_acc))
    return outs


def _attn_kernel(q_ref, kp_ref, kc_ref, kn_ref, vp_ref, vc_ref, vn_ref, bias_ref, o_ref, lse_ref, *, sub_len):
    tq = ATTN_TQ
    kcat = jnp.concatenate([kp_ref[...], kc_ref[...], kn_ref[...]], axis=0)
    vcat = jnp.concatenate([vp_ref[...], vc_ref[...], vn_ref[...]], axis=0)
    first = pl.program_id(1) * (ATTN_PAIR * tq) - A_RADIUS
    blocks = [(q_ref[j * tq:(j + 1) * tq, :], kcat[j * tq:j * tq + 2 * tq, :], vcat[j * tq:j * tq + 2 * tq, :],
               first + j * tq) for j in range(ATTN_PAIR)]
    for j, (acc, lse) in enumerate(_banded_heads(blocks, bias_ref, sub_len)):
        o_ref[j * tq:(j + 1) * tq, :] = acc
        lse_ref[j * tq:(j + 1) * tq, :] = lse


def _attn_call(proj, bias, b, s):
    rows = ATTN_PAIR * ATTN_TQ
    nstep = s // rows
    edge = rows // A_RADIUS
    pv = proj.reshape(b, s, NP)
    qo, ko, vo = (_SEG[n] * LANE // GW for n in ("aq", "ak", "av"))
    blk, eblk = (None, rows, GW), (None, A_RADIUS, GW)
    cur = lambda off: (lambda bi, i: (bi, i, off))
    prv = lambda off: (lambda bi, i: (bi, jnp.maximum(i * edge - 1, 0), off))
    nxt = lambda off: (lambda bi, i: (bi, jnp.minimum((i + 1) * edge, nstep * edge - 1), off))
    out_map = lambda bi, i: (bi, i, 0)
    o, lse = pl.pallas_call(
        functools.partial(_attn_kernel, sub_len=s), name="window_attn_g0",
        grid=(b, nstep),
        in_specs=[pl.BlockSpec(blk, cur(qo)),
                  pl.BlockSpec(eblk, prv(ko)), pl.BlockSpec(blk, cur(ko)), pl.BlockSpec(eblk, nxt(ko)),
                  pl.BlockSpec(eblk, prv(vo)), pl.BlockSpec(blk, cur(vo)), pl.BlockSpec(eblk, nxt(vo)),
                  pl.BlockSpec((A_HEADS_PER_GROUP, ATTN_TQ, 2 * ATTN_TQ), lambda bi, i: (0, 0, 0))],
        out_specs=[pl.BlockSpec(blk, out_map), pl.BlockSpec(blk, out_map)],
        out_shape=[jax.ShapeDtypeStruct((b, s, GW), F32), jax.ShapeDtypeStruct((b, s, GW), F32)],
        compiler_params=_cparams(("parallel", "parallel")),
    )(pv, pv, pv, pv, pv, pv, pv, bias)
    return o.reshape(b * s, GW), lse.reshape(b * s, GW)


def _dil_attn_kernel(q_ref, kp_ref, kc_ref, kn_ref, vp_ref, vc_ref, vn_ref, bias_ref, o_ref, lse_ref, *,
                     dil, sub_len):
    tq, half = ATTN_TQ, A_RADIUS
    first_key = pl.program_id(1) * tq - half

    def residues(it, carry):
        def rows(ref, r, n):
            return jnp.concatenate([ref[c, pl.ds(r, n, stride=dil), :] for c in range(GW // LANE)], axis=-1)

        rs = [it * ATTN_PAIR + j for j in range(ATTN_PAIR)]
        blocks = []
        for r in rs:
            q = rows(q_ref, r, tq).astype(BF16)
            kwin = jnp.concatenate([rows(kp_ref, r, half), rows(kc_ref, r, tq), rows(kn_ref, r, half)],
                                   axis=0).astype(BF16)
            vwin = jnp.concatenate([rows(vp_ref, r, half), rows(vc_ref, r, tq), rows(vn_ref, r, half)],
                                   axis=0).astype(BF16)
            blocks.append((q, kwin, vwin, first_key))
        for r, (acc, lse) in zip(rs, _banded_heads(blocks, bias_ref, sub_len)):
            for c in range(GW // LANE):
                o_ref[c, pl.ds(r, tq, stride=dil), :] = acc[:, c * LANE:(c + 1) * LANE]
                lse_ref[c, pl.ds(r, tq, stride=dil), :] = lse[:, c * LANE:(c + 1) * LANE]
        return carry

    lax.fori_loop(0, dil // ATTN_PAIR, residues, 0)


def _dil_attn_call(pd, bias, b, s, g, dil):
    tq = ATTN_TQ
    rows = tq * dil
    nblk = s // rows
    nslab = GW // LANE
    pv = pd.reshape(ND // LANE, b, s, LANE)
    qo, ko, vo = (3 * (g - 1) + j for j in range(3))
    blk, hblk = (nslab, None, rows, LANE), (nslab, None, rows // 2, LANE)
    cur = lambda off: (lambda bi, i: (off, bi, i, 0))
    prv = lambda off: (lambda bi, i: (off, bi, jnp.maximum(2 * i - 1, 0), 0))
    nxt = lambda off: (lambda bi, i: (off, bi, jnp.minimum(2 * i + 2, 2 * nblk - 1), 0))
    out_map = lambda bi, i: (0, bi, i, 0)
    o, lse = pl.pallas_call(
        functools.partial(_dil_attn_kernel, dil=dil, sub_len=s // dil), name=f"dilated_attn_g{g}",
        grid=(b, nblk),
        in_specs=[pl.BlockSpec(blk, cur(qo)),
                  pl.BlockSpec(hblk, prv(ko)), pl.BlockSpec(blk, cur(ko)), pl.BlockSpec(hblk, nxt(ko)),
                  pl.BlockSpec(hblk, prv(vo)), pl.BlockSpec(blk, cur(vo)), pl.BlockSpec(hblk, nxt(vo)),
                  pl.BlockSpec((A_HEADS_PER_GROUP, tq, 2 * tq), lambda bi, i: (0, 0, 0))],
        out_specs=[pl.BlockSpec(blk, out_map), pl.BlockSpec(blk, out_map)],
        out_shape=[jax.ShapeDtypeStruct((nslab, b, s, LANE), F32)] * 2,
        compiler_params=_cparams(("parallel", "parallel")),
    )(pv, pv, pv, pv, pv, pv, pv, bias)
    return o.reshape(nslab, b * s, LANE), lse.reshape(nslab, b * s, LANE)


def _t5_bucket(rel):
    nb = REL_BUCKETS // 2
    max_exact = nb // 2
    ret = jnp.where(rel > 0, nb, 0)
    n = jnp.abs(rel)
    nf = jnp.maximum(n, 1).astype(F32)
    large = max_exact + (jnp.log(nf / max_exact) / math.log(REL_MAX_DIST / max_exact)
                         * (nb - max_exact)).astype(jnp.int32)
    large = jnp.minimum(large, nb - 1)
    return ret + jnp.where(n < max_exact, n, large)


def _attn_bias_table(rel_bias, g, dil, tq=128):
    qi = jnp.arange(tq)[:, None]
    kj = jnp.arange(2 * tq)[None, :] - A_RADIUS
    rel = kj - qi
    tab = rel_bias[:, g * A_HEADS_PER_GROUP:(g + 1) * A_HEADS_PER_GROUP].astype(F32)
    bucket = _t5_bucket(rel * dil)[None]
    bias = jnp.zeros((A_HEADS_PER_GROUP,) + rel.shape, F32)
    for bkt in range(REL_BUCKETS):
        bias = jnp.where(bucket == bkt, tab[bkt][:, None, None], bias)
    return jnp.where((jnp.abs(rel) <= A_RADIUS)[None], bias, NEG_INF)


def _s5_toeplitz_kernel(bcf_ref, pwf_ref, bcb_ref, pwb_ref, d_ref, t_ref, gf_ref, gb_ref):
    gf_ref[...] = jnp.dot(bcf_ref[...], pwf_ref[...], precision=HIGHEST, preferred_element_type=F32)
    gb_ref[...] = jnp.dot(bcb_ref[...], pwb_ref[...], precision=HIGHEST, preferred_element_type=F32)
    c = S5_CHUNK
    row = lax.broadcasted_iota(jnp.int32, (c, c), 0)
    col = lax.broadcasted_iota(jnp.int32, (c, c), 1)

    def body(ci, carry):
        dval = d_ref[pl.ds(ci, 1), :]
        for co in range(S5_GROUP):
            r = ci * S5_GROUP + co
            gf = jnp.broadcast_to(gf_ref[pl.ds(r, 1), :], (c, c))
            gb = jnp.broadcast_to(gb_ref[pl.ds(r, 1), :], (c, c))
            tf = pltpu.roll(gf, 0, 1, stride=1, stride_axis=0)
            tb = pltpu.roll(gb, 1, 1, stride=1, stride_axis=0)
            tile = jnp.where(col >= row, tf, 0.0) + jnp.where(row >= col, tb, 0.0)
            tile = tile + jnp.where((row == col) & (ci == co), dval, 0.0)
            t_ref[pl.ds(pl.multiple_of(ci * c, c), c), co * c:(co + 1) * c] = tile.astype(BF16)
        return carry

    lax.fori_loop(0, S5_GROUP, body, 0)


def _s5_toeplitz_call(bcf, pwf, bcb, pwb, dskip):
    n = S5_GROUP * S5_CHUNK
    return pl.pallas_call(
        _s5_toeplitz_kernel, name="s5_toeplitz",
        grid=(S5_GROUPS,),
        in_specs=[pl.BlockSpec((None, S5_GROUP * S5_GROUP, 2 * S5_STATE), lambda g: (g, 0, 0)),
                  pl.BlockSpec((None, 2 * S5_STATE, S5_CHUNK), lambda g: (g, 0, 0)),
                  pl.BlockSpec((None, S5_GROUP * S5_GROUP, 2 * S5_STATE), lambda g: (g, 0, 0)),
                  pl.BlockSpec((None, 2 * S5_STATE, S5_CHUNK), lambda g: (g, 0, 0)),
                  pl.BlockSpec((None, S5_GROUP, LANE), lambda g: (g, 0, 0))],
        out_specs=pl.BlockSpec((None, n, n), lambda g: (g, 0, 0)),
        out_shape=jax.ShapeDtypeStruct((S5_GROUPS, n, n), BF16),
        scratch_shapes=[pltpu.VMEM((S5_GROUP * S5_GROUP, S5_CHUNK), F32),
                        pltpu.VMEM((S5_GROUP * S5_GROUP, S5_CHUNK), F32)],
        compiler_params=_cparams(("parallel",)),
    )(bcf, pwf, bcb, pwb, dskip)


def _s5_tables(lam_re, lam_im, log_step, b_re, b_im, c_re, c_im, d_skip):
    c = S5_CHUNK
    k = jnp.arange(c, dtype=F32)
    per_dir = []
    for direction in range(2):
        step = jnp.exp(log_step[direction].astype(F32))[:, None]
        lr, li = lam_re[direction].astype(F32), lam_im[direction].astype(F32)
        mag = jnp.exp(lr * step)
        ar, ai = mag * jnp.cos(li * step), mag * jnp.sin(li * step)
        nr, ni = ar - 1.0, ai
        den = lr * lr + li * li
        fr = (nr * lr + ni * li) / den
        fi = (ni * lr - nr * li) / den
        br, bi = b_re[direction].astype(F32), b_im[direction].astype(F32)
        bbr = fr[..., None] * br - fi[..., None] * bi
        bbi = fr[..., None] * bi + fi[..., None] * br
        cr, cim = c_re[direction].astype(F32), c_im[direction].astype(F32)

        def power(e, log_mag=lr * step, phase=li * step):
            m = jnp.exp(e[None, None, :] * log_mag[..., None])
            th = e[None, None, :] * phase[..., None]
            return m * jnp.cos(th), m * jnp.sin(th)

        per_dir.append(dict(bbr=bbr, bbi=bbi, cr=cr, cim=cim, power=power))

    def bc_table(d):
        bbr_t = jnp.transpose(d["bbr"], (0, 2, 1))[:, :, None, :]
        bbi_t = jnp.transpose(d["bbi"], (0, 2, 1))[:, :, None, :]
        cr, cim = d["cr"][:, None], d["cim"][:, None]
        re = bbr_t * cr - bbi_t * cim
        im = bbr_t * cim + bbi_t * cr
        return jnp.concatenate([re, -im], axis=-1).reshape(S5_GROUPS, S5_GROUP * S5_GROUP, 2 * S5_STATE)

    f, bw = per_dir
    pfr, pfi = f["power"](k)
    pbr, pbi = bw["power"](c - 1 - k)
    pwf = jnp.concatenate([pfr, pfi], axis=1)
    pwb = jnp.concatenate([pbr, pbi], axis=1)
    dsk = jnp.broadcast_to(d_skip.astype(F32).reshape(S5_GROUPS, S5_GROUP, 1), (S5_GROUPS, S5_GROUP, LANE))
    toep = _s5_toeplitz_call(bc_table(f), pwf, bc_table(bw), pwb, dsk)

    seg4 = lambda f_re, b_re, f_im, b_im, axis: jnp.concatenate([f_re, b_re, f_im, b_im], axis=axis)
    tp = lambda z: jnp.transpose(z, (0, 2, 1))

    pfr_i, pfi_i = f["power"](c - 1 - k)
    pbr_i, pbi_i = bw["power"](k)
    a_in = seg4(tp(f["bbr"]), tp(bw["bbr"]), tp(f["bbr"]), tp(bw["bbr"]), -1)
    b_in = seg4(-tp(f["bbi"]), -tp(bw["bbi"]), tp(f["bbi"]), tp(bw["bbi"]), -1)
    x_in = seg4(tp(pfr_i), tp(pbr_i), tp(pfi_i), tp(pbi_i), -1)
    y_in = seg4(tp(pfi_i), tp(pbi_i), tp(pfr_i), tp(pbr_i), -1)
    win = (a_in[:, :, None, :] * x_in[:, None, :, :] + b_in[:, :, None, :] * y_in[:, None, :, :]
           ).reshape(S5_GROUPS, S5_GROUP * c, 4 * S5_STATE).astype(BF16)

    pfr_o, pfi_o = f["power"](k + 1.0)
    pbr_o, pbi_o = bw["power"](c - k)
    a_out = seg4(tp(f["cr"]), tp(bw["cr"]), -tp(f["cr"]), -tp(bw["cr"]), 1)
    b_out = seg4(-tp(f["cim"]), -tp(bw["cim"]), -tp(f["cim"]), -tp(bw["cim"]), 1)
    x_out = seg4(pfr_o, pbr_o, pfi_o, pbi_o, 1)
    y_out = seg4(pfi_o, pbi_o, pfr_o, pbr_o, 1)
    wout = (a_out[:, :, :, None] * x_out[:, :, None, :] + b_out[:, :, :, None] * y_out[:, :, None, :]
            ).reshape(S5_GROUPS, 4 * S5_STATE, S5_GROUP * c).astype(BF16)

    cc = jnp.asarray([float(c)], F32)
    afr, afi = f["power"](cc)
    abr, abi = bw["power"](cc)
    a_re = jnp.concatenate([afr[..., 0], abr[..., 0]], axis=-1).reshape(1, -1)
    a_im = jnp.concatenate([afi[..., 0], abi[..., 0]], axis=-1).reshape(1, -1)
    return toep, win, wout, a_re, a_im


def _s5_contrib_kernel(u_ref, win_ref, re_ref, im_ref):
    uc = jnp.concatenate([u_ref[ci] for ci in range(S5_GROUP)], axis=-1)
    res = jnp.dot(uc, win_ref[...], preferred_element_type=F32)
    nc, nb = re_ref.shape[0], re_ref.shape[1]
    half = 2 * S5_STATE
    for bi in range(nb):
        re_ref[:, bi, :] = res[bi * nc:(bi + 1) * nc, :half]
        im_ref[:, bi, :] = res[bi * nc:(bi + 1) * nc, half:]


def _s5_contrib_call(ut, win, b, nc):
    nch = b * nc
    half = 2 * S5_STATE
    uv = ut.reshape(S5_GROUPS, S5_GROUP, nch, S5_CHUNK)
    return pl.pallas_call(
        _s5_contrib_kernel, name="s5_contrib",
        grid=(S5_GROUPS,),
        in_specs=[pl.BlockSpec((None, S5_GROUP, nch, S5_CHUNK), lambda g: (g, 0, 0, 0)),
                  pl.BlockSpec((None, S5_GROUP * S5_CHUNK, 4 * S5_STATE), lambda g: (g, 0, 0))],
        out_specs=[pl.BlockSpec((nc, b, half), lambda g: (0, 0, g))] * 2,
        out_shape=[jax.ShapeDtypeStruct((nc, b, S5_GROUPS * half), F32)] * 2,
        compiler_params=_cparams(("parallel",)),
    )(uv, win)


def _s5_scan_kernel(cre_ref, cim_ref, are_ref, aim_ref, ore_ref, oim_ref):
    nc, nb, width = cre_ref.shape
    a_re, a_im = are_ref[...], aim_ref[...]
    lane = lax.broadcasted_iota(jnp.int32, (1, width), 1)
    fwd_lane = (lane % (2 * S5_STATE)) < S5_STATE
    zero = jnp.zeros((nb, width), F32)

    def advance(s, c):
        sr, si = s
        return a_re * sr - a_im * si + cre_ref[c], a_re * si + a_im * sr + cim_ref[c]

    def up(c, s):
        ore_ref[c] = s[0]
        oim_ref[c] = s[1]
        return advance(s, c)

    lax.fori_loop(0, nc, up, (zero, zero))

    def down(i, s):
        c = nc - 1 - i
        ore_ref[c] = jnp.where(fwd_lane, ore_ref[c], s[0])
        oim_ref[c] = jnp.where(fwd_lane, oim_ref[c], s[1])
        return advance(s, c)

    lax.fori_loop(0, nc, down, (zero, zero))


def _s5_scan_call(cre, cim, a_re, a_im):
    return pl.pallas_call(
        _s5_scan_kernel, name="s5_chunk_scan",
        out_shape=[jax.ShapeDtypeStruct(cre.shape, F32)] * 2,
        compiler_params=pltpu.CompilerParams(vmem_limit_bytes=VMEM_LIMIT),
    )(cre, cim, a_re, a_im)


def _gelu_tanh(y):
    return 0.5 * y * (1.0 + jnp.tanh(math.sqrt(2.0 / math.pi) * (y + 0.044715 * (y * y * y))))


def _s5_out_kernel(u_ref, t_ref, sre_ref, sim_ref, wout_ref, z_ref):
    uc = jnp.concatenate([u_ref[ci] for ci in range(S5_GROUP)], axis=-1)
    y = jnp.dot(uc, t_ref[...], preferred_element_type=F32)
    sp = jnp.concatenate([jnp.concatenate([sre_ref[:, bi, :], sim_ref[:, bi, :]], axis=-1)
                          for bi in range(sre_ref.shape[1])], axis=0)
    y = y + jnp.dot(sp.astype(BF16), wout_ref[...], preferred_element_type=F32)
    z = _gelu_tanh(y)
    for co in range(S5_GROUP):
        z_ref[co] = z[:, co * S5_CHUNK:(co + 1) * S5_CHUNK].astype(BF16)


def _s5_out_call(ut, toep, s_re, s_im, wout, b, nc):
    nch = b * nc
    uv = ut.reshape(S5_GROUPS, S5_GROUP, nch, S5_CHUNK)
    n = S5_GROUP * S5_CHUNK
    half = 2 * S5_STATE
    zt = pl.pallas_call(
        _s5_out_kernel, name="s5_out",
        grid=(S5_GROUPS,),
        in_specs=[pl.BlockSpec((None, S5_GROUP, nch, S5_CHUNK), lambda g: (g, 0, 0, 0)),
                  pl.BlockSpec((None, n, n), lambda g: (g, 0, 0)),
                  pl.BlockSpec((nc, b, half), lambda g: (0, 0, g)),
                  pl.BlockSpec((nc, b, half), lambda g: (0, 0, g)),
                  pl.BlockSpec((None, 4 * S5_STATE, n), lambda g: (g, 0, 0))],
        out_specs=pl.BlockSpec((None, S5_GROUP, nch, S5_CHUNK), lambda g: (g, 0, 0, 0)),
        out_shape=jax.ShapeDtypeStruct((S5_GROUPS, S5_GROUP, nch, S5_CHUNK), BF16),
        compiler_params=_cparams(("parallel",)),
    )(uv, toep, s_re, s_im, wout)
    return zt.reshape(S5_WIDTH, nch * S5_CHUNK)


def _ret_tables(s):
    c = RET_CHUNK
    half = RET_QK_DIM // 2
    inv = ROPE_BASE ** (-jnp.arange(half, dtype=F32) / half)
    ang = jnp.arange(s, dtype=F32)[:, None] * inv[None, :]
    cos = jnp.tile(jnp.cos(ang), (1, 2 * RET_HEADS))
    sin = jnp.tile(jnp.sin(ang), (1, 2 * RET_HEADS))
    hidx = jnp.arange(RET_HEADS, dtype=F32)
    lgf = jnp.log1p(-jnp.exp2(-5.0 - hidx))
    lgb = jnp.log1p(-jnp.exp2(-5.5 - hidx))
    j = jnp.arange(c, dtype=F32)
    rel = j[:, None] - j[None, :]
    intra = jnp.exp(jnp.abs(rel)[None] * jnp.where(rel[None] >= 0, lgf[:, None, None], lgb[:, None, None]))
    qk_f = jnp.repeat(lgf, RET_QK_DIM)[None, :]
    qk_b = jnp.repeat(lgb, RET_QK_DIM)[None, :]
    v_f = jnp.repeat(lgf, RET_V_DIM)[None, :]
    v_b = jnp.repeat(lgb, RET_V_DIM)[None, :]
    tabs = dict(
        cos=cos, sin=sin, intra=intra,
        tail_f=jnp.exp((c - 1 - j)[:, None] * qk_f), head_b=jnp.exp(j[:, None] * qk_b),
        decq_f=jnp.exp((j + 1.0)[:, None] * qk_f), decq_b=jnp.exp((c - j)[:, None] * qk_b),
        chunk_f=jnp.exp(c * v_f), chunk_b=jnp.exp(c * v_b),
        bd=(jnp.arange(RET_HEADS * RET_QK_DIM)[:, None] // RET_QK_DIM
            == jnp.arange(RET_HEADS * RET_V_DIM)[None, :] // RET_V_DIM).astype(F32))
    return tabs


def _rot(x_ref, xs_ref, cos_ref, sin_ref):
    return x_ref[...].astype(F32) * cos_ref[...] + xs_ref[...].astype(F32) * sin_ref[...]


def _ret_kernel(q_ref, qs_ref, k_ref, ks_ref, v_ref, g_ref, cos_ref, sin_ref, intra_ref, tailf_ref, headb_ref,
                decqf_ref, decqb_ref, chunkf_ref, chunkb_ref, o_ref, st_ref, sfd_ref, sbd_ref, sball_ref, *, nc):
    i = pl.program_id(1)
    qk, dv = RET_QK_DIM, RET_V_DIM
    vb = v_ref[...]
    kr = _rot(k_ref, ks_ref, cos_ref, sin_ref) * (RET_QK_DIM ** -0.5)

    def update_state(weight_ref, decay_ref):
        kwt = jnp.transpose(kr * weight_ref[...])
        for h in range(RET_HEADS):
            kv = jnp.dot(kwt[h * qk:(h + 1) * qk, :].astype(BF16), vb[:, h * dv:(h + 1) * dv],
                         preferred_element_type=F32)
            st_ref[h] = st_ref[h] * decay_ref[:, h * dv:(h + 1) * dv] + kv

    @pl.when((i == 0) | (i == nc))
    def _():
        st_ref[...] = jnp.zeros_like(st_ref)

    @pl.when(i == 0)
    def _():
        sfd_ref[...] = jnp.zeros_like(sfd_ref)
        sbd_ref[...] = jnp.zeros_like(sbd_ref)

    @pl.when(i < nc)
    def _():
        c = nc - 1 - i
        sball_ref[c] = st_ref[...].astype(BF16)
        update_state(headb_ref, chunkb_ref)

    @pl.when(i >= nc)
    def _():
        c = i - nc
        for h in range(RET_HEADS):
            sfd_ref[h * qk:(h + 1) * qk, h * dv:(h + 1) * dv] = st_ref[h].astype(BF16)
            sbd_ref[h * qk:(h + 1) * qk, h * dv:(h + 1) * dv] = sball_ref[c, h]
        qr = _rot(q_ref, qs_ref, cos_ref, sin_ref)
        qb, kb = qr.astype(BF16), kr.astype(BF16)
        lane_head = lax.broadcasted_iota(jnp.int32, (1, qb.shape[1]), 1) // qk
        inter = jnp.dot((qr * decqf_ref[...]).astype(BF16), sfd_ref[...], preferred_element_type=F32)
        inter = inter + jnp.dot((qr * decqb_ref[...]).astype(BF16), sbd_ref[...], preferred_element_type=F32)
        heads = range(RET_HEADS)
        vcols = [slice(h * dv, (h + 1) * dv) for h in heads]
        qhs = [jnp.where(lane_head == h, qb, jnp.zeros_like(qb)) for h in heads]
        scs = [lax.dot_general(qh, kb, (((1,), (1,)), ((), ())), preferred_element_type=F32) * intra_ref[h]
               for h, qh in zip(heads, qhs)]
        ohs = [jnp.dot(sc.astype(BF16), vb[:, vs], preferred_element_type=F32) + inter[:, vs]
               for sc, vs in zip(scs, vcols)]
        mus = [jnp.mean(oh, axis=-1, keepdims=True) for oh in ohs]
        dds = [oh - mu for oh, mu in zip(ohs, mus)]
        vrs = [jnp.mean(dd * dd, axis=-1, keepdims=True) for dd in dds]
        on = jnp.concatenate([dd * lax.rsqrt(var + LN_EPS) for dd, var in zip(dds, vrs)], axis=-1)
        gg = g_ref[...].astype(F32)
        o_ref[...] = (on * (gg * _sigmoid(gg))).astype(BF16)
        update_state(tailf_ref, chunkf_ref)


def _retention_call(proj, tabs, b, s):
    c = RET_CHUNK
    nc = s // c
    qkw, vw = RET_HEADS * RET_QK_DIM, RET_HEADS * RET_V_DIM
    pv = proj.reshape(b, s, NP)
    qk_blk, v_blk = (None, c, qkw), (None, c, vw)
    qo, qso, ko, kso = (_SEG[n] * LANE // qkw for n in ("rq", "rqs", "rk", "rks"))
    vo, go = (_SEG[n] * LANE // vw for n in ("rv", "rg"))
    const2 = lambda shape: pl.BlockSpec(shape, lambda bi, i: (0, 0))
    chunk = lambda i: jnp.where(i < nc, nc - 1 - i, i - nc)
    fchunk = lambda i: jnp.maximum(i - nc, 0)
    both = lambda off: (lambda bi, i: (bi, chunk(i), off))
    fonly = lambda off: (lambda bi, i: (bi, fchunk(i), off))
    yc = pl.pallas_call(
        functools.partial(_ret_kernel, nc=nc), name="retention",
        grid=(b, 2 * nc),
        in_specs=[pl.BlockSpec(qk_blk, fonly(qo)), pl.BlockSpec(qk_blk, fonly(qso)),
                  pl.BlockSpec(qk_blk, both(ko)), pl.BlockSpec(qk_blk, both(kso)),
                  pl.BlockSpec(v_blk, both(vo)), pl.BlockSpec(v_blk, fonly(go)),
                  pl.BlockSpec((c, qkw), lambda bi, i: (chunk(i), 0)),
                  pl.BlockSpec((c, qkw), lambda bi, i: (chunk(i), 0)),
                  pl.BlockSpec((RET_HEADS, c, c), lambda bi, i: (0, 0, 0)),
                  const2((c, qkw)), const2((c, qkw)), const2((c, qkw)), const2((c, qkw)),
                  const2((1, vw)), const2((1, vw))],
        out_specs=pl.BlockSpec((None, c, vw), lambda bi, i: (bi, fchunk(i), 0)),
        out_shape=jax.ShapeDtypeStruct((b, s, vw), BF16),
        scratch_shapes=[pltpu.VMEM((RET_HEADS, RET_QK_DIM, RET_V_DIM), F32),
                        pltpu.VMEM((qkw, vw), BF16), pltpu.VMEM((qkw, vw), BF16),
                        pltpu.VMEM((nc, RET_HEADS, RET_QK_DIM, RET_V_DIM), BF16)],
        compiler_params=_cparams(("parallel", "arbitrary")),
    )(pv, pv, pv, pv, pv, pv, tabs["cos"], tabs["sin"], tabs["intra"], tabs["tail_f"], tabs["head_b"],
      tabs["decq_f"], tabs["decq_b"], tabs["chunk_f"], tabs["chunk_b"])
    return yc.reshape(b * s, vw)


def _mm_kernel(x_ref, w_ref, o_ref):
    o_ref[...] = jnp.dot(x_ref[...].astype(BF16), w_ref[...], preferred_element_type=F32).astype(o_ref.dtype)


def _mem_kv_call(mem2d, w_kv):
    m, n = mem2d.shape[0], w_kv.shape[1]
    tm = 512
    return pl.pallas_call(
        _mm_kernel, name="mem_kv_proj",
        grid=(m // tm,),
        in_specs=[pl.BlockSpec((tm, D_MODEL), lambda i: (i, 0)),
                  pl.BlockSpec((D_MODEL, n), lambda i: (0, 0))],
        out_specs=pl.BlockSpec((tm, n), lambda i: (i, 0)),
        out_shape=jax.ShapeDtypeStruct((m, n), BF16),
        compiler_params=_cparams(("parallel",)),
    )(mem2d, w_kv)


def _mem_attn_kernel(q_ref, k_ref, v_ref, o_ref):
    q = q_ref[...]
    cols = [slice(h * MEM_HEAD_PAD, (h + 1) * MEM_HEAD_PAD) for h in range(MEM_HEADS)]
    ss = [lax.dot_general(q[:, hs], k_ref[:, hs], (((1,), (1,)), ((), ())),
                          preferred_element_type=F32) * (MEM_HEAD_DIM ** -0.5) for hs in cols]
    ms = [jnp.max(s, axis=-1, keepdims=True) for s in ss]
    es = [jnp.exp(s - m) for s, m in zip(ss, ms)]
    ps = [e / jnp.sum(e, axis=-1, keepdims=True) for e in es]
    outs = [jnp.dot(p.astype(BF16), v_ref[:, hs], preferred_element_type=F32) for p, hs in zip(ps, cols)]
    o_ref[...] = jnp.concatenate(outs, axis=-1).astype(BF16)


def _mem_attn_call(proj, kv, b, s):
    tq = 256
    w = MEM_HEADS * MEM_HEAD_PAD
    pv = proj.reshape(b, s, NP)
    kvv = kv.reshape(b, N_MEM, 2 * w)
    qo = _SEG["mq"] * LANE // w
    yd = pl.pallas_call(
        _mem_attn_kernel, name="mem_attn",
        grid=(b, s // tq),
        in_specs=[pl.BlockSpec((None, tq, w), lambda bi, i: (bi, i, qo)),
                  pl.BlockSpec((None, N_MEM, w), lambda bi, i: (bi, 0, 0)),
                  pl.BlockSpec((None, N_MEM, w), lambda bi, i: (bi, 0, 1))],
        out_specs=pl.BlockSpec((None, tq, w), lambda bi, i: (bi, i, 0)),
        out_shape=jax.ShapeDtypeStruct((b, s, w), BF16),
        compiler_params=_cparams(("parallel", "parallel")),
    )(pv, kvv, kvv)
    return yd.reshape(b * s, w)


def _merge_kernel(x_ref, ao0_ref, ao1_ref, ao2_ref, al0_ref, al1_ref, al2_ref, zt_ref, yc_ref, yd_ref, gl_ref,
                  wglut_ref, bglu_ref, wba_ref, wbb_ref, wbc_ref, wbd_ref, wout_ref, g1_ref, b1_ref,
                  x1_ref):
    tm = x_ref.shape[0]
    slabs = lambda ref: jnp.concatenate([ref[c] for c in range(ref.shape[0])], axis=-1)
    lses = [al0_ref[...], slabs(al1_ref), slabs(al2_ref)]
    aos = [ao0_ref[...], slabs(ao1_ref), slabs(ao2_ref)]
    mx = jnp.maximum(jnp.maximum(lses[0], lses[1]), lses[2])
    es = [jnp.exp(l - mx) for l in lses]
    tot = es[0] + es[1] + es[2]
    ya = jnp.concatenate([o * (e / tot) for o, e in zip(aos, es)], axis=-1).astype(BF16)
    zt = zt_ref[...]
    glu = jnp.dot(wglut_ref[...], zt, preferred_element_type=F32) + bglu_ref[...]
    ybt = zt.astype(F32) * _sigmoid(glu)
    yb = jnp.transpose(ybt).astype(BF16)
    merged = jnp.zeros((tm, D_MODEL), F32)
    for n, (y, w_ref) in enumerate(((ya, wba_ref), (yb, wbb_ref), (yc_ref[...], wbc_ref), (yd_ref[...], wbd_ref))):
        bo = jnp.dot(y, w_ref[...], preferred_element_type=F32)
        merged = merged + (jnp.tanh(gl_ref[:, n * D_MODEL:(n + 1) * D_MODEL].astype(F32)) + 1.0) * bo
    h = DEEPNORM_ALPHA * x_ref[...] + jnp.dot(merged.astype(BF16), wout_ref[...], preferred_element_type=F32)
    x1_ref[...] = _ln(h, g1_ref[...], b1_ref[...])


def _router_kernel(x1_ref, rwh_ref, rwl_ref, rb_ref, tri_ref, x1p_ref, route_ref, counts_ref, carry_ref):
    tm = x1_ref.shape[0]

    @pl.when(pl.program_id(0) == 0)
    def _():
        carry_ref[...] = jnp.zeros_like(carry_ref)

    x1 = x1_ref[...]
    x1p_ref[...] = _pack_bf16_pairs(x1)
    x_hi = x1.astype(BF16)
    x_lo = (x1 - x_hi.astype(F32)).astype(BF16)
    logits = (jnp.dot(x_hi, rwh_ref[...], preferred_element_type=F32)
              + jnp.dot(x_lo, rwh_ref[...], preferred_element_type=F32)
              + jnp.dot(x_hi, rwl_ref[...], preferred_element_type=F32)) + rb_ref[...]
    lane = lax.broadcasted_iota(jnp.int32, (tm, LANE), 1)
    work = logits
    sel, vals, hots = [], [], []
    for _ in range(TOP_K):
        mk = jnp.max(work, axis=-1, keepdims=True)
        ik = jnp.min(jnp.where(work == mk, lane, LANE), axis=-1, keepdims=True)
        hot = lane == ik
        work = jnp.where(hot, -jnp.inf, work)
        sel.append(ik)
        vals.append(mk)
        hots.append(hot)
    ex = [jnp.exp(v - vals[0]) for v in vals]
    den = ex[0] + ex[1] + ex[2] + ex[3]
    chosen = (hots[0] | hots[1] | hots[2] | hots[3]).astype(F32)
    before = jnp.dot(tri_ref[...], chosen.astype(BF16), preferred_element_type=F32) + carry_ref[0:1, :]
    route = jnp.zeros((tm, LANE), F32)
    for k in range(TOP_K):
        rank = jnp.sum(jnp.where(hots[k], before, 0.0), axis=-1, keepdims=True)
        route = jnp.where(lane == k, sel[k].astype(F32), route)
        route = jnp.where(lane == TOP_K + k, ex[k] / den, route)
        route = jnp.where(lane == 2 * TOP_K + k, rank, route)
    route_ref[...] = route
    new_carry = carry_ref[0:1, :] + jnp.sum(chosen, axis=0, keepdims=True)
    carry_ref[...] = jnp.broadcast_to(new_carry, carry_ref.shape)
    counts_ref[...] = jnp.broadcast_to(new_carry, counts_ref.shape)


def _merge_call(x, aos, alses, zt, yc, yd, proj, lw):
    t = x.shape[0]
    tm = 256
    gw = GW
    gate_blk = 4 * D_MODEL
    row = lambda w: pl.BlockSpec((tm, w), lambda i: (i, 0))
    slab = pl.BlockSpec((gw // LANE, tm, LANE), lambda i: (0, i, 0))
    full = lambda a: pl.BlockSpec(a.shape, lambda i: (0,) * a.ndim)
    consts = (lw["w_glu_t"], lw["b_glu"], lw["wb_a"], lw["wb_b"], lw["wb_c"], lw["wb_d"], lw["w_out"],
              lw["ln1_g"], lw["ln1_b"])
    return pl.pallas_call(
        _merge_kernel, name="branch_merge",
        grid=(t // tm,),
        in_specs=[row(D_MODEL)] + [row(gw), slab, slab] * 2
                 + [pl.BlockSpec((S5_WIDTH, tm), lambda i: (0, i)),
                  row(RET_HEADS * RET_V_DIM), row(MEM_HEADS * MEM_HEAD_PAD),
                  pl.BlockSpec((tm, gate_blk), lambda i: (i, _SEG["gate"] * LANE // gate_blk))]
                 + [full(a) for a in consts],
        out_specs=row(D_MODEL),
        out_shape=jax.ShapeDtypeStruct((t, D_MODEL), F32),
        compiler_params=_cparams(("parallel",)),
    )(x, *aos, *alses, zt, yc, yd, proj, *consts)


def _router_call(x1, lw):
    t = x1.shape[0]
    tm = 1024
    row = lambda w: pl.BlockSpec((tm, w), lambda i: (i, 0))
    full = lambda a: pl.BlockSpec(a.shape, lambda i: (0,) * a.ndim)
    tri = jnp.tril(jnp.ones((tm, tm), BF16), -1)
    consts = (lw["router_w_hi"], lw["router_w_lo"], lw["router_b"], tri)
    return pl.pallas_call(
        _router_kernel, name="moe_router",
        grid=(t // tm,),
        in_specs=[row(D_MODEL)] + [full(a) for a in consts],
        out_specs=[row(D_MODEL // 2), row(LANE), pl.BlockSpec((8, LANE), lambda i: (0, 0))],
        out_shape=[jax.ShapeDtypeStruct((t, D_MODEL // 2), jnp.uint32),
                   jax.ShapeDtypeStruct((t, LANE), F32),
                   jax.ShapeDtypeStruct((8, LANE), F32)],
        scratch_shapes=[pltpu.VMEM((8, LANE), F32)],
        compiler_params=_cparams(("arbitrary",)),
    )(x1, *consts)


def _pack_bf16_pairs(x):
    n = x.shape[1] // 2
    hi = lax.bitcast_convert_type(x[:, :n].astype(BF16).astype(F32), jnp.uint32)
    lo = lax.bitcast_convert_type(x[:, n:].astype(BF16).astype(F32), jnp.uint32)
    return hi | (lo >> 16)


def _unpack_bf16_pairs(w):
    hi = lax.bitcast_convert_type(w & jnp.uint32(0xFFFF0000), F32)
    lo = lax.bitcast_convert_type(w << 16, F32)
    return jnp.concatenate([hi, lo], axis=-1)


def _sc_index_rows(idx):
    window = SC_GATHER_WINDOW
    return jnp.pad(idx.reshape(-1, window), ((0, 0), (0, LANE - window)))


def _sc_scatter_rows(x, idx, n_out):
    n, d = x.shape
    m = idx.shape[0]
    window = SC_GATHER_WINDOW
    src_steps = n // window
    mesh = plsc.VectorSubcoreMesh(core_axis_name="core", subcore_axis_name="subcore")

    @pl.kernel(out_type=jax.ShapeDtypeStruct((n_out, d), x.dtype), mesh=mesh, scratch_types=[])
    def scatter_kernel(x_hbm, i_hbm, o_hbm):
        def body(x_vmem, i_vmem):
            pltpu.sync_copy(x_vmem, o_hbm.at[i_vmem.at[0, pl.ds(0, window)]])

        pltpu.emit_pipeline(
            body,
            grid=(m // window,),
            in_specs=[pl.BlockSpec((window, d), index_map=lambda i: (i % src_steps, 0)),
                      pl.BlockSpec((1, LANE), index_map=lambda i: (i, 0))],
            out_specs=[],
            core_axis_name=("core", "subcore"),
            dimension_semantics=(pltpu.PARALLEL,),
            trace_scopes=False,
        )(x_hbm, i_hbm)

    return scatter_kernel(x, _sc_index_rows(idx))


def _sc_gather_rows(x, idx):
    m, d = idx.shape[0], x.shape[1]
    window = SC_GATHER_WINDOW
    mesh = plsc.VectorSubcoreMesh(core_axis_name="core", subcore_axis_name="subcore")

    @pl.kernel(out_type=jax.ShapeDtypeStruct((m, d), x.dtype), mesh=mesh, scratch_types=[])
    def gather_kernel(x_hbm, i_hbm, o_hbm):
        def body(i_vmem, o_vmem):
            pltpu.sync_copy(x_hbm.at[i_vmem.at[0, pl.ds(0, window)]], o_vmem)

        pltpu.emit_pipeline(
            body,
            grid=(m // window,),
            in_specs=[pl.BlockSpec((1, LANE), index_map=lambda i: (i, 0))],
            out_specs=[pl.BlockSpec((window, d), index_map=lambda i: (i, 0))],
            core_axis_name=("core", "subcore"),
            dimension_semantics=(pltpu.PARALLEL,),
            trace_scopes=False,
        )(i_hbm, o_hbm)

    return gather_kernel(x, _sc_index_rows(idx))


def _moe_expert_kernel(be_ref, nu_ref, nv_ref, x_ref, w1_ref, b1_ref, w2_ref, b2_ref, o_ref, w1s_ref, w2s_ref):
    i = pl.program_id(0)

    @pl.when((i == 0) | (be_ref[i] != be_ref[jnp.maximum(i - 1, 0)]))
    def _():
        w1s_ref[...] = w1_ref[...].astype(BF16)
        w2s_ref[...] = w2_ref[...].astype(BF16)

    @pl.when(i < nu_ref[0])
    def _():
        rows = lax.broadcasted_iota(jnp.int32, (x_ref.shape[0], 1), 0)
        x = jnp.where(rows < nv_ref[i], _unpack_bf16_pairs(x_ref[...]), 0.0)
        h = jnp.dot(x.astype(BF16), w1s_ref[...], preferred_element_type=F32) + b1_ref[...]
        gate = jnp.minimum(h[:, :D_FF], SWIGLU_LIMIT)
        up = jnp.clip(h[:, D_FF:], -SWIGLU_LIMIT, SWIGLU_LIMIT)
        hid = gate * _sigmoid(SWIGLU_ALPHA * gate) * (up + 1.0)
        y = jnp.dot(hid.astype(BF16), w2s_ref[...], preferred_element_type=F32) + b2_ref[...]
        o_ref[...] = _pack_bf16_pairs(y)

    @pl.when(i >= nu_ref[0])
    def _():
        o_ref[...] = jnp.zeros_like(o_ref)


def _moe_expert_call(xg, block_expert, n_used, n_valid, layer, w1, b1, w2, b2):
    bm = MOE_BLOCK_ROWS
    n_blocks = xg.shape[0] // bm
    half = D_MODEL // 2
    grid_spec = pltpu.PrefetchScalarGridSpec(
        num_scalar_prefetch=3, grid=(n_blocks,),
        in_specs=[pl.BlockSpec((bm, half), lambda i, be, nu, nv: (i, 0)),
                  pl.BlockSpec((None, None, D_MODEL, 2 * D_FF), lambda i, be, nu, nv: (layer, be[i], 0, 0)),
                  pl.BlockSpec((None, 1, 2 * D_FF), lambda i, be, nu, nv: (be[i], 0, 0)),
                  pl.BlockSpec((None, None, D_FF, D_MODEL), lambda i, be, nu, nv: (layer, be[i], 0, 0)),
                  pl.BlockSpec((None, 1, D_MODEL), lambda i, be, nu, nv: (be[i], 0, 0))],
        out_specs=pl.BlockSpec((bm, half), lambda i, be, nu, nv: (i, 0)),
        scratch_shapes=[pltpu.VMEM((D_MODEL, 2 * D_FF), BF16), pltpu.VMEM((D_FF, D_MODEL), BF16)])
    return pl.pallas_call(
        _moe_expert_kernel, name="moe_experts",
        grid_spec=grid_spec,
        out_shape=jax.ShapeDtypeStruct((n_blocks * bm, half), jnp.uint32),
        compiler_params=_cparams(("arbitrary",)),
    )(block_expert, n_used, n_valid, xg, w1, b1, w2, b2)


def _moe_combine_kernel(x_ref, route_ref, y0_ref, y1_ref, y2_ref, y3_ref, g_ref, b_ref, o_ref):
    route = route_ref[...]
    moe = jnp.zeros(x_ref.shape, F32)
    for k, y_ref in enumerate((y0_ref, y1_ref, y2_ref, y3_ref)):
        moe = moe + route[:, TOP_K + k:TOP_K + k + 1] * _unpack_bf16_pairs(y_ref[...])
    o_ref[...] = _ln(DEEPNORM_ALPHA * x_ref[...] + moe, g_ref[...], b_ref[...])


def _moe_combine_call(x1, route, yg, g2, b2):
    t = x1.shape[0]
    tm = 512
    nt = t // tm
    ysel = lambda k: pl.BlockSpec((tm, D_MODEL // 2), lambda i: (k * nt + i, 0))
    return pl.pallas_call(
        _moe_combine_kernel, name="moe_combine_ln",
        grid=(nt,),
        in_specs=[pl.BlockSpec((tm, D_MODEL), lambda i: (i, 0)),
                  pl.BlockSpec((tm, LANE), lambda i: (i, 0))]
                 + [ysel(k) for k in range(TOP_K)]
                 + [pl.BlockSpec((1, D_MODEL), lambda i: (0, 0)),
                    pl.BlockSpec((1, D_MODEL), lambda i: (0, 0))],
        out_specs=pl.BlockSpec((tm, D_MODEL), lambda i: (i, 0)),
        out_shape=jax.ShapeDtypeStruct((t, D_MODEL), F32),
        compiler_params=_cparams(("parallel",)),
    )(x1, route, yg, yg, yg, yg, g2, b2)


def _moe(x1, x1p, route, counts, lw):
    t = x1.shape[0]
    bm = MOE_BLOCK_ROWS
    n_assign = t * TOP_K
    n_blocks = n_assign // bm + N_EXPERTS
    idx = route[:, :TOP_K].astype(jnp.int32)
    rank = route[:, 2 * TOP_K:3 * TOP_K].astype(jnp.int32)
    cnt = counts[0, :N_EXPERTS].astype(jnp.int32)
    padded = (cnt + bm - 1) // bm * bm
    pad_ends = jnp.cumsum(padded)
    pad_starts = pad_ends - padded
    dest = jnp.transpose(jnp.take(pad_starts, idx) + rank).reshape(-1)
    block_start = jnp.arange(n_blocks, dtype=jnp.int32) * bm
    block_expert = jnp.minimum(jnp.sum(pad_ends[None, :] <= block_start[:, None], axis=1),
                               N_EXPERTS - 1).astype(jnp.int32)
    n_valid = jnp.clip(jnp.take(cnt, block_expert) - (block_start - jnp.take(pad_starts, block_expert)), 0, bm)
    n_used = (pad_ends[-1:] // bm).astype(jnp.int32)
    xg = _sc_scatter_rows(x1p, dest, n_blocks * bm)
    yb = _moe_expert_call(xg, block_expert, n_used, n_valid.astype(jnp.int32), lw["layer"],
                          lw["moe_w1"], lw["moe_b1"], lw["moe_w2"], lw["moe_b2"])
    yg = _sc_gather_rows(yb, dest)
    return _moe_combine_call(x1, route, yg, lw["ln2_g"], lw["ln2_b"])


def _rotate_half_cols(w):
    half = RET_QK_DIM // 2
    w4 = w.reshape(w.shape[0], RET_HEADS, 2, half)
    return jnp.concatenate([-w4[:, :, 1:], w4[:, :, :1]], axis=2).reshape(w.shape)


def _pad_heads(w, axis):
    shp = list(w.shape)
    shp[axis:axis + 1] = [MEM_HEADS, MEM_HEAD_DIM]
    w = w.reshape(shp)
    pad = [(0, 0)] * w.ndim
    pad[axis + 1] = (0, MEM_HEAD_PAD - MEM_HEAD_DIM)
    w = jnp.pad(w, pad)
    shp[axis:axis + 2] = [MEM_HEADS * MEM_HEAD_PAD]
    return w.reshape(shp)


def _layer_weights(l, p):
    w_in = p["w_in"][l]
    aw, sw, qkw, vw, mw = 768, S5_WIDTH, RET_HEADS * RET_QK_DIM, RET_HEADS * RET_V_DIM, 768
    offs = [0]
    for wdt in (aw, aw, aw, sw, qkw, qkw, vw, vw, mw, 4 * D_MODEL):
        offs.append(offs[-1] + wdt)
    aq, ak, av, su, rq, rk, rv, rg, mq, gate = (w_in[:, offs[i]:offs[i + 1]] for i in range(10))
    seg = dict(gate=0.5 * gate, mq=_pad_heads(mq, 1), aq=aq[:, :GW], ak=ak[:, :GW], av=av[:, :GW],
               pad=jnp.zeros((D_MODEL, 2 * LANE), F32), rv=rv, rg=rg,
               rq=rq, rqs=_rotate_half_cols(rq), rk=rk, rks=_rotate_half_cols(rk))
    order = sorted(_SEG, key=_SEG.get)
    w_p = jnp.concatenate([seg[n] for n in order], axis=1).astype(BF16)
    assert w_p.shape[1] == NP
    w_d = jnp.concatenate([w[:, g * GW:(g + 1) * GW] for g in (1, 2) for w in (aq, ak, av)], axis=1).astype(BF16)
    wkv = p["w_mem_kv"][l]
    w_kv = jnp.concatenate([_pad_heads(wkv[:, :768], 1), _pad_heads(wkv[:, 768:], 1)], axis=1).astype(BF16)
    wb = p["w_branch"][l]
    rw = jnp.pad(p["router_w"][l].astype(F32), ((0, 0), (0, LANE - N_EXPERTS)))
    rb = jnp.pad(p["router_b"][l].astype(F32), (0, LANE - N_EXPERTS), constant_values=NEG_INF)
    return dict(
        w_p=w_p, w_d=w_d, w_su_t=jnp.transpose(su).astype(BF16), w_kv=w_kv,
        w_glu_t=jnp.transpose(p["s5_w_glu"][l]).astype(BF16), b_glu=p["s5_b_glu"][l].astype(F32).reshape(-1, 1),
        wb_a=wb[0].astype(BF16), wb_b=wb[1].astype(BF16), wb_c=wb[2].astype(BF16),
        wb_d=_pad_heads(wb[3], 0).astype(BF16), w_out=(0.5 * p["w_out"][l]).astype(BF16),
        ln1_g=p["ln1_g"][l].reshape(1, -1), ln1_b=p["ln1_b"][l].reshape(1, -1),
        router_w_hi=rw.astype(BF16), router_w_lo=(rw - rw.astype(BF16).astype(F32)).astype(BF16),
        router_b=rb.reshape(1, -1),
        layer=l, moe_w1=p["moe_w1"], moe_b1=p["moe_b1"][l].astype(F32).reshape(N_EXPERTS, 1, -1),
        moe_w2=p["moe_w2"], moe_b2=p["moe_b2"][l].astype(F32).reshape(N_EXPERTS, 1, -1),
        ln2_g=p["ln2_g"][l].reshape(1, -1), ln2_b=p["ln2_b"][l].reshape(1, -1),
        s5=_s5_tables(p["s5_lam_re"][l], p["s5_lam_im"][l], p["s5_log_step"][l], p["s5_b_re"][l],
                      p["s5_b_im"][l], p["s5_c_re"][l], p["s5_c_im"][l], p["s5_d"][l]))


def _trunk_layer(x, mem2d, b, s, lw, attn_bias, ret_tabs):
    proj, ut, pd = _proj_call(x, lw["w_p"], lw["w_su_t"], lw["w_d"])
    outs = [_attn_call(proj, attn_bias[0], b, s)]
    outs += [_dil_attn_call(pd, attn_bias[g], b, s, g, A_GROUPS[g][1]) for g in (1, 2)]
    aos, alses = [o for o, _ in outs], [l for _, l in outs]
    toep, win, wout, a_re, a_im = lw["s5"]
    nc = s // S5_CHUNK
    c_re, c_im = _s5_contrib_call(ut, win, b, nc)
    s_re, s_im = _s5_scan_call(c_re, c_im, a_re, a_im)
    zt = _s5_out_call(ut, toep, s_re, s_im, wout, b, nc)
    yc = _retention_call(proj, ret_tabs, b, s)
    kv = _mem_kv_call(mem2d, lw["w_kv"])
    yd = _mem_attn_call(proj, kv, b, s)
    x1 = _merge_call(x, aos, alses, zt, yc, yd, proj, lw)
    x1p, route, counts = _router_call(x1, lw)
    return _moe(x1, x1p, route, counts, lw)


def kernel(x_prompt, x_sample, mem_prompt, mem_sample, ln_in_g, ln_in_b, rel_bias, w_in, s5_lam_re, s5_lam_im, s5_log_step, s5_b_re, s5_b_im, s5_c_re, s5_c_im, s5_d, s5_w_glu, s5_b_glu, w_mem_kv, w_branch, w_out, ln1_g, ln1_b, router_w, router_b, moe_w1, moe_b1, moe_w2, moe_b2, ln2_g, ln2_b):
    p = dict(w_in=w_in, s5_lam_re=s5_lam_re, s5_lam_im=s5_lam_im, s5_log_step=s5_log_step, s5_b_re=s5_b_re,
             s5_b_im=s5_b_im, s5_c_re=s5_c_re, s5_c_im=s5_c_im, s5_d=s5_d, s5_w_glu=s5_w_glu, s5_b_glu=s5_b_glu,
             w_mem_kv=w_mem_kv, w_branch=w_branch, w_out=w_out, ln1_g=ln1_g, ln1_b=ln1_b, router_w=router_w,
             router_b=router_b, moe_w1=moe_w1, moe_b1=moe_b1, moe_w2=moe_w2, moe_b2=moe_b2, ln2_g=ln2_g,
             ln2_b=ln2_b)
    attn_bias = [_attn_bias_table(rel_bias, g, dil) for g, (_, dil) in enumerate(A_GROUPS)]
    trunks = []
    for x, mem in ((x_prompt, mem_prompt), (x_sample, mem_sample)):
        b, s, _ = x.shape
        trunks.append(dict(x=_layer_norm_call(x.reshape(b * s, D_MODEL), ln_in_g, ln_in_b),
                           mem=mem.reshape(b * N_MEM, D_MODEL), b=b, s=s, tabs=_ret_tables(s)))
    for l in range(DEPTH):
        lw = _layer_weights(l, p)
        for tr in trunks:
            tr["x"] = _trunk_layer(tr["x"], tr["mem"], tr["b"], tr["s"], lw, attn_bias, tr["tabs"])
    return tuple(tr["x"].reshape(tr["b"], tr["s"], D_MODEL) for tr in trunks)
```

```python
import functools
import math

import jax
import jax.numpy as jnp
from jax import lax
from jax.experimental import pallas as pl
from jax.experimental.pallas import tpu as pltpu
from jax.experimental.pallas import tpu_sc as plsc

F32 = jnp.float32
BF16 = jnp.bfloat16
HIGHEST = lax.Precision.HIGHEST

D_MODEL = 1024
DEPTH = 4
N_MEM = 256
A_GROUPS = ((128, 1), (512, 4), (2048, 16))
A_HEADS_PER_GROUP = 4
A_HEADS = 12
A_HEAD_DIM = 64
A_RADIUS = 64
REL_BUCKETS = 32
REL_MAX_DIST = 1024
S5_GROUP = 16
S5_WIDTH = 768
S5_GROUPS = 48
S5_STATE = 64
RET_HEADS = 6
RET_QK_DIM = 64
RET_V_DIM = 128
RET_CHUNK = 128
ROPE_BASE = 10000.0
MEM_HEADS = 4
MEM_HEAD_DIM = 192
MEM_HEAD_PAD = 256
N_EXPERTS = 32
TOP_K = 4
D_FF = 1024
SWIGLU_LIMIT = 7.0
SWIGLU_ALPHA = 1.702
LN_EPS = 1e-5
DEEPNORM_ALPHA = (2 * DEPTH) ** 0.25
NEG_INF = -1e30

LANE = 128
VMEM_LIMIT = 56 * 1024 * 1024
S5_CHUNK = 128
MOE_BLOCK_ROWS = 512
SC_GATHER_WINDOW = 64

_SEG = {}
_off = 0
for _name, _width, _align in (
        ("gate", 32, 8), ("mq", 8, 8), ("aq", 2, 2), ("ak", 2, 2), ("av", 2, 2), ("pad", 2, 2),
        ("rv", 6, 6), ("rg", 6, 6), ("rq", 3, 3), ("rqs", 3, 3), ("rk", 3, 3), ("rks", 3, 3)):
    assert _off % _align == 0, (_name, _off)
    _SEG[_name] = _off
    _off += _width
NP_UNITS = _off
NP = NP_UNITS * LANE
assert NP % 256 == 0
GW = A_HEADS_PER_GROUP * A_HEAD_DIM
ND = 2 * 3 * GW


def _cparams(sem, vmem=VMEM_LIMIT):
    return pltpu.CompilerParams(dimension_semantics=sem, vmem_limit_bytes=vmem)


def _sigmoid(x):
    return 0.5 * jnp.tanh(0.5 * x) + 0.5


def _ln(h, g, b):
    mu = jnp.mean(h, axis=-1, keepdims=True)
    d = h - mu
    var = jnp.mean(d * d, axis=-1, keepdims=True)
    return d * lax.rsqrt(var + LN_EPS) * g + b


def _ln_kernel(x_ref, g_ref, b_ref, o_ref):
    o_ref[...] = _ln(x_ref[...], g_ref[...], b_ref[...])


def _layer_norm_call(x, g, b):
    t = x.shape[0]
    tm = 512
    return pl.pallas_call(
        _ln_kernel, name="input_ln",
        grid=(t // tm,),
        in_specs=[pl.BlockSpec((tm, D_MODEL), lambda i: (i, 0)),
                  pl.BlockSpec((1, D_MODEL), lambda i: (0, 0)),
                  pl.BlockSpec((1, D_MODEL), lambda i: (0, 0))],
        out_specs=pl.BlockSpec((tm, D_MODEL), lambda i: (i, 0)),
        out_shape=jax.ShapeDtypeStruct((t, D_MODEL), F32),
        compiler_params=_cparams(("parallel",)),
    )(x, g.reshape(1, -1), b.reshape(1, -1))


def _proj_kernel(x_ref, w_ref, p_ref):
    p_ref[...] = jnp.dot(x_ref[...].astype(BF16), w_ref[...], preferred_element_type=F32).astype(BF16)


def _proj_aux_kernel(x_ref, wsut_ref, wd_ref, ut_ref, pd_ref):
    xb = x_ref[...].astype(BF16)
    ut_ref[...] = lax.dot_general(wsut_ref[...], xb, (((1,), (1,)), ((), ())),
                                  preferred_element_type=F32).astype(BF16)
    pd = jnp.dot(xb, wd_ref[...], preferred_element_type=F32)
    for c in range(ND // LANE):
        pd_ref[c] = pd[:, c * LANE:(c + 1) * LANE]


def _proj_call(x, w_p, w_su_t, w_d):
    t = x.shape[0]
    tm, tn = 512, NP // 3
    proj = pl.pallas_call(
        _proj_kernel, name="in_proj",
        grid=(NP // tn, t // tm),
        in_specs=[pl.BlockSpec((tm, D_MODEL), lambda j, i: (i, 0)),
                  pl.BlockSpec((D_MODEL, tn), lambda j, i: (0, j))],
        out_specs=pl.BlockSpec((tm, tn), lambda j, i: (i, j)),
        out_shape=jax.ShapeDtypeStruct((t, NP), BF16),
        compiler_params=_cparams(("parallel", "parallel")),
    )(x, w_p)
    ut, pd = pl.pallas_call(
        _proj_aux_kernel, name="in_proj_s5_dilated",
        grid=(t // tm,),
        in_specs=[pl.BlockSpec((tm, D_MODEL), lambda i: (i, 0)),
                  pl.BlockSpec((S5_WIDTH, D_MODEL), lambda i: (0, 0)),
                  pl.BlockSpec((D_MODEL, ND), lambda i: (0, 0))],
        out_specs=[pl.BlockSpec((S5_WIDTH, tm), lambda i: (0, i)),
                   pl.BlockSpec((ND // LANE, tm, LANE), lambda i: (0, i, 0))],
        out_shape=[jax.ShapeDtypeStruct((S5_WIDTH, t), BF16),
                   jax.ShapeDtypeStruct((ND // LANE, t, LANE), F32)],
        compiler_params=_cparams(("parallel",)),
    )(x, w_su_t, w_d)
    return proj, ut, pd


ATTN_TQ = 2 * A_RADIUS
ATTN_PAIR = 4


def _banded_heads(blocks, bias_ref, sub_len):
    tq, nk = blocks[0][0].shape[0], blocks[0][1].shape[0]
    width = blocks[0][0].shape[1]
    col = lax.broadcasted_iota(jnp.int32, (tq, nk), 1)
    valids = [(blk[3] + col >= 0) & (blk[3] + col < sub_len) for blk in blocks]
    lane_head = lax.broadcasted_iota(jnp.int32, (1, width), 1) // A_HEAD_DIM
    scale = jnp.asarray(A_HEAD_DIM ** -0.5, BF16)
    pairs = [(bk, h) for bk in range(len(blocks)) for h in range(A_HEADS_PER_GROUP)]
    qs = [jnp.where(lane_head == h, blocks[bk][0], jnp.zeros_like(blocks[bk][0])) * scale for bk, h in pairs]
    ss = [lax.dot_general(qh, blocks[bk][1], (((1,), (1,)), ((), ())), preferred_element_type=F32)
          for qh, (bk, h) in zip(qs, pairs)]
    ss = [jnp.where(valids[bk], s + bias_ref[h], NEG_INF) for s, (bk, h) in zip(ss, pairs)]
    ms = [jnp.max(s, axis=-1, keepdims=True) for s in ss]
    ps = [jnp.exp(s - m) for s, m in zip(ss, ms)]
    dens = [jnp.sum(p, axis=-1, keepdims=True) for p in ps]
    ohs = [jnp.dot(p.astype(BF16), blocks[bk][2], preferred_element_type=F32) for p, (bk, h) in zip(ps, pairs)]
    outs = []
    for bk in range(len(blocks)):
        acc = jnp.zeros((tq, width), F32)
        lse_acc = jnp.zeros((tq, width), F32)
        for j, (pb, h) in enumerate(pairs):
            if pb == bk:
                hm = lane_head == h
                acc = jnp.where(hm, ohs[j] / dens[j], acc)
                lse_acc = jnp.where(hm, ms[j] + jnp.log(dens[j]), lse_acc)
        outs.append((acc, lse_acc))
    return outs


def _attn_kernel(q_ref, kp_ref, kc_ref, kn_ref, vp_ref, vc_ref, vn_ref, bias_ref, o_ref, lse_ref, *, sub_len):
    tq = ATTN_TQ
    kcat = jnp.concatenate([kp_ref[...], kc_ref[...], kn_ref[...]], axis=0)
    vcat = jnp.concatenate([vp_ref[...], vc_ref[...], vn_ref[...]], axis=0)
    first = pl.program_id(1) * (ATTN_PAIR * tq) - A_RADIUS
    blocks = [(q_ref[j * tq:(j + 1) * tq, :], kcat[j * tq:j * tq + 2 * tq, :], vcat[j * tq:j * tq + 2 * tq, :],
               first + j * tq) for j in range(ATTN_PAIR)]
    for j, (acc, lse) in enumerate(_banded_heads(blocks, bias_ref, sub_len)):
        o_ref[j * tq:(j + 1) * tq, :] = acc
        lse_ref[j * tq:(j + 1) * tq, :] = lse


def _attn_call(proj, bias, b, s):
    rows = ATTN_PAIR * ATTN_TQ
    nstep = s // rows
    edge = rows // A_RADIUS
    pv = proj.reshape(b, s, NP)
    qo, ko, vo = (_SEG[n] * LANE // GW for n in ("aq", "ak", "av"))
    blk, eblk = (None, rows, GW), (None, A_RADIUS, GW)
    cur = lambda off: (lambda bi, i: (bi, i, off))
    prv = lambda off: (lambda bi, i: (bi, jnp.maximum(i * edge - 1, 0), off))
    nxt = lambda off: (lambda bi, i: (bi, jnp.minimum((i + 1) * edge, nstep * edge - 1), off))
    out_map = lambda bi, i: (bi, i, 0)
    o, lse = pl.pallas_call(
        functools.partial(_attn_kernel, sub_len=s), name="window_attn_g0",
        grid=(b, nstep),
        in_specs=[pl.BlockSpec(blk, cur(qo)),
                  pl.BlockSpec(eblk, prv(ko)), pl.BlockSpec(blk, cur(ko)), pl.BlockSpec(eblk, nxt(ko)),
                  pl.BlockSpec(eblk, prv(vo)), pl.BlockSpec(blk, cur(vo)), pl.BlockSpec(eblk, nxt(vo)),
                  pl.BlockSpec((A_HEADS_PER_GROUP, ATTN_TQ, 2 * ATTN_TQ), lambda bi, i: (0, 0, 0))],
        out_specs=[pl.BlockSpec(blk, out_map), pl.BlockSpec(blk, out_map)],
        out_shape=[jax.ShapeDtypeStruct((b, s, GW), F32), jax.ShapeDtypeStruct((b, s, GW), F32)],
        compiler_params=_cparams(("parallel", "parallel")),
    )(pv, pv, pv, pv, pv, pv, pv, bias)
    return o.reshape(b * s, GW), lse.reshape(b * s, GW)


def _dil_attn_kernel(q_ref, kp_ref, kc_ref, kn_ref, vp_ref, vc_ref, vn_ref, bias_ref, o_ref, lse_ref, *,
                     dil, sub_len):
    tq, half = ATTN_TQ, A_RADIUS
    first_key = pl.program_id(1) * tq - half

    def residues(it, carry):
        def rows(ref, r, n):
            return jnp.concatenate([ref[c, pl.ds(r, n, stride=dil), :] for c in range(GW // LANE)], axis=-1)

        rs = [it * ATTN_PAIR + j for j in range(ATTN_PAIR)]
        blocks = []
        for r in rs:
            q = rows(q_ref, r, tq).astype(BF16)
            kwin = jnp.concatenate([rows(kp_ref, r, half), rows(kc_ref, r, tq), rows(kn_ref, r, half)],
                                   axis=0).astype(BF16)
            vwin = jnp.concatenate([rows(vp_ref, r, half), rows(vc_ref, r, tq), rows(vn_ref, r, half)],
                                   axis=0).astype(BF16)
            blocks.append((q, kwin, vwin, first_key))
        for r, (acc, lse) in zip(rs, _banded_heads(blocks, bias_ref, sub_len)):
            for c in range(GW // LANE):
                o_ref[c, pl.ds(r, tq, stride=dil), :] = acc[:, c * LANE:(c + 1) * LANE]
                lse_ref[c, pl.ds(r, tq, stride=dil), :] = lse[:, c * LANE:(c + 1) * LANE]
        return carry

    lax.fori_loop(0, dil // ATTN_PAIR, residues, 0)


def _dil_attn_call(pd, bias, b, s, g, dil):
    tq = ATTN_TQ
    rows = tq * dil
    nblk = s // rows
    nslab = GW // LANE
    pv = pd.reshape(ND // LANE, b, s, LANE)
    qo, ko, vo = (3 * (g - 1) + j for j in range(3))
    blk, hblk = (nslab, None, rows, LANE), (nslab, None, rows // 2, LANE)
    cur = lambda off: (lambda bi, i: (off, bi, i, 0))
    prv = lambda off: (lambda bi, i: (off, bi, jnp.maximum(2 * i - 1, 0), 0))
    nxt = lambda off: (lambda bi, i: (off, bi, jnp.minimum(2 * i + 2, 2 * nblk - 1), 0))
    out_map = lambda bi, i: (0, bi, i, 0)
    o, lse = pl.pallas_call(
        functools.partial(_dil_attn_kernel, dil=dil, sub_len=s // dil), name=f"dilated_attn_g{g}",
        grid=(b, nblk),
        in_specs=[pl.BlockSpec(blk, cur(qo)),
                  pl.BlockSpec(hblk, prv(ko)), pl.BlockSpec(blk, cur(ko)), pl.BlockSpec(hblk, nxt(ko)),
                  pl.BlockSpec(hblk, prv(vo)), pl.BlockSpec(blk, cur(vo)), pl.BlockSpec(hblk, nxt(vo)),
                  pl.BlockSpec((A_HEADS_PER_GROUP, tq, 2 * tq), lambda bi, i: (0, 0, 0))],
        out_specs=[pl.BlockSpec(blk, out_map), pl.BlockSpec(blk, out_map)],
        out_shape=[jax.ShapeDtypeStruct((nslab, b, s, LANE), F32)] * 2,
        compiler_params=_cparams(("parallel", "parallel")),
    )(pv, pv, pv, pv, pv, pv, pv, bias)
    return o.reshape(nslab, b * s, LANE), lse.reshape(nslab, b * s, LANE)


def _t5_bucket(rel):
    nb = REL_BUCKETS // 2
    max_exact = nb // 2
    ret = jnp.where(rel > 0, nb, 0)
    n = jnp.abs(rel)
    nf = jnp.maximum(n, 1).astype(F32)
    large = max_exact + (jnp.log(nf / max_exact) / math.log(REL_MAX_DIST / max_exact)
                         * (nb - max_exact)).astype(jnp.int32)
    large = jnp.minimum(large, nb - 1)
    return ret + jnp.where(n < max_exact, n, large)


def _attn_bias_table(rel_bias, g, dil, tq=128):
    qi = jnp.arange(tq)[:, None]
    kj = jnp.arange(2 * tq)[None, :] - A_RADIUS
    rel = kj - qi
    tab = rel_bias[:, g * A_HEADS_PER_GROUP:(g + 1) * A_HEADS_PER_GROUP].astype(F32)
    bucket = _t5_bucket(rel * dil)[None]
    bias = jnp.zeros((A_HEADS_PER_GROUP,) + rel.shape, F32)
    for bkt in range(REL_BUCKETS):
        bias = jnp.where(bucket == bkt, tab[bkt][:, None, None], bias)
    return jnp.where((jnp.abs(rel) <= A_RADIUS)[None], bias, NEG_INF)


def _s5_toeplitz_kernel(bcf_ref, pwf_ref, bcb_ref, pwb_ref, d_ref, t_ref, gf_ref, gb_ref):
    gf_ref[...] = jnp.dot(bcf_ref[...], pwf_ref[...], precision=HIGHEST, preferred_element_type=F32)
    gb_ref[...] = jnp.dot(bcb_ref[...], pwb_ref[...], precision=HIGHEST, preferred_element_type=F32)
    c = S5_CHUNK
    row = lax.broadcasted_iota(jnp.int32, (c, c), 0)
    col = lax.broadcasted_iota(jnp.int32, (c, c), 1)

    def body(ci, carry):
        dval = d_ref[pl.ds(ci, 1), :]
        for co in range(S5_GROUP):
            r = ci * S5_GROUP + co
            gf = jnp.broadcast_to(gf_ref[pl.ds(r, 1), :], (c, c))
            gb = jnp.broadcast_to(gb_ref[pl.ds(r, 1), :], (c, c))
            tf = pltpu.roll(gf, 0, 1, stride=1, stride_axis=0)
            tb = pltpu.roll(gb, 1, 1, stride=1, stride_axis=0)
            tile = jnp.where(col >= row, tf, 0.0) + jnp.where(row >= col, tb, 0.0)
            tile = tile + jnp.where((row == col) & (ci == co), dval, 0.0)
            t_ref[pl.ds(pl.multiple_of(ci * c, c), c), co * c:(co + 1) * c] = tile.astype(BF16)
        return carry

    lax.fori_loop(0, S5_GROUP, body, 0)


def _s5_toeplitz_call(bcf, pwf, bcb, pwb, dskip):
    n = S5_GROUP * S5_CHUNK
    return pl.pallas_call(
        _s5_toeplitz_kernel, name="s5_toeplitz",
        grid=(S5_GROUPS,),
        in_specs=[pl.BlockSpec((None, S5_GROUP * S5_GROUP, 2 * S5_STATE), lambda g: (g, 0, 0)),
                  pl.BlockSpec((None, 2 * S5_STATE, S5_CHUNK), lambda g: (g, 0, 0)),
                  pl.BlockSpec((None, S5_GROUP * S5_GROUP, 2 * S5_STATE), lambda g: (g, 0, 0)),
                  pl.BlockSpec((None, 2 * S5_STATE, S5_CHUNK), lambda g: (g, 0, 0)),
                  pl.BlockSpec((None, S5_GROUP, LANE), lambda g: (g, 0, 0))],
        out_specs=pl.BlockSpec((None, n, n), lambda g: (g, 0, 0)),
        out_shape=jax.ShapeDtypeStruct((S5_GROUPS, n, n), BF16),
        scratch_shapes=[pltpu.VMEM((S5_GROUP * S5_GROUP, S5_CHUNK), F32),
                        pltpu.VMEM((S5_GROUP * S5_GROUP, S5_CHUNK), F32)],
        compiler_params=_cparams(("parallel",)),
    )(bcf, pwf, bcb, pwb, dskip)


def _s5_tables(lam_re, lam_im, log_step, b_re, b_im, c_re, c_im, d_skip):
    c = S5_CHUNK
    k = jnp.arange(c, dtype=F32)
    per_dir = []
    for direction in range(2):
        step = jnp.exp(log_step[direction].astype(F32))[:, None]
        lr, li = lam_re[direction].astype(F32), lam_im[direction].astype(F32)
        mag = jnp.exp(lr * step)
        ar, ai = mag * jnp.cos(li * step), mag * jnp.sin(li * step)
        nr, ni = ar - 1.0, ai
        den = lr * lr + li * li
        fr = (nr * lr + ni * li) / den
        fi = (ni * lr - nr * li) / den
        br, bi = b_re[direction].astype(F32), b_im[direction].astype(F32)
        bbr = fr[..., None] * br - fi[..., None] * bi
        bbi = fr[..., None] * bi + fi[..., None] * br
        cr, cim = c_re[direction].astype(F32), c_im[direction].astype(F32)

        def power(e, log_mag=lr * step, phase=li * step):
            m = jnp.exp(e[None, None, :] * log_mag[..., None])
            th = e[None, None, :] * phase[..., None]
            return m * jnp.cos(th), m * jnp.sin(th)

        per_dir.append(dict(bbr=bbr, bbi=bbi, cr=cr, cim=cim, power=power))

    def bc_table(d):
        bbr_t = jnp.transpose(d["bbr"], (0, 2, 1))[:, :, None, :]
        bbi_t = jnp.transpose(d["bbi"], (0, 2, 1))[:, :, None, :]
        cr, cim = d["cr"][:, None], d["cim"][:, None]
        re = bbr_t * cr - bbi_t * cim
        im = bbr_t * cim + bbi_t * cr
        return jnp.concatenate([re, -im], axis=-1).reshape(S5_GROUPS, S5_GROUP * S5_GROUP, 2 * S5_STATE)

    f, bw = per_dir
    pfr, pfi = f["power"](k)
    pbr, pbi = bw["power"](c - 1 - k)
    pwf = jnp.concatenate([pfr, pfi], axis=1)
    pwb = jnp.concatenate([pbr, pbi], axis=1)
    dsk = jnp.broadcast_to(d_skip.astype(F32).reshape(S5_GROUPS, S5_GROUP, 1), (S5_GROUPS, S5_GROUP, LANE))
    toep = _s5_toeplitz_call(bc_table(f), pwf, bc_table(bw), pwb, dsk)

    seg4 = lambda f_re, b_re, f_im, b_im, axis: jnp.concatenate([f_re, b_re, f_im, b_im], axis=axis)
    tp = lambda z: jnp.transpose(z, (0, 2, 1))

    pfr_i, pfi_i = f["power"](c - 1 - k)
    pbr_i, pbi_i = bw["power"](k)
    a_in = seg4(tp(f["bbr"]), tp(bw["bbr"]), tp(f["bbr"]), tp(bw["bbr"]), -1)
    b_in = seg4(-tp(f["bbi"]), -tp(bw["bbi"]), tp(f["bbi"]), tp(bw["bbi"]), -1)
    x_in = seg4(tp(pfr_i), tp(pbr_i), tp(pfi_i), tp(pbi_i), -1)
    y_in = seg4(tp(pfi_i), tp(pbi_i), tp(pfr_i), tp(pbr_i), -1)
    win = (a_in[:, :, None, :] * x_in[:, None, :, :] + b_in[:, :, None, :] * y_in[:, None, :, :]
           ).reshape(S5_GROUPS, S5_GROUP * c, 4 * S5_STATE).astype(BF16)

    pfr_o, pfi_o = f["power"](k + 1.0)
    pbr_o, pbi_o = bw["power"](c - k)
    a_out = seg4(tp(f["cr"]), tp(bw["cr"]), -tp(f["cr"]), -tp(bw["cr"]), 1)
    b_out = seg4(-tp(f["cim"]), -tp(bw["cim"]), -tp(f["cim"]), -tp(bw["cim"]), 1)
    x_out = seg4(pfr_o, pbr_o, pfi_o, pbi_o, 1)
    y_out = seg4(pfi_o, pbi_o, pfr_o, pbr_o, 1)
    wout = (a_out[:, :, :, None] * x_out[:, :, None, :] + b_out[:, :, :, None] * y_out[:, :, None, :]
            ).reshape(S5_GROUPS, 4 * S5_STATE, S5_GROUP * c).astype(BF16)

    cc = jnp.asarray([float(c)], F32)
    afr, afi = f["power"](cc)
    abr, abi = bw["power"](cc)
    a_re = jnp.concatenate([afr[..., 0], abr[..., 0]], axis=-1).reshape(1, -1)
    a_im = jnp.concatenate([afi[..., 0], abi[..., 0]], axis=-1).reshape(1, -1)
    return toep, win, wout, a_re, a_im


def _s5_contrib_kernel(u_ref, win_ref, re_ref, im_ref):
    uc = jnp.concatenate([u_ref[ci] for ci in range(S5_GROUP)], axis=-1)
    res = jnp.dot(uc, win_ref[...], preferred_element_type=F32)
    nc, nb = re_ref.shape[0], re_ref.shape[1]
    half = 2 * S5_STATE
    for bi in range(nb):
        re_ref[:, bi, :] = res[bi * nc:(bi + 1) * nc, :half]
        im_ref[:, bi, :] = res[bi * nc:(bi + 1) * nc, half:]


def _s5_contrib_call(ut, win, b, nc):
    nch = b * nc
    half = 2 * S5_STATE
    uv = ut.reshape(S5_GROUPS, S5_GROUP, nch, S5_CHUNK)
    return pl.pallas_call(
        _s5_contrib_kernel, name="s5_contrib",
        grid=(S5_GROUPS,),
        in_specs=[pl.BlockSpec((None, S5_GROUP, nch, S5_CHUNK), lambda g: (g, 0, 0, 0)),
                  pl.BlockSpec((None, S5_GROUP * S5_CHUNK, 4 * S5_STATE), lambda g: (g, 0, 0))],
        out_specs=[pl.BlockSpec((nc, b, half), lambda g: (0, 0, g))] * 2,
        out_shape=[jax.ShapeDtypeStruct((nc, b, S5_GROUPS * half), F32)] * 2,
        compiler_params=_cparams(("parallel",)),
    )(uv, win)


def _s5_scan_kernel(cre_ref, cim_ref, are_ref, aim_ref, ore_ref, oim_ref):
    nc, nb, width = cre_ref.shape
    a_re, a_im = are_ref[...], aim_ref[...]
    lane = lax.broadcasted_iota(jnp.int32, (1, width), 1)
    fwd_lane = (lane % (2 * S5_STATE)) < S5_STATE
    zero = jnp.zeros((nb, width), F32)

    def advance(s, c):
        sr, si = s
        return a_re * sr - a_im * si + cre_ref[c], a_re * si + a_im * sr + cim_ref[c]

    def up(c, s):
        ore_ref[c] = s[0]
        oim_ref[c] = s[1]
        return advance(s, c)

    lax.fori_loop(0, nc, up, (zero, zero))

    def down(i, s):
        c = nc - 1 - i
        ore_ref[c] = jnp.where(fwd_lane, ore_ref[c], s[0])
        oim_ref[c] = jnp.where(fwd_lane, oim_ref[c], s[1])
        return advance(s, c)

    lax.fori_loop(0, nc, down, (zero, zero))


def _s5_scan_call(cre, cim, a_re, a_im):
    return pl.pallas_call(
        _s5_scan_kernel, name="s5_chunk_scan",
        out_shape=[jax.ShapeDtypeStruct(cre.shape, F32)] * 2,
        compiler_params=pltpu.CompilerParams(vmem_limit_bytes=VMEM_LIMIT),
    )(cre, cim, a_re, a_im)


def _gelu_tanh(y):
    return 0.5 * y * (1.0 + jnp.tanh(math.sqrt(2.0 / math.pi) * (y + 0.044715 * (y * y * y))))


def _s5_out_kernel(u_ref, t_ref, sre_ref, sim_ref, wout_ref, z_ref):
    uc = jnp.concatenate([u_ref[ci] for ci in range(S5_GROUP)], axis=-1)
    y = jnp.dot(uc, t_ref[...], preferred_element_type=F32)
    sp = jnp.concatenate([jnp.concatenate([sre_ref[:, bi, :], sim_ref[:, bi, :]], axis=-1)
                          for bi in range(sre_ref.shape[1])], axis=0)
    y = y + jnp.dot(sp.astype(BF16), wout_ref[...], preferred_element_type=F32)
    z = _gelu_tanh(y)
    for co in range(S5_GROUP):
        z_ref[co] = z[:, co * S5_CHUNK:(co + 1) * S5_CHUNK].astype(BF16)


def _s5_out_call(ut, toep, s_re, s_im, wout, b, nc):
    nch = b * nc
    uv = ut.reshape(S5_GROUPS, S5_GROUP, nch, S5_CHUNK)
    n = S5_GROUP * S5_CHUNK
    half = 2 * S5_STATE
    zt = pl.pallas_call(
        _s5_out_kernel, name="s5_out",
        grid=(S5_GROUPS,),
        in_specs=[pl.BlockSpec((None, S5_GROUP, nch, S5_CHUNK), lambda g: (g, 0, 0, 0)),
                  pl.BlockSpec((None, n, n), lambda g: (g, 0, 0)),
                  pl.BlockSpec((nc, b, half), lambda g: (0, 0, g)),
                  pl.BlockSpec((nc, b, half), lambda g: (0, 0, g)),
                  pl.BlockSpec((None, 4 * S5_STATE, n), lambda g: (g, 0, 0))],
        out_specs=pl.BlockSpec((None, S5_GROUP, nch, S5_CHUNK), lambda g: (g, 0, 0, 0)),
        out_shape=jax.ShapeDtypeStruct((S5_GROUPS, S5_GROUP, nch, S5_CHUNK), BF16),
        compiler_params=_cparams(("parallel",)),
    )(uv, toep, s_re, s_im, wout)
    return zt.reshape(S5_WIDTH, nch * S5_CHUNK)


def _ret_tables(s):
    c = RET_CHUNK
    half = RET_QK_DIM // 2
    inv = ROPE_BASE ** (-jnp.arange(half, dtype=F32) / half)
    ang = jnp.arange(s, dtype=F32)[:, None] * inv[None, :]
    cos = jnp.tile(jnp.cos(ang), (1, 2 * RET_HEADS))
    sin = jnp.tile(jnp.sin(ang), (1, 2 * RET_HEADS))
    hidx = jnp.arange(RET_HEADS, dtype=F32)
    lgf = jnp.log1p(-jnp.exp2(-5.0 - hidx))
    lgb = jnp.log1p(-jnp.exp2(-5.5 - hidx))
    j = jnp.arange(c, dtype=F32)
    rel = j[:, None] - j[None, :]
    intra = jnp.exp(jnp.abs(rel)[None] * jnp.where(rel[None] >= 0, lgf[:, None, None], lgb[:, None, None]))
    qk_f = jnp.repeat(lgf, RET_QK_DIM)[None, :]
    qk_b = jnp.repeat(lgb, RET_QK_DIM)[None, :]
    v_f = jnp.repeat(lgf, RET_V_DIM)[None, :]
    v_b = jnp.repeat(lgb, RET_V_DIM)[None, :]
    tabs = dict(
        cos=cos, sin=sin, intra=intra,
        tail_f=jnp.exp((c - 1 - j)[:, None] * qk_f), head_b=jnp.exp(j[:, None] * qk_b),
        decq_f=jnp.exp((j + 1.0)[:, None] * qk_f), decq_b=jnp.exp((c - j)[:, None] * qk_b),
        chunk_f=jnp.exp(c * v_f), chunk_b=jnp.exp(c * v_b))
    return tabs


def _rot(x_ref, xs_ref, cos_ref, sin_ref):
    return x_ref[...].astype(F32) * cos_ref[...] + xs_ref[...].astype(F32) * sin_ref[...]


def _ret_kernel(q_ref, qs_ref, k_ref, ks_ref, v_ref, g_ref, cos_ref, sin_ref, intra_ref, tailf_ref, headb_ref,
                decqf_ref, decqb_ref, chunkf_ref, chunkb_ref, o_ref, st_ref, sfd_ref, sbd_ref, sball_ref, *, nc):
    i = pl.program_id(1)
    qk, dv = RET_QK_DIM, RET_V_DIM
    vb = v_ref[...]
    kr = _rot(k_ref, ks_ref, cos_ref, sin_ref) * (RET_QK_DIM ** -0.5)

    def update_state(weight_ref, decay_ref):
        kwt = jnp.transpose(kr * weight_ref[...])
        for h in range(RET_HEADS):
            kv = jnp.dot(kwt[h * qk:(h + 1) * qk, :].astype(BF16), vb[:, h * dv:(h + 1) * dv],
                         preferred_element_type=F32)
            st_ref[h] = st_ref[h] * decay_ref[:, h * dv:(h + 1) * dv] + kv

    @pl.when((i == 0) | (i == nc))
    def _():
        st_ref[...] = jnp.zeros_like(st_ref)

    @pl.when(i == 0)
    def _():
        sfd_ref[...] = jnp.zeros_like(sfd_ref)
        sbd_ref[...] = jnp.zeros_like(sbd_ref)

    @pl.when(i < nc)
    def _():
        c = nc - 1 - i
        sball_ref[c] = st_ref[...].astype(BF16)
        update_state(headb_ref, chunkb_ref)

    @pl.when(i >= nc)
    def _():
        c = i - nc
        for h in range(RET_HEADS):
            sfd_ref[h * qk:(h + 1) * qk, h * dv:(h + 1) * dv] = st_ref[h].astype(BF16)
            sbd_ref[h * qk:(h + 1) * qk, h * dv:(h + 1) * dv] = sball_ref[c, h]
        qr = _rot(q_ref, qs_ref, cos_ref, sin_ref)
        qb, kb = qr.astype(BF16), kr.astype(BF16)
        lane_head = lax.broadcasted_iota(jnp.int32, (1, qb.shape[1]), 1) // qk
        inter = jnp.dot((qr * decqf_ref[...]).astype(BF16), sfd_ref[...], preferred_element_type=F32)
        inter = inter + jnp.dot((qr * decqb_ref[...]).astype(BF16), sbd_ref[...], preferred_element_type=F32)
        heads = range(RET_HEADS)
        vcols = [slice(h * dv, (h + 1) * dv) for h in heads]
        qhs = [jnp.where(lane_head == h, qb, jnp.zeros_like(qb)) for h in heads]
        scs = [lax.dot_general(qh, kb, (((1,), (1,)), ((), ())), preferred_element_type=F32) * intra_ref[h]
               for h, qh in zip(heads, qhs)]
        ohs = [jnp.dot(sc.astype(BF16), vb[:, vs], preferred_element_type=F32) + inter[:, vs]
               for sc, vs in zip(scs, vcols)]
        mus = [jnp.mean(oh, axis=-1, keepdims=True) for oh in ohs]
        dds = [oh - mu for oh, mu in zip(ohs, mus)]
        vrs = [jnp.mean(dd * dd, axis=-1, keepdims=True) for dd in dds]
        on = jnp.concatenate([dd * lax.rsqrt(var + LN_EPS) for dd, var in zip(dds, vrs)], axis=-1)
        gg = g_ref[...].astype(F32)
        o_ref[...] = (on * (gg * _sigmoid(gg))).astype(BF16)
        update_state(tailf_ref, chunkf_ref)


def _retention_call(proj, tabs, b, s):
    c = RET_CHUNK
    nc = s // c
    qkw, vw = RET_HEADS * RET_QK_DIM, RET_HEADS * RET_V_DIM
    pv = proj.reshape(b, s, NP)
    qk_blk, v_blk = (None, c, qkw), (None, c, vw)
    qo, qso, ko, kso = (_SEG[n] * LANE // qkw for n in ("rq", "rqs", "rk", "rks"))
    vo, go = (_SEG[n] * LANE // vw for n in ("rv", "rg"))
    const2 = lambda shape: pl.BlockSpec(shape, lambda bi, i: (0, 0))
    chunk = lambda i: jnp.where(i < nc, nc - 1 - i, i - nc)
    fchunk = lambda i: jnp.maximum(i - nc, 0)
    both = lambda off: (lambda bi, i: (bi, chunk(i), off))
    fonly = lambda off: (lambda bi, i: (bi, fchunk(i), off))
    yc = pl.pallas_call(
        functools.partial(_ret_kernel, nc=nc), name="retention",
        grid=(b, 2 * nc),
        in_specs=[pl.BlockSpec(qk_blk, fonly(qo)), pl.BlockSpec(qk_blk, fonly(qso)),
                  pl.BlockSpec(qk_blk, both(ko)), pl.BlockSpec(qk_blk, both(kso)),
                  pl.BlockSpec(v_blk, both(vo)), pl.BlockSpec(v_blk, fonly(go)),
                  pl.BlockSpec((c, qkw), lambda bi, i: (chunk(i), 0)),
                  pl.BlockSpec((c, qkw), lambda bi, i: (chunk(i), 0)),
                  pl.BlockSpec((RET_HEADS, c, c), lambda bi, i: (0, 0, 0)),
                  const2((c, qkw)), const2((c, qkw)), const2((c, qkw)), const2((c, qkw)),
                  const2((1, vw)), const2((1, vw))],
        out_specs=pl.BlockSpec((None, c, vw), lambda bi, i: (bi, fchunk(i), 0)),
        out_shape=jax.ShapeDtypeStruct((b, s, vw), BF16),
        scratch_shapes=[pltpu.VMEM((RET_HEADS, RET_QK_DIM, RET_V_DIM), F32),
                        pltpu.VMEM((qkw, vw), BF16), pltpu.VMEM((qkw, vw), BF16),
                        pltpu.VMEM((nc, RET_HEADS, RET_QK_DIM, RET_V_DIM), BF16)],
        compiler_params=_cparams(("parallel", "arbitrary")),
    )(pv, pv, pv, pv, pv, pv, tabs["cos"], tabs["sin"], tabs["intra"], tabs["tail_f"], tabs["head_b"],
      tabs["decq_f"], tabs["decq_b"], tabs["chunk_f"], tabs["chunk_b"])
    return yc.reshape(b * s, vw)


def _mm_kernel(x_ref, w_ref, o_ref):
    o_ref[...] = jnp.dot(x_ref[...].astype(BF16), w_ref[...], preferred_element_type=F32).astype(o_ref.dtype)


def _mem_kv_call(mem2d, w_kv):
    m, n = mem2d.shape[0], w_kv.shape[1]
    tm = 512
    return pl.pallas_call(
        _mm_kernel, name="mem_kv_proj",
        grid=(m // tm,),
        in_specs=[pl.BlockSpec((tm, D_MODEL), lambda i: (i, 0)),
                  pl.BlockSpec((D_MODEL, n), lambda i: (0, 0))],
        out_specs=pl.BlockSpec((tm, n), lambda i: (i, 0)),
        out_shape=jax.ShapeDtypeStruct((m, n), BF16),
        compiler_params=_cparams(("parallel",)),
    )(mem2d, w_kv)


def _mem_attn_kernel(q_ref, k_ref, v_ref, o_ref):
    q = q_ref[...]
    cols = [slice(h * MEM_HEAD_PAD, (h + 1) * MEM_HEAD_PAD) for h in range(MEM_HEADS)]
    ss = [lax.dot_general(q[:, hs], k_ref[:, hs], (((1,), (1,)), ((), ())),
                          preferred_element_type=F32) * (MEM_HEAD_DIM ** -0.5) for hs in cols]
    ms = [jnp.max(s, axis=-1, keepdims=True) for s in ss]
    es = [jnp.exp(s - m) for s, m in zip(ss, ms)]
    ps = [e / jnp.sum(e, axis=-1, keepdims=True) for e in es]
    outs = [jnp.dot(p.astype(BF16), v_ref[:, hs], preferred_element_type=F32) for p, hs in zip(ps, cols)]
    o_ref[...] = jnp.concatenate(outs, axis=-1).astype(BF16)


def _mem_attn_call(proj, kv, b, s):
    tq = 256
    w = MEM_HEADS * MEM_HEAD_PAD
    pv = proj.reshape(b, s, NP)
    kvv = kv.reshape(b, N_MEM, 2 * w)
    qo = _SEG["mq"] * LANE // w
    yd = pl.pallas_call(
        _mem_attn_kernel, name="mem_attn",
        grid=(b, s // tq),
        in_specs=[pl.BlockSpec((None, tq, w), lambda bi, i: (bi, i, qo)),
                  pl.BlockSpec((None, N_MEM, w), lambda bi, i: (bi, 0, 0)),
                  pl.BlockSpec((None, N_MEM, w), lambda bi, i: (bi, 0, 1))],
        out_specs=pl.BlockSpec((None, tq, w), lambda bi, i: (bi, i, 0)),
        out_shape=jax.ShapeDtypeStruct((b, s, w), BF16),
        compiler_params=_cparams(("parallel", "parallel")),
    )(pv, kvv, kvv)
    return yd.reshape(b * s, w)


def _merge_kernel(x_ref, ao0_ref, ao1_ref, ao2_ref, al0_ref, al1_ref, al2_ref, zt_ref, yc_ref, yd_ref, gl_ref,
                  wglut_ref, bglu_ref, wba_ref, wbb_ref, wbc_ref, wbd_ref, wout_ref, g1_ref, b1_ref,
                  x1_ref):
    tm = x_ref.shape[0]
    slabs = lambda ref: jnp.concatenate([ref[c] for c in range(ref.shape[0])], axis=-1)
    lses = [al0_ref[...], slabs(al1_ref), slabs(al2_ref)]
    aos = [ao0_ref[...], slabs(ao1_ref), slabs(ao2_ref)]
    mx = jnp.maximum(jnp.maximum(lses[0], lses[1]), lses[2])
    es = [jnp.exp(l - mx) for l in lses]
    tot = es[0] + es[1] + es[2]
    ya = jnp.concatenate([o * (e / tot) for o, e in zip(aos, es)], axis=-1).astype(BF16)
    zt = zt_ref[...]
    glu = jnp.dot(wglut_ref[...], zt, preferred_element_type=F32) + bglu_ref[...]
    ybt = zt.astype(F32) * _sigmoid(glu)
    yb = jnp.transpose(ybt).astype(BF16)
    merged = jnp.zeros((tm, D_MODEL), F32)
    for n, (y, w_ref) in enumerate(((ya, wba_ref), (yb, wbb_ref), (yc_ref[...], wbc_ref), (yd_ref[...], wbd_ref))):
        bo = jnp.dot(y, w_ref[...], preferred_element_type=F32)
        merged = merged + (jnp.tanh(gl_ref[:, n * D_MODEL:(n + 1) * D_MODEL].astype(F32)) + 1.0) * bo
    h = DEEPNORM_ALPHA * x_ref[...] + jnp.dot(merged.astype(BF16), wout_ref[...], preferred_element_type=F32)
    x1_ref[...] = _ln(h, g1_ref[...], b1_ref[...])


def _router_kernel(x1_ref, rwh_ref, rwl_ref, rb_ref, tri_ref, x1p_ref, route_ref, counts_ref, carry_ref):
    tm = x1_ref.shape[0]

    @pl.when(pl.program_id(0) == 0)
    def _():
        carry_ref[...] = jnp.zeros_like(carry_ref)

    x1 = x1_ref[...]
    x1p_ref[...] = _pack_bf16_pairs(x1)
    x_hi = x1.astype(BF16)
    x_lo = (x1 - x_hi.astype(F32)).astype(BF16)
    logits = (jnp.dot(x_hi, rwh_ref[...], preferred_element_type=F32)
              + jnp.dot(x_lo, rwh_ref[...], preferred_element_type=F32)
              + jnp.dot(x_hi, rwl_ref[...], preferred_element_type=F32)) + rb_ref[...]
    lane = lax.broadcasted_iota(jnp.int32, (tm, LANE), 1)
    work = logits
    sel, vals, hots = [], [], []
    for _ in range(TOP_K):
        mk = jnp.max(work, axis=-1, keepdims=True)
        ik = jnp.min(jnp.where(work == mk, lane, LANE), axis=-1, keepdims=True)
        hot = lane == ik
        work = jnp.where(hot, -jnp.inf, work)
        sel.append(ik)
        vals.append(mk)
        hots.append(hot)
    ex = [jnp.exp(v - vals[0]) for v in vals]
    den = ex[0] + ex[1] + ex[2] + ex[3]
    chosen = (hots[0] | hots[1] | hots[2] | hots[3]).astype(F32)
    before = jnp.dot(tri_ref[...], chosen.astype(BF16), preferred_element_type=F32) + carry_ref[0:1, :]
    route = jnp.zeros((tm, LANE), F32)
    for k in range(TOP_K):
        rank = jnp.sum(jnp.where(hots[k], before, 0.0), axis=-1, keepdims=True)
        route = jnp.where(lane == k, sel[k].astype(F32), route)
        route = jnp.where(lane == TOP_K + k, ex[k] / den, route)
        route = jnp.where(lane == 2 * TOP_K + k, rank, route)
    route_ref[...] = route
    new_carry = carry_ref[0:1, :] + jnp.sum(chosen, axis=0, keepdims=True)
    carry_ref[...] = jnp.broadcast_to(new_carry, carry_ref.shape)
    counts_ref[...] = jnp.broadcast_to(new_carry, counts_ref.shape)


def _merge_call(x, aos, alses, zt, yc, yd, proj, lw):
    t = x.shape[0]
    tm = 256
    gw = GW
    gate_blk = 4 * D_MODEL
    row = lambda w: pl.BlockSpec((tm, w), lambda i: (i, 0))
    slab = pl.BlockSpec((gw // LANE, tm, LANE), lambda i: (0, i, 0))
    full = lambda a: pl.BlockSpec(a.shape, lambda i: (0,) * a.ndim)
    consts = (lw["w_glu_t"], lw["b_glu"], lw["wb_a"], lw["wb_b"], lw["wb_c"], lw["wb_d"], lw["w_out"],
              lw["ln1_g"], lw["ln1_b"])
    return pl.pallas_call(
        _merge_kernel, name="branch_merge",
        grid=(t // tm,),
        in_specs=[row(D_MODEL)] + [row(gw), slab, slab] * 2
                 + [pl.BlockSpec((S5_WIDTH, tm), lambda i: (0, i)),
                  row(RET_HEADS * RET_V_DIM), row(MEM_HEADS * MEM_HEAD_PAD),
                  pl.BlockSpec((tm, gate_blk), lambda i: (i, _SEG["gate"] * LANE // gate_blk))]
                 + [full(a) for a in consts],
        out_specs=row(D_MODEL),
        out_shape=jax.ShapeDtypeStruct((t, D_MODEL), F32),
        compiler_params=_cparams(("parallel",)),
    )(x, *aos, *alses, zt, yc, yd, proj, *consts)


def _router_call(x1, lw):
    t = x1.shape[0]
    tm = 1024
    row = lambda w: pl.BlockSpec((tm, w), lambda i: (i, 0))
    full = lambda a: pl.BlockSpec(a.shape, lambda i: (0,) * a.ndim)
    tri = jnp.tril(jnp.ones((tm, tm), BF16), -1)
    consts = (lw["router_w_hi"], lw["router_w_lo"], lw["router_b"], tri)
    return pl.pallas_call(
        _router_kernel, name="moe_router",
        grid=(t // tm,),
        in_specs=[row(D_MODEL)] + [full(a) for a in consts],
        out_specs=[row(D_MODEL // 2), row(LANE), pl.BlockSpec((8, LANE), lambda i: (0, 0))],
        out_shape=[jax.ShapeDtypeStruct((t, D_MODEL // 2), jnp.uint32),
                   jax.ShapeDtypeStruct((t, LANE), F32),
                   jax.ShapeDtypeStruct((8, LANE), F32)],
        scratch_shapes=[pltpu.VMEM((8, LANE), F32)],
        compiler_params=_cparams(("arbitrary",)),
    )(x1, *consts)


def _pack_bf16_pairs(x):
    n = x.shape[1] // 2
    hi = lax.bitcast_convert_type(x[:, :n].astype(BF16).astype(F32), jnp.uint32)
    lo = lax.bitcast_convert_type(x[:, n:].astype(BF16).astype(F32), jnp.uint32)
    return hi | (lo >> 16)


def _unpack_bf16_pairs(w):
    hi = lax.bitcast_convert_type(w & jnp.uint32(0xFFFF0000), F32)
    lo = lax.bitcast_convert_type(w << 16, F32)
    return jnp.concatenate([hi, lo], axis=-1)


def _sc_index_rows(idx):
    window = SC_GATHER_WINDOW
    return jnp.pad(idx.reshape(-1, window), ((0, 0), (0, LANE - window)))


def _sc_scatter_rows(x, idx, n_out):
    n, d = x.shape
    m = idx.shape[0]
    window = SC_GATHER_WINDOW
    src_steps = n // window
    mesh = plsc.VectorSubcoreMesh(core_axis_name="core", subcore_axis_name="subcore")

    @pl.kernel(out_type=jax.ShapeDtypeStruct((n_out, d), x.dtype), mesh=mesh, scratch_types=[])
    def scatter_kernel(x_hbm, i_hbm, o_hbm):
        def body(x_vmem, i_vmem):
            pltpu.sync_copy(x_vmem, o_hbm.at[i_vmem.at[0, pl.ds(0, window)]])

        pltpu.emit_pipeline(
            body,
            grid=(m // window,),
            in_specs=[pl.BlockSpec((window, d), index_map=lambda i: (i % src_steps, 0)),
                      pl.BlockSpec((1, LANE), index_map=lambda i: (i, 0))],
            out_specs=[],
            core_axis_name=("core", "subcore"),
            dimension_semantics=(pltpu.PARALLEL,),
            trace_scopes=False,
        )(x_hbm, i_hbm)

    return scatter_kernel(x, _sc_index_rows(idx))


def _sc_gather_rows(x, idx):
    m, d = idx.shape[0], x.shape[1]
    window = SC_GATHER_WINDOW
    mesh = plsc.VectorSubcoreMesh(core_axis_name="core", subcore_axis_name="subcore")

    @pl.kernel(out_type=jax.ShapeDtypeStruct((m, d), x.dtype), mesh=mesh, scratch_types=[])
    def gather_kernel(x_hbm, i_hbm, o_hbm):
        def body(i_vmem, o_vmem):
            pltpu.sync_copy(x_hbm.at[i_vmem.at[0, pl.ds(0, window)]], o_vmem)

        pltpu.emit_pipeline(
            body,
            grid=(m // window,),
            in_specs=[pl.BlockSpec((1, LANE), index_map=lambda i: (i, 0))],
            out_specs=[pl.BlockSpec((window, d), index_map=lambda i: (i, 0))],
            core_axis_name=("core", "subcore"),
            dimension_semantics=(pltpu.PARALLEL,),
            trace_scopes=False,
        )(i_hbm, o_hbm)

    return gather_kernel(x, _sc_index_rows(idx))


def _moe_expert_kernel(be_ref, nu_ref, nv_ref, x_ref, w1_ref, b1_ref, w2_ref, b2_ref, o_ref, w1s_ref, w2s_ref):
    i = pl.program_id(0)

    @pl.when((i == 0) | (be_ref[i] != be_ref[jnp.maximum(i - 1, 0)]))
    def _():
        w1s_ref[...] = w1_ref[...].astype(BF16)
        w2s_ref[...] = w2_ref[...].astype(BF16)

    @pl.when(i < nu_ref[0])
    def _():
        rows = lax.broadcasted_iota(jnp.int32, (x_ref.shape[0], 1), 0)
        x = jnp.where(rows < nv_ref[i], _unpack_bf16_pairs(x_ref[...]), 0.0)
        h = jnp.dot(x.astype(BF16), w1s_ref[...], preferred_element_type=F32) + b1_ref[...]
        gate = jnp.minimum(h[:, :D_FF], SWIGLU_LIMIT)
        up = jnp.clip(h[:, D_FF:], -SWIGLU_LIMIT, SWIGLU_LIMIT)
        hid = gate * _sigmoid(SWIGLU_ALPHA * gate) * (up + 1.0)
        y = jnp.dot(hid.astype(BF16), w2s_ref[...], preferred_element_type=F32) + b2_ref[...]
        o_ref[...] = _pack_bf16_pairs(y)

    @pl.when(i >= nu_ref[0])
    def _():
        o_ref[...] = jnp.zeros_like(o_ref)


def _moe_expert_call(xg, block_expert, n_used, n_valid, layer, w1, b1, w2, b2):
    bm = MOE_BLOCK_ROWS
    n_blocks = xg.shape[0] // bm
    half = D_MODEL // 2
    grid_spec = pltpu.PrefetchScalarGridSpec(
        num_scalar_prefetch=3, grid=(n_blocks,),
        in_specs=[pl.BlockSpec((bm, half), lambda i, be, nu, nv: (i, 0)),
                  pl.BlockSpec((None, None, D_MODEL, 2 * D_FF), lambda i, be, nu, nv: (layer, be[i], 0, 0)),
                  pl.BlockSpec((None, 1, 2 * D_FF), lambda i, be, nu, nv: (be[i], 0, 0)),
                  pl.BlockSpec((None, None, D_FF, D_MODEL), lambda i, be, nu, nv: (layer, be[i], 0, 0)),
                  pl.BlockSpec((None, 1, D_MODEL), lambda i, be, nu, nv: (be[i], 0, 0))],
        out_specs=pl.BlockSpec((bm, half), lambda i, be, nu, nv: (i, 0)),
        scratch_shapes=[pltpu.VMEM((D_MODEL, 2 * D_FF), BF16), pltpu.VMEM((D_FF, D_MODEL), BF16)])
    return pl.pallas_call(
        _moe_expert_kernel, name="moe_experts",
        grid_spec=grid_spec,
        out_shape=jax.ShapeDtypeStruct((n_blocks * bm, half), jnp.uint32),
        compiler_params=_cparams(("arbitrary",)),
    )(block_expert, n_used, n_valid, xg, w1, b1, w2, b2)


def _moe_combine_kernel(x_ref, route_ref, y0_ref, y1_ref, y2_ref, y3_ref, g_ref, b_ref, o_ref):
    route = route_ref[...]
    moe = jnp.zeros(x_ref.shape, F32)
    for k, y_ref in enumerate((y0_ref, y1_ref, y2_ref, y3_ref)):
        moe = moe + route[:, TOP_K + k:TOP_K + k + 1] * _unpack_bf16_pairs(y_ref[...])
    o_ref[...] = _ln(DEEPNORM_ALPHA * x_ref[...] + moe, g_ref[...], b_ref[...])


def _moe_combine_call(x1, route, yg, g2, b2):
    t = x1.shape[0]
    tm = 512
    nt = t // tm
    ysel = lambda k: pl.BlockSpec((tm, D_MODEL // 2), lambda i: (k * nt + i, 0))
    return pl.pallas_call(
        _moe_combine_kernel, name="moe_combine_ln",
        grid=(nt,),
        in_specs=[pl.BlockSpec((tm, D_MODEL), lambda i: (i, 0)),
                  pl.BlockSpec((tm, LANE), lambda i: (i, 0))]
                 + [ysel(k) for k in range(TOP_K)]
                 + [pl.BlockSpec((1, D_MODEL), lambda i: (0, 0)),
                    pl.BlockSpec((1, D_MODEL), lambda i: (0, 0))],
        out_specs=pl.BlockSpec((tm, D_MODEL), lambda i: (i, 0)),
        out_shape=jax.ShapeDtypeStruct((t, D_MODEL), F32),
        compiler_params=_cparams(("parallel",)),
    )(x1, route, yg, yg, yg, yg, g2, b2)


def _moe(x1, x1p, route, counts, lw):
    t = x1.shape[0]
    bm = MOE_BLOCK_ROWS
    n_assign = t * TOP_K
    n_blocks = n_assign // bm + N_EXPERTS
    idx = route[:, :TOP_K].astype(jnp.int32)
    rank = route[:, 2 * TOP_K:3 * TOP_K].astype(jnp.int32)
    cnt = counts[0, :N_EXPERTS].astype(jnp.int32)
    padded = (cnt + bm - 1) // bm * bm
    pad_ends = jnp.cumsum(padded)
    pad_starts = pad_ends - padded
    dest = jnp.transpose(jnp.take(pad_starts, idx) + rank).reshape(-1)
    block_start = jnp.arange(n_blocks, dtype=jnp.int32) * bm
    block_expert = jnp.minimum(jnp.sum(pad_ends[None, :] <= block_start[:, None], axis=1),
                               N_EXPERTS - 1).astype(jnp.int32)
    n_valid = jnp.clip(jnp.take(cnt, block_expert) - (block_start - jnp.take(pad_starts, block_expert)), 0, bm)
    n_used = (pad_ends[-1:] // bm).astype(jnp.int32)
    xg = _sc_scatter_rows(x1p, dest, n_blocks * bm)
    yb = _moe_expert_call(xg, block_expert, n_used, n_valid.astype(jnp.int32), lw["layer"],
                          lw["moe_w1"], lw["moe_b1"], lw["moe_w2"], lw["moe_b2"])
    yg = _sc_gather_rows(yb, dest)
    return _moe_combine_call(x1, route, yg, lw["ln2_g"], lw["ln2_b"])


def _rotate_half_cols(w):
    half = RET_QK_DIM // 2
    w4 = w.reshape(w.shape[0], RET_HEADS, 2, half)
    return jnp.concatenate([-w4[:, :, 1:], w4[:, :, :1]], axis=2).reshape(w.shape)


def _pad_heads(w, axis):
    shp = list(w.shape)
    shp[axis:axis + 1] = [MEM_HEADS, MEM_HEAD_DIM]
    w = w.reshape(shp)
    pad = [(0, 0)] * w.ndim
    pad[axis + 1] = (0, MEM_HEAD_PAD - MEM_HEAD_DIM)
    w = jnp.pad(w, pad)
    shp[axis:axis + 2] = [MEM_HEADS * MEM_HEAD_PAD]
    return w.reshape(shp)


def _layer_weights(l, p):
    w_in = p["w_in"][l]
    aw, sw, qkw, vw, mw = 768, S5_WIDTH, RET_HEADS * RET_QK_DIM, RET_HEADS * RET_V_DIM, 768
    offs = [0]
    for wdt in (aw, aw, aw, sw, qkw, qkw, vw, vw, mw, 4 * D_MODEL):
        offs.append(offs[-1] + wdt)
    aq, ak, av, su, rq, rk, rv, rg, mq, gate = (w_in[:, offs[i]:offs[i + 1]] for i in range(10))
    seg = dict(gate=0.5 * gate, mq=_pad_heads(mq, 1), aq=aq[:, :GW], ak=ak[:, :GW], av=av[:, :GW],
               pad=jnp.zeros((D_MODEL, 2 * LANE), F32), rv=rv, rg=rg,
               rq=rq, rqs=_rotate_half_cols(rq), rk=rk, rks=_rotate_half_cols(rk))
    order = sorted(_SEG, key=_SEG.get)
    w_p = jnp.concatenate([seg[n] for n in order], axis=1).astype(BF16)
    assert w_p.shape[1] == NP
    w_d = jnp.concatenate([w[:, g * GW:(g + 1) * GW] for g in (1, 2) for w in (aq, ak, av)], axis=1).astype(BF16)
    wkv = p["w_mem_kv"][l]
    w_kv = jnp.concatenate([_pad_heads(wkv[:, :768], 1), _pad_heads(wkv[:, 768:], 1)], axis=1).astype(BF16)
    wb = p["w_branch"][l]
    rw = jnp.pad(p["router_w"][l].astype(F32), ((0, 0), (0, LANE - N_EXPERTS)))
    rb = jnp.pad(p["router_b"][l].astype(F32), (0, LANE - N_EXPERTS), constant_values=NEG_INF)
    return dict(
        w_p=w_p, w_d=w_d, w_su_t=jnp.transpose(su).astype(BF16), w_kv=w_kv,
        w_glu_t=jnp.transpose(p["s5_w_glu"][l]).astype(BF16), b_glu=p["s5_b_glu"][l].astype(F32).reshape(-1, 1),
        wb_a=wb[0].astype(BF16), wb_b=wb[1].astype(BF16), wb_c=wb[2].astype(BF16),
        wb_d=_pad_heads(wb[3], 0).astype(BF16), w_out=(0.5 * p["w_out"][l]).astype(BF16),
        ln1_g=p["ln1_g"][l].reshape(1, -1), ln1_b=p["ln1_b"][l].reshape(1, -1),
        router_w_hi=rw.astype(BF16), router_w_lo=(rw - rw.astype(BF16).astype(F32)).astype(BF16),
        router_b=rb.reshape(1, -1),
        layer=l, moe_w1=p["moe_w1"], moe_b1=p["moe_b1"][l].astype(F32).reshape(N_EXPERTS, 1, -1),
        moe_w2=p["moe_w2"], moe_b2=p["moe_b2"][l].astype(F32).reshape(N_EXPERTS, 1, -1),
        ln2_g=p["ln2_g"][l].reshape(1, -1), ln2_b=p["ln2_b"][l].reshape(1, -1),
        s5=_s5_tables(p["s5_lam_re"][l], p["s5_lam_im"][l], p["s5_log_step"][l], p["s5_b_re"][l],
                      p["s5_b_im"][l], p["s5_c_re"][l], p["s5_c_im"][l], p["s5_d"][l]))


def _trunk_layer(x, mem2d, b, s, lw, attn_bias, ret_tabs):
    proj, ut, pd = _proj_call(x, lw["w_p"], lw["w_su_t"], lw["w_d"])
    outs = [_attn_call(proj, attn_bias[0], b, s)]
    outs += [_dil_attn_call(pd, attn_bias[g], b, s, g, A_GROUPS[g][1]) for g in (1, 2)]
    aos, alses = [o for o, _ in outs], [l for _, l in outs]
    toep, win, wout, a_re, a_im = lw["s5"]
    nc = s // S5_CHUNK
    c_re, c_im = _s5_contrib_call(ut, win, b, nc)
    s_re, s_im = _s5_scan_call(c_re, c_im, a_re, a_im)
    zt = _s5_out_call(ut, toep, s_re, s_im, wout, b, nc)
    yc = _retention_call(proj, ret_tabs, b, s)
    kv = _mem_kv_call(mem2d, lw["w_kv"])
    yd = _mem_attn_call(proj, kv, b, s)
    x1 = _merge_call(x, aos, alses, zt, yc, yd, proj, lw)
    x1p, route, counts = _router_call(x1, lw)
    return _moe(x1, x1p, route, counts, lw)


def kernel(x_prompt, x_sample, mem_prompt, mem_sample, ln_in_g, ln_in_b, rel_bias, w_in, s5_lam_re, s5_lam_im, s5_log_step, s5_b_re, s5_b_im, s5_c_re, s5_c_im, s5_d, s5_w_glu, s5_b_glu, w_mem_kv, w_branch, w_out, ln1_g, ln1_b, router_w, router_b, moe_w1, moe_b1, moe_w2, moe_b2, ln2_g, ln2_b):
    p = dict(w_in=w_in, s5_lam_re=s5_lam_re, s5_lam_im=s5_lam_im, s5_log_step=s5_log_step, s5_b_re=s5_b_re,
             s5_b_im=s5_b_im, s5_c_re=s5_c_re, s5_c_im=s5_c_im, s5_d=s5_d, s5_w_glu=s5_w_glu, s5_b_glu=s5_b_glu,
             w_mem_kv=w_mem_kv, w_branch=w_branch, w_out=w_out, ln1_g=ln1_g, ln1_b=ln1_b, router_w=router_w,
             router_b=router_b, moe_w1=moe_w1, moe_b1=moe_b1, moe_w2=moe_w2, moe_b2=moe_b2, ln2_g=ln2_g,
             ln2_b=ln2_b)
    attn_bias = [_attn_bias_table(rel_bias, g, dil) for g, (_, dil) in enumerate(A_GROUPS)]
    trunks = []
    for x, mem in ((x_prompt, mem_prompt), (x_sample, mem_sample)):
        b, s, _ = x.shape
        trunks.append(dict(x=_layer_norm_call(x.reshape(b * s, D_MODEL), ln_in_g, ln_in_b),
                           mem=mem.reshape(b * N_MEM, D_MODEL), b=b, s=s, tabs=_ret_tables(s)))
    for l in range(DEPTH):
        lw = _layer_weights(l, p)
        for tr in trunks:
            tr["x"] = _trunk_layer(tr["x"], tr["mem"], tr["b"], tr["s"], lw, attn_bias, tr["tabs"])
    return tuple(tr["x"].reshape(tr["b"], tr["s"], D_MODEL) for tr in trunks)
```

```python
import functools
import math

import jax
import jax.numpy as jnp
from jax import lax
from jax.experimental import pallas as pl
from jax.experimental.pallas import tpu as pltpu
from jax.experimental.pallas import tpu_sc as plsc

F32 = jnp.float32
BF16 = jnp.bfloat16
HIGHEST = lax.Precision.HIGHEST

D_MODEL = 1024
DEPTH = 4
N_MEM = 256
A_GROUPS = ((128, 1), (512, 4), (2048, 16))
A_HEADS_PER_GROUP = 4
A_HEADS = 12
A_HEAD_DIM = 64
A_RADIUS = 64
REL_BUCKETS = 32
REL_MAX_DIST = 1024
S5_GROUP = 16
S5_WIDTH = 768
S5_GROUPS = 48
S5_STATE = 64
RET_HEADS = 6
RET_QK_DIM = 64
RET_V_DIM = 128
RET_CHUNK = 128
ROPE_BASE = 10000.0
MEM_HEADS = 4
MEM_HEAD_DIM = 192
MEM_HEAD_PAD = 256
N_EXPERTS = 32
TOP_K = 4
D_FF = 1024
SWIGLU_LIMIT = 7.0
SWIGLU_ALPHA = 1.702
LN_EPS = 1e-5
DEEPNORM_ALPHA = (2 * DEPTH) ** 0.25
NEG_INF = -1e30

LANE = 128
VMEM_LIMIT = 56 * 1024 * 1024
S5_CHUNK = 128
MOE_BLOCK_ROWS = 512
SC_GATHER_WINDOW = 64

_SEG = {}
_off = 0
for _name, _width, _align in (
        ("gate", 32, 8), ("mq", 8, 8), ("aq", 2, 2), ("ak", 2, 2), ("av", 2, 2), ("pad", 2, 2),
        ("rv", 6, 6), ("rg", 6, 6), ("rq", 3, 3), ("rqs", 3, 3), ("rk", 3, 3), ("rks", 3, 3)):
    assert _off % _align == 0, (_name, _off)
    _SEG[_name] = _off
    _off += _width
NP_UNITS = _off
NP = NP_UNITS * LANE
assert NP % 256 == 0
GW = A_HEADS_PER_GROUP * A_HEAD_DIM
ND = 2 * 3 * GW


def _cparams(sem, vmem=VMEM_LIMIT):
    return pltpu.CompilerParams(dimension_semantics=sem, vmem_limit_bytes=vmem)


def _sigmoid(x):
    return 0.5 * jnp.tanh(0.5 * x) + 0.5


def _ln(h, g, b):
    mu = jnp.mean(h, axis=-1, keepdims=True)
    d = h - mu
    var = jnp.mean(d * d, axis=-1, keepdims=True)
    return d * lax.rsqrt(var + LN_EPS) * g + b


def _ln_kernel(x_ref, g_ref, b_ref, o_ref):
    o_ref[...] = _ln(x_ref[...], g_ref[...], b_ref[...])


def _layer_norm_call(x, g, b):
    t = x.shape[0]
    tm = 512
    return pl.pallas_call(
        _ln_kernel, name="input_ln",
        grid=(t // tm,),
        in_specs=[pl.BlockSpec((tm, D_MODEL), lambda i: (i, 0)),
                  pl.BlockSpec((1, D_MODEL), lambda i: (0, 0)),
                  pl.BlockSpec((1, D_MODEL), lambda i: (0, 0))],
        out_specs=pl.BlockSpec((tm, D_MODEL), lambda i: (i, 0)),
        out_shape=jax.ShapeDtypeStruct((t, D_MODEL), F32),
        compiler_params=_cparams(("parallel",)),
    )(x, g.reshape(1, -1), b.reshape(1, -1))


def _proj_kernel(x_ref, w_ref, p_ref):
    p_ref[...] = jnp.dot(x_ref[...].astype(BF16), w_ref[...], preferred_element_type=F32).astype(BF16)


def _proj_aux_kernel(x_ref, wsut_ref, wd_ref, ut_ref, pd_ref):
    xb = x_ref[...].astype(BF16)
    ut_ref[...] = lax.dot_general(wsut_ref[...], xb, (((1,), (1,)), ((), ())),
                                  preferred_element_type=F32).astype(BF16)
    pd = jnp.dot(xb, wd_ref[...], preferred_element_type=F32)
    for c in range(ND // LANE):
        pd_ref[c] = pd[:, c * LANE:(c + 1) * LANE]


def _proj_call(x, w_p, w_su_t, w_d):
    t = x.shape[0]
    tm, tn = 512, NP // 3
    proj = pl.pallas_call(
        _proj_kernel, name="in_proj",
        grid=(NP // tn, t // tm),
        in_specs=[pl.BlockSpec((tm, D_MODEL), lambda j, i: (i, 0)),
                  pl.BlockSpec((D_MODEL, tn), lambda j, i: (0, j))],
        out_specs=pl.BlockSpec((tm, tn), lambda j, i: (i, j)),
        out_shape=jax.ShapeDtypeStruct((t, NP), BF16),
        compiler_params=_cparams(("parallel", "parallel")),
    )(x, w_p)
    ut, pd = pl.pallas_call(
        _proj_aux_kernel, name="in_proj_s5_dilated",
        grid=(t // tm,),
        in_specs=[pl.BlockSpec((tm, D_MODEL), lambda i: (i, 0)),
                  pl.BlockSpec((S5_WIDTH, D_MODEL), lambda i: (0, 0)),
                  pl.BlockSpec((D_MODEL, ND), lambda i: (0, 0))],
        out_specs=[pl.BlockSpec((S5_WIDTH, tm), lambda i: (0, i)),
                   pl.BlockSpec((ND // LANE, tm, LANE), lambda i: (0, i, 0))],
        out_shape=[jax.ShapeDtypeStruct((S5_WIDTH, t), BF16),
                   jax.ShapeDtypeStruct((ND // LANE, t, LANE), F32)],
        compiler_params=_cparams(("parallel",)),
    )(x, w_su_t, w_d)
    return proj, ut, pd


ATTN_TQ = 2 * A_RADIUS
ATTN_PAIR = 4


def _banded_heads(blocks, bias_ref, sub_len):
    tq, nk = blocks[0][0].shape[0], blocks[0][1].shape[0]
    width = blocks[0][0].shape[1]
    col = lax.broadcasted_iota(jnp.int32, (tq, nk), 1)
    valids = [(blk[3] + col >= 0) & (blk[3] + col < sub_len) for blk in blocks]
    lane_head = lax.broadcasted_iota(jnp.int32, (1, width), 1) // A_HEAD_DIM
    scale = jnp.asarray(A_HEAD_DIM ** -0.5, BF16)
    pairs = [(bk, h) for bk in range(len(blocks)) for h in range(A_HEADS_PER_GROUP)]
    qs = [jnp.where(lane_head == h, blocks[bk][0], jnp.zeros_like(blocks[bk][0])) * scale for bk, h in pairs]
    ss = [lax.dot_general(qh, blocks[bk][1], (((1,), (1,)), ((), ())), preferred_element_type=F32)
          for qh, (bk, h) in zip(qs, pairs)]
    ss = [jnp.where(valids[bk], s + bias_ref[h], NEG_INF) for s, (bk, h) in zip(ss, pairs)]
    ms = [jnp.max(s, axis=-1, keepdims=True) for s in ss]
    ps = [jnp.exp(s - m) for s, m in zip(ss, ms)]
    dens = [jnp.sum(p, axis=-1, keepdims=True) for p in ps]
    ohs = [jnp.dot(p.astype(BF16), blocks[bk][2], preferred_element_type=F32) for p, (bk, h) in zip(ps, pairs)]
    outs = []
    for bk in range(len(blocks)):
        acc = jnp.zeros((tq, width), F32)
        lse_acc = jnp.zeros((tq, width), F32)
        for j, (pb, h) in enumerate(pairs):
            if pb == bk:
                hm = lane_head == h
                acc = jnp.where(hm, ohs[j] / dens[j], acc)
                lse_acc = jnp.where(hm, ms[j] + jnp.log(dens[j]), lse_acc)
        outs.append((acc, lse_acc))
    return outs


def _attn_kernel(q_ref, kp_ref, kc_ref, kn_ref, vp_ref, vc_ref, vn_ref, bias_ref, o_ref, lse_ref, *, sub_len):
    tq = ATTN_TQ
    kcat = jnp.concatenate([kp_ref[...], kc_ref[...], kn_ref[...]], axis=0)
    vcat = jnp.concatenate([vp_ref[...], vc_ref[...], vn_ref[...]], axis=0)
    first = pl.program_id(1) * (ATTN_PAIR * tq) - A_RADIUS
    blocks = [(q_ref[j * tq:(j + 1) * tq, :], kcat[j * tq:j * tq + 2 * tq, :], vcat[j * tq:j * tq + 2 * tq, :],
               first + j * tq) for j in range(ATTN_PAIR)]
    for j, (acc, lse) in enumerate(_banded_heads(blocks, bias_ref, sub_len)):
        o_ref[j * tq:(j + 1) * tq, :] = acc
        lse_ref[j * tq:(j + 1) * tq, :] = lse


def _attn_call(proj, bias, b, s):
    rows = ATTN_PAIR * ATTN_TQ
    nstep = s // rows
    edge = rows // A_RADIUS
    pv = proj.reshape(b, s, NP)
    qo, ko, vo = (_SEG[n] * LANE // GW for n in ("aq", "ak", "av"))
    blk, eblk = (None, rows, GW), (None, A_RADIUS, GW)
    cur = lambda off: (lambda bi, i: (bi, i, off))
    prv = lambda off: (lambda bi, i: (bi, jnp.maximum(i * edge - 1, 0), off))
    nxt = lambda off: (lambda bi, i: (bi, jnp.minimum((i + 1) * edge, nstep * edge - 1), off))
    out_map = lambda bi, i: (bi, i, 0)
    o, lse = pl.pallas_call(
        functools.partial(_attn_kernel, sub_len=s), name="window_attn_g0",
        grid=(b, nstep),
        in_specs=[pl.BlockSpec(blk, cur(qo)),
                  pl.BlockSpec(eblk, prv(ko)), pl.BlockSpec(blk, cur(ko)), pl.BlockSpec(eblk, nxt(ko)),
                  pl.BlockSpec(eblk, prv(vo)), pl.BlockSpec(blk, cur(vo)), pl.BlockSpec(eblk, nxt(vo)),
                  pl.BlockSpec((A_HEADS_PER_GROUP, ATTN_TQ, 2 * ATTN_TQ), lambda bi, i: (0, 0, 0))],
        out_specs=[pl.BlockSpec(blk, out_map), pl.BlockSpec(blk, out_map)],
        out_shape=[jax.ShapeDtypeStruct((b, s, GW), F32), jax.ShapeDtypeStruct((b, s, GW), F32)],
        compiler_params=_cparams(("parallel", "parallel")),
    )(pv, pv, pv, pv, pv, pv, pv, bias)
    return o.reshape(b * s, GW), lse.reshape(b * s, GW)


def _dil_attn_kernel(q_ref, kp_ref, kc_ref, kn_ref, vp_ref, vc_ref, vn_ref, bias_ref, o_ref, lse_ref, *,
                     dil, sub_len):
    tq, half = ATTN_TQ, A_RADIUS
    first_key = pl.program_id(1) * tq - half

    def residues(it, carry):
        def rows(ref, r, n):
            return jnp.concatenate([ref[c, pl.ds(r, n, stride=dil), :] for c in range(GW // LANE)], axis=-1)

        rs = [it * ATTN_PAIR + j for j in range(ATTN_PAIR)]
        blocks = []
        for r in rs:
            q = rows(q_ref, r, tq).astype(BF16)
            kwin = jnp.concatenate([rows(kp_ref, r, half), rows(kc_ref, r, tq), rows(kn_ref, r, half)],
                                   axis=0).astype(BF16)
            vwin = jnp.concatenate([rows(vp_ref, r, half), rows(vc_ref, r, tq), rows(vn_ref, r, half)],
                                   axis=0).astype(BF16)
            blocks.append((q, kwin, vwin, first_key))
        for r, (acc, lse) in zip(rs, _banded_heads(blocks, bias_ref, sub_len)):
            for c in range(GW // LANE):
                o_ref[c, pl.ds(r, tq, stride=dil), :] = acc[:, c * LANE:(c + 1) * LANE]
                lse_ref[c, pl.ds(r, tq, stride=dil), :] = lse[:, c * LANE:(c + 1) * LANE]
        return carry

    lax.fori_loop(0, dil // ATTN_PAIR, residues, 0)


def _dil_attn_call(pd, bias, b, s, g, dil):
    tq = ATTN_TQ
    rows = tq * dil
    nblk = s // rows
    nslab = GW // LANE
    pv = pd.reshape(ND // LANE, b, s, LANE)
    qo, ko, vo = (3 * (g - 1) + j for j in range(3))
    blk, hblk = (nslab, None, rows, LANE), (nslab, None, rows // 2, LANE)
    cur = lambda off: (lambda bi, i: (off, bi, i, 0))
    prv = lambda off: (lambda bi, i: (off, bi, jnp.maximum(2 * i - 1, 0), 0))
    nxt = lambda off: (lambda bi, i: (off, bi, jnp.minimum(2 * i + 2, 2 * nblk - 1), 0))
    out_map = lambda bi, i: (0, bi, i, 0)
    o, lse = pl.pallas_call(
        functools.partial(_dil_attn_kernel, dil=dil, sub_len=s // dil), name=f"dilated_attn_g{g}",
        grid=(b, nblk),
        in_specs=[pl.BlockSpec(blk, cur(qo)),
                  pl.BlockSpec(hblk, prv(ko)), pl.BlockSpec(blk, cur(ko)), pl.BlockSpec(hblk, nxt(ko)),
                  pl.BlockSpec(hblk, prv(vo)), pl.BlockSpec(blk, cur(vo)), pl.BlockSpec(hblk, nxt(vo)),
                  pl.BlockSpec((A_HEADS_PER_GROUP, tq, 2 * tq), lambda bi, i: (0, 0, 0))],
        out_specs=[pl.BlockSpec(blk, out_map), pl.BlockSpec(blk, out_map)],
        out_shape=[jax.ShapeDtypeStruct((nslab, b, s, LANE), F32)] * 2,
        compiler_params=_cparams(("parallel", "parallel")),
    )(pv, pv, pv, pv, pv, pv, pv, bias)
    return o.reshape(nslab, b * s, LANE), lse.reshape(nslab, b * s, LANE)


def _t5_bucket(rel):
    nb = REL_BUCKETS // 2
    max_exact = nb // 2
    ret = jnp.where(rel > 0, nb, 0)
    n = jnp.abs(rel)
    nf = jnp.maximum(n, 1).astype(F32)
    large = max_exact + (jnp.log(nf / max_exact) / math.log(REL_MAX_DIST / max_exact)
                         * (nb - max_exact)).astype(jnp.int32)
    large = jnp.minimum(large, nb - 1)
    return ret + jnp.where(n < max_exact, n, large)


def _attn_bias_table(rel_bias, g, dil, tq=128):
    qi = jnp.arange(tq)[:, None]
    kj = jnp.arange(2 * tq)[None, :] - A_RADIUS
    rel = kj - qi
    tab = rel_bias[:, g * A_HEADS_PER_GROUP:(g + 1) * A_HEADS_PER_GROUP].astype(F32)
    bucket = _t5_bucket(rel * dil)[None]
    bias = jnp.zeros((A_HEADS_PER_GROUP,) + rel.shape, F32)
    for bkt in range(REL_BUCKETS):
        bias = jnp.where(bucket == bkt, tab[bkt][:, None, None], bias)
    return jnp.where((jnp.abs(rel) <= A_RADIUS)[None], bias, NEG_INF)


def _s5_toeplitz_kernel(bcf_ref, pwf_ref, bcb_ref, pwb_ref, d_ref, t_ref, gf_ref, gb_ref):
    gf_ref[...] = jnp.dot(bcf_ref[...], pwf_ref[...], precision=HIGHEST, preferred_element_type=F32)
    gb_ref[...] = jnp.dot(bcb_ref[...], pwb_ref[...], precision=HIGHEST, preferred_element_type=F32)
    c = S5_CHUNK
    row = lax.broadcasted_iota(jnp.int32, (c, c), 0)
    col = lax.broadcasted_iota(jnp.int32, (c, c), 1)

    def body(ci, carry):
        dval = d_ref[pl.ds(ci, 1), :]
        for co in range(S5_GROUP):
            r = ci * S5_GROUP + co
            gf = jnp.broadcast_to(gf_ref[pl.ds(r, 1), :], (c, c))
            gb = jnp.broadcast_to(gb_ref[pl.ds(r, 1), :], (c, c))
            tf = pltpu.roll(gf, 0, 1, stride=1, stride_axis=0)
            tb = pltpu.roll(gb, 1, 1, stride=1, stride_axis=0)
            tile = jnp.where(col >= row, tf, 0.0) + jnp.where(row >= col, tb, 0.0)
            tile = tile + jnp.where((row == col) & (ci == co), dval, 0.0)
            t_ref[pl.ds(pl.multiple_of(ci * c, c), c), co * c:(co + 1) * c] = tile.astype(BF16)
        return carry

    lax.fori_loop(0, S5_GROUP, body, 0)


def _s5_toeplitz_call(bcf, pwf, bcb, pwb, dskip):
    n = S5_GROUP * S5_CHUNK
    return pl.pallas_call(
        _s5_toeplitz_kernel, name="s5_toeplitz",
        grid=(S5_GROUPS,),
        in_specs=[pl.BlockSpec((None, S5_GROUP * S5_GROUP, 2 * S5_STATE), lambda g: (g, 0, 0)),
                  pl.BlockSpec((None, 2 * S5_STATE, S5_CHUNK), lambda g: (g, 0, 0)),
                  pl.BlockSpec((None, S5_GROUP * S5_GROUP, 2 * S5_STATE), lambda g: (g, 0, 0)),
                  pl.BlockSpec((None, 2 * S5_STATE, S5_CHUNK), lambda g: (g, 0, 0)),
                  pl.BlockSpec((None, S5_GROUP, LANE), lambda g: (g, 0, 0))],
        out_specs=pl.BlockSpec((None, n, n), lambda g: (g, 0, 0)),
        out_shape=jax.ShapeDtypeStruct((S5_GROUPS, n, n), BF16),
        scratch_shapes=[pltpu.VMEM((S5_GROUP * S5_GROUP, S5_CHUNK), F32),
                        pltpu.VMEM((S5_GROUP * S5_GROUP, S5_CHUNK), F32)],
        compiler_params=_cparams(("parallel",)),
    )(bcf, pwf, bcb, pwb, dskip)


def _s5_tables(lam_re, lam_im, log_step, b_re, b_im, c_re, c_im, d_skip):
    c = S5_CHUNK
    k = jnp.arange(c, dtype=F32)
    per_dir = []
    for direction in range(2):
        step = jnp.exp(log_step[direction].astype(F32))[:, None]
        lr, li = lam_re[direction].astype(F32), lam_im[direction].astype(F32)
        mag = jnp.exp(lr * step)
        ar, ai = mag * jnp.cos(li * step), mag * jnp.sin(li * step)
        nr, ni = ar - 1.0, ai
        den = lr * lr + li * li
        fr = (nr * lr + ni * li) / den
        fi = (ni * lr - nr * li) / den
        br, bi = b_re[direction].astype(F32), b_im[direction].astype(F32)
        bbr = fr[..., None] * br - fi[..., None] * bi
        bbi = fr[..., None] * bi + fi[..., None] * br
        cr, cim = c_re[direction].astype(F32), c_im[direction].astype(F32)

        def power(e, log_mag=lr * step, phase=li * step):
            m = jnp.exp(e[None, None, :] * log_mag[..., None])
            th = e[None, None, :] * phase[..., None]
            return m * jnp.cos(th), m * jnp.sin(th)

        per_dir.append(dict(bbr=bbr, bbi=bbi, cr=cr, cim=cim, power=power))

    def bc_table(d):
        bbr_t = jnp.transpose(d["bbr"], (0, 2, 1))[:, :, None, :]
        bbi_t = jnp.transpose(d["bbi"], (0, 2, 1))[:, :, None, :]
        cr, cim = d["cr"][:, None], d["cim"][:, None]
        re = bbr_t * cr - bbi_t * cim
        im = bbr_t * cim + bbi_t * cr
        return jnp.concatenate([re, -im], axis=-1).reshape(S5_GROUPS, S5_GROUP * S5_GROUP, 2 * S5_STATE)

    f, bw = per_dir
    pfr, pfi = f["power"](k)
    pbr, pbi = bw["power"](c - 1 - k)
    pwf = jnp.concatenate([pfr, pfi], axis=1)
    pwb = jnp.concatenate([pbr, pbi], axis=1)
    dsk = jnp.broadcast_to(d_skip.astype(F32).reshape(S5_GROUPS, S5_GROUP, 1), (S5_GROUPS, S5_GROUP, LANE))
    toep = _s5_toeplitz_call(bc_table(f), pwf, bc_table(bw), pwb, dsk)

    seg4 = lambda f_re, b_re, f_im, b_im, axis: jnp.concatenate([f_re, b_re, f_im, b_im], axis=axis)
    tp = lambda z: jnp.transpose(z, (0, 2, 1))

    pfr_i, pfi_i = f["power"](c - 1 - k)
    pbr_i, pbi_i = bw["power"](k)
    a_in = seg4(tp(f["bbr"]), tp(bw["bbr"]), tp(f["bbr"]), tp(bw["bbr"]), -1)
    b_in = seg4(-tp(f["bbi"]), -tp(bw["bbi"]), tp(f["bbi"]), tp(bw["bbi"]), -1)
    x_in = seg4(tp(pfr_i), tp(pbr_i), tp(pfi_i), tp(pbi_i), -1)
    y_in = seg4(tp(pfi_i), tp(pbi_i), tp(pfr_i), tp(pbr_i), -1)
    win = (a_in[:, :, None, :] * x_in[:, None, :, :] + b_in[:, :, None, :] * y_in[:, None, :, :]
           ).reshape(S5_GROUPS, S5_GROUP * c, 4 * S5_STATE).astype(BF16)

    pfr_o, pfi_o = f["power"](k + 1.0)
    pbr_o, pbi_o = bw["power"](c - k)
    a_out = seg4(tp(f["cr"]), tp(bw["cr"]), -tp(f["cr"]), -tp(bw["cr"]), 1)
    b_out = seg4(-tp(f["cim"]), -tp(bw["cim"]), -tp(f["cim"]), -tp(bw["cim"]), 1)
    x_out = seg4(pfr_o, pbr_o, pfi_o, pbi_o, 1)
    y_out = seg4(pfi_o, pbi_o, pfr_o, pbr_o, 1)
    wout = (a_out[:, :, :, None] * x_out[:, :, None, :] + b_out[:, :, :, None] * y_out[:, :, None, :]
            ).reshape(S5_GROUPS, 4 * S5_STATE, S5_GROUP * c).astype(BF16)

    cc = jnp.asarray([float(c)], F32)
    afr, afi = f["power"](cc)
    abr, abi = bw["power"](cc)
    a_re = jnp.concatenate([afr[..., 0], abr[..., 0]], axis=-1).reshape(1, -1)
    a_im = jnp.concatenate([afi[..., 0], abi[..., 0]], axis=-1).reshape(1, -1)
    return toep, win, wout, a_re, a_im


def _s5_contrib_kernel(u_ref, win_ref, re_ref, im_ref):
    uc = jnp.concatenate([u_ref[ci] for ci in range(S5_GROUP)], axis=-1)
    res = jnp.dot(uc, win_ref[...], preferred_element_type=F32)
    nc, nb = re_ref.shape[0], re_ref.shape[1]
    half = 2 * S5_STATE
    for bi in range(nb):
        re_ref[:, bi, :] = res[bi * nc:(bi + 1) * nc, :half]
        im_ref[:, bi, :] = res[bi * nc:(bi + 1) * nc, half:]


def _s5_contrib_call(ut, win, b, nc):
    nch = b * nc
    half = 2 * S5_STATE
    uv = ut.reshape(S5_GROUPS, S5_GROUP, nch, S5_CHUNK)
    return pl.pallas_call(
        _s5_contrib_kernel, name="s5_contrib",
        grid=(S5_GROUPS,),
        in_specs=[pl.BlockSpec((None, S5_GROUP, nch, S5_CHUNK), lambda g: (g, 0, 0, 0)),
                  pl.BlockSpec((None, S5_GROUP * S5_CHUNK, 4 * S5_STATE), lambda g: (g, 0, 0))],
        out_specs=[pl.BlockSpec((nc, b, half), lambda g: (0, 0, g))] * 2,
        out_shape=[jax.ShapeDtypeStruct((nc, b, S5_GROUPS * half), F32)] * 2,
        compiler_params=_cparams(("parallel",)),
    )(uv, win)


def _s5_scan_kernel(cre_ref, cim_ref, are_ref, aim_ref, ore_ref, oim_ref):
    nc, nb, width = cre_ref.shape
    a_re, a_im = are_ref[...], aim_ref[...]
    lane = lax.broadcasted_iota(jnp.int32, (1, width), 1)
    fwd_lane = (lane % (2 * S5_STATE)) < S5_STATE
    zero = jnp.zeros((nb, width), F32)

    def advance(s, c):
        sr, si = s
        return a_re * sr - a_im * si + cre_ref[c], a_re * si + a_im * sr + cim_ref[c]

    def up(c, s):
        ore_ref[c] = s[0]
        oim_ref[c] = s[1]
        return advance(s, c)

    lax.fori_loop(0, nc, up, (zero, zero))

    def down(i, s):
        c = nc - 1 - i
        ore_ref[c] = jnp.where(fwd_lane, ore_ref[c], s[0])
        oim_ref[c] = jnp.where(fwd_lane, oim_ref[c], s[1])
        return advance(s, c)

    lax.fori_loop(0, nc, down, (zero, zero))


def _s5_scan_call(cre, cim, a_re, a_im):
    return pl.pallas_call(
        _s5_scan_kernel, name="s5_chunk_scan",
        out_shape=[jax.ShapeDtypeStruct(cre.shape, F32)] * 2,
        compiler_params=pltpu.CompilerParams(vmem_limit_bytes=VMEM_LIMIT),
    )(cre, cim, a_re, a_im)


def _gelu_tanh(y):
    return 0.5 * y * (1.0 + jnp.tanh(math.sqrt(2.0 / math.pi) * (y + 0.044715 * (y * y * y))))


def _s5_out_kernel(u_ref, t_ref, sre_ref, sim_ref, wout_ref, z_ref):
    uc = jnp.concatenate([u_ref[ci] for ci in range(S5_GROUP)], axis=-1)
    y = jnp.dot(uc, t_ref[...], preferred_element_type=F32)
    sp = jnp.concatenate([jnp.concatenate([sre_ref[:, bi, :], sim_ref[:, bi, :]], axis=-1)
                          for bi in range(sre_ref.shape[1])], axis=0)
    y = y + jnp.dot(sp.astype(BF16), wout_ref[...], preferred_element_type=F32)
    z = _gelu_tanh(y)
    for co in range(S5_GROUP):
        z_ref[co] = z[:, co * S5_CHUNK:(co + 1) * S5_CHUNK].astype(BF16)


def _s5_out_call(ut, toep, s_re, s_im, wout, b, nc):
    nch = b * nc
    uv = ut.reshape(S5_GROUPS, S5_GROUP, nch, S5_CHUNK)
    n = S5_GROUP * S5_CHUNK
    half = 2 * S5_STATE
    zt = pl.pallas_call(
        _s5_out_kernel, name="s5_out",
        grid=(S5_GROUPS,),
        in_specs=[pl.BlockSpec((None, S5_GROUP, nch, S5_CHUNK), lambda g: (g, 0, 0, 0)),
                  pl.BlockSpec((None, n, n), lambda g: (g, 0, 0)),
                  pl.BlockSpec((nc, b, half), lambda g: (0, 0, g)),
                  pl.BlockSpec((nc, b, half), lambda g: (0, 0, g)),
                  pl.BlockSpec((None, 4 * S5_STATE, n), lambda g: (g, 0, 0))],
        out_specs=pl.BlockSpec((None, S5_GROUP, nch, S5_CHUNK), lambda g: (g, 0, 0, 0)),
        out_shape=jax.ShapeDtypeStruct((S5_GROUPS, S5_GROUP, nch, S5_CHUNK), BF16),
        compiler_params=_cparams(("parallel",)),
    )(uv, toep, s_re, s_im, wout)
    return zt.reshape(S5_WIDTH, nch * S5_CHUNK)


def _ret_tables(s):
    c = RET_CHUNK
    half = RET_QK_DIM // 2
    inv = ROPE_BASE ** (-jnp.arange(half, dtype=F32) / half)
    ang = jnp.arange(s, dtype=F32)[:, None] * inv[None, :]
    cos = jnp.tile(jnp.cos(ang), (1, 2 * RET_HEADS))
    sin = jnp.tile(jnp.sin(ang), (1, 2 * RET_HEADS))
    hidx = jnp.arange(RET_HEADS, dtype=F32)
    lgf = jnp.log1p(-jnp.exp2(-5.0 - hidx))
    lgb = jnp.log1p(-jnp.exp2(-5.5 - hidx))
    j = jnp.arange(c, dtype=F32)
    rel = j[:, None] - j[None, :]
    intra = jnp.exp(jnp.abs(rel)[None] * jnp.where(rel[None] >= 0, lgf[:, None, None], lgb[:, None, None]))
    qk_f = jnp.repeat(lgf, RET_QK_DIM)[None, :]
    qk_b = jnp.repeat(lgb, RET_QK_DIM)[None, :]
    v_f = jnp.repeat(lgf, RET_V_DIM)[None, :]
    v_b = jnp.repeat(lgb, RET_V_DIM)[None, :]
    tabs = dict(
        cos=cos, sin=sin, intra=intra,
        tail_f=jnp.exp((c - 1 - j)[:, None] * qk_f), head_b=jnp.exp(j[:, None] * qk_b),
        decq_f=jnp.exp((j + 1.0)[:, None] * qk_f), decq_b=jnp.exp((c - j)[:, None] * qk_b),
        chunk_f=jnp.exp(c * v_f), chunk_b=jnp.exp(c * v_b))
    return tabs


def _rot(x_ref, xs_ref, cos_ref, sin_ref):
    return x_ref[...].astype(F32) * cos_ref[...] + xs_ref[...].astype(F32) * sin_ref[...]


def _ret_kernel(q_ref, qs_ref, k_ref, ks_ref, v_ref, g_ref, cos_ref, sin_ref, intra_ref, tailf_ref, headb_ref,
                decqf_ref, decqb_ref, chunkf_ref, chunkb_ref, o_ref, st_ref, sfd_ref, sbd_ref, sball_ref, *, nc):
    i = pl.program_id(1)
    qk, dv = RET_QK_DIM, RET_V_DIM
    vb = v_ref[...]
    kr = _rot(k_ref, ks_ref, cos_ref, sin_ref) * (RET_QK_DIM ** -0.5)

    def update_state(weight_ref, decay_ref):
        kwt = jnp.transpose(kr * weight_ref[...])
        for h in range(RET_HEADS):
            kv = jnp.dot(kwt[h * qk:(h + 1) * qk, :].astype(BF16), vb[:, h * dv:(h + 1) * dv],
                         preferred_element_type=F32)
            st_ref[h] = st_ref[h] * decay_ref[:, h * dv:(h + 1) * dv] + kv

    @pl.when((i == 0) | (i == nc))
    def _():
        st_ref[...] = jnp.zeros_like(st_ref)

    @pl.when(i == 0)
    def _():
        sfd_ref[...] = jnp.zeros_like(sfd_ref)
        sbd_ref[...] = jnp.zeros_like(sbd_ref)

    @pl.when(i < nc)
    def _():
        c = nc - 1 - i
        sball_ref[c] = st_ref[...].astype(BF16)
        update_state(headb_ref, chunkb_ref)

    @pl.when(i >= nc)
    def _():
        c = i - nc
        for h in range(RET_HEADS):
            sfd_ref[h * qk:(h + 1) * qk, h * dv:(h + 1) * dv] = st_ref[h].astype(BF16)
            sbd_ref[h * qk:(h + 1) * qk, h * dv:(h + 1) * dv] = sball_ref[c, h]
        qr = _rot(q_ref, qs_ref, cos_ref, sin_ref)
        qb, kb = qr.astype(BF16), kr.astype(BF16)
        lane_head = lax.broadcasted_iota(jnp.int32, (1, qb.shape[1]), 1) // qk
        inter = jnp.dot((qr * decqf_ref[...]).astype(BF16), sfd_ref[...], preferred_element_type=F32)
        inter = inter + jnp.dot((qr * decqb_ref[...]).astype(BF16), sbd_ref[...], preferred_element_type=F32)
        heads = range(RET_HEADS)
        vcols = [slice(h * dv, (h + 1) * dv) for h in heads]
        qhs = [jnp.where(lane_head == h, qb, jnp.zeros_like(qb)) for h in heads]
        scs = [lax.dot_general(qh, kb, (((1,), (1,)), ((), ())), preferred_element_type=F32) * intra_ref[h]
               for h, qh in zip(heads, qhs)]
        ohs = [jnp.dot(sc.astype(BF16), vb[:, vs], preferred_element_type=F32) + inter[:, vs]
               for sc, vs in zip(scs, vcols)]
        mus = [jnp.mean(oh, axis=-1, keepdims=True) for oh in ohs]
        dds = [oh - mu for oh, mu in zip(ohs, mus)]
        vrs = [jnp.mean(dd * dd, axis=-1, keepdims=True) for dd in dds]
        on = jnp.concatenate([dd * lax.rsqrt(var + LN_EPS) for dd, var in zip(dds, vrs)], axis=-1)
        gg = g_ref[...].astype(F32)
        o_ref[...] = (on * (gg * _sigmoid(gg))).astype(BF16)
        update_state(tailf_ref, chunkf_ref)


def _retention_call(proj, tabs, b, s):
    c = RET_CHUNK
    nc = s // c
    qkw, vw = RET_HEADS * RET_QK_DIM, RET_HEADS * RET_V_DIM
    pv = proj.reshape(b, s, NP)
    qk_blk, v_blk = (None, c, qkw), (None, c, vw)
    qo, qso, ko, kso = (_SEG[n] * LANE // qkw for n in ("rq", "rqs", "rk", "rks"))
    vo, go = (_SEG[n] * LANE // vw for n in ("rv", "rg"))
    const2 = lambda shape: pl.BlockSpec(shape, lambda bi, i: (0, 0))
    chunk = lambda i: jnp.where(i < nc, nc - 1 - i, i - nc)
    fchunk = lambda i: jnp.maximum(i - nc, 0)
    both = lambda off: (lambda bi, i: (bi, chunk(i), off))
    fonly = lambda off: (lambda bi, i: (bi, fchunk(i), off))
    yc = pl.pallas_call(
        functools.partial(_ret_kernel, nc=nc), name="retention",
        grid=(b, 2 * nc),
        in_specs=[pl.BlockSpec(qk_blk, fonly(qo)), pl.BlockSpec(qk_blk, fonly(qso)),
                  pl.BlockSpec(qk_blk, both(ko)), pl.BlockSpec(qk_blk, both(kso)),
                  pl.BlockSpec(v_blk, both(vo)), pl.BlockSpec(v_blk, fonly(go)),
                  pl.BlockSpec((c, qkw), lambda bi, i: (chunk(i), 0)),
                  pl.BlockSpec((c, qkw), lambda bi, i: (chunk(i), 0)),
                  pl.BlockSpec((RET_HEADS, c, c), lambda bi, i: (0, 0, 0)),
                  const2((c, qkw)), const2((c, qkw)), const2((c, qkw)), const2((c, qkw)),
                  const2((1, vw)), const2((1, vw))],
        out_specs=pl.BlockSpec((None, c, vw), lambda bi, i: (bi, fchunk(i), 0)),
        out_shape=jax.ShapeDtypeStruct((b, s, vw), BF16),
        scratch_shapes=[pltpu.VMEM((RET_HEADS, RET_QK_DIM, RET_V_DIM), F32),
                        pltpu.VMEM((qkw, vw), BF16), pltpu.VMEM((qkw, vw), BF16),
                        pltpu.VMEM((nc, RET_HEADS, RET_QK_DIM, RET_V_DIM), BF16)],
        compiler_params=_cparams(("parallel", "arbitrary")),
    )(pv, pv, pv, pv, pv, pv, tabs["cos"], tabs["sin"], tabs["intra"], tabs["tail_f"], tabs["head_b"],
      tabs["decq_f"], tabs["decq_b"], tabs["chunk_f"], tabs["chunk_b"])
    return yc.reshape(b * s, vw)


def _mm_kernel(x_ref, w_ref, o_ref):
    o_ref[...] = jnp.dot(x_ref[...].astype(BF16), w_ref[...], preferred_element_type=F32).astype(o_ref.dtype)


def _mem_kv_call(mem2d, w_kv):
    m, n = mem2d.shape[0], w_kv.shape[1]
    tm = 512
    return pl.pallas_call(
        _mm_kernel, name="mem_kv_proj",
        grid=(m // tm,),
        in_specs=[pl.BlockSpec((tm, D_MODEL), lambda i: (i, 0)),
                  pl.BlockSpec((D_MODEL, n), lambda i: (0, 0))],
        out_specs=pl.BlockSpec((tm, n), lambda i: (i, 0)),
        out_shape=jax.ShapeDtypeStruct((m, n), BF16),
        compiler_params=_cparams(("parallel",)),
    )(mem2d, w_kv)


def _mem_attn_kernel(q_ref, k_ref, v_ref, o_ref):
    q = q_ref[...]
    cols = [slice(h * MEM_HEAD_PAD, (h + 1) * MEM_HEAD_PAD) for h in range(MEM_HEADS)]
    ss = [lax.dot_general(q[:, hs], k_ref[:, hs], (((1,), (1,)), ((), ())),
                          preferred_element_type=F32) * (MEM_HEAD_DIM ** -0.5) for hs in cols]
    ms = [jnp.max(s, axis=-1, keepdims=True) for s in ss]
    es = [jnp.exp(s - m) for s, m in zip(ss, ms)]
    ps = [e / jnp.sum(e, axis=-1, keepdims=True) for e in es]
    outs = [jnp.dot(p.astype(BF16), v_ref[:, hs], preferred_element_type=F32) for p, hs in zip(ps, cols)]
    o_ref[...] = jnp.concatenate(outs, axis=-1).astype(BF16)


def _mem_attn_call(proj, kv, b, s):
    tq = 256
    w = MEM_HEADS * MEM_HEAD_PAD
    pv = proj.reshape(b, s, NP)
    kvv = kv.reshape(b, N_MEM, 2 * w)
    qo = _SEG["mq"] * LANE // w
    yd = pl.pallas_call(
        _mem_attn_kernel, name="mem_attn",
        grid=(b, s // tq),
        in_specs=[pl.BlockSpec((None, tq, w), lambda bi, i: (bi, i, qo)),
                  pl.BlockSpec((None, N_MEM, w), lambda bi, i: (bi, 0, 0)),
                  pl.BlockSpec((None, N_MEM, w), lambda bi, i: (bi, 0, 1))],
        out_specs=pl.BlockSpec((None, tq, w), lambda bi, i: (bi, i, 0)),
        out_shape=jax.ShapeDtypeStruct((b, s, w), BF16),
        compiler_params=_cparams(("parallel", "parallel")),
    )(pv, kvv, kvv)
    return yd.reshape(b * s, w)


def _merge_kernel(x_ref, ao0_ref, ao1_ref, ao2_ref, al0_ref, al1_ref, al2_ref, zt_ref, yc_ref, yd_ref, gl_ref,
                  wglut_ref, bglu_ref, wba_ref, wbb_ref, wbc_ref, wbd_ref, wout_ref, g1_ref, b1_ref,
                  x1_ref):
    tm = x_ref.shape[0]
    slabs = lambda ref: jnp.concatenate([ref[c] for c in range(ref.shape[0])], axis=-1)
    lses = [al0_ref[...], slabs(al1_ref), slabs(al2_ref)]
    aos = [ao0_ref[...], slabs(ao1_ref), slabs(ao2_ref)]
    mx = jnp.maximum(jnp.maximum(lses[0], lses[1]), lses[2])
    es = [jnp.exp(l - mx) for l in lses]
    tot = es[0] + es[1] + es[2]
    ya = jnp.concatenate([o * (e / tot) for o, e in zip(aos, es)], axis=-1).astype(BF16)
    zt = zt_ref[...]
    glu = jnp.dot(wglut_ref[...], zt, preferred_element_type=F32) + bglu_ref[...]
    ybt = zt.astype(F32) * _sigmoid(glu)
    yb = jnp.transpose(ybt).astype(BF16)
    merged = jnp.zeros((tm, D_MODEL), F32)
    for n, (y, w_ref) in enumerate(((ya, wba_ref), (yb, wbb_ref), (yc_ref[...], wbc_ref), (yd_ref[...], wbd_ref))):
        bo = jnp.dot(y, w_ref[...], preferred_element_type=F32)
        merged = merged + (jnp.tanh(gl_ref[:, n * D_MODEL:(n + 1) * D_MODEL].astype(F32)) + 1.0) * bo
    h = DEEPNORM_ALPHA * x_ref[...] + jnp.dot(merged.astype(BF16), wout_ref[...], preferred_element_type=F32)
    x1_ref[...] = _ln(h, g1_ref[...], b1_ref[...])


def _router_kernel(x1_ref, rwh_ref, rwl_ref, rb_ref, tri_ref, x1p_ref, route_ref, counts_ref, carry_ref):
    tm = x1_ref.shape[0]

    @pl.when(pl.program_id(0) == 0)
    def _():
        carry_ref[...] = jnp.zeros_like(carry_ref)

    x1 = x1_ref[...]
    x1p_ref[...] = _pack_bf16_pairs(x1)
    x_hi = x1.astype(BF16)
    x_lo = (x1 - x_hi.astype(F32)).astype(BF16)
    logits = (jnp.dot(x_hi, rwh_ref[...], preferred_element_type=F32)
              + jnp.dot(x_lo, rwh_ref[...], preferred_element_type=F32)
              + jnp.dot(x_hi, rwl_ref[...], preferred_element_type=F32)) + rb_ref[...]
    lane = lax.broadcasted_iota(jnp.int32, (tm, LANE), 1)
    work = logits
    sel, vals, hots = [], [], []
    for _ in range(TOP_K):
        mk = jnp.max(work, axis=-1, keepdims=True)
        ik = jnp.min(jnp.where(work == mk, lane, LANE), axis=-1, keepdims=True)
        hot = lane == ik
        work = jnp.where(hot, -jnp.inf, work)
        sel.append(ik)
        vals.append(mk)
        hots.append(hot)
    ex = [jnp.exp(v - vals[0]) for v in vals]
    den = ex[0] + ex[1] + ex[2] + ex[3]
    chosen = (hots[0] | hots[1] | hots[2] | hots[3]).astype(F32)
    before = jnp.dot(tri_ref[...], chosen.astype(BF16), preferred_element_type=F32) + carry_ref[0:1, :]
    route = jnp.zeros((tm, LANE), F32)
    for k in range(TOP_K):
        rank = jnp.sum(jnp.where(hots[k], before, 0.0), axis=-1, keepdims=True)
        route = jnp.where(lane == k, sel[k].astype(F32), route)
        route = jnp.where(lane == TOP_K + k, ex[k] / den, route)
        route = jnp.where(lane == 2 * TOP_K + k, rank, route)
    route_ref[...] = route
    new_carry = carry_ref[0:1, :] + jnp.sum(chosen, axis=0, keepdims=True)
    carry_ref[...] = jnp.broadcast_to(new_carry, carry_ref.shape)
    counts_ref[...] = jnp.broadcast_to(new_carry, counts_ref.shape)


def _merge_call(x, aos, alses, zt, yc, yd, proj, lw):
    t = x.shape[0]
    tm = 512
    gw = GW
    gate_blk = 4 * D_MODEL
    row = lambda w: pl.BlockSpec((tm, w), lambda i: (i, 0))
    slab = pl.BlockSpec((gw // LANE, tm, LANE), lambda i: (0, i, 0))
    full = lambda a: pl.BlockSpec(a.shape, lambda i: (0,) * a.ndim, pipeline_mode=pl.Buffered(1))
    consts = (lw["w_glu_t"], lw["b_glu"], lw["wb_a"], lw["wb_b"], lw["wb_c"], lw["wb_d"], lw["w_out"],
              lw["ln1_g"], lw["ln1_b"])
    return pl.pallas_call(
        _merge_kernel, name="branch_merge",
        grid=(t // tm,),
        in_specs=[row(D_MODEL)] + [row(gw), slab, slab] * 2
                 + [pl.BlockSpec((S5_WIDTH, tm), lambda i: (0, i)),
                  row(RET_HEADS * RET_V_DIM), row(MEM_HEADS * MEM_HEAD_PAD),
                  pl.BlockSpec((tm, gate_blk), lambda i: (i, _SEG["gate"] * LANE // gate_blk))]
                 + [full(a) for a in consts],
        out_specs=row(D_MODEL),
        out_shape=jax.ShapeDtypeStruct((t, D_MODEL), F32),
        compiler_params=_cparams(("parallel",)),
    )(x, *aos, *alses, zt, yc, yd, proj, *consts)


def _router_call(x1, lw):
    t = x1.shape[0]
    tm = 1024
    row = lambda w: pl.BlockSpec((tm, w), lambda i: (i, 0))
    full = lambda a: pl.BlockSpec(a.shape, lambda i: (0,) * a.ndim)
    tri = jnp.tril(jnp.ones((tm, tm), BF16), -1)
    consts = (lw["router_w_hi"], lw["router_w_lo"], lw["router_b"], tri)
    return pl.pallas_call(
        _router_kernel, name="moe_router",
        grid=(t // tm,),
        in_specs=[row(D_MODEL)] + [full(a) for a in consts],
        out_specs=[row(D_MODEL // 2), row(LANE), pl.BlockSpec((8, LANE), lambda i: (0, 0))],
        out_shape=[jax.ShapeDtypeStruct((t, D_MODEL // 2), jnp.uint32),
                   jax.ShapeDtypeStruct((t, LANE), F32),
                   jax.ShapeDtypeStruct((8, LANE), F32)],
        scratch_shapes=[pltpu.VMEM((8, LANE), F32)],
        compiler_params=_cparams(("arbitrary",)),
    )(x1, *consts)


def _pack_bf16_pairs(x):
    n = x.shape[1] // 2
    hi = lax.bitcast_convert_type(x[:, :n].astype(BF16).astype(F32), jnp.uint32)
    lo = lax.bitcast_convert_type(x[:, n:].astype(BF16).astype(F32), jnp.uint32)
    return hi | (lo >> 16)


def _unpack_bf16_pairs(w):
    hi = lax.bitcast_convert_type(w & jnp.uint32(0xFFFF0000), F32)
    lo = lax.bitcast_convert_type(w << 16, F32)
    return jnp.concatenate([hi, lo], axis=-1)


def _sc_index_rows(idx):
    window = SC_GATHER_WINDOW
    return jnp.pad(idx.reshape(-1, window), ((0, 0), (0, LANE - window)))


def _sc_scatter_rows(x, idx, n_out):
    n, d = x.shape
    m = idx.shape[0]
    window = SC_GATHER_WINDOW
    src_steps = n // window
    mesh = plsc.VectorSubcoreMesh(core_axis_name="core", subcore_axis_name="subcore")

    @pl.kernel(out_type=jax.ShapeDtypeStruct((n_out, d), x.dtype), mesh=mesh, scratch_types=[])
    def scatter_kernel(x_hbm, i_hbm, o_hbm):
        def body(x_vmem, i_vmem):
            pltpu.sync_copy(x_vmem, o_hbm.at[i_vmem.at[0, pl.ds(0, window)]])

        pltpu.emit_pipeline(
            body,
            grid=(m // window,),
            in_specs=[pl.BlockSpec((window, d), index_map=lambda i: (i % src_steps, 0)),
                      pl.BlockSpec((1, LANE), index_map=lambda i: (i, 0))],
            out_specs=[],
            core_axis_name=("core", "subcore"),
            dimension_semantics=(pltpu.PARALLEL,),
            trace_scopes=False,
        )(x_hbm, i_hbm)

    return scatter_kernel(x, _sc_index_rows(idx))


def _sc_gather_rows(x, idx):
    m, d = idx.shape[0], x.shape[1]
    window = SC_GATHER_WINDOW
    mesh = plsc.VectorSubcoreMesh(core_axis_name="core", subcore_axis_name="subcore")

    @pl.kernel(out_type=jax.ShapeDtypeStruct((m, d), x.dtype), mesh=mesh, scratch_types=[])
    def gather_kernel(x_hbm, i_hbm, o_hbm):
        def body(i_vmem, o_vmem):
            pltpu.sync_copy(x_hbm.at[i_vmem.at[0, pl.ds(0, window)]], o_vmem)

        pltpu.emit_pipeline(
            body,
            grid=(m // window,),
            in_specs=[pl.BlockSpec((1, LANE), index_map=lambda i: (i, 0))],
            out_specs=[pl.BlockSpec((window, d), index_map=lambda i: (i, 0))],
            core_axis_name=("core", "subcore"),
            dimension_semantics=(pltpu.PARALLEL,),
            trace_scopes=False,
        )(i_hbm, o_hbm)

    return gather_kernel(x, _sc_index_rows(idx))


def _moe_expert_kernel(be_ref, nu_ref, nv_ref, x_ref, w1_ref, b1_ref, w2_ref, b2_ref, o_ref, w1s_ref, w2s_ref):
    i = pl.program_id(0)

    @pl.when((i == 0) | (be_ref[i] != be_ref[jnp.maximum(i - 1, 0)]))
    def _():
        w1s_ref[...] = w1_ref[...].astype(BF16)
        w2s_ref[...] = w2_ref[...].astype(BF16)

    @pl.when(i < nu_ref[0])
    def _():
        rows = lax.broadcasted_iota(jnp.int32, (x_ref.shape[0], 1), 0)
        x = jnp.where(rows < nv_ref[i], _unpack_bf16_pairs(x_ref[...]), 0.0)
        h = jnp.dot(x.astype(BF16), w1s_ref[...], preferred_element_type=F32) + b1_ref[...]
        gate = jnp.minimum(h[:, :D_FF], SWIGLU_LIMIT)
        up = jnp.clip(h[:, D_FF:], -SWIGLU_LIMIT, SWIGLU_LIMIT)
        hid = gate * _sigmoid(SWIGLU_ALPHA * gate) * (up + 1.0)
        y = jnp.dot(hid.astype(BF16), w2s_ref[...], preferred_element_type=F32) + b2_ref[...]
        o_ref[...] = _pack_bf16_pairs(y)

    @pl.when(i >= nu_ref[0])
    def _():
        o_ref[...] = jnp.zeros_like(o_ref)


def _moe_expert_call(xg, block_expert, n_used, n_valid, layer, w1, b1, w2, b2):
    bm = MOE_BLOCK_ROWS
    n_blocks = xg.shape[0] // bm
    half = D_MODEL // 2
    grid_spec = pltpu.PrefetchScalarGridSpec(
        num_scalar_prefetch=3, grid=(n_blocks,),
        in_specs=[pl.BlockSpec((bm, half), lambda i, be, nu, nv: (i, 0)),
                  pl.BlockSpec((None, None, D_MODEL, 2 * D_FF), lambda i, be, nu, nv: (layer, be[i], 0, 0)),
                  pl.BlockSpec((None, 1, 2 * D_FF), lambda i, be, nu, nv: (be[i], 0, 0)),
                  pl.BlockSpec((None, None, D_FF, D_MODEL), lambda i, be, nu, nv: (layer, be[i], 0, 0)),
                  pl.BlockSpec((None, 1, D_MODEL), lambda i, be, nu, nv: (be[i], 0, 0))],
        out_specs=pl.BlockSpec((bm, half), lambda i, be, nu, nv: (i, 0)),
        scratch_shapes=[pltpu.VMEM((D_MODEL, 2 * D_FF), BF16), pltpu.VMEM((D_FF, D_MODEL), BF16)])
    return pl.pallas_call(
        _moe_expert_kernel, name="moe_experts",
        grid_spec=grid_spec,
        out_shape=jax.ShapeDtypeStruct((n_blocks * bm, half), jnp.uint32),
        compiler_params=_cparams(("arbitrary",)),
    )(block_expert, n_used, n_valid, xg, w1, b1, w2, b2)


def _moe_combine_kernel(x_ref, route_ref, y0_ref, y1_ref, y2_ref, y3_ref, g_ref, b_ref, o_ref):
    route = route_ref[...]
    moe = jnp.zeros(x_ref.shape, F32)
    for k, y_ref in enumerate((y0_ref, y1_ref, y2_ref, y3_ref)):
        moe = moe + route[:, TOP_K + k:TOP_K + k + 1] * _unpack_bf16_pairs(y_ref[...])
    o_ref[...] = _ln(DEEPNORM_ALPHA * x_ref[...] + moe, g_ref[...], b_ref[...])


def _moe_combine_call(x1, route, yg, g2, b2):
    t = x1.shape[0]
    tm = 512
    nt = t // tm
    ysel = lambda k: pl.BlockSpec((tm, D_MODEL // 2), lambda i: (k * nt + i, 0))
    return pl.pallas_call(
        _moe_combine_kernel, name="moe_combine_ln",
        grid=(nt,),
        in_specs=[pl.BlockSpec((tm, D_MODEL), lambda i: (i, 0)),
                  pl.BlockSpec((tm, LANE), lambda i: (i, 0))]
                 + [ysel(k) for k in range(TOP_K)]
                 + [pl.BlockSpec((1, D_MODEL), lambda i: (0, 0)),
                    pl.BlockSpec((1, D_MODEL), lambda i: (0, 0))],
        out_specs=pl.BlockSpec((tm, D_MODEL), lambda i: (i, 0)),
        out_shape=jax.ShapeDtypeStruct((t, D_MODEL), F32),
        compiler_params=_cparams(("parallel",)),
    )(x1, route, yg, yg, yg, yg, g2, b2)


def _moe(x1, x1p, route, counts, lw):
    t = x1.shape[0]
    bm = MOE_BLOCK_ROWS
    n_assign = t * TOP_K
    n_blocks = n_assign // bm + N_EXPERTS
    idx = route[:, :TOP_K].astype(jnp.int32)
    rank = route[:, 2 * TOP_K:3 * TOP_K].astype(jnp.int32)
    cnt = counts[0, :N_EXPERTS].astype(jnp.int32)
    padded = (cnt + bm - 1) // bm * bm
    pad_ends = jnp.cumsum(padded)
    pad_starts = pad_ends - padded
    dest = jnp.transpose(jnp.take(pad_starts, idx) + rank).reshape(-1)
    block_start = jnp.arange(n_blocks, dtype=jnp.int32) * bm
    block_expert = jnp.minimum(jnp.sum(pad_ends[None, :] <= block_start[:, None], axis=1),
                               N_EXPERTS - 1).astype(jnp.int32)
    n_valid = jnp.clip(jnp.take(cnt, block_expert) - (block_start - jnp.take(pad_starts, block_expert)), 0, bm)
    n_used = (pad_ends[-1:] // bm).astype(jnp.int32)
    xg = _sc_scatter_rows(x1p, dest, n_blocks * bm)
    yb = _moe_expert_call(xg, block_expert, n_used, n_valid.astype(jnp.int32), lw["layer"],
                          lw["moe_w1"], lw["moe_b1"], lw["moe_w2"], lw["moe_b2"])
    yg = _sc_gather_rows(yb, dest)
    return _moe_combine_call(x1, route, yg, lw["ln2_g"], lw["ln2_b"])


def _rotate_half_cols(w):
    half = RET_QK_DIM // 2
    w4 = w.reshape(w.shape[0], RET_HEADS, 2, half)
    return jnp.concatenate([-w4[:, :, 1:], w4[:, :, :1]], axis=2).reshape(w.shape)


def _pad_heads(w, axis):
    shp = list(w.shape)
    shp[axis:axis + 1] = [MEM_HEADS, MEM_HEAD_DIM]
    w = w.reshape(shp)
    pad = [(0, 0)] * w.ndim
    pad[axis + 1] = (0, MEM_HEAD_PAD - MEM_HEAD_DIM)
    w = jnp.pad(w, pad)
    shp[axis:axis + 2] = [MEM_HEADS * MEM_HEAD_PAD]
    return w.reshape(shp)


def _layer_weights(l, p):
    w_in = p["w_in"][l]
    aw, sw, qkw, vw, mw = 768, S5_WIDTH, RET_HEADS * RET_QK_DIM, RET_HEADS * RET_V_DIM, 768
    offs = [0]
    for wdt in (aw, aw, aw, sw, qkw, qkw, vw, vw, mw, 4 * D_MODEL):
        offs.append(offs[-1] + wdt)
    aq, ak, av, su, rq, rk, rv, rg, mq, gate = (w_in[:, offs[i]:offs[i + 1]] for i in range(10))
    seg = dict(gate=0.5 * gate, mq=_pad_heads(mq, 1), aq=aq[:, :GW], ak=ak[:, :GW], av=av[:, :GW],
               pad=jnp.zeros((D_MODEL, 2 * LANE), F32), rv=rv, rg=rg,
               rq=rq, rqs=_rotate_half_cols(rq), rk=rk, rks=_rotate_half_cols(rk))
    order = sorted(_SEG, key=_SEG.get)
    w_p = jnp.concatenate([seg[n] for n in order], axis=1).astype(BF16)
    assert w_p.shape[1] == NP
    w_d = jnp.concatenate([w[:, g * GW:(g + 1) * GW] for g in (1, 2) for w in (aq, ak, av)], axis=1).astype(BF16)
    wkv = p["w_mem_kv"][l]
    w_kv = jnp.concatenate([_pad_heads(wkv[:, :768], 1), _pad_heads(wkv[:, 768:], 1)], axis=1).astype(BF16)
    wb = p["w_branch"][l]
    rw = jnp.pad(p["router_w"][l].astype(F32), ((0, 0), (0, LANE - N_EXPERTS)))
    rb = jnp.pad(p["router_b"][l].astype(F32), (0, LANE - N_EXPERTS), constant_values=NEG_INF)
    return dict(
        w_p=w_p, w_d=w_d, w_su_t=jnp.transpose(su).astype(BF16), w_kv=w_kv,
        w_glu_t=jnp.transpose(p["s5_w_glu"][l]).astype(BF16), b_glu=p["s5_b_glu"][l].astype(F32).reshape(-1, 1),
        wb_a=wb[0].astype(BF16), wb_b=wb[1].astype(BF16), wb_c=wb[2].astype(BF16),
        wb_d=_pad_heads(wb[3], 0).astype(BF16), w_out=(0.5 * p["w_out"][l]).astype(BF16),
        ln1_g=p["ln1_g"][l].reshape(1, -1), ln1_b=p["ln1_b"][l].reshape(1, -1),
        router_w_hi=rw.astype(BF16), router_w_lo=(rw - rw.astype(BF16).astype(F32)).astype(BF16),
        router_b=rb.reshape(1, -1),
        layer=l, moe_w1=p["moe_w1"], moe_b1=p["moe_b1"][l].astype(F32).reshape(N_EXPERTS, 1, -1),
        moe_w2=p["moe_w2"], moe_b2=p["moe_b2"][l].astype(F32).reshape(N_EXPERTS, 1, -1),
        ln2_g=p["ln2_g"][l].reshape(1, -1), ln2_b=p["ln2_b"][l].reshape(1, -1),
        s5=_s5_tables(p["s5_lam_re"][l], p["s5_lam_im"][l], p["s5_log_step"][l], p["s5_b_re"][l],
                      p["s5_b_im"][l], p["s5_c_re"][l], p["s5_c_im"][l], p["s5_d"][l]))


def _trunk_layer(x, mem2d, b, s, lw, attn_bias, ret_tabs):
    proj, ut, pd = _proj_call(x, lw["w_p"], lw["w_su_t"], lw["w_d"])
    outs = [_attn_call(proj, attn_bias[0], b, s)]
    outs += [_dil_attn_call(pd, attn_bias[g], b, s, g, A_GROUPS[g][1]) for g in (1, 2)]
    aos, alses = [o for o, _ in outs], [l for _, l in outs]
    toep, win, wout, a_re, a_im = lw["s5"]
    nc = s // S5_CHUNK
    c_re, c_im = _s5_contrib_call(ut, win, b, nc)
    s_re, s_im = _s5_scan_call(c_re, c_im, a_re, a_im)
    zt = _s5_out_call(ut, toep, s_re, s_im, wout, b, nc)
    yc = _retention_call(proj, ret_tabs, b, s)
    kv = _mem_kv_call(mem2d, lw["w_kv"])
    yd = _mem_attn_call(proj, kv, b, s)
    x1 = _merge_call(x, aos, alses, zt, yc, yd, proj, lw)
    x1p, route, counts = _router_call(x1, lw)
    return _moe(x1, x1p, route, counts, lw)


def kernel(x_prompt, x_sample, mem_prompt, mem_sample, ln_in_g, ln_in_b, rel_bias, w_in, s5_lam_re, s5_lam_im, s5_log_step, s5_b_re, s5_b_im, s5_c_re, s5_c_im, s5_d, s5_w_glu, s5_b_glu, w_mem_kv, w_branch, w_out, ln1_g, ln1_b, router_w, router_b, moe_w1, moe_b1, moe_w2, moe_b2, ln2_g, ln2_b):
    p = dict(w_in=w_in, s5_lam_re=s5_lam_re, s5_lam_im=s5_lam_im, s5_log_step=s5_log_step, s5_b_re=s5_b_re,
             s5_b_im=s5_b_im, s5_c_re=s5_c_re, s5_c_im=s5_c_im, s5_d=s5_d, s5_w_glu=s5_w_glu, s5_b_glu=s5_b_glu,
             w_mem_kv=w_mem_kv, w_branch=w_branch, w_out=w_out, ln1_g=ln1_g, ln1_b=ln1_b, router_w=router_w,
             router_b=router_b, moe_w1=moe_w1, moe_b1=moe_b1, moe_w2=moe_w2, moe_b2=moe_b2, ln2_g=ln2_g,
             ln2_b=ln2_b)
    attn_bias = [_attn_bias_table(rel_bias, g, dil) for g, (_, dil) in enumerate(A_GROUPS)]
    trunks = []
    for x, mem in ((x_prompt, mem_prompt), (x_sample, mem_sample)):
        b, s, _ = x.shape
        trunks.append(dict(x=_layer_norm_call(x.reshape(b * s, D_MODEL), ln_in_g, ln_in_b),
                           mem=mem.reshape(b * N_MEM, D_MODEL), b=b, s=s, tabs=_ret_tables(s)))
    for l in range(DEPTH):
        lw = _layer_weights(l, p)
        for tr in trunks:
            tr["x"] = _trunk_layer(tr["x"], tr["mem"], tr["b"], tr["s"], lw, attn_bias, tr["tabs"])
    return tuple(tr["x"].reshape(tr["b"], tr["s"], D_MODEL) for tr in trunks)
```
